```python
import jax, jax.numpy as jnp
from jax import lax
import numpy as np

D_MODEL = 1024
BATCH = 8
SEQ = 4096
DEPTH = 2
DEC_BATCH = 2
DEC_SEQ = 16384
PAST_LEN = 128

HEAD_DIM = 64
MIX_WIDTH = D_MODEL
A_WIDTH = MIX_WIDTH // 2
A_HEADS = A_WIDTH // HEAD_DIM
A_KV_HEADS = A_HEADS // 4
A_KV_WIDTH = A_KV_HEADS * HEAD_DIM
WINDOW = 128
A_BLOCK = WINDOW
ROPE_THETA = 500000.0
ROPE_DIM = HEAD_DIM // 4
B_WIDTH = MIX_WIDTH // 4
B_HEADS = B_WIDTH // HEAD_DIM
GRID_W = 64
WIN_H = 8
WIN_W = 16
C_WIDTH = MIX_WIDTH - A_WIDTH - B_WIDTH
C_HEADS = C_WIDTH // HEAD_DIM
C_LOWRANK = 16
C_TAU = 16.0
C_CHUNK = 64
N_EXPERTS = 16
EC_CAPACITY = 2
D_EXPERT = D_MODEL
EPS = 1e-6
NEG_INF = -1e30
IN_SIZES = (A_WIDTH, A_KV_WIDTH, A_KV_WIDTH, B_WIDTH, B_WIDTH, B_WIDTH, C_WIDTH, C_WIDTH, C_WIDTH, C_WIDTH, C_LOWRANK, C_LOWRANK)
IN_WIDTH = sum(IN_SIZES)
SPLIT_POINTS = tuple(int(s) for s in np.cumsum(IN_SIZES)[:-1])

kernel_name = 'hybrid_bidir_encoder_hymba_ec'


def rmsnorm(x, g):
    xf = x.astype(jnp.float32)
    y = xf * lax.rsqrt(jnp.mean(xf * xf, axis=-1, keepdims=True) + EPS)
    return (y * g.astype(jnp.float32)).astype(x.dtype)


def rope_partial(x, pos):
    half = ROPE_DIM // 2
    inv = jnp.power(jnp.float32(ROPE_THETA), -jnp.arange(half, dtype=jnp.float32) * (2.0 / ROPE_DIM))
    ang = pos.astype(jnp.float32)[:, None] * inv[None, :]
    cos = jnp.cos(ang)[None, :, None, :]
    sin = jnp.sin(ang)[None, :, None, :]
    xr = x[..., :ROPE_DIM].astype(jnp.float32)
    x1, x2 = xr[..., :half], xr[..., half:]
    rot = jnp.concatenate([x1 * cos - x2 * sin, x2 * cos + x1 * sin], axis=-1).astype(x.dtype)
    return jnp.concatenate([rot, x[..., ROPE_DIM:]], axis=-1)


def window_gqa(q, k, v, sink):
    Bsz, T = q.shape[0], q.shape[1]
    nb = T // A_BLOCK
    grp = A_HEADS // A_KV_HEADS
    pos = jnp.arange(T)
    q = rope_partial(q, pos)
    k = rope_partial(k, pos)
    pad = ((0, 0), (WINDOW, WINDOW), (0, 0), (0, 0))
    kp, vp = jnp.pad(k, pad), jnp.pad(v, pad)

    def key_blocks(xp):
        parts = [xp[:, i * A_BLOCK:i * A_BLOCK + T].reshape(Bsz, nb, A_BLOCK, A_KV_HEADS, HEAD_DIM) for i in range(3)]
        return jnp.concatenate(parts, axis=2)

    kb, vb = key_blocks(kp), key_blocks(vp)
    qb = q.reshape(Bsz, nb, A_BLOCK, A_KV_HEADS, grp, HEAD_DIM)
    qq = jnp.arange(A_BLOCK)[:, None]
    kk = jnp.arange(3 * A_BLOCK)[None, :]
    kpos = jnp.arange(nb)[:, None, None] * A_BLOCK - A_BLOCK + kk[None]
    mask = (jnp.abs(kk - A_BLOCK - qq) <= WINDOW)[None] & (kpos >= 0) & (kpos < T)
    s = jnp.einsum('bnqhgd,bnkhd->bnhgqk', qb, kb, preferred_element_type=jnp.float32) * (HEAD_DIM ** -0.5)
    s = jnp.where(mask[None, :, None, None], s, NEG_INF)
    sink_b = sink.astype(jnp.float32).reshape(A_KV_HEADS, grp)[None, None, :, :, None, None]
    m = jnp.maximum(jnp.max(s, axis=-1, keepdims=True), sink_b)
    p = jnp.exp(s - m)
    p = p / (jnp.sum(p, axis=-1, keepdims=True) + jnp.exp(sink_b - m))
    o = jnp.einsum('bnhgqk,bnkhd->bnqhgd', p.astype(v.dtype), vb)
    return o.reshape(Bsz, T, A_WIDTH)


def neighborhood_attn(q, k, v, rel_bias):
    Bsz, T = q.shape[0], q.shape[1]
    rows = T // GRID_W
    kh = min(WIN_H, rows)
    r = jnp.arange(rows)
    ridx = jnp.clip(r - kh // 2, 0, rows - kh)[:, None] + jnp.arange(kh)[None, :]
    c = jnp.arange(GRID_W)
    cstart = jnp.clip(c - WIN_W // 2, 0, GRID_W - WIN_W)
    w = jnp.arange(GRID_W)
    colmask = (w[None, :] >= cstart[:, None]) & (w[None, :] < cstart[:, None] + WIN_W)
    row_off = ridx - r[:, None] + (WIN_H - 1)
    col_off = jnp.clip(w[None, :] - c[:, None] + (WIN_W - 1), 0, 2 * WIN_W - 2)
    bias = rel_bias.astype(jnp.float32)[:, row_off][..., col_off]
    bias = bias.transpose(1, 0, 3, 2, 4)
    bias = jnp.where(colmask[None, None, :, None, :], bias, NEG_INF)
    qg = q.reshape(Bsz, rows, GRID_W, B_HEADS, HEAD_DIM)
    kg = k.reshape(Bsz, rows, GRID_W, B_HEADS, HEAD_DIM)[:, ridx]
    vg = v.reshape(Bsz, rows, GRID_W, B_HEADS, HEAD_DIM)[:, ridx]
    s = jnp.einsum('brchd,brkwhd->brhckw', qg, kg, preferred_element_type=jnp.float32) * (HEAD_DIM ** -0.5) + bias[None]
    p = jax.nn.softmax(s.reshape(Bsz, rows, B_HEADS, GRID_W, kh * GRID_W), axis=-1).reshape(s.shape)
    o = jnp.einsum('brhckw,brkwhd->brchd', p.astype(v.dtype), vg)
    return o.reshape(Bsz, T, B_WIDTH)


def gla_scan(q, k, v, log_a):
    Bsz, T, H, Dk = q.shape
    Dv = v.shape[-1]
    n = T // C_CHUNK
    cs = lambda t: t.reshape(Bsz, n, C_CHUNK, H, t.shape[-1])
    q, k, v, la = cs(q), cs(k), cs(v), cs(log_a)
    b = jnp.cumsum(la, axis=2)
    b_last = b[:, :, -1:]
    q_i = q * jnp.exp(b)
    k_i = k * jnp.exp(-b)
    k_e = k * jnp.exp(b_last - b)
    tri = jnp.tril(jnp.ones((C_CHUNK, C_CHUNK), dtype=bool))
    a_in = jnp.where(tri, jnp.einsum('bnchk,bnshk->bnhcs', q_i, k_i), 0.0)
    o_intra = jnp.einsum('bnhcs,bnshv->bnchv', a_in, v)
    d_state = jnp.einsum('bnchk,bnchv->bnhkv', k_e, v)
    decay = jnp.exp(b_last[:, :, 0])

    def step(state, inp):
        dec, ds = inp
        return dec[..., None] * state + ds, state

    s0 = jnp.zeros((Bsz, H, Dk, Dv), jnp.float32)
    _, s_prev = lax.scan(step, s0, (jnp.moveaxis(decay, 1, 0), jnp.moveaxis(d_state, 1, 0)))
    s_prev = jnp.moveaxis(s_prev, 0, 1)
    o_inter = jnp.einsum('bnchk,bnhkv->bnchv', q_i, s_prev)
    return (o_intra + o_inter).reshape(Bsz, T, H, Dv)


def gla_mixer(cq, ck, cv, cg, caf, cab, w2_f, b_f, w2_b, b_b, norm_g):
    Bsz, T = cq.shape[0], cq.shape[1]
    f32 = jnp.float32
    shp = (Bsz, T, C_HEADS, HEAD_DIM)
    q = cq.astype(f32).reshape(shp) * (HEAD_DIM ** -0.5)
    k = ck.astype(f32).reshape(shp)
    v = cv.astype(f32).reshape(shp)
    la_f = (jax.nn.log_sigmoid((caf @ w2_f).astype(f32) + b_f.astype(f32)) / C_TAU).reshape(shp)
    la_b = (jax.nn.log_sigmoid((cab @ w2_b).astype(f32) + b_b.astype(f32)) / C_TAU).reshape(shp)
    flip = lambda t: jnp.flip(t, axis=1)
    o = gla_scan(q, k, v, la_f) + flip(gla_scan(flip(q), flip(k), flip(v), flip(la_b)))
    o = rmsnorm(o, norm_g.reshape(C_HEADS, HEAD_DIM)).reshape(Bsz, T, C_WIDTH)
    return (o * jax.nn.silu(cg.astype(f32))).astype(cq.dtype)


def expert_choice_ffn(x, w_router, w_gate, w_up, w_down):
    Bsz, T, D = x.shape
    n_tok = Bsz * T
    cap = EC_CAPACITY * n_tok // N_EXPERTS
    xf = x.reshape(n_tok, D)
    aff = jax.nn.softmax((xf @ w_router).astype(jnp.float32), axis=-1)
    gate, idx = lax.top_k(aff.T, cap)
    xe = xf[idx]
    hg = jnp.einsum('ecd,edf->ecf', xe, w_gate)
    hu = jnp.einsum('ecd,edf->ecf', xe, w_up)
    ye = jnp.einsum('ecf,efd->ecd', jax.nn.silu(hg) * hu, w_down)
    ye = ye * gate[..., None].astype(ye.dtype)
    out = jnp.zeros_like(xf).at[idx.reshape(-1)].add(ye.reshape(-1, D))
    return out.reshape(Bsz, T, D)


def encoder_layer(x, g1, w_in, a_sink, a_norm_g, b_rel_bias, b_norm_g, c_w2f, c_bf, c_w2b, c_bb, c_norm_g, w_out, g2, w_router, w_gate, w_up, w_down):
    Bsz, T = x.shape[0], x.shape[1]
    h = rmsnorm(x, g1)
    aq, ak, av, bq, bk, bv, cq, ck, cv, cg, caf, cab = jnp.split(h @ w_in, SPLIT_POINTS, axis=-1)
    heads = lambda t, nh: t.reshape(Bsz, T, nh, HEAD_DIM)
    oa = rmsnorm(window_gqa(heads(aq, A_HEADS), heads(ak, A_KV_HEADS), heads(av, A_KV_HEADS), a_sink), a_norm_g)
    ob = rmsnorm(neighborhood_attn(heads(bq, B_HEADS), heads(bk, B_HEADS), heads(bv, B_HEADS), b_rel_bias), b_norm_g)
    oc = gla_mixer(cq, ck, cv, cg, caf, cab, c_w2f, c_bf, c_w2b, c_bb, c_norm_g)
    x = x + jnp.concatenate([oa, ob, oc], axis=-1) @ w_out
    x = x + expert_choice_ffn(rmsnorm(x, g2), w_router, w_gate, w_up, w_down)
    return x


def trunk(x, norm1_g, w_in, a_sink, a_norm_g, b_rel_bias, b_norm_g, c_alpha_w2_f, c_alpha_b_f, c_alpha_w2_b, c_alpha_b_b, c_norm_g, w_out, norm2_g, w_router, w_gate, w_up, w_down, final_g):
    for l in range(DEPTH):
        x = encoder_layer(x, norm1_g[l], w_in[l], a_sink[l], a_norm_g[l], b_rel_bias[l], b_norm_g[l], c_alpha_w2_f[l], c_alpha_b_f[l], c_alpha_w2_b[l], c_alpha_b_b[l], c_norm_g[l], w_out[l], norm2_g[l], w_router[l], w_gate[l], w_up[l], w_down[l])
    return rmsnorm(x, final_g)


def setup_inputs(seed: int = 0) -> dict:
    key = jax.random.key(seed)
    ks = jax.random.split(key, 24)
    nrm = lambda k, shape, scale: jax.random.normal(k, shape, jnp.float32) * scale
    gain = lambda k, shape: 1.0 + 0.05 * jax.random.normal(k, shape, jnp.float32)
    return {
        'x_prompt': nrm(ks[0], (BATCH, SEQ, D_MODEL), 1.0),
        'x_sample': nrm(ks[1], (DEC_BATCH, DEC_SEQ, D_MODEL), 1.0),
        'norm1_g': gain(ks[2], (DEPTH, D_MODEL)),
        'w_in': nrm(ks[3], (DEPTH, D_MODEL, IN_WIDTH), D_MODEL ** -0.5),
        'a_sink': nrm(ks[4], (DEPTH, A_HEADS), 0.5),
        'a_norm_g': gain(ks[5], (DEPTH, A_WIDTH)),
        'b_rel_bias': nrm(ks[6], (DEPTH, B_HEADS, 2 * WIN_H - 1, 2 * WIN_W - 1), 0.1),
        'b_norm_g': gain(ks[7], (DEPTH, B_WIDTH)),
        'c_alpha_w2_f': nrm(ks[8], (DEPTH, C_LOWRANK, C_WIDTH), C_LOWRANK ** -0.5),
        'c_alpha_b_f': nrm(ks[9], (DEPTH, C_WIDTH), 0.1),
        'c_alpha_w2_b': nrm(ks[10], (DEPTH, C_LOWRANK, C_WIDTH), C_LOWRANK ** -0.5),
        'c_alpha_b_b': nrm(ks[11], (DEPTH, C_WIDTH), 0.1),
        'c_norm_g': gain(ks[12], (DEPTH, C_WIDTH)),
        'w_out': nrm(ks[13], (DEPTH, MIX_WIDTH, D_MODEL), MIX_WIDTH ** -0.5),
        'norm2_g': gain(ks[14], (DEPTH, D_MODEL)),
        'w_router': nrm(ks[15], (DEPTH, D_MODEL, N_EXPERTS), D_MODEL ** -0.5),
        'w_gate': nrm(ks[16], (DEPTH, N_EXPERTS, D_MODEL, D_EXPERT), D_MODEL ** -0.5),
        'w_up': nrm(ks[17], (DEPTH, N_EXPERTS, D_MODEL, D_EXPERT), D_MODEL ** -0.5),
        'w_down': nrm(ks[18], (DEPTH, N_EXPERTS, D_EXPERT, D_MODEL), D_EXPERT ** -0.5),
        'final_g': gain(ks[19], (D_MODEL,)),
    }


def reference(x_prompt, x_sample, norm1_g, w_in, a_sink, a_norm_g, b_rel_bias, b_norm_g, c_alpha_w2_f, c_alpha_b_f, c_alpha_w2_b, c_alpha_b_b, c_norm_g, w_out, norm2_g, w_router, w_gate, w_up, w_down, final_g):
    y_prompt = trunk(x_prompt, norm1_g, w_in, a_sink, a_norm_g, b_rel_bias, b_norm_g, c_alpha_w2_f, c_alpha_b_f, c_alpha_w2_b, c_alpha_b_b, c_norm_g, w_out, norm2_g, w_router, w_gate, w_up, w_down, final_g)
    y_sample = trunk(x_sample, norm1_g, w_in, a_sink, a_norm_g, b_rel_bias, b_norm_g, c_alpha_w2_f, c_alpha_b_f, c_alpha_w2_b, c_alpha_b_b, c_norm_g, w_out, norm2_g, w_router, w_gate, w_up, w_down, final_g)
    return (y_prompt, y_sample)
```

```python
import functools

import jax
import jax.numpy as jnp
from jax import lax
from jax.experimental import pallas as pl
from jax.experimental.pallas import tpu as pltpu

F32 = jnp.float32
BF16 = jnp.bfloat16

D_MODEL = 1024
HEAD_DIM = 64
A_WIDTH = 512
A_HEADS = 8
A_KV_HEADS = 2
A_KV_WIDTH = 128
WINDOW = 128
ROPE_THETA = 500000.0
ROPE_DIM = 16
B_WIDTH = 256
B_HEADS = 4
GRID_W = 64
WIN_H = 8
WIN_W = 16
C_WIDTH = 256
C_HEADS = 4
C_LOWRANK = 16
C_TAU = 16.0
C_CHUNK = 64
N_EXPERTS = 16
EC_CAPACITY = 2
EPS = 1e-6
NEG_INF = -1e30
IN_WIDTH = 2592

_OFF_AQ, _OFF_AK, _OFF_AV = 0, 512, 640
_OFF_BQ, _OFF_BK, _OFF_BV = 768, 1024, 1280
_OFF_CQ, _OFF_CK, _OFF_CV, _OFF_CG = 1536, 1792, 2048, 2304
_OFF_LR = 2560

LANES = 128
VMEM_LIMIT = 56 * 1024 * 1024

IN_PROJ_ROWS = 512
GLA_BLOCK = 512
ROUTE_TILE = 256
MOE_TILE = 2048
MOE_SLOTS = 128


def _cparams(sem):
    return pltpu.CompilerParams(dimension_semantics=sem, vmem_limit_bytes=VMEM_LIMIT)


def _nt_dot(a, b):
    return lax.dot_general(a, b, (((1,), (1,)), ((), ())), preferred_element_type=F32)


def _tn_dot(a, b):
    return lax.dot_general(a, b, (((0,), (0,)), ((), ())), preferred_element_type=F32)


def _dot(a, b):
    return jnp.dot(a, b, preferred_element_type=F32)


def _rms(x, g):
    return x * lax.rsqrt(jnp.mean(x * x, axis=-1, keepdims=True) + EPS) * g


def _in_proj_kernel(x_ref, g_ref, w_ref, cos_ref, s1_ref, s2_ref, w2_ref, b2_ref,
                    aq_ref, ak_ref, av_ref, bq_ref, bk_ref, bv_ref,
                    cq_ref, ck_ref, cv_ref, cg_ref, la_ref):
    h = _rms(x_ref[...], g_ref[...]).astype(BF16)

    def proj(lo, width):
        return _dot(h, w_ref[:, lo:lo + width])

    cos, s1, s2 = cos_ref[...], s1_ref[...], s2_ref[...]

    def rope(t):
        return t * cos + pltpu.roll(t, LANES - ROPE_DIM // 2, 1) * s1 + pltpu.roll(t, ROPE_DIM // 2, 1) * s2

    scale = HEAD_DIM ** -0.5
    aq = proj(_OFF_AQ, A_WIDTH)
    for c in range(A_WIDTH // LANES):
        aq_ref[:, c * LANES:(c + 1) * LANES] = (rope(aq[:, c * LANES:(c + 1) * LANES]) * scale).astype(BF16)
    ak_ref[...] = rope(proj(_OFF_AK, A_KV_WIDTH)).astype(BF16)
    av_ref[...] = proj(_OFF_AV, A_KV_WIDTH).astype(BF16)
    bq_ref[...] = (proj(_OFF_BQ, B_WIDTH) * scale).astype(BF16)
    bk_ref[...] = proj(_OFF_BK, B_WIDTH).astype(BF16)
    bv_ref[...] = proj(_OFF_BV, B_WIDTH).astype(BF16)
    cq_ref[...] = proj(_OFF_CQ, C_WIDTH) * scale
    ck_ref[...] = proj(_OFF_CK, C_WIDTH)
    cv_ref[...] = proj(_OFF_CV, C_WIDTH).astype(BF16)
    cg_ref[...] = proj(_OFF_CG, C_WIDTH)
    lr = proj(_OFF_LR, 2 * C_LOWRANK)
    z = jnp.dot(lr, w2_ref[...], preferred_element_type=F32, precision=lax.Precision.HIGHEST) + b2_ref[...]
    la_ref[...] = (jnp.minimum(z, 0.0) - jnp.log(1.0 + jnp.exp(-jnp.abs(z)))) * (1.0 / C_TAU)


def _rope_tables(seq):
    half = ROPE_DIM // 2
    inv = jnp.power(jnp.float32(ROPE_THETA), -jnp.arange(half, dtype=F32) * (2.0 / ROPE_DIM))
    ang = jnp.arange(seq, dtype=F32)[:, None] * inv[None, :]
    cos, sin = jnp.cos(ang), jnp.sin(ang)
    ones = jnp.ones((seq, HEAD_DIM - ROPE_DIM), F32)
    zeros = jnp.zeros((seq, HEAD_DIM - half), F32)
    c = jnp.concatenate([cos, cos, ones], axis=1)
    s1 = jnp.concatenate([-sin, zeros], axis=1)
    s2 = jnp.concatenate([jnp.zeros((seq, half), F32), sin, jnp.zeros((seq, HEAD_DIM - ROPE_DIM), F32)], axis=1)
    two = lambda t: jnp.concatenate([t, t], axis=1)
    return two(c), two(s1), two(s2)


def _in_proj(x, g1, w_in_bf, w2, b2, seq):
    n = x.shape[0]
    tm = min(IN_PROJ_ROWS, seq)
    per_seq = seq // tm
    cos, s1, s2 = _rope_tables(seq)
    row = lambda width: pl.BlockSpec((tm, width), lambda i: (i, 0))
    full = lambda a: pl.BlockSpec(a.shape, lambda i: (0,) * a.ndim)
    tab = pl.BlockSpec((tm, LANES), lambda i: (i % per_seq, 0))
    widths = [(A_WIDTH, BF16), (A_KV_WIDTH, BF16), (A_KV_WIDTH, BF16), (B_WIDTH, BF16), (B_WIDTH, BF16),
              (B_WIDTH, BF16), (C_WIDTH, F32), (C_WIDTH, F32), (C_WIDTH, BF16), (C_WIDTH, F32), (2 * C_WIDTH, F32)]
    return pl.pallas_call(
        _in_proj_kernel,
        grid=(n // tm,),
        in_specs=[row(D_MODEL), full(g1), full(w_in_bf), tab, tab, tab, full(w2), full(b2)],
        out_specs=[row(w) for w, _ in widths],
        out_shape=[jax.ShapeDtypeStruct((n, w), dt) for w, dt in widths],
        compiler_params=_cparams(("parallel",)),
        name="in_proj",
    )(x, g1, w_in_bf, cos, s1, s2, w2, b2)


def _attn_a_kernel(sink_ref, q_ref, kp_ref, kc_ref, kn_ref, vp_ref, vc_ref, vn_ref, g_ref, o_ref, acc_ref, *, seq):
    i = pl.program_id(1)
    k = jnp.concatenate([kp_ref[...], kc_ref[...], kn_ref[...]], axis=0)
    v = jnp.concatenate([vp_ref[...], vc_ref[...], vn_ref[...]], axis=0)
    qq = lax.broadcasted_iota(jnp.int32, (WINDOW, 3 * WINDOW), 0)
    kk = lax.broadcasted_iota(jnp.int32, (WINDOW, 3 * WINDOW), 1)
    kpos = i * WINDOW - WINDOW + kk
    mask = (jnp.abs(kk - WINDOW - qq) <= WINDOW) & (kpos >= 0) & (kpos < seq)
    grp = A_HEADS // A_KV_HEADS
    for h in range(A_HEADS):
        kv = h // grp
        qh = q_ref[:, h * HEAD_DIM:(h + 1) * HEAD_DIM]
        s = _nt_dot(qh, k[:, kv * HEAD_DIM:(kv + 1) * HEAD_DIM])
        s = jnp.where(mask, s, NEG_INF)
        sink = sink_ref[h]
        m = jnp.maximum(jnp.max(s, axis=-1, keepdims=True), sink)
        p = jnp.exp(s - m)
        den = jnp.sum(p, axis=-1, keepdims=True) + jnp.exp(sink - m)
        o = _dot(p.astype(BF16), v[:, kv * HEAD_DIM:(kv + 1) * HEAD_DIM])
        acc_ref[:, h * HEAD_DIM:(h + 1) * HEAD_DIM] = o / den
    o_ref[...] = _rms(acc_ref[...], g_ref[...]).astype(BF16)


def _attn_a(aq, ak, av, sink, g, bsz, seq):
    nb = seq // WINDOW
    qspec = pl.BlockSpec((WINDOW, A_WIDTH), lambda b, i: (b * nb + i, 0))
    kprev = pl.BlockSpec((WINDOW, A_KV_WIDTH), lambda b, i: (b * nb + jnp.maximum(i - 1, 0), 0))
    kcur = pl.BlockSpec((WINDOW, A_KV_WIDTH), lambda b, i: (b * nb + i, 0))
    knext = pl.BlockSpec((WINDOW, A_KV_WIDTH), lambda b, i: (b * nb + jnp.minimum(i + 1, nb - 1), 0))
    return pl.pallas_call(
        functools.partial(_attn_a_kernel, seq=seq),
        grid=(bsz, nb),
        in_specs=[pl.BlockSpec(memory_space=pltpu.SMEM), qspec, kprev, kcur, knext, kprev, kcur, knext,
                  pl.BlockSpec((1, A_WIDTH), lambda b, i: (0, 0))],
        out_specs=qspec,
        out_shape=jax.ShapeDtypeStruct((bsz * seq, A_WIDTH), BF16),
        scratch_shapes=[pltpu.VMEM((WINDOW, A_WIDTH), F32)],
        compiler_params=_cparams(("parallel", "parallel")),
        name="attn_a",
    )(sink, aq, ak, ak, ak, av, av, av, g)


B_GROUP = 8
B_KEYS = WIN_H * GRID_W


def _bias_table_kernel(rel_ref, o_ref):
    c = lax.broadcasted_iota(jnp.int32, (GRID_W, GRID_W), 0)
    w = lax.broadcasted_iota(jnp.int32, (GRID_W, GRID_W), 1)
    cstart = jnp.clip(c - WIN_W // 2, 0, GRID_W - WIN_W)
    colmask = (w >= cstart) & (w < cstart + WIN_W)
    col_off = jnp.clip(w - c + (WIN_W - 1), 0, 2 * WIN_W - 2)
    n_ro, n_co = 2 * WIN_H - 1, 2 * WIN_W - 1
    for h in range(B_HEADS):
        for ro in range(n_ro):
            def body(j, acc):
                return jnp.where(col_off == j, rel_ref[(h * n_ro + ro) * n_co + j], acc)
            t = lax.fori_loop(0, n_co, body, jnp.zeros((GRID_W, GRID_W), F32))
            t = jnp.where(colmask, t, NEG_INF)
            for p in range(WIN_H):
                kidx = ro - (WIN_H - 1) + p
                if 0 <= kidx < WIN_H:
                    o_ref[p, h, :, kidx * GRID_W:(kidx + 1) * GRID_W] = t


def _bias_table(rel_bias):
    return pl.pallas_call(
        _bias_table_kernel,
        in_specs=[pl.BlockSpec(memory_space=pltpu.SMEM)],
        out_shape=jax.ShapeDtypeStruct((WIN_H, B_HEADS, GRID_W, B_KEYS), F32),
        name="bias_table",
    )(rel_bias.reshape(-1))


def _attn_b_kernel(q_ref, kp_ref, kc_ref, kn_ref, vp_ref, vc_ref, vn_ref, bias_ref, g_ref, o_ref,
                   kbuf, vbuf, acc_ref, *, rows):
    gidx = pl.program_id(1)
    blk = B_GROUP * GRID_W
    kbuf[0:blk, :] = kp_ref[...]
    kbuf[blk:2 * blk, :] = kc_ref[...]
    kbuf[2 * blk:3 * blk, :] = kn_ref[...]
    vbuf[0:blk, :] = vp_ref[...]
    vbuf[blk:2 * blk, :] = vc_ref[...]
    vbuf[2 * blk:3 * blk, :] = vn_ref[...]
    for j in range(B_GROUP):
        r = gidx * B_GROUP + j
        start = jnp.clip(r - WIN_H // 2, 0, rows - WIN_H)
        pat = r - start
        loc = pl.multiple_of((start - gidx * B_GROUP + B_GROUP) * GRID_W, GRID_W)
        kw = kbuf[pl.ds(loc, B_KEYS), :]
        vw = vbuf[pl.ds(loc, B_KEYS), :]
        for h in range(B_HEADS):
            qh = q_ref[j * GRID_W:(j + 1) * GRID_W, h * HEAD_DIM:(h + 1) * HEAD_DIM]
            s = _nt_dot(qh, kw[:, h * HEAD_DIM:(h + 1) * HEAD_DIM]) + bias_ref[pat, h]
            m = jnp.max(s, axis=-1, keepdims=True)
            p = jnp.exp(s - m)
            den = jnp.sum(p, axis=-1, keepdims=True)
            o = _dot(p.astype(BF16), vw[:, h * HEAD_DIM:(h + 1) * HEAD_DIM])
            acc_ref[j * GRID_W:(j + 1) * GRID_W, h * HEAD_DIM:(h + 1) * HEAD_DIM] = o / den
    o_ref[...] = _rms(acc_ref[...], g_ref[...]).astype(BF16)


def _attn_b(bq, bk, bv, bias, g, bsz, seq):
    rows = seq // GRID_W
    assert rows % B_GROUP == 0 and rows >= WIN_H
    ng = rows // B_GROUP
    blk = B_GROUP * GRID_W
    cur = pl.BlockSpec((blk, B_WIDTH), lambda b, i: (b * ng + i, 0))
    prev = pl.BlockSpec((blk, B_WIDTH), lambda b, i: (b * ng + jnp.maximum(i - 1, 0), 0))
    nxt = pl.BlockSpec((blk, B_WIDTH), lambda b, i: (b * ng + jnp.minimum(i + 1, ng - 1), 0))
    return pl.pallas_call(
        functools.partial(_attn_b_kernel, rows=rows),
        grid=(bsz, ng),
        in_specs=[cur, prev, cur, nxt, prev, cur, nxt,
                  pl.BlockSpec(bias.shape, lambda b, i: (0, 0, 0, 0)),
                  pl.BlockSpec((1, B_WIDTH), lambda b, i: (0, 0))],
        out_specs=cur,
        out_shape=jax.ShapeDtypeStruct((bsz * seq, B_WIDTH), BF16),
        scratch_shapes=[pltpu.VMEM((3 * blk, B_WIDTH), BF16), pltpu.VMEM((3 * blk, B_WIDTH), BF16),
                        pltpu.VMEM((blk, B_WIDTH), F32)],
        compiler_params=_cparams(("parallel", "parallel")),
        name="attn_b",
    )(bq, bk, bk, bk, bv, bv, bv, bias, g)


def _gla_dir(q_ref, k_ref, v_ref, la_ref, o_ref, st_ref, reverse):
    ci = lax.broadcasted_iota(jnp.int32, (C_CHUNK, C_CHUNK), 0)
    cj = lax.broadcasted_iota(jnp.int32, (C_CHUNK, C_CHUNK), 1)
    tri = (cj >= ci) if reverse else (cj <= ci)
    cum_w = tri.astype(F32)
    n_chunks = q_ref.shape[0] // C_CHUNK
    order = range(n_chunks - 1, -1, -1) if reverse else range(n_chunks)
    for c in order:
        rows = slice(c * C_CHUNK, (c + 1) * C_CHUNK)
        la = la_ref[rows, :]
        b = jnp.dot(cum_w, la, preferred_element_type=F32, precision=lax.Precision.HIGHEST)
        b_tot = b[0:1, :] if reverse else b[C_CHUNK - 1:C_CHUNK, :]
        q_i = (q_ref[rows, :] * jnp.exp(b)).astype(BF16)
        kf = k_ref[rows, :]
        k_i = (kf * jnp.exp(-b)).astype(BF16)
        k_e = (kf * jnp.exp(b_tot - b)).astype(BF16)
        decay = jnp.exp(b_tot)
        v = v_ref[rows, :]
        for h in range(C_HEADS):
            cols = slice(h * HEAD_DIM, (h + 1) * HEAD_DIM)
            a = jnp.where(tri, _nt_dot(q_i[:, cols], k_i[:, cols]), 0.0)
            st = st_ref[h]
            o = _dot(a.astype(BF16), v[:, cols]) + _nt_dot(q_i[:, cols], st.astype(BF16))
            o_ref[rows, cols] = o
            st_ref[h] = st * decay[:, cols] + _tn_dot(v[:, cols], k_e[:, cols])


def _gla_kernel(qf_ref, kf_ref, vf_ref, laf_ref, qb_ref, kb_ref, vb_ref, lab_ref, of_ref, ob_ref, sf_ref, sb_ref):
    @pl.when(pl.program_id(1) == 0)
    def _():
        sf_ref[...] = jnp.zeros_like(sf_ref)
        sb_ref[...] = jnp.zeros_like(sb_ref)

    _gla_dir(qf_ref, kf_ref, vf_ref, laf_ref, of_ref, sf_ref, reverse=False)
    _gla_dir(qb_ref, kb_ref, vb_ref, lab_ref, ob_ref, sb_ref, reverse=True)


def _gla(cq, ck, cv, la, bsz, seq):
    tb = min(GLA_BLOCK, seq)
    nb = seq // tb
    fwd = lambda col: pl.BlockSpec((tb, C_WIDTH), lambda b, i: (b * nb + i, col))
    bwd = lambda col: pl.BlockSpec((tb, C_WIDTH), lambda b, i: (b * nb + nb - 1 - i, col))
    out = jax.ShapeDtypeStruct((bsz * seq, C_WIDTH), F32)
    state = pltpu.VMEM((C_HEADS, HEAD_DIM, HEAD_DIM), F32)
    return pl.pallas_call(
        _gla_kernel,
        grid=(bsz, nb),
        in_specs=[fwd(0), fwd(0), fwd(0), fwd(0), bwd(0), bwd(0), bwd(0), bwd(1)],
        out_specs=[fwd(0), bwd(0)],
        out_shape=[out, out],
        scratch_shapes=[state, state],
        compiler_params=_cparams(("parallel", "arbitrary")),
        name="gla",
    )(cq, ck, cv, la, cq, ck, cv, la)


def _split_bf16(x):
    hi = x.astype(BF16)
    return hi, (x - hi.astype(F32)).astype(BF16)


def _out_proj_kernel(x_ref, oa_ref, ob_ref, of_ref, obw_ref, cg_ref, cng_ref, wo_ref, g2_ref, wr_hi_ref, wr_lo_ref,
                     x1_ref, h2_ref, aff_ref):
    o = of_ref[...] + obw_ref[...]
    gi = lax.broadcasted_iota(jnp.int32, (C_WIDTH, C_WIDTH), 0) // HEAD_DIM
    gj = lax.broadcasted_iota(jnp.int32, (C_WIDTH, C_WIDTH), 1) // HEAD_DIM
    ones_bd = (gi == gj).astype(BF16)
    sq_hi, sq_lo = _split_bf16(o * o)
    ms = (_dot(sq_hi, ones_bd) + _dot(sq_lo, ones_bd)) * (1.0 / HEAD_DIM)
    cg = cg_ref[...]
    oc = (o * lax.rsqrt(ms + EPS) * cng_ref[...]) * (cg / (1.0 + jnp.exp(-cg)))
    y = _dot(oa_ref[...], wo_ref[0:A_WIDTH, :])
    y += _dot(ob_ref[...], wo_ref[A_WIDTH:A_WIDTH + B_WIDTH, :])
    y += _dot(oc.astype(BF16), wo_ref[A_WIDTH + B_WIDTH:, :])
    x1 = x_ref[...] + y
    x1_ref[...] = x1
    h2 = _rms(x1, g2_ref[...])
    h2_ref[...] = h2.astype(BF16)
    h_hi, h_lo = _split_bf16(h2)
    logits = _nt_dot(wr_hi_ref[...], h_hi) + _nt_dot(wr_hi_ref[...], h_lo) + _nt_dot(wr_lo_ref[...], h_hi)
    m = jnp.max(logits, axis=0, keepdims=True)
    p = jnp.exp(logits - m)
    aff_ref[...] = p / jnp.sum(p, axis=0, keepdims=True)


def _out_proj(x, oa, ob, o_f, o_b, cg, cng, wo_bf, g2, wr_hi, wr_lo):
    n = x.shape[0]
    tm = min(IN_PROJ_ROWS, n)
    row = lambda width: pl.BlockSpec((tm, width), lambda i: (i, 0))
    full = lambda a: pl.BlockSpec(a.shape, lambda i: (0,) * a.ndim)
    return pl.pallas_call(
        _out_proj_kernel,
        grid=(n // tm,),
        in_specs=[row(D_MODEL), row(A_WIDTH), row(B_WIDTH), row(C_WIDTH), row(C_WIDTH), row(C_WIDTH),
                  full(cng), full(wo_bf), full(g2), full(wr_hi), full(wr_lo)],
        out_specs=[row(D_MODEL), row(D_MODEL), pl.BlockSpec((N_EXPERTS, tm), lambda i: (0, i))],
        out_shape=[jax.ShapeDtypeStruct((n, D_MODEL), F32), jax.ShapeDtypeStruct((n, D_MODEL), BF16),
                   jax.ShapeDtypeStruct((N_EXPERTS, n), F32)],
        compiler_params=_cparams(("parallel",)),
        name="out_proj",
    )(x, oa, ob, o_f, o_b, cg, cng, wo_bf, g2, wr_hi, wr_lo)


def _route_kernel(aff_ref, pos_ref, offs_ref, *, cap):
    n = aff_ref.shape[1]
    nt = n // ROUTE_TILE
    bits = pltpu.bitcast(aff_ref[...], jnp.int32)

    def search(it, thr):
        cand = thr | jnp.left_shift(jnp.int32(1), 30 - it)
        cnt = jnp.sum(jnp.where(bits >= cand, 1.0, 0.0), axis=1, keepdims=True)
        return jnp.where(cnt >= cap, cand, thr)

    thr = lax.fori_loop(0, 31, search, jnp.zeros((N_EXPERTS, 1), jnp.int32))
    n_gt = jnp.sum(jnp.where(bits > thr, 1.0, 0.0), axis=1, keepdims=True)
    n_tie = cap - n_gt

    ui = lax.broadcasted_iota(jnp.int32, (ROUTE_TILE, ROUTE_TILE), 0)
    uj = lax.broadcasted_iota(jnp.int32, (ROUTE_TILE, ROUTE_TILE), 1)
    upper = (ui <= uj).astype(BF16)
    lane = lax.broadcasted_iota(jnp.int32, offs_ref.shape, 1)

    def tile(i, carry):
        c_gt, c_eq, offs = carry
        col = pl.multiple_of(i * ROUTE_TILE, ROUTE_TILE)
        b = pltpu.bitcast(aff_ref[:, pl.ds(col, ROUTE_TILE)], jnp.int32)
        gt = jnp.where(b > thr, 1.0, 0.0)
        eq = jnp.where(b == thr, 1.0, 0.0)
        inc = _dot(jnp.concatenate([gt, eq], axis=0).astype(BF16), upper)
        gt_before = c_gt + inc[:N_EXPERTS] - gt
        eq_before = c_eq + inc[N_EXPERTS:] - eq
        sel = (gt > 0.0) | ((eq > 0.0) & (eq_before < n_tie))
        pos = gt_before + jnp.minimum(eq_before, n_tie)
        pos_ref[:, pl.ds(col, ROUTE_TILE)] = jnp.where(sel, pos, -1.0).astype(jnp.int32)
        start = c_gt + jnp.minimum(c_eq, n_tie)
        offs = jnp.where(lane == i, start.astype(jnp.int32), offs)
        return (c_gt + jnp.sum(gt, axis=1, keepdims=True), c_eq + jnp.sum(eq, axis=1, keepdims=True), offs)

    zero = jnp.zeros((N_EXPERTS, 1), F32)
    offs = jnp.where(lane == nt, cap, 0).astype(jnp.int32)
    _, _, offs = lax.fori_loop(0, nt, tile, (zero, zero, offs))
    offs_ref[...] = offs


def _route(aff_t, cap):
    n = aff_t.shape[1]
    nt = n // ROUTE_TILE
    return pl.pallas_call(
        functools.partial(_route_kernel, cap=cap),
        out_shape=[jax.ShapeDtypeStruct((N_EXPERTS, n), jnp.int32),
                   jax.ShapeDtypeStruct((N_EXPERTS, nt + 1), jnp.int32)],
        compiler_params=pltpu.CompilerParams(vmem_limit_bytes=VMEM_LIMIT),
        name="route",
    )(aff_t)


def _moe_kernel(offs_ref, h_ref, x1_ref, pos_ref, gate_ref, wg_ref, wu_ref, wd_ref, fg_ref, o_ref,
                xg_ref, gs_ref, *, final_norm):
    t = pl.program_id(0)
    e = pl.program_id(1)
    subs = h_ref.shape[0] // ROUTE_TILE

    @pl.when(e == 0)
    def _():
        o_ref[...] = x1_ref[...]

    first = offs_ref[e, t * subs]
    last = offs_ref[e, (t + 1) * subs]
    base = (first // 8) * 8
    n_blocks = jnp.where(last > first, (last - base + MOE_SLOTS - 1) // MOE_SLOTS, 0)
    slot = lax.broadcasted_iota(jnp.int32, (MOE_SLOTS, ROUTE_TILE), 0)

    def onehot(s, lo):
        pos = pos_ref[pl.ds(e, 1), s * ROUTE_TILE:(s + 1) * ROUTE_TILE]
        return (pos - lo) == slot

    def block(b, carry):
        lo = base + b * MOE_SLOTS
        xg_ref[...] = jnp.zeros_like(xg_ref)
        gs_ref[...] = jnp.zeros_like(gs_ref)
        for s in range(subs):
            touches = (offs_ref[e, t * subs + s + 1] > lo) & (offs_ref[e, t * subs + s] < lo + MOE_SLOTS)

            @pl.when(touches)
            def _():
                hit = onehot(s, lo)
                xg_ref[...] += _dot(jnp.where(hit, 1.0, 0.0).astype(BF16), h_ref[s * ROUTE_TILE:(s + 1) * ROUTE_TILE, :])
                gate = gate_ref[pl.ds(e, 1), s * ROUTE_TILE:(s + 1) * ROUTE_TILE]
                gs_ref[...] += jnp.sum(jnp.where(hit, gate, 0.0), axis=1, keepdims=True)

        xg = xg_ref[...].astype(BF16)
        hg = _dot(xg, wg_ref[0])
        hu = _dot(xg, wu_ref[0])
        act = (hg / (1.0 + jnp.exp(-hg))) * hu * gs_ref[...]
        y = _dot(act.astype(BF16), wd_ref[0])
        y_hi, y_lo = _split_bf16(y)
        for s in range(subs):
            touches = (offs_ref[e, t * subs + s + 1] > lo) & (offs_ref[e, t * subs + s] < lo + MOE_SLOTS)

            @pl.when(touches)
            def _():
                hit = jnp.where(onehot(s, lo), 1.0, 0.0).astype(BF16)
                rows = slice(s * ROUTE_TILE, (s + 1) * ROUTE_TILE)
                o_ref[rows, :] += _tn_dot(hit, y_hi) + _tn_dot(hit, y_lo)
        return carry

    lax.fori_loop(0, n_blocks, block, 0)

    if final_norm:
        @pl.when(e == N_EXPERTS - 1)
        def _():
            o_ref[...] = _rms(o_ref[...], fg_ref[...])


def _moe(offs, h2, x1, pos, gate, wg, wu, wd, final_g, final_norm):
    n = h2.shape[0]
    tt = min(MOE_TILE, n)
    tok = pl.BlockSpec((tt, D_MODEL), lambda t, e, offs: (t, 0))
    per_tok = pl.BlockSpec((N_EXPERTS, tt), lambda t, e, offs: (0, t))
    wspec = pl.BlockSpec((1, D_MODEL, D_MODEL), lambda t, e, offs: (e, 0, 0))
    return pl.pallas_call(
        functools.partial(_moe_kernel, final_norm=final_norm),
        grid_spec=pltpu.PrefetchScalarGridSpec(
            num_scalar_prefetch=1,
            grid=(n // tt, N_EXPERTS),
            in_specs=[tok, tok, per_tok, per_tok, wspec, wspec, wspec,
                      pl.BlockSpec((1, D_MODEL), lambda t, e, offs: (0, 0))],
            out_specs=tok,
            scratch_shapes=[pltpu.VMEM((MOE_SLOTS, D_MODEL), F32), pltpu.VMEM((MOE_SLOTS, 1), F32)],
        ),
        out_shape=jax.ShapeDtypeStruct((n, D_MODEL), F32),
        compiler_params=_cparams(("parallel", "arbitrary")),
        name="moe",
    )(offs, h2, x1, pos, gate, wg, wu, wd, final_g)


def _prep_layer(l, norm1_g, w_in, a_sink, a_norm_g, b_rel_bias, b_norm_g, c_alpha_w2_f, c_alpha_b_f,
                c_alpha_w2_b, c_alpha_b_b, c_norm_g, w_out, norm2_g, w_router, w_gate, w_up, w_down):
    z = jnp.zeros((C_LOWRANK, C_WIDTH), F32)
    w2 = jnp.concatenate([jnp.concatenate([c_alpha_w2_f[l], z], axis=1),
                          jnp.concatenate([z, c_alpha_w2_b[l]], axis=1)], axis=0)
    b2 = jnp.concatenate([c_alpha_b_f[l], c_alpha_b_b[l]])[None, :]
    wr_t = w_router[l].T
    wr_hi = wr_t.astype(BF16)
    wr_lo = (wr_t - wr_hi.astype(F32)).astype(BF16)
    return dict(
        g1=norm1_g[l][None, :], w_in=w_in[l].astype(BF16), w2=w2, b2=b2,
        sink=a_sink[l], a_g=a_norm_g[l][None, :], bias=_bias_table(b_rel_bias[l]), b_g=b_norm_g[l][None, :],
        c_g=c_norm_g[l][None, :], w_out=w_out[l].astype(BF16), g2=norm2_g[l][None, :],
        wr_hi=wr_hi, wr_lo=wr_lo, wg=w_gate[l].astype(BF16), wu=w_up[l].astype(BF16), wd=w_down[l].astype(BF16))


def _layer(x, p, bsz, seq, final_g, final_norm):
    n = bsz * seq
    aq, ak, av, bq, bk, bv, cq, ck, cv, cg, la = _in_proj(x, p["g1"], p["w_in"], p["w2"], p["b2"], seq)
    oa = _attn_a(aq, ak, av, p["sink"], p["a_g"], bsz, seq)
    ob = _attn_b(bq, bk, bv, p["bias"], p["b_g"], bsz, seq)
    o_f, o_b = _gla(cq, ck, cv, la, bsz, seq)
    x1, h2, aff = _out_proj(x, oa, ob, o_f, o_b, cg, p["c_g"], p["w_out"], p["g2"], p["wr_hi"], p["wr_lo"])
    cap = EC_CAPACITY * n // N_EXPERTS
    pos, offs = _route(aff, cap)
    return _moe(offs, h2, x1, pos, aff, p["wg"], p["wu"], p["wd"], final_g, final_norm)


def _trunk(x, layers, final_g):
    bsz, seq, _ = x.shape
    y = x.reshape(bsz * seq, D_MODEL)
    for l, p in enumerate(layers):
        y = _layer(y, p, bsz, seq, final_g, final_norm=(l == len(layers) - 1))
    return y.reshape(bsz, seq, D_MODEL)


def kernel(x_prompt, x_sample, norm1_g, w_in, a_sink, a_norm_g, b_rel_bias, b_norm_g, c_alpha_w2_f, c_alpha_b_f,
           c_alpha_w2_b, c_alpha_b_b, c_norm_g, w_out, norm2_g, w_router, w_gate, w_up, w_down, final_g):
    depth = w_in.shape[0]
    layers = [_prep_layer(l, norm1_g, w_in, a_sink, a_norm_g, b_rel_bias, b_norm_g, c_alpha_w2_f, c_alpha_b_f,
                          c_alpha_w2_b, c_alpha_b_b, c_norm_g, w_out, norm2_g, w_router, w_gate, w_up, w_down)
              for l in range(depth)]
    fg = final_g[None, :]
    return _trunk(x_prompt, layers, fg), _trunk(x_sample, layers, fg)
```

```python
import functools

import jax
import jax.numpy as jnp
from jax import lax
from jax.experimental import pallas as pl
from jax.experimental.pallas import tpu as pltpu

F32 = jnp.float32
BF16 = jnp.bfloat16

D_MODEL = 1024
HEAD_DIM = 64
A_WIDTH = 512
A_HEADS = 8
A_KV_HEADS = 2
A_KV_WIDTH = 128
WINDOW = 128
ROPE_THETA = 500000.0
ROPE_DIM = 16
B_WIDTH = 256
B_HEADS = 4
GRID_W = 64
WIN_H = 8
WIN_W = 16
C_WIDTH = 256
C_HEADS = 4
C_LOWRANK = 16
C_TAU = 16.0
C_CHUNK = 64
N_EXPERTS = 16
EC_CAPACITY = 2
EPS = 1e-6
NEG_INF = -1e30
IN_WIDTH = 2592

_OFF_AQ, _OFF_AK, _OFF_AV = 0, 512, 640
_OFF_BQ, _OFF_BK, _OFF_BV = 768, 1024, 1280
_OFF_CQ, _OFF_CK, _OFF_CV, _OFF_CG = 1536, 1792, 2048, 2304
_OFF_LR = 2560

LANES = 128
VMEM_LIMIT = 56 * 1024 * 1024

IN_PROJ_ROWS = 512
GLA_BLOCK = 512
ROUTE_TILE = 256
MOE_TILE = 2048
MOE_GROUP = 4
MOE_WIN = 64
MOE_ALIGN = 16
MOE_SLOW = 128


def _cparams(sem):
    return pltpu.CompilerParams(dimension_semantics=sem, vmem_limit_bytes=VMEM_LIMIT)


def _nt_dot(a, b):
    return lax.dot_general(a, b, (((1,), (1,)), ((), ())), preferred_element_type=F32)


def _tn_dot(a, b):
    return lax.dot_general(a, b, (((0,), (0,)), ((), ())), preferred_element_type=F32)


def _dot(a, b):
    return jnp.dot(a, b, preferred_element_type=F32)


def _rms(x, g):
    return x * lax.rsqrt(jnp.mean(x * x, axis=-1, keepdims=True) + EPS) * g


def _same_head_mask(n_rows, rows_per_head):
    width = (n_rows // rows_per_head) * HEAD_DIM
    ri = lax.broadcasted_iota(jnp.int32, (n_rows, width), 0) // rows_per_head
    li = lax.broadcasted_iota(jnp.int32, (n_rows, width), 1) // HEAD_DIM
    return ri == li


def _split_bf16(x):
    hi = x.astype(BF16)
    return hi, (x - hi.astype(F32)).astype(BF16)


def _in_proj_kernel(x_ref, g_ref, w_ref, cos_ref, s1_ref, s2_ref, w2_ref, b2_ref,
                    aq_ref, ak_ref, av_ref, bq_ref, bk_ref, bv_ref,
                    cq_ref, ck_ref, cv_ref, cg_ref, la_ref):
    h = _rms(x_ref[...], g_ref[...]).astype(BF16)

    def proj(lo, width):
        return _dot(h, w_ref[:, lo:lo + width])

    cos, s1, s2 = cos_ref[...], s1_ref[...], s2_ref[...]

    def rope(t):
        return t * cos + pltpu.roll(t, LANES - ROPE_DIM // 2, 1) * s1 + pltpu.roll(t, ROPE_DIM // 2, 1) * s2

    scale = HEAD_DIM ** -0.5
    aq = proj(_OFF_AQ, A_WIDTH)
    for c in range(A_WIDTH // LANES):
        aq_ref[:, c * LANES:(c + 1) * LANES] = (rope(aq[:, c * LANES:(c + 1) * LANES]) * scale).astype(BF16)
    ak_ref[...] = rope(proj(_OFF_AK, A_KV_WIDTH)).astype(BF16)
    av_ref[...] = proj(_OFF_AV, A_KV_WIDTH).astype(BF16)
    bq_ref[...] = (proj(_OFF_BQ, B_WIDTH) * scale).astype(BF16)
    bk_ref[...] = proj(_OFF_BK, B_WIDTH).astype(BF16)
    bv_ref[...] = proj(_OFF_BV, B_WIDTH).astype(BF16)
    cq_ref[...] = proj(_OFF_CQ, C_WIDTH) * scale
    ck_ref[...] = proj(_OFF_CK, C_WIDTH)
    cv_ref[...] = proj(_OFF_CV, C_WIDTH).astype(BF16)
    cg_ref[...] = proj(_OFF_CG, C_WIDTH)
    lr = proj(_OFF_LR, 2 * C_LOWRANK)
    z = jnp.dot(lr, w2_ref[...], preferred_element_type=F32, precision=lax.Precision.HIGHEST) + b2_ref[...]
    la_ref[...] = (jnp.minimum(z, 0.0) - jnp.log(1.0 + jnp.exp(-jnp.abs(z)))) * (1.0 / C_TAU)


def _rope_tables(seq):
    half = ROPE_DIM // 2
    inv = jnp.power(jnp.float32(ROPE_THETA), -jnp.arange(half, dtype=F32) * (2.0 / ROPE_DIM))
    ang = jnp.arange(seq, dtype=F32)[:, None] * inv[None, :]
    cos, sin = jnp.cos(ang), jnp.sin(ang)
    ones = jnp.ones((seq, HEAD_DIM - ROPE_DIM), F32)
    zeros = jnp.zeros((seq, HEAD_DIM - half), F32)
    c = jnp.concatenate([cos, cos, ones], axis=1)
    s1 = jnp.concatenate([-sin, zeros], axis=1)
    s2 = jnp.concatenate([jnp.zeros((seq, half), F32), sin, jnp.zeros((seq, HEAD_DIM - ROPE_DIM), F32)], axis=1)
    two = lambda t: jnp.concatenate([t, t], axis=1)
    return two(c), two(s1), two(s2)


def _in_proj(x, g1, w_in_bf, w2, b2, seq):
    n = x.shape[0]
    tm = min(IN_PROJ_ROWS, seq)
    per_seq = seq // tm
    cos, s1, s2 = _rope_tables(seq)
    row = lambda width: pl.BlockSpec((tm, width), lambda i: (i, 0))
    full = lambda a: pl.BlockSpec(a.shape, lambda i: (0,) * a.ndim)
    tab = pl.BlockSpec((tm, LANES), lambda i: (i % per_seq, 0))
    widths = [(A_WIDTH, BF16), (A_KV_WIDTH, BF16), (A_KV_WIDTH, BF16), (B_WIDTH, BF16), (B_WIDTH, BF16),
              (B_WIDTH, BF16), (C_WIDTH, F32), (C_WIDTH, F32), (C_WIDTH, BF16), (C_WIDTH, F32), (2 * C_WIDTH, F32)]
    return pl.pallas_call(
        _in_proj_kernel,
        grid=(n // tm,),
        in_specs=[row(D_MODEL), full(g1), full(w_in_bf), tab, tab, tab, full(w2), full(b2)],
        out_specs=[row(w) for w, _ in widths],
        out_shape=[jax.ShapeDtypeStruct((n, w), dt) for w, dt in widths],
        compiler_params=_cparams(("parallel",)),
        name="in_proj",
    )(x, g1, w_in_bf, cos, s1, s2, w2, b2)


def _attn_a_kernel(sink_ref, q_ref, kp_ref, kc_ref, kn_ref, vp_ref, vc_ref, vn_ref, g_ref, o_ref, acc_ref, *, nb):
    i = pl.program_id(1)
    grp = A_HEADS // A_KV_HEADS
    k = jnp.concatenate([kp_ref[...], kc_ref[...], kn_ref[...]], axis=0)
    v = jnp.concatenate([vp_ref[...], vc_ref[...], vn_ref[...]], axis=0)
    qq = lax.broadcasted_iota(jnp.int32, (WINDOW, 3 * WINDOW), 0)
    kk = lax.broadcasted_iota(jnp.int32, (WINDOW, 3 * WINDOW), 1)
    off_seq = ((kk < WINDOW) & (i == 0)) | ((kk >= 2 * WINDOW) & (i == nb - 1))
    visible = (jnp.abs(kk - WINDOW - qq) <= WINDOW) & jnp.logical_not(off_seq)
    bias = jnp.where(visible, 0.0, NEG_INF)
    bias = jnp.concatenate([bias] * grp, axis=0)
    head_row = lax.broadcasted_iota(jnp.int32, (grp * WINDOW, 1), 0) // WINDOW
    for kv in range(A_KV_HEADS):
        heads = range(kv * grp, (kv + 1) * grp)
        q = jnp.concatenate([q_ref[:, h * HEAD_DIM:(h + 1) * HEAD_DIM] for h in heads], axis=0)
        s = _nt_dot(q, k[:, kv * HEAD_DIM:(kv + 1) * HEAD_DIM]) + bias
        sink = jnp.zeros((grp * WINDOW, 1), F32)
        for g_i, h in enumerate(heads):
            sink = jnp.where(head_row == g_i, sink_ref[h], sink)
        m = jnp.maximum(jnp.max(s, axis=-1, keepdims=True), sink)
        p = jnp.exp(s - m)
        den = jnp.sum(p, axis=-1, keepdims=True) + jnp.exp(sink - m)
        o = _dot(p.astype(BF16), v[:, kv * HEAD_DIM:(kv + 1) * HEAD_DIM]) / den
        for g_i, h in enumerate(heads):
            acc_ref[:, h * HEAD_DIM:(h + 1) * HEAD_DIM] = o[g_i * WINDOW:(g_i + 1) * WINDOW, :]
    o_ref[...] = _rms(acc_ref[...], g_ref[...]).astype(BF16)


def _attn_a(aq, ak, av, sink, g, bsz, seq):
    nb = seq // WINDOW
    qspec = pl.BlockSpec((WINDOW, A_WIDTH), lambda b, i: (b * nb + i, 0))
    kprev = pl.BlockSpec((WINDOW, A_KV_WIDTH), lambda b, i: (b * nb + jnp.maximum(i - 1, 0), 0))
    kcur = pl.BlockSpec((WINDOW, A_KV_WIDTH), lambda b, i: (b * nb + i, 0))
    knext = pl.BlockSpec((WINDOW, A_KV_WIDTH), lambda b, i: (b * nb + jnp.minimum(i + 1, nb - 1), 0))
    return pl.pallas_call(
        functools.partial(_attn_a_kernel, nb=nb),
        grid=(bsz, nb),
        in_specs=[pl.BlockSpec(memory_space=pltpu.SMEM), qspec, kprev, kcur, knext, kprev, kcur, knext,
                  pl.BlockSpec((1, A_WIDTH), lambda b, i: (0, 0))],
        out_specs=qspec,
        out_shape=jax.ShapeDtypeStruct((bsz * seq, A_WIDTH), BF16),
        scratch_shapes=[pltpu.VMEM((WINDOW, A_WIDTH), F32)],
        compiler_params=_cparams(("parallel", "parallel")),
        name="attn_a",
    )(sink, aq, ak, ak, ak, av, av, av, g)


B_GROUP = 8
B_KEYS = WIN_H * GRID_W


def _bias_table_kernel(rel_ref, o_ref):
    c = lax.broadcasted_iota(jnp.int32, (GRID_W, GRID_W), 0)
    w = lax.broadcasted_iota(jnp.int32, (GRID_W, GRID_W), 1)
    cstart = jnp.clip(c - WIN_W // 2, 0, GRID_W - WIN_W)
    colmask = (w >= cstart) & (w < cstart + WIN_W)
    col_off = jnp.clip(w - c + (WIN_W - 1), 0, 2 * WIN_W - 2)
    n_ro, n_co = 2 * WIN_H - 1, 2 * WIN_W - 1
    for h in range(B_HEADS):
        for ro in range(n_ro):
            def body(j, acc):
                return jnp.where(col_off == j, rel_ref[(h * n_ro + ro) * n_co + j], acc)
            t = lax.fori_loop(0, n_co, body, jnp.zeros((GRID_W, GRID_W), F32))
            t = jnp.where(colmask, t, NEG_INF)
            for p in range(WIN_H):
                kidx = ro - (WIN_H - 1) + p
                if 0 <= kidx < WIN_H:
                    o_ref[p, h, :, kidx * GRID_W:(kidx + 1) * GRID_W] = t


def _bias_table(rel_bias):
    return pl.pallas_call(
        _bias_table_kernel,
        in_specs=[pl.BlockSpec(memory_space=pltpu.SMEM)],
        out_shape=jax.ShapeDtypeStruct((WIN_H, B_HEADS, GRID_W, B_KEYS), F32),
        name="bias_table",
    )(rel_bias.reshape(-1)).reshape(WIN_H, B_HEADS * GRID_W, B_KEYS)


def _attn_b_kernel(q_ref, kp_ref, kc_ref, kn_ref, vp_ref, vc_ref, vn_ref, bias_ref, g_ref, o_ref,
                   kbuf, vbuf, acc_ref, *, rows):
    gidx = pl.program_id(1)
    blk = B_GROUP * GRID_W
    kbuf[0:blk, :] = kp_ref[...]
    kbuf[blk:2 * blk, :] = kc_ref[...]
    kbuf[2 * blk:3 * blk, :] = kn_ref[...]
    vbuf[0:blk, :] = vp_ref[...]
    vbuf[blk:2 * blk, :] = vc_ref[...]
    vbuf[2 * blk:3 * blk, :] = vn_ref[...]
    same_head = _same_head_mask(B_HEADS * GRID_W, GRID_W)
    for j in range(B_GROUP):
        r = gidx * B_GROUP + j
        start = jnp.clip(r - WIN_H // 2, 0, rows - WIN_H)
        pat = r - start
        loc = pl.multiple_of((start - gidx * B_GROUP + B_GROUP) * GRID_W, GRID_W)
        kw = kbuf[pl.ds(loc, B_KEYS), :]
        vw = vbuf[pl.ds(loc, B_KEYS), :]
        qj = q_ref[j * GRID_W:(j + 1) * GRID_W, :]
        q = jnp.where(same_head, jnp.concatenate([qj] * B_HEADS, axis=0), jnp.zeros((), BF16))
        s = _nt_dot(q, kw) + bias_ref[pat]
        m = jnp.max(s, axis=-1, keepdims=True)
        p = jnp.exp(s - m)
        den = jnp.sum(p, axis=-1, keepdims=True)
        o = jnp.where(same_head, _dot(p.astype(BF16), vw) / den, 0.0)
        acc_ref[j * GRID_W:(j + 1) * GRID_W, :] = sum(o[h * GRID_W:(h + 1) * GRID_W, :] for h in range(B_HEADS))
    o_ref[...] = _rms(acc_ref[...], g_ref[...]).astype(BF16)


def _attn_b(bq, bk, bv, bias, g, bsz, seq):
    rows = seq // GRID_W
    assert rows % B_GROUP == 0 and rows >= WIN_H
    ng = rows // B_GROUP
    blk = B_GROUP * GRID_W
    cur = pl.BlockSpec((blk, B_WIDTH), lambda b, i: (b * ng + i, 0))
    prev = pl.BlockSpec((blk, B_WIDTH), lambda b, i: (b * ng + jnp.maximum(i - 1, 0), 0))
    nxt = pl.BlockSpec((blk, B_WIDTH), lambda b, i: (b * ng + jnp.minimum(i + 1, ng - 1), 0))
    return pl.pallas_call(
        functools.partial(_attn_b_kernel, rows=rows),
        grid=(bsz, ng),
        in_specs=[cur, prev, cur, nxt, prev, cur, nxt,
                  pl.BlockSpec(bias.shape, lambda b, i: (0, 0, 0)),
                  pl.BlockSpec((1, B_WIDTH), lambda b, i: (0, 0))],
        out_specs=cur,
        out_shape=jax.ShapeDtypeStruct((bsz * seq, B_WIDTH), BF16),
        scratch_shapes=[pltpu.VMEM((3 * blk, B_WIDTH), BF16), pltpu.VMEM((3 * blk, B_WIDTH), BF16),
                        pltpu.VMEM((blk, B_WIDTH), F32)],
        compiler_params=_cparams(("parallel", "parallel")),
        name="attn_b",
    )(bq, bk, bk, bk, bv, bv, bv, bias, g)


def _gla_dir(q_ref, k_ref, v_ref, la_ref, o_ref, st_ref, reverse):
    tb = q_ref.shape[0]
    n_chunks = tb // C_CHUNK
    bi = lax.broadcasted_iota(jnp.int32, (tb, tb), 0)
    bj = lax.broadcasted_iota(jnp.int32, (tb, tb), 1)
    in_chunk = (bi // C_CHUNK) == (bj // C_CHUNK)
    cum_w = (in_chunk & ((bj >= bi) if reverse else (bj <= bi))).astype(BF16)
    la_hi, la_lo = _split_bf16(la_ref[...])
    b_all = _dot(cum_w, la_hi) + _dot(cum_w, la_lo)
    same_head = _same_head_mask(C_HEADS * C_CHUNK, C_CHUNK)
    ti = lax.broadcasted_iota(jnp.int32, (C_CHUNK, C_HEADS * C_CHUNK), 0)
    si = lax.broadcasted_iota(jnp.int32, (C_CHUNK, C_HEADS * C_CHUNK), 1) % C_CHUNK
    tri = (si >= ti) if reverse else (si <= ti)
    zero = jnp.zeros((), BF16)
    order = range(n_chunks - 1, -1, -1) if reverse else range(n_chunks)
    for c in order:
        rows = slice(c * C_CHUNK, (c + 1) * C_CHUNK)
        b = b_all[rows, :]
        b_tot = b[0:1, :] if reverse else b[C_CHUNK - 1:C_CHUNK, :]
        q_i = (q_ref[rows, :] * jnp.exp(b)).astype(BF16)
        kf = k_ref[rows, :]
        k_i = (kf * jnp.exp(-b)).astype(BF16)
        k_e = (kf * jnp.exp(b_tot - b)).astype(BF16)
        decay = jnp.exp(b_tot)
        v = v_ref[rows, :]
        k_bd = jnp.where(same_head, jnp.concatenate([k_i] * C_HEADS, axis=0), zero)
        v_bd = jnp.where(same_head, jnp.concatenate([v] * C_HEADS, axis=0), zero)
        a = jnp.where(tri, _nt_dot(q_i, k_bd), 0.0)
        st = st_ref[...]
        o_ref[rows, :] = _dot(a.astype(BF16), v_bd) + _nt_dot(q_i, st.astype(BF16))
        st_ref[...] = st * decay + jnp.where(same_head, _tn_dot(v, k_e), 0.0)


def _gla_kernel(qf_ref, kf_ref, vf_ref, laf_ref, qb_ref, kb_ref, vb_ref, lab_ref, of_ref, ob_ref, sf_ref, sb_ref):
    @pl.when(pl.program_id(1) == 0)
    def _():
        sf_ref[...] = jnp.zeros_like(sf_ref)
        sb_ref[...] = jnp.zeros_like(sb_ref)

    _gla_dir(qf_ref, kf_ref, vf_ref, laf_ref, of_ref, sf_ref, reverse=False)
    _gla_dir(qb_ref, kb_ref, vb_ref, lab_ref, ob_ref, sb_ref, reverse=True)


def _gla(cq, ck, cv, la, bsz, seq):
    tb = min(GLA_BLOCK, seq)
    nb = seq // tb
    fwd = lambda col: pl.BlockSpec((tb, C_WIDTH), lambda b, i: (b * nb + i, col))
    bwd = lambda col: pl.BlockSpec((tb, C_WIDTH), lambda b, i: (b * nb + nb - 1 - i, col))
    out = jax.ShapeDtypeStruct((bsz * seq, C_WIDTH), F32)
    state = pltpu.VMEM((C_WIDTH, C_WIDTH), F32)
    return pl.pallas_call(
        _gla_kernel,
        grid=(bsz, nb),
        in_specs=[fwd(0), fwd(0), fwd(0), fwd(0), bwd(0), bwd(0), bwd(0), bwd(1)],
        out_specs=[fwd(0), bwd(0)],
        out_shape=[out, out],
        scratch_shapes=[state, state],
        compiler_params=_cparams(("parallel", "arbitrary")),
        name="gla",
    )(cq, ck, cv, la, cq, ck, cv, la)


def _out_proj_kernel(x_ref, oa_ref, ob_ref, of_ref, obw_ref, cg_ref, cng_ref, wo_ref, g2_ref, wr_hi_ref, wr_lo_ref,
                     x1_ref, h2_ref, aff_ref):
    o = of_ref[...] + obw_ref[...]
    gi = lax.broadcasted_iota(jnp.int32, (C_WIDTH, C_WIDTH), 0) // HEAD_DIM
    gj = lax.broadcasted_iota(jnp.int32, (C_WIDTH, C_WIDTH), 1) // HEAD_DIM
    ones_bd = (gi == gj).astype(BF16)
    sq_hi, sq_lo = _split_bf16(o * o)
    ms = (_dot(sq_hi, ones_bd) + _dot(sq_lo, ones_bd)) * (1.0 / HEAD_DIM)
    cg = cg_ref[...]
    oc = (o * lax.rsqrt(ms + EPS) * cng_ref[...]) * (cg / (1.0 + jnp.exp(-cg)))
    y = _dot(oa_ref[...], wo_ref[0:A_WIDTH, :])
    y += _dot(ob_ref[...], wo_ref[A_WIDTH:A_WIDTH + B_WIDTH, :])
    y += _dot(oc.astype(BF16), wo_ref[A_WIDTH + B_WIDTH:, :])
    x1 = x_ref[...] + y
    x1_ref[...] = x1
    h2 = _rms(x1, g2_ref[...])
    h2_ref[...] = h2.astype(BF16)
    h_hi, h_lo = _split_bf16(h2)
    logits = _nt_dot(wr_hi_ref[...], h_hi) + _nt_dot(wr_hi_ref[...], h_lo) + _nt_dot(wr_lo_ref[...], h_hi)
    m = jnp.max(logits, axis=0, keepdims=True)
    p = jnp.exp(logits - m)
    aff_ref[...] = p / jnp.sum(p, axis=0, keepdims=True)


def _out_proj(x, oa, ob, o_f, o_b, cg, cng, wo_bf, g2, wr_hi, wr_lo):
    n = x.shape[0]
    tm = min(IN_PROJ_ROWS, n)
    row = lambda width: pl.BlockSpec((tm, width), lambda i: (i, 0))
    full = lambda a: pl.BlockSpec(a.shape, lambda i: (0,) * a.ndim)
    return pl.pallas_call(
        _out_proj_kernel,
        grid=(n // tm,),
        in_specs=[row(D_MODEL), row(A_WIDTH), row(B_WIDTH), row(C_WIDTH), row(C_WIDTH), row(C_WIDTH),
                  full(cng), full(wo_bf), full(g2), full(wr_hi), full(wr_lo)],
        out_specs=[row(D_MODEL), row(D_MODEL), pl.BlockSpec((N_EXPERTS, tm), lambda i: (0, i))],
        out_shape=[jax.ShapeDtypeStruct((n, D_MODEL), F32), jax.ShapeDtypeStruct((n, D_MODEL), BF16),
                   jax.ShapeDtypeStruct((N_EXPERTS, n), F32)],
        compiler_params=_cparams(("parallel",)),
        name="out_proj",
    )(x, oa, ob, o_f, o_b, cg, cng, wo_bf, g2, wr_hi, wr_lo)


def _route_kernel(aff_ref, pos_ref, offs_ref, *, cap):
    n = aff_ref.shape[1]
    nt = n // ROUTE_TILE
    bits = pltpu.bitcast(aff_ref[...], jnp.int32)

    def search(it, thr):
        cand = thr | jnp.left_shift(jnp.int32(1), 30 - it)
        cnt = jnp.sum(jnp.where(bits >= cand, 1.0, 0.0), axis=1, keepdims=True)
        return jnp.where(cnt >= cap, cand, thr)

    thr = lax.fori_loop(0, 31, search, jnp.zeros((N_EXPERTS, 1), jnp.int32))
    n_gt = jnp.sum(jnp.where(bits > thr, 1.0, 0.0), axis=1, keepdims=True)
    n_tie = cap - n_gt

    ui = lax.broadcasted_iota(jnp.int32, (ROUTE_TILE, ROUTE_TILE), 0)
    uj = lax.broadcasted_iota(jnp.int32, (ROUTE_TILE, ROUTE_TILE), 1)
    upper = (ui <= uj).astype(BF16)
    lane = lax.broadcasted_iota(jnp.int32, offs_ref.shape, 1)

    def tile(i, carry):
        c_gt, c_eq, offs = carry
        col = pl.multiple_of(i * ROUTE_TILE, ROUTE_TILE)
        b = pltpu.bitcast(aff_ref[:, pl.ds(col, ROUTE_TILE)], jnp.int32)
        gt = jnp.where(b > thr, 1.0, 0.0)
        eq = jnp.where(b == thr, 1.0, 0.0)
        inc = _dot(jnp.concatenate([gt, eq], axis=0).astype(BF16), upper)
        gt_before = c_gt + inc[:N_EXPERTS] - gt
        eq_before = c_eq + inc[N_EXPERTS:] - eq
        sel = (gt > 0.0) | ((eq > 0.0) & (eq_before < n_tie))
        pos = gt_before + jnp.minimum(eq_before, n_tie)
        pos_ref[:, pl.ds(col, ROUTE_TILE)] = jnp.where(sel, pos, -1.0).astype(jnp.int32)
        start = c_gt + jnp.minimum(c_eq, n_tie)
        offs = jnp.where(lane == i, start.astype(jnp.int32), offs)
        return (c_gt + jnp.sum(gt, axis=1, keepdims=True), c_eq + jnp.sum(eq, axis=1, keepdims=True), offs)

    zero = jnp.zeros((N_EXPERTS, 1), F32)
    offs = jnp.where(lane == nt, cap, 0).astype(jnp.int32)
    _, _, offs = lax.fori_loop(0, nt, tile, (zero, zero, offs))
    offs_ref[...] = offs


def _route(aff_t, cap):
    n = aff_t.shape[1]
    nt = n // ROUTE_TILE
    return pl.pallas_call(
        functools.partial(_route_kernel, cap=cap),
        out_shape=[jax.ShapeDtypeStruct((N_EXPERTS, n), jnp.int32),
                   jax.ShapeDtypeStruct((N_EXPERTS, nt + 1), jnp.int32)],
        compiler_params=pltpu.CompilerParams(vmem_limit_bytes=VMEM_LIMIT),
        name="route",
    )(aff_t)


def _moe_kernel(offs_ref, h_ref, x1_ref, pos_ref, gate_ref, wg_ref, wu_ref, wd_ref, fg_ref, o_ref,
                x_ref, y_ref, gs_ref, *, final_norm, rows):
    t = pl.program_id(0)
    e = pl.program_id(1)
    tt = h_ref.shape[0]
    subs = tt // ROUTE_TILE
    g_loc = e % MOE_GROUP

    def align(v):
        return (v // MOE_ALIGN) * MOE_ALIGN

    def off(x, s):
        return offs_ref[x, t * subs + s]

    @pl.when(e == 0)
    def _():
        o_ref[...] = x1_ref[...]

    slot = lax.broadcasted_iota(jnp.int32, (MOE_WIN, ROUTE_TILE), 0)
    row = lax.broadcasted_iota(jnp.int32, (MOE_WIN, 1), 0)

    def passes(e0, s):
        k = jnp.int32(1)
        for g in range(MOE_GROUP):
            need = off(e0 + g, s + 1) - align(off(e0 + g, s))
            k = jnp.maximum(k, (need + MOE_WIN - 1) // MOE_WIN)
        return k

    def windows(e0, s, j):
        out = []
        for g in range(MOE_GROUP):
            x = e0 + g
            base = align(off(x, 0))
            ws = align(off(x, s)) + j * MOE_WIN
            pos = pos_ref[pl.ds(x, 1), s * ROUTE_TILE:(s + 1) * ROUTE_TILE]
            hit = ((pos - ws) == slot) & (pos < base + rows)
            rel = pl.multiple_of(jnp.minimum(ws - base, rows), MOE_ALIGN)
            out.append((hit, rel, ws))
        return out

    def onehot(wins):
        return jnp.concatenate([jnp.where(hit, 1.0, 0.0) for hit, _, _ in wins], axis=0).astype(BF16)

    @pl.when(g_loc == 0)
    def _gather():
        x_ref[:, rows:, :] = jnp.zeros((MOE_GROUP, MOE_WIN, D_MODEL), BF16)
        y_ref[:, rows:, :] = jnp.zeros((MOE_GROUP, MOE_WIN, D_MODEL), BF16)
        gs_ref[:, rows:, :] = jnp.zeros((MOE_GROUP, MOE_WIN, 1), F32)
        for s in range(subs):
            cols = slice(s * ROUTE_TILE, (s + 1) * ROUTE_TILE)

            def gather_pass(j, carry):
                wins = windows(e, s, j)
                r = _dot(onehot(wins), h_ref[cols, :])
                for g, (hit, rel, ws) in enumerate(wins):
                    gate = gate_ref[pl.ds(e + g, 1), cols]
                    gsum = jnp.sum(jnp.where(hit, gate, 0.0), axis=1, keepdims=True)
                    fresh = row >= ((off(e + g, s) - ws) if s > 0 else 0)
                    win = pl.ds(rel, MOE_WIN)
                    x_ref[g, win, :] = jnp.where(fresh, r[g * MOE_WIN:(g + 1) * MOE_WIN, :].astype(BF16), x_ref[g, win, :])
                    gs_ref[g, win, :] = jnp.where(fresh, gsum, gs_ref[g, win, :])
                return carry

            lax.fori_loop(0, passes(e, s), gather_pass, 0)
        k_last = passes(e, subs - 1)
        for g in range(MOE_GROUP):
            end = align(off(e + g, subs - 1)) + k_last * MOE_WIN - align(off(e + g, 0))

            def fill(i, carry):
                r0 = pl.multiple_of(i * MOE_ALIGN, MOE_ALIGN)
                x_ref[g, pl.ds(r0, MOE_ALIGN), :] = jnp.zeros((MOE_ALIGN, D_MODEL), BF16)
                gs_ref[g, pl.ds(r0, MOE_ALIGN), :] = jnp.zeros((MOE_ALIGN, 1), F32)
                return carry

            lax.fori_loop(jnp.minimum(end, rows) // MOE_ALIGN, rows // MOE_ALIGN, fill, 0)

    def ffn(xg, gate_col):
        hg = _dot(xg, wg_ref[0])
        hu = _dot(xg, wu_ref[0])
        act = (hg / (1.0 + jnp.exp(-hg))) * hu * gate_col
        return _dot(act.astype(BF16), wd_ref[0]).astype(BF16)

    y_ref[g_loc, 0:rows, :] = ffn(x_ref[g_loc, 0:rows, :], gs_ref[g_loc, 0:rows, :])

    base_e = align(off(e, 0))
    last_e = off(e, subs)
    n_slow = jnp.where(last_e > base_e + rows, (last_e - base_e - rows + MOE_SLOW - 1) // MOE_SLOW, 0)

    def slow_block(b, carry):
        lo = base_e + rows + b * MOE_SLOW
        hit = (pos_ref[pl.ds(e, 1), :] - lo) == lax.broadcasted_iota(jnp.int32, (MOE_SLOW, tt), 0)
        oh = jnp.where(hit, 1.0, 0.0).astype(BF16)
        gsum = jnp.sum(jnp.where(hit, gate_ref[pl.ds(e, 1), :], 0.0), axis=1, keepdims=True)
        y = ffn(_dot(oh, h_ref[...]).astype(BF16), gsum)
        o_ref[...] += _tn_dot(oh, y)
        return carry

    lax.fori_loop(0, n_slow, slow_block, 0)

    @pl.when(g_loc == MOE_GROUP - 1)
    def _combine():
        e0 = e - (MOE_GROUP - 1)
        for s in range(subs):
            cols = slice(s * ROUTE_TILE, (s + 1) * ROUTE_TILE)

            def combine_pass(j, carry):
                wins = windows(e0, s, j)
                ycat = jnp.concatenate([y_ref[g, pl.ds(rel, MOE_WIN), :] for g, (_, rel, _) in enumerate(wins)], axis=0)
                o_ref[cols, :] += _tn_dot(onehot(wins), ycat)
                return carry

            lax.fori_loop(0, passes(e0, s), combine_pass, 0)

    if final_norm:
        @pl.when(e == N_EXPERTS - 1)
        def _():
            o_ref[...] = _rms(o_ref[...], fg_ref[...])


def _moe(offs, h2, x1, pos, gate, wg, wu, wd, final_g, final_norm):
    n = h2.shape[0]
    tt = min(MOE_TILE, n)
    rows = (tt * EC_CAPACITY // N_EXPERTS) * 5 // 4 // MOE_ALIGN * MOE_ALIGN
    tok = pl.BlockSpec((tt, D_MODEL), lambda t, e, offs: (t, 0))
    tok_once = pl.BlockSpec((tt, D_MODEL), lambda t, e, offs: (t, 0), pipeline_mode=pl.Buffered(1))
    per_tok = pl.BlockSpec((N_EXPERTS, tt), lambda t, e, offs: (0, t))
    wspec = pl.BlockSpec((1, D_MODEL, D_MODEL), lambda t, e, offs: (e, 0, 0))
    buf = lambda width, dt: pltpu.VMEM((MOE_GROUP, rows + MOE_WIN, width), dt)
    return pl.pallas_call(
        functools.partial(_moe_kernel, final_norm=final_norm, rows=rows),
        grid_spec=pltpu.PrefetchScalarGridSpec(
            num_scalar_prefetch=1,
            grid=(n // tt, N_EXPERTS),
            in_specs=[tok_once, tok_once, per_tok, per_tok, wspec, wspec, wspec,
                      pl.BlockSpec((1, D_MODEL), lambda t, e, offs: (0, 0))],
            out_specs=tok,
            scratch_shapes=[buf(D_MODEL, BF16), buf(D_MODEL, BF16), buf(1, F32)],
        ),
        out_shape=jax.ShapeDtypeStruct((n, D_MODEL), F32),
        compiler_params=_cparams(("parallel", "arbitrary")),
        name="moe",
    )(offs, h2, x1, pos, gate, wg, wu, wd, final_g)


def _prep_layer(l, norm1_g, w_in, a_sink, a_norm_g, b_rel_bias, b_norm_g, c_alpha_w2_f, c_alpha_b_f,
                c_alpha_w2_b, c_alpha_b_b, c_norm_g, w_out, norm2_g, w_router, w_gate, w_up, w_down):
    z = jnp.zeros((C_LOWRANK, C_WIDTH), F32)
    w2 = jnp.concatenate([jnp.concatenate([c_alpha_w2_f[l], z], axis=1),
                          jnp.concatenate([z, c_alpha_w2_b[l]], axis=1)], axis=0)
    b2 = jnp.concatenate([c_alpha_b_f[l], c_alpha_b_b[l]])[None, :]
    wr_t = w_router[l].T
    wr_hi = wr_t.astype(BF16)
    wr_lo = (wr_t - wr_hi.astype(F32)).astype(BF16)
    return dict(
        g1=norm1_g[l][None, :], w_in=w_in[l].astype(BF16), w2=w2, b2=b2,
        sink=a_sink[l], a_g=a_norm_g[l][None, :], bias=_bias_table(b_rel_bias[l]), b_g=b_norm_g[l][None, :],
        c_g=c_norm_g[l][None, :], w_out=w_out[l].astype(BF16), g2=norm2_g[l][None, :],
        wr_hi=wr_hi, wr_lo=wr_lo, wg=w_gate[l].astype(BF16), wu=w_up[l].astype(BF16), wd=w_down[l].astype(BF16))


def _layer(x, p, bsz, seq, final_g, final_norm):
    n = bsz * seq
    aq, ak, av, bq, bk, bv, cq, ck, cv, cg, la = _in_proj(x, p["g1"], p["w_in"], p["w2"], p["b2"], seq)
    oa = _attn_a(aq, ak, av, p["sink"], p["a_g"], bsz, seq)
    ob = _attn_b(bq, bk, bv, p["bias"], p["b_g"], bsz, seq)
    o_f, o_b = _gla(cq, ck, cv, la, bsz, seq)
    x1, h2, aff = _out_proj(x, oa, ob, o_f, o_b, cg, p["c_g"], p["w_out"], p["g2"], p["wr_hi"], p["wr_lo"])
    cap = EC_CAPACITY * n // N_EXPERTS
    pos, offs = _route(aff, cap)
    return _moe(offs, h2, x1, pos, aff, p["wg"], p["wu"], p["wd"], final_g, final_norm)


def _trunk(x, layers, final_g):
    bsz, seq, _ = x.shape
    y = x.reshape(bsz * seq, D_MODEL)
    for l, p in enumerate(layers):
        y = _layer(y, p, bsz, seq, final_g, final_norm=(l == len(layers) - 1))
    return y.reshape(bsz, seq, D_MODEL)


def kernel(x_prompt, x_sample, norm1_g, w_in, a_sink, a_norm_g, b_rel_bias, b_norm_g, c_alpha_w2_f, c_alpha_b_f,
           c_alpha_w2_b, c_alpha_b_b, c_norm_g, w_out, norm2_g, w_router, w_gate, w_up, w_down, final_g):
    depth = w_in.shape[0]
    layers = [_prep_layer(l, norm1_g, w_in, a_sink, a_norm_g, b_rel_bias, b_norm_g, c_alpha_w2_f, c_alpha_b_f,
                          c_alpha_w2_b, c_alpha_b_b, c_norm_g, w_out, norm2_g, w_router, w_gate, w_up, w_down)
              for l in range(depth)]
    fg = final_g[None, :]
    return _trunk(x_prompt, layers, fg), _trunk(x_sample, layers, fg)
```

```python
import functools

import jax
import jax.numpy as jnp
from jax import lax
from jax.experimental import pallas as pl
from jax.experimental.pallas import tpu as pltpu

F32 = jnp.float32
BF16 = jnp.bfloat16

D_MODEL = 1024
HEAD_DIM = 64
A_WIDTH = 512
A_HEADS = 8
A_KV_HEADS = 2
A_KV_WIDTH = 128
WINDOW = 128
ROPE_THETA = 500000.0
ROPE_DIM = 16
B_WIDTH = 256
B_HEADS = 4
GRID_W = 64
WIN_H = 8
WIN_W = 16
C_WIDTH = 256
C_HEADS = 4
C_LOWRANK = 16
C_TAU = 16.0
C_CHUNK = 64
N_EXPERTS = 16
EC_CAPACITY = 2
EPS = 1e-6
NEG_INF = -1e30
IN_WIDTH = 2592

_OFF_AQ, _OFF_AK, _OFF_AV = 0, 512, 640
_OFF_BQ, _OFF_BK, _OFF_BV = 768, 1024, 1280
_OFF_CQ, _OFF_CK, _OFF_CV, _OFF_CG = 1536, 1792, 2048, 2304
_OFF_LR = 2560

LANES = 128
VMEM_LIMIT = 56 * 1024 * 1024

IN_PROJ_ROWS = 512
GLA_BLOCK = 512
ROUTE_TILE = 256
MOE_TILE = 2048
MOE_STEP = 4
MOE_PAIR = 4
MOE_WIN = 128
MOE_ALIGN = 16
MOE_CHUNK = 128
MOE_FFN_ROWS = 512
XE_WIDTH = D_MODEL + LANES


def _cparams(sem):
    return pltpu.CompilerParams(dimension_semantics=sem, vmem_limit_bytes=VMEM_LIMIT)


def _nt_dot(a, b):
    return lax.dot_general(a, b, (((1,), (1,)), ((), ())), preferred_element_type=F32)


def _tn_dot(a, b):
    return lax.dot_general(a, b, (((0,), (0,)), ((), ())), preferred_element_type=F32)


def _dot(a, b):
    return jnp.dot(a, b, preferred_element_type=F32)


def _rms(x, g):
    return x * lax.rsqrt(jnp.mean(x * x, axis=-1, keepdims=True) + EPS) * g


def _same_head_mask(n_rows, rows_per_head):
    width = (n_rows // rows_per_head) * HEAD_DIM
    ri = lax.broadcasted_iota(jnp.int32, (n_rows, width), 0) // rows_per_head
    li = lax.broadcasted_iota(jnp.int32, (n_rows, width), 1) // HEAD_DIM
    return ri == li


def _split_bf16(x):
    hi = x.astype(BF16)
    return hi, (x - hi.astype(F32)).astype(BF16)


def _in_proj_kernel(x_ref, g_ref, w_ref, cos_ref, s1_ref, s2_ref, w2_ref, b2_ref,
                    aq_ref, ak_ref, av_ref, bq_ref, bk_ref, bv_ref,
                    cq_ref, ck_ref, cv_ref, cg_ref, la_ref):
    h = _rms(x_ref[...], g_ref[...]).astype(BF16)

    def proj(lo, width):
        return _dot(h, w_ref[:, lo:lo + width])

    cos, s1, s2 = cos_ref[...], s1_ref[...], s2_ref[...]

    def rope(t):
        return t * cos + pltpu.roll(t, LANES - ROPE_DIM // 2, 1) * s1 + pltpu.roll(t, ROPE_DIM // 2, 1) * s2

    scale = HEAD_DIM ** -0.5
    aq = proj(_OFF_AQ, A_WIDTH)
    for c in range(A_WIDTH // LANES):
        aq_ref[:, c * LANES:(c + 1) * LANES] = (rope(aq[:, c * LANES:(c + 1) * LANES]) * scale).astype(BF16)
    ak_ref[...] = rope(proj(_OFF_AK, A_KV_WIDTH)).astype(BF16)
    av_ref[...] = proj(_OFF_AV, A_KV_WIDTH).astype(BF16)
    bq_ref[...] = (proj(_OFF_BQ, B_WIDTH) * scale).astype(BF16)
    bk_ref[...] = proj(_OFF_BK, B_WIDTH).astype(BF16)
    bv_ref[...] = proj(_OFF_BV, B_WIDTH).astype(BF16)
    cq_ref[...] = proj(_OFF_CQ, C_WIDTH) * scale
    ck_ref[...] = proj(_OFF_CK, C_WIDTH)
    cv_ref[...] = proj(_OFF_CV, C_WIDTH).astype(BF16)
    cg_ref[...] = proj(_OFF_CG, C_WIDTH)
    lr = proj(_OFF_LR, 2 * C_LOWRANK)
    z = jnp.dot(lr, w2_ref[...], preferred_element_type=F32, precision=lax.Precision.HIGHEST) + b2_ref[...]
    la_ref[...] = (jnp.minimum(z, 0.0) - jnp.log(1.0 + jnp.exp(-jnp.abs(z)))) * (1.0 / C_TAU)


def _rope_tables(seq):
    half = ROPE_DIM // 2
    inv = jnp.power(jnp.float32(ROPE_THETA), -jnp.arange(half, dtype=F32) * (2.0 / ROPE_DIM))
    ang = jnp.arange(seq, dtype=F32)[:, None] * inv[None, :]
    cos, sin = jnp.cos(ang), jnp.sin(ang)
    ones = jnp.ones((seq, HEAD_DIM - ROPE_DIM), F32)
    zeros = jnp.zeros((seq, HEAD_DIM - half), F32)
    c = jnp.concatenate([cos, cos, ones], axis=1)
    s1 = jnp.concatenate([-sin, zeros], axis=1)
    s2 = jnp.concatenate([jnp.zeros((seq, half), F32), sin, jnp.zeros((seq, HEAD_DIM - ROPE_DIM), F32)], axis=1)
    two = lambda t: jnp.concatenate([t, t], axis=1)
    return two(c), two(s1), two(s2)


def _in_proj(x, g1, w_in_bf, w2, b2, seq):
    n = x.shape[0]
    tm = min(IN_PROJ_ROWS, seq)
    per_seq = seq // tm
    cos, s1, s2 = _rope_tables(seq)
    row = lambda width: pl.BlockSpec((tm, width), lambda i: (i, 0))
    full = lambda a: pl.BlockSpec(a.shape, lambda i: (0,) * a.ndim)
    tab = pl.BlockSpec((tm, LANES), lambda i: (i % per_seq, 0))
    widths = [(A_WIDTH, BF16), (A_KV_WIDTH, BF16), (A_KV_WIDTH, BF16), (B_WIDTH, BF16), (B_WIDTH, BF16),
              (B_WIDTH, BF16), (C_WIDTH, F32), (C_WIDTH, F32), (C_WIDTH, BF16), (C_WIDTH, F32), (2 * C_WIDTH, F32)]
    return pl.pallas_call(
        _in_proj_kernel,
        grid=(n // tm,),
        in_specs=[row(D_MODEL), full(g1), full(w_in_bf), tab, tab, tab, full(w2), full(b2)],
        out_specs=[row(w) for w, _ in widths],
        out_shape=[jax.ShapeDtypeStruct((n, w), dt) for w, dt in widths],
        compiler_params=_cparams(("parallel",)),
        name="in_proj",
    )(x, g1, w_in_bf, cos, s1, s2, w2, b2)


def _attn_a_kernel(sink_ref, q_ref, kp_ref, kc_ref, kn_ref, vp_ref, vc_ref, vn_ref, g_ref, o_ref, acc_ref, *, nb):
    i = pl.program_id(1)
    grp = A_HEADS // A_KV_HEADS
    k = jnp.concatenate([kp_ref[...], kc_ref[...], kn_ref[...]], axis=0)
    v = jnp.concatenate([vp_ref[...], vc_ref[...], vn_ref[...]], axis=0)
    qq = lax.broadcasted_iota(jnp.int32, (WINDOW, 3 * WINDOW), 0)
    kk = lax.broadcasted_iota(jnp.int32, (WINDOW, 3 * WINDOW), 1)
    off_seq = ((kk < WINDOW) & (i == 0)) | ((kk >= 2 * WINDOW) & (i == nb - 1))
    visible = (jnp.abs(kk - WINDOW - qq) <= WINDOW) & jnp.logical_not(off_seq)
    bias = jnp.where(visible, 0.0, NEG_INF)
    bias = jnp.concatenate([bias] * grp, axis=0)
    head_row = lax.broadcasted_iota(jnp.int32, (grp * WINDOW, 1), 0) // WINDOW
    for kv in range(A_KV_HEADS):
        heads = range(kv * grp, (kv + 1) * grp)
        q = jnp.concatenate([q_ref[:, h * HEAD_DIM:(h + 1) * HEAD_DIM] for h in heads], axis=0)
        s = _nt_dot(q, k[:, kv * HEAD_DIM:(kv + 1) * HEAD_DIM]) + bias
        sink = jnp.zeros((grp * WINDOW, 1), F32)
        for g_i, h in enumerate(heads):
            sink = jnp.where(head_row == g_i, sink_ref[h], sink)
        m = jnp.maximum(jnp.max(s, axis=-1, keepdims=True), sink)
        p = jnp.exp(s - m)
        den = jnp.sum(p, axis=-1, keepdims=True) + jnp.exp(sink - m)
        o = _dot(p.astype(BF16), v[:, kv * HEAD_DIM:(kv + 1) * HEAD_DIM]) / den
        for g_i, h in enumerate(heads):
            acc_ref[:, h * HEAD_DIM:(h + 1) * HEAD_DIM] = o[g_i * WINDOW:(g_i + 1) * WINDOW, :]
    o_ref[...] = _rms(acc_ref[...], g_ref[...]).astype(BF16)


def _attn_a(aq, ak, av, sink, g, bsz, seq):
    nb = seq // WINDOW
    qspec = pl.BlockSpec((WINDOW, A_WIDTH), lambda b, i: (b * nb + i, 0))
    kprev = pl.BlockSpec((WINDOW, A_KV_WIDTH), lambda b, i: (b * nb + jnp.maximum(i - 1, 0), 0))
    kcur = pl.BlockSpec((WINDOW, A_KV_WIDTH), lambda b, i: (b * nb + i, 0))
    knext = pl.BlockSpec((WINDOW, A_KV_WIDTH), lambda b, i: (b * nb + jnp.minimum(i + 1, nb - 1), 0))
    return pl.pallas_call(
        functools.partial(_attn_a_kernel, nb=nb),
        grid=(bsz, nb),
        in_specs=[pl.BlockSpec(memory_space=pltpu.SMEM), qspec, kprev, kcur, knext, kprev, kcur, knext,
                  pl.BlockSpec((1, A_WIDTH), lambda b, i: (0, 0))],
        out_specs=qspec,
        out_shape=jax.ShapeDtypeStruct((bsz * seq, A_WIDTH), BF16),
        scratch_shapes=[pltpu.VMEM((WINDOW, A_WIDTH), F32)],
        compiler_params=_cparams(("parallel", "parallel")),
        name="attn_a",
    )(sink, aq, ak, ak, ak, av, av, av, g)


B_GROUP = 8
B_KEYS = WIN_H * GRID_W


def _bias_table_kernel(rel_ref, o_ref):
    c = lax.broadcasted_iota(jnp.int32, (GRID_W, GRID_W), 0)
    w = lax.broadcasted_iota(jnp.int32, (GRID_W, GRID_W), 1)
    cstart = jnp.clip(c - WIN_W // 2, 0, GRID_W - WIN_W)
    colmask = (w >= cstart) & (w < cstart + WIN_W)
    col_off = jnp.clip(w - c + (WIN_W - 1), 0, 2 * WIN_W - 2)
    n_ro, n_co = 2 * WIN_H - 1, 2 * WIN_W - 1
    for h in range(B_HEADS):
        for ro in range(n_ro):
            def body(j, acc):
                return jnp.where(col_off == j, rel_ref[(h * n_ro + ro) * n_co + j], acc)
            t = lax.fori_loop(0, n_co, body, jnp.zeros((GRID_W, GRID_W), F32))
            t = jnp.where(colmask, t, NEG_INF)
            for p in range(WIN_H):
                kidx = ro - (WIN_H - 1) + p
                if 0 <= kidx < WIN_H:
                    o_ref[p, h, :, kidx * GRID_W:(kidx + 1) * GRID_W] = t


def _bias_table(rel_bias):
    return pl.pallas_call(
        _bias_table_kernel,
        in_specs=[pl.BlockSpec(memory_space=pltpu.SMEM)],
        out_shape=jax.ShapeDtypeStruct((WIN_H, B_HEADS, GRID_W, B_KEYS), F32),
        name="bias_table",
    )(rel_bias.reshape(-1)).reshape(WIN_H, B_HEADS * GRID_W, B_KEYS)


def _attn_b_kernel(q_ref, kp_ref, kc_ref, kn_ref, vp_ref, vc_ref, vn_ref, bias_ref, g_ref, o_ref,
                   kbuf, vbuf, acc_ref, *, rows):
    gidx = pl.program_id(1)
    blk = B_GROUP * GRID_W
    kbuf[0:blk, :] = kp_ref[...]
    kbuf[blk:2 * blk, :] = kc_ref[...]
    kbuf[2 * blk:3 * blk, :] = kn_ref[...]
    vbuf[0:blk, :] = vp_ref[...]
    vbuf[blk:2 * blk, :] = vc_ref[...]
    vbuf[2 * blk:3 * blk, :] = vn_ref[...]
    same_head = _same_head_mask(B_HEADS * GRID_W, GRID_W)
    for j in range(B_GROUP):
        r = gidx * B_GROUP + j
        start = jnp.clip(r - WIN_H // 2, 0, rows - WIN_H)
        pat = r - start
        loc = pl.multiple_of((start - gidx * B_GROUP + B_GROUP) * GRID_W, GRID_W)
        kw = kbuf[pl.ds(loc, B_KEYS), :]
        vw = vbuf[pl.ds(loc, B_KEYS), :]
        qj = q_ref[j * GRID_W:(j + 1) * GRID_W, :]
        q = jnp.where(same_head, jnp.concatenate([qj] * B_HEADS, axis=0), jnp.zeros((), BF16))
        s = _nt_dot(q, kw) + bias_ref[pat]
        m = jnp.max(s, axis=-1, keepdims=True)
        p = jnp.exp(s - m)
        den = jnp.sum(p, axis=-1, keepdims=True)
        o = jnp.where(same_head, _dot(p.astype(BF16), vw) / den, 0.0)
        acc_ref[j * GRID_W:(j + 1) * GRID_W, :] = sum(o[h * GRID_W:(h + 1) * GRID_W, :] for h in range(B_HEADS))
    o_ref[...] = _rms(acc_ref[...], g_ref[...]).astype(BF16)


def _attn_b(bq, bk, bv, bias, g, bsz, seq):
    rows = seq // GRID_W
    assert rows % B_GROUP == 0 and rows >= WIN_H
    ng = rows // B_GROUP
    blk = B_GROUP * GRID_W
    cur = pl.BlockSpec((blk, B_WIDTH), lambda b, i: (b * ng + i, 0))
    prev = pl.BlockSpec((blk, B_WIDTH), lambda b, i: (b * ng + jnp.maximum(i - 1, 0), 0))
    nxt = pl.BlockSpec((blk, B_WIDTH), lambda b, i: (b * ng + jnp.minimum(i + 1, ng - 1), 0))
    return pl.pallas_call(
        functools.partial(_attn_b_kernel, rows=rows),
        grid=(bsz, ng),
        in_specs=[cur, prev, cur, nxt, prev, cur, nxt,
                  pl.BlockSpec(bias.shape, lambda b, i: (0, 0, 0)),
                  pl.BlockSpec((1, B_WIDTH), lambda b, i: (0, 0))],
        out_specs=cur,
        out_shape=jax.ShapeDtypeStruct((bsz * seq, B_WIDTH), BF16),
        scratch_shapes=[pltpu.VMEM((3 * blk, B_WIDTH), BF16), pltpu.VMEM((3 * blk, B_WIDTH), BF16),
                        pltpu.VMEM((blk, B_WIDTH), F32)],
        compiler_params=_cparams(("parallel", "parallel")),
        name="attn_b",
    )(bq, bk, bk, bk, bv, bv, bv, bias, g)


def _gla_dir(q_ref, k_ref, v_ref, la_ref, o_ref, st_ref, reverse):
    tb = q_ref.shape[0]
    n_chunks = tb // C_CHUNK
    bi = lax.broadcasted_iota(jnp.int32, (tb, tb), 0)
    bj = lax.broadcasted_iota(jnp.int32, (tb, tb), 1)
    in_chunk = (bi // C_CHUNK) == (bj // C_CHUNK)
    cum_w = (in_chunk & ((bj >= bi) if reverse else (bj <= bi))).astype(BF16)
    la_hi, la_lo = _split_bf16(la_ref[...])
    b_all = _dot(cum_w, la_hi) + _dot(cum_w, la_lo)
    same_head = _same_head_mask(C_HEADS * C_CHUNK, C_CHUNK)
    ti = lax.broadcasted_iota(jnp.int32, (C_CHUNK, C_HEADS * C_CHUNK), 0)
    si = lax.broadcasted_iota(jnp.int32, (C_CHUNK, C_HEADS * C_CHUNK), 1) % C_CHUNK
    tri = (si >= ti) if reverse else (si <= ti)
    zero = jnp.zeros((), BF16)
    order = range(n_chunks - 1, -1, -1) if reverse else range(n_chunks)
    for c in order:
        rows = slice(c * C_CHUNK, (c + 1) * C_CHUNK)
        b = b_all[rows, :]
        b_tot = b[0:1, :] if reverse else b[C_CHUNK - 1:C_CHUNK, :]
        q_i = (q_ref[rows, :] * jnp.exp(b)).astype(BF16)
        kf = k_ref[rows, :]
        k_i = (kf * jnp.exp(-b)).astype(BF16)
        k_e = (kf * jnp.exp(b_tot - b)).astype(BF16)
        decay = jnp.exp(b_tot)
        v = v_ref[rows, :]
        k_bd = jnp.where(same_head, jnp.concatenate([k_i] * C_HEADS, axis=0), zero)
        v_bd = jnp.where(same_head, jnp.concatenate([v] * C_HEADS, axis=0), zero)
        a = jnp.where(tri, _nt_dot(q_i, k_bd), 0.0)
        st = st_ref[...]
        o_ref[rows, :] = _dot(a.astype(BF16), v_bd) + _nt_dot(q_i, st.astype(BF16))
        st_ref[...] = st * decay + jnp.where(same_head, _tn_dot(v, k_e), 0.0)


def _gla_kernel(qf_ref, kf_ref, vf_ref, laf_ref, qb_ref, kb_ref, vb_ref, lab_ref, of_ref, ob_ref, sf_ref, sb_ref):
    @pl.when(pl.program_id(1) == 0)
    def _():
        sf_ref[...] = jnp.zeros_like(sf_ref)
        sb_ref[...] = jnp.zeros_like(sb_ref)

    _gla_dir(qf_ref, kf_ref, vf_ref, laf_ref, of_ref, sf_ref, reverse=False)
    _gla_dir(qb_ref, kb_ref, vb_ref, lab_ref, ob_ref, sb_ref, reverse=True)


def _gla(cq, ck, cv, la, bsz, seq):
    tb = min(GLA_BLOCK, seq)
    nb = seq // tb
    fwd = lambda col: pl.BlockSpec((tb, C_WIDTH), lambda b, i: (b * nb + i, col))
    bwd = lambda col: pl.BlockSpec((tb, C_WIDTH), lambda b, i: (b * nb + nb - 1 - i, col))
    out = jax.ShapeDtypeStruct((bsz * seq, C_WIDTH), F32)
    state = pltpu.VMEM((C_WIDTH, C_WIDTH), F32)
    return pl.pallas_call(
        _gla_kernel,
        grid=(bsz, nb),
        in_specs=[fwd(0), fwd(0), fwd(0), fwd(0), bwd(0), bwd(0), bwd(0), bwd(1)],
        out_specs=[fwd(0), bwd(0)],
        out_shape=[out, out],
        scratch_shapes=[state, state],
        compiler_params=_cparams(("parallel", "arbitrary")),
        name="gla",
    )(cq, ck, cv, la, cq, ck, cv, la)


def _out_proj_kernel(x_ref, oa_ref, ob_ref, of_ref, obw_ref, cg_ref, cng_ref, wo_ref, g2_ref, wr_hi_ref, wr_lo_ref,
                     x1_ref, h2_ref, aff_ref):
    o = of_ref[...] + obw_ref[...]
    gi = lax.broadcasted_iota(jnp.int32, (C_WIDTH, C_WIDTH), 0) // HEAD_DIM
    gj = lax.broadcasted_iota(jnp.int32, (C_WIDTH, C_WIDTH), 1) // HEAD_DIM
    ones_bd = (gi == gj).astype(BF16)
    sq_hi, sq_lo = _split_bf16(o * o)
    ms = (_dot(sq_hi, ones_bd) + _dot(sq_lo, ones_bd)) * (1.0 / HEAD_DIM)
    cg = cg_ref[...]
    oc = (o * lax.rsqrt(ms + EPS) * cng_ref[...]) * (cg / (1.0 + jnp.exp(-cg)))
    y = _dot(oa_ref[...], wo_ref[0:A_WIDTH, :])
    y += _dot(ob_ref[...], wo_ref[A_WIDTH:A_WIDTH + B_WIDTH, :])
    y += _dot(oc.astype(BF16), wo_ref[A_WIDTH + B_WIDTH:, :])
    x1 = x_ref[...] + y
    x1_ref[...] = x1
    h2 = _rms(x1, g2_ref[...])
    h2_ref[...] = h2.astype(BF16)
    h_hi, h_lo = _split_bf16(h2)
    logits = _nt_dot(wr_hi_ref[...], h_hi) + _nt_dot(wr_hi_ref[...], h_lo) + _nt_dot(wr_lo_ref[...], h_hi)
    m = jnp.max(logits, axis=0, keepdims=True)
    p = jnp.exp(logits - m)
    aff_ref[...] = p / jnp.sum(p, axis=0, keepdims=True)


def _out_proj(x, oa, ob, o_f, o_b, cg, cng, wo_bf, g2, wr_hi, wr_lo):
    n = x.shape[0]
    tm = min(IN_PROJ_ROWS, n)
    row = lambda width: pl.BlockSpec((tm, width), lambda i: (i, 0))
    full = lambda a: pl.BlockSpec(a.shape, lambda i: (0,) * a.ndim)
    return pl.pallas_call(
        _out_proj_kernel,
        grid=(n // tm,),
        in_specs=[row(D_MODEL), row(A_WIDTH), row(B_WIDTH), row(C_WIDTH), row(C_WIDTH), row(C_WIDTH),
                  full(cng), full(wo_bf), full(g2), full(wr_hi), full(wr_lo)],
        out_specs=[row(D_MODEL), row(D_MODEL), pl.BlockSpec((N_EXPERTS, tm), lambda i: (0, i))],
        out_shape=[jax.ShapeDtypeStruct((n, D_MODEL), F32), jax.ShapeDtypeStruct((n, D_MODEL), BF16),
                   jax.ShapeDtypeStruct((N_EXPERTS, n), F32)],
        compiler_params=_cparams(("parallel",)),
        name="out_proj",
    )(x, oa, ob, o_f, o_b, cg, cng, wo_bf, g2, wr_hi, wr_lo)


def _route_kernel(aff_ref, pos_ref, offs_ref, *, cap):
    n = aff_ref.shape[1]
    nt = n // ROUTE_TILE
    bits = pltpu.bitcast(aff_ref[...], jnp.int32)

    def search(it, thr):
        cand = thr | jnp.left_shift(jnp.int32(1), 30 - it)
        cnt = jnp.sum(jnp.where(bits >= cand, 1.0, 0.0), axis=1, keepdims=True)
        return jnp.where(cnt >= cap, cand, thr)

    thr = lax.fori_loop(0, 31, search, jnp.zeros((N_EXPERTS, 1), jnp.int32))
    n_gt = jnp.sum(jnp.where(bits > thr, 1.0, 0.0), axis=1, keepdims=True)
    n_tie = cap - n_gt

    ui = lax.broadcasted_iota(jnp.int32, (ROUTE_TILE, ROUTE_TILE), 0)
    uj = lax.broadcasted_iota(jnp.int32, (ROUTE_TILE, ROUTE_TILE), 1)
    upper = (ui <= uj).astype(BF16)
    lane = lax.broadcasted_iota(jnp.int32, offs_ref.shape, 1)

    def tile(i, carry):
        c_gt, c_eq, offs = carry
        col = pl.multiple_of(i * ROUTE_TILE, ROUTE_TILE)
        b = pltpu.bitcast(aff_ref[:, pl.ds(col, ROUTE_TILE)], jnp.int32)
        gt = jnp.where(b > thr, 1.0, 0.0)
        eq = jnp.where(b == thr, 1.0, 0.0)
        inc = _dot(jnp.concatenate([gt, eq], axis=0).astype(BF16), upper)
        gt_before = c_gt + inc[:N_EXPERTS] - gt
        eq_before = c_eq + inc[N_EXPERTS:] - eq
        sel = (gt > 0.0) | ((eq > 0.0) & (eq_before < n_tie))
        pos = gt_before + jnp.minimum(eq_before, n_tie)
        pos_ref[:, pl.ds(col, ROUTE_TILE)] = jnp.where(sel, pos, -1.0).astype(jnp.int32)
        start = c_gt + jnp.minimum(c_eq, n_tie)
        offs = jnp.where(lane == i, start.astype(jnp.int32), offs)
        return (c_gt + jnp.sum(gt, axis=1, keepdims=True), c_eq + jnp.sum(eq, axis=1, keepdims=True), offs)

    zero = jnp.zeros((N_EXPERTS, 1), F32)
    offs = jnp.where(lane == nt, cap, 0).astype(jnp.int32)
    _, _, offs = lax.fori_loop(0, nt, tile, (zero, zero, offs))
    offs_ref[...] = offs


def _route(aff_t, cap):
    n = aff_t.shape[1]
    nt = n // ROUTE_TILE
    return pl.pallas_call(
        functools.partial(_route_kernel, cap=cap),
        out_shape=[jax.ShapeDtypeStruct((N_EXPERTS, n), jnp.int32),
                   jax.ShapeDtypeStruct((N_EXPERTS, nt + 1), jnp.int32)],
        compiler_params=pltpu.CompilerParams(vmem_limit_bytes=VMEM_LIMIT),
        name="route",
    )(aff_t)


def _align(v):
    return (v // MOE_ALIGN) * MOE_ALIGN


def _moe_passes(offs_ref, t, subs, x0, s):
    k = jnp.int32(1)
    for g in range(MOE_PAIR):
        need = offs_ref[x0 + g, t * subs + s + 1] - _align(offs_ref[x0 + g, t * subs + s])
        k = jnp.maximum(k, (need + MOE_WIN - 1) // MOE_WIN)
    return k


def _moe_windows(offs_ref, pos_ref, t, subs, x0, s, j):
    slot = lax.broadcasted_iota(jnp.int32, (MOE_WIN, ROUTE_TILE), 0)
    hits, rels = [], []
    for g in range(MOE_PAIR):
        x = x0 + g
        ws = _align(offs_ref[x, t * subs + s]) + j * MOE_WIN
        pos = pos_ref[pl.ds(x, 1), s * ROUTE_TILE:(s + 1) * ROUTE_TILE]
        hits.append((pos - ws) == slot)
        rels.append(pl.multiple_of(ws - _align(offs_ref[x, t * subs]), MOE_ALIGN))
    onehot = jnp.concatenate([jnp.where(h, 1.0, 0.0) for h in hits], axis=0).astype(BF16)
    return hits, rels, onehot


def _moe_chunks(offs_ref, t, subs, x):
    base = _align(offs_ref[x, t * subs])
    return base, (offs_ref[x, (t + 1) * subs] - base + MOE_CHUNK - 1) // MOE_CHUNK


def _moe_gather_kernel(offs_ref, h_ref, pos_ref, gate_ref, xe_ref, x_ref, carry_ref, sem):
    t = pl.program_id(0)
    e0 = pl.program_id(1) * MOE_STEP
    tt = h_ref.shape[0]
    subs = tt // ROUTE_TILE

    def off(x, s):
        return offs_ref[x, t * subs + s]

    @pl.when(t == 0)
    def _():
        carry_ref[pl.ds(e0, MOE_STEP)] = jnp.zeros((MOE_STEP, MOE_ALIGN, XE_WIDTH), BF16)
        cap = xe_ref.shape[1] - MOE_CHUNK
        x_ref[0, 0:MOE_CHUNK, :] = jnp.zeros((MOE_CHUNK, XE_WIDTH), BF16)
        pads = [pltpu.make_async_copy(x_ref.at[0, 0:MOE_CHUNK, :], xe_ref.at[e0 + g, cap:cap + MOE_CHUNK, :], sem)
                for g in range(MOE_STEP)]
        for p in pads:
            p.start()
        for p in pads:
            p.wait()

    for g in range(MOE_STEP):
        x_ref[g, 0:MOE_ALIGN, :] = carry_ref[e0 + g]

    row = lax.broadcasted_iota(jnp.int32, (MOE_WIN, 1), 0)
    lane = lax.broadcasted_iota(jnp.int32, (MOE_WIN, XE_WIDTH - D_MODEL), 1)

    def gather_pass(pair, s, j):
        x0 = e0 + pair * MOE_PAIR
        cols = slice(s * ROUTE_TILE, (s + 1) * ROUTE_TILE)
        hits, rels, onehot = _moe_windows(offs_ref, pos_ref, t, subs, x0, s, j)
        r = _dot(onehot, h_ref[cols, :])
        for g in range(MOE_PAIR):
            gate = jnp.sum(jnp.where(hits[g], gate_ref[pl.ds(x0 + g, 1), cols], 0.0), axis=1, keepdims=True)
            g_hi = gate.astype(BF16).astype(F32)
            extra = jnp.where(lane == 0, g_hi, jnp.where(lane == 1, gate - g_hi, 0.0))
            new = jnp.concatenate([r[g * MOE_WIN:(g + 1) * MOE_WIN, :], extra], axis=1)
            ws = _align(off(x0 + g, s)) + j * MOE_WIN
            own = row >= off(x0 + g, s) - ws
            buf = pair * MOE_PAIR + g
            if isinstance(j, int):
                head = pl.ds(rels[g], MOE_ALIGN)
                x_ref[buf, head, :] = jnp.where(own[:MOE_ALIGN], new[:MOE_ALIGN],
                                                x_ref[buf, head, :].astype(F32)).astype(BF16)
                x_ref[buf, pl.ds(rels[g] + MOE_ALIGN, MOE_WIN - MOE_ALIGN), :] = new[MOE_ALIGN:].astype(BF16)
            else:
                own = own & (row < off(x0 + g, s + 1) - ws)
                win = pl.ds(rels[g], MOE_WIN)
                x_ref[buf, win, :] = jnp.where(own, new, x_ref[buf, win, :].astype(F32)).astype(BF16)

    for pair in range(MOE_STEP // MOE_PAIR):
        for s in range(subs):
            gather_pass(pair, s, 0)
    for pair in range(MOE_STEP // MOE_PAIR):
        for s in range(subs):
            lax.fori_loop(1, _moe_passes(offs_ref, t, subs, e0 + pair * MOE_PAIR, s),
                          lambda j, carry: (gather_pass(pair, s, j), carry)[1], 0)

    def copy(g, c):
        base, _ = _moe_chunks(offs_ref, t, subs, e0 + g)
        r0 = pl.multiple_of(c * MOE_CHUNK, MOE_CHUNK)
        return pltpu.make_async_copy(x_ref.at[g, pl.ds(r0, MOE_CHUNK), :],
                                     xe_ref.at[e0 + g, pl.ds(pl.multiple_of(base + r0, MOE_ALIGN), MOE_CHUNK), :], sem)

    for g in range(MOE_STEP):
        x = e0 + g
        base, n_chunks = _moe_chunks(offs_ref, t, subs, x)
        last_group = pl.multiple_of(_align(off(x, subs)) - base, MOE_ALIGN)
        tail = x_ref[g, pl.ds(last_group, MOE_ALIGN), :].astype(F32)
        tail = jnp.where(row[:MOE_ALIGN] < off(x, subs) - base - last_group, tail, 0.0).astype(BF16)
        x_ref[g, pl.ds(last_group, MOE_ALIGN), :] = tail
        carry_ref[x] = tail

        def fill(i, carry):
            x_ref[g, pl.ds(pl.multiple_of(i * MOE_ALIGN, MOE_ALIGN), MOE_ALIGN), :] = jnp.zeros((MOE_ALIGN, XE_WIDTH), BF16)
            return carry

        lax.fori_loop(last_group // MOE_ALIGN + 1, n_chunks * MOE_CHUNK // MOE_ALIGN, fill, 0)
        lax.fori_loop(0, n_chunks, lambda c, carry: (copy(g, c).start(), carry)[1], 0)
    for g in range(MOE_STEP):
        _, n_chunks = _moe_chunks(offs_ref, t, subs, e0 + g)
        lax.fori_loop(0, n_chunks, lambda c, carry: (copy(g, c).wait(), carry)[1], 0)


def _moe_ffn_kernel(x_ref, wg_ref, wu_ref, wd_ref, y_ref, wg_bf, wu_bf, wd_bf):
    j = pl.program_id(1)
    last = pl.num_programs(1) - 1

    @pl.when(j == 0)
    def _():
        wg_bf[...] = wg_ref[0].astype(BF16)
        wu_bf[...] = wu_ref[0].astype(BF16)
        wd_bf[...] = wd_ref[0].astype(BF16)

    @pl.when(j < last)
    def _():
        x = x_ref[0, :, 0:D_MODEL]
        gate = jnp.sum(x_ref[0, :, D_MODEL:XE_WIDTH].astype(F32), axis=1, keepdims=True)
        hg = _dot(x, wg_bf[...])
        hu = _dot(x, wu_bf[...])
        act = (hg / (1.0 + jnp.exp(-hg))) * hu * gate
        y_ref[0] = _dot(act.astype(BF16), wd_bf[...]).astype(BF16)

    @pl.when(j == last)
    def _():
        y_ref[0] = jnp.zeros(y_ref.shape[1:], BF16)


def _moe_combine_kernel(offs_ref, x1_ref, pos_ref, y_ref, fg_ref, o_ref, y_buf, sem, *, final_norm):
    t = pl.program_id(0)
    q = pl.program_id(1)
    e0 = q * MOE_STEP
    tt = x1_ref.shape[0]
    subs = tt // ROUTE_TILE

    @pl.when((t == 0) & (q == 0))
    def _():
        y_buf[...] = jnp.zeros_like(y_buf)

    @pl.when(q == 0)
    def _():
        o_ref[...] = x1_ref[...]

    def copy(g, c):
        base, _ = _moe_chunks(offs_ref, t, subs, e0 + g)
        r0 = pl.multiple_of(c * MOE_CHUNK, MOE_CHUNK)
        return pltpu.make_async_copy(y_ref.at[e0 + g, pl.ds(pl.multiple_of(base + r0, MOE_ALIGN), MOE_CHUNK), :],
                                     y_buf.at[g, pl.ds(r0, MOE_CHUNK), :], sem)

    for g in range(MOE_STEP):
        _, n_chunks = _moe_chunks(offs_ref, t, subs, e0 + g)
        lax.fori_loop(0, n_chunks, lambda c, carry: (copy(g, c).start(), carry)[1], 0)
    for g in range(MOE_STEP):
        _, n_chunks = _moe_chunks(offs_ref, t, subs, e0 + g)
        lax.fori_loop(0, n_chunks, lambda c, carry: (copy(g, c).wait(), carry)[1], 0)

    def combine_pass(pair, s, j):
        cols = slice(s * ROUTE_TILE, (s + 1) * ROUTE_TILE)
        _, rels, onehot = _moe_windows(offs_ref, pos_ref, t, subs, e0 + pair * MOE_PAIR, s, j)
        ycat = jnp.concatenate([y_buf[pair * MOE_PAIR + g, pl.ds(rels[g], MOE_WIN), :]
                                for g in range(MOE_PAIR)], axis=0)
        o_ref[cols, :] += _tn_dot(onehot, ycat)

    for pair in range(MOE_STEP // MOE_PAIR):
        for s in range(subs):
            combine_pass(pair, s, 0)
    for pair in range(MOE_STEP // MOE_PAIR):
        for s in range(subs):
            lax.fori_loop(1, _moe_passes(offs_ref, t, subs, e0 + pair * MOE_PAIR, s),
                          lambda j, carry: (combine_pass(pair, s, j), carry)[1], 0)

    if final_norm:
        @pl.when(q == pl.num_programs(1) - 1)
        def _():
            o_ref[...] = _rms(o_ref[...], fg_ref[...])


def _moe(offs, h2, x1, pos, gate, wg, wu, wd, final_g, final_norm, cap):
    n = h2.shape[0]
    tt = min(MOE_TILE, n)
    assert cap % MOE_CHUNK == 0 and n % tt == 0
    steps = N_EXPERTS // MOE_STEP
    stage_rows = tt + MOE_ALIGN + 3 * MOE_WIN
    tok = pl.BlockSpec((tt, D_MODEL), lambda t, q, offs: (t, 0))
    per_tok = pl.BlockSpec((N_EXPERTS, tt), lambda t, q, offs: (0, t))
    hbm = pl.BlockSpec(memory_space=pl.ANY)
    sems = ("arbitrary", "arbitrary")

    xe = pl.pallas_call(
        _moe_gather_kernel,
        grid_spec=pltpu.PrefetchScalarGridSpec(
            num_scalar_prefetch=1, grid=(n // tt, steps),
            in_specs=[tok, per_tok, per_tok], out_specs=hbm,
            scratch_shapes=[pltpu.VMEM((MOE_STEP, stage_rows, XE_WIDTH), BF16),
                            pltpu.VMEM((N_EXPERTS, MOE_ALIGN, XE_WIDTH), BF16), pltpu.SemaphoreType.DMA(())],
        ),
        out_shape=jax.ShapeDtypeStruct((N_EXPERTS, cap + MOE_CHUNK, XE_WIDTH), BF16),
        compiler_params=_cparams(sems),
        name="moe_gather",
    )(offs, h2, pos, gate)

    fb = min(MOE_FFN_ROWS, cap)
    nb = cap // fb
    wspec = pl.BlockSpec((1, D_MODEL, D_MODEL), lambda e, j: (e, 0, 0))
    w_bf = pltpu.VMEM((D_MODEL, D_MODEL), BF16)
    y = pl.pallas_call(
        _moe_ffn_kernel,
        grid=(N_EXPERTS, nb + 1),
        in_specs=[pl.BlockSpec((1, fb, XE_WIDTH), lambda e, j: (e, jnp.minimum(j, nb - 1), 0)), wspec, wspec, wspec],
        out_specs=pl.BlockSpec((1, fb, D_MODEL), lambda e, j: (e, j, 0)),
        out_shape=jax.ShapeDtypeStruct((N_EXPERTS, cap + fb, D_MODEL), BF16),
        scratch_shapes=[w_bf, w_bf, w_bf],
        compiler_params=_cparams(("parallel", "arbitrary")),
        name="moe_ffn",
    )(xe, wg, wu, wd)

    return pl.pallas_call(
        functools.partial(_moe_combine_kernel, final_norm=final_norm),
        grid_spec=pltpu.PrefetchScalarGridSpec(
            num_scalar_prefetch=1, grid=(n // tt, steps),
            in_specs=[tok, per_tok, hbm, pl.BlockSpec((1, D_MODEL), lambda t, q, offs: (0, 0))],
            out_specs=tok,
            scratch_shapes=[pltpu.VMEM((MOE_STEP, stage_rows, D_MODEL), BF16), pltpu.SemaphoreType.DMA(())],
        ),
        out_shape=jax.ShapeDtypeStruct((n, D_MODEL), F32),
        compiler_params=_cparams(sems),
        name="moe_combine",
    )(offs, x1, pos, y, final_g)


def _prep_layer(l, norm1_g, w_in, a_sink, a_norm_g, b_rel_bias, b_norm_g, c_alpha_w2_f, c_alpha_b_f,
                c_alpha_w2_b, c_alpha_b_b, c_norm_g, w_out, norm2_g, w_router, w_gate, w_up, w_down):
    z = jnp.zeros((C_LOWRANK, C_WIDTH), F32)
    w2 = jnp.concatenate([jnp.concatenate([c_alpha_w2_f[l], z], axis=1),
                          jnp.concatenate([z, c_alpha_w2_b[l]], axis=1)], axis=0)
    b2 = jnp.concatenate([c_alpha_b_f[l], c_alpha_b_b[l]])[None, :]
    wr_t = w_router[l].T
    wr_hi = wr_t.astype(BF16)
    wr_lo = (wr_t - wr_hi.astype(F32)).astype(BF16)
    return dict(
        g1=norm1_g[l][None, :], w_in=w_in[l].astype(BF16), w2=w2, b2=b2,
        sink=a_sink[l], a_g=a_norm_g[l][None, :], bias=_bias_table(b_rel_bias[l]), b_g=b_norm_g[l][None, :],
        c_g=c_norm_g[l][None, :], w_out=w_out[l].astype(BF16), g2=norm2_g[l][None, :],
        wr_hi=wr_hi, wr_lo=wr_lo, wg=w_gate[l], wu=w_up[l], wd=w_down[l])


def _layer(x, p, bsz, seq, final_g, final_norm):
    n = bsz * seq
    aq, ak, av, bq, bk, bv, cq, ck, cv, cg, la = _in_proj(x, p["g1"], p["w_in"], p["w2"], p["b2"], seq)
    oa = _attn_a(aq, ak, av, p["sink"], p["a_g"], bsz, seq)
    ob = _attn_b(bq, bk, bv, p["bias"], p["b_g"], bsz, seq)
    o_f, o_b = _gla(cq, ck, cv, la, bsz, seq)
    x1, h2, aff = _out_proj(x, oa, ob, o_f, o_b, cg, p["c_g"], p["w_out"], p["g2"], p["wr_hi"], p["wr_lo"])
    cap = EC_CAPACITY * n // N_EXPERTS
    pos, offs = _route(aff, cap)
    return _moe(offs, h2, x1, pos, aff, p["wg"], p["wu"], p["wd"], final_g, final_norm, cap)


def _trunk(x, layers, final_g):
    bsz, seq, _ = x.shape
    y = x.reshape(bsz * seq, D_MODEL)
    for l, p in enumerate(layers):
        y = _layer(y, p, bsz, seq, final_g, final_norm=(l == len(layers) - 1))
    return y.reshape(bsz, seq, D_MODEL)


def kernel(x_prompt, x_sample, norm1_g, w_in, a_sink, a_norm_g, b_rel_bias, b_norm_g, c_alpha_w2_f, c_alpha_b_f,
           c_alpha_w2_b, c_alpha_b_b, c_norm_g, w_out, norm2_g, w_router, w_gate, w_up, w_down, final_g):
    depth = w_in.shape[0]
    layers = [_prep_layer(l, norm1_g, w_in, a_sink, a_norm_g, b_rel_bias, b_norm_g, c_alpha_w2_f, c_alpha_b_f,
                          c_alpha_w2_b, c_alpha_b_b, c_norm_g, w_out, norm2_g, w_router, w_gate, w_up, w_down)
              for l in range(depth)]
    fg = final_g[None, :]
    return _trunk(x_prompt, layers, fg), _trunk(x_sample, layers, fg)
```

```python
import functools

import jax
import jax.numpy as jnp
from jax import lax
from jax.experimental import pallas as pl
from jax.experimental.pallas import tpu as pltpu

F32 = jnp.float32
BF16 = jnp.bfloat16

D_MODEL = 1024
HEAD_DIM = 64
A_WIDTH = 512
A_HEADS = 8
A_KV_HEADS = 2
A_KV_WIDTH = 128
WINDOW = 128
ROPE_THETA = 500000.0
ROPE_DIM = 16
B_WIDTH = 256
B_HEADS = 4
GRID_W = 64
WIN_H = 8
WIN_W = 16
C_WIDTH = 256
C_HEADS = 4
C_LOWRANK = 16
C_TAU = 16.0
C_CHUNK = 64
N_EXPERTS = 16
EC_CAPACITY = 2
EPS = 1e-6
NEG_INF = -1e30
IN_WIDTH = 2592

_OFF_AQ, _OFF_AK, _OFF_AV = 0, 512, 640
_OFF_BQ, _OFF_BK, _OFF_BV = 768, 1024, 1280
_OFF_CQ, _OFF_CK, _OFF_CV, _OFF_CG = 1536, 1792, 2048, 2304
_OFF_LR = 2560

LANES = 128
VMEM_LIMIT = 56 * 1024 * 1024

IN_PROJ_ROWS = 512
GLA_BLOCK = 512
ROUTE_TILE = 256
MOE_TILE = 2048
MOE_COMBINE_TILE = 1024
MOE_STEP = 4
MOE_PAIR = 4
MOE_WIN = 128
MOE_ALIGN = 16
MOE_CHUNK = 128
MOE_FFN_ROWS = 512
XE_WIDTH = D_MODEL + LANES


def _cparams(sem):
    return pltpu.CompilerParams(dimension_semantics=sem, vmem_limit_bytes=VMEM_LIMIT)


def _nt_dot(a, b):
    return lax.dot_general(a, b, (((1,), (1,)), ((), ())), preferred_element_type=F32)


def _tn_dot(a, b):
    return lax.dot_general(a, b, (((0,), (0,)), ((), ())), preferred_element_type=F32)


def _dot(a, b):
    return jnp.dot(a, b, preferred_element_type=F32)


def _rms(x, g):
    return x * lax.rsqrt(jnp.mean(x * x, axis=-1, keepdims=True) + EPS) * g


def _same_head_mask(n_rows, rows_per_head):
    width = (n_rows // rows_per_head) * HEAD_DIM
    ri = lax.broadcasted_iota(jnp.int32, (n_rows, width), 0) // rows_per_head
    li = lax.broadcasted_iota(jnp.int32, (n_rows, width), 1) // HEAD_DIM
    return ri == li


def _split_bf16(x):
    hi = x.astype(BF16)
    return hi, (x - hi.astype(F32)).astype(BF16)


def _in_proj_kernel(x_ref, g_ref, w_ref, cos_ref, s1_ref, s2_ref, w2_ref, b2_ref,
                    aq_ref, ak_ref, av_ref, bq_ref, bk_ref, bv_ref,
                    cq_ref, ck_ref, cv_ref, cg_ref, la_ref):
    h = _rms(x_ref[...], g_ref[...]).astype(BF16)

    def proj(lo, width):
        return _dot(h, w_ref[:, lo:lo + width])

    cos, s1, s2 = cos_ref[...], s1_ref[...], s2_ref[...]

    def rope(t):
        return t * cos + pltpu.roll(t, LANES - ROPE_DIM // 2, 1) * s1 + pltpu.roll(t, ROPE_DIM // 2, 1) * s2

    scale = HEAD_DIM ** -0.5
    aq = proj(_OFF_AQ, A_WIDTH)
    for c in range(A_WIDTH // LANES):
        aq_ref[:, c * LANES:(c + 1) * LANES] = (rope(aq[:, c * LANES:(c + 1) * LANES]) * scale).astype(BF16)
    ak_ref[...] = rope(proj(_OFF_AK, A_KV_WIDTH)).astype(BF16)
    av_ref[...] = proj(_OFF_AV, A_KV_WIDTH).astype(BF16)
    bq_ref[...] = (proj(_OFF_BQ, B_WIDTH) * scale).astype(BF16)
    bk_ref[...] = proj(_OFF_BK, B_WIDTH).astype(BF16)
    bv_ref[...] = proj(_OFF_BV, B_WIDTH).astype(BF16)
    cq_ref[...] = proj(_OFF_CQ, C_WIDTH) * scale
    ck_ref[...] = proj(_OFF_CK, C_WIDTH)
    cv_ref[...] = proj(_OFF_CV, C_WIDTH).astype(BF16)
    cg_ref[...] = proj(_OFF_CG, C_WIDTH)
    lr = proj(_OFF_LR, 2 * C_LOWRANK)
    z = jnp.dot(lr, w2_ref[...], preferred_element_type=F32, precision=lax.Precision.HIGHEST) + b2_ref[...]
    la_ref[...] = (jnp.minimum(z, 0.0) - jnp.log(1.0 + jnp.exp(-jnp.abs(z)))) * (1.0 / C_TAU)


def _rope_tables(seq):
    half = ROPE_DIM // 2
    inv = jnp.power(jnp.float32(ROPE_THETA), -jnp.arange(half, dtype=F32) * (2.0 / ROPE_DIM))
    ang = jnp.arange(seq, dtype=F32)[:, None] * inv[None, :]
    cos, sin = jnp.cos(ang), jnp.sin(ang)
    ones = jnp.ones((seq, HEAD_DIM - ROPE_DIM), F32)
    zeros = jnp.zeros((seq, HEAD_DIM - half), F32)
    c = jnp.concatenate([cos, cos, ones], axis=1)
    s1 = jnp.concatenate([-sin, zeros], axis=1)
    s2 = jnp.concatenate([jnp.zeros((seq, half), F32), sin, jnp.zeros((seq, HEAD_DIM - ROPE_DIM), F32)], axis=1)
    two = lambda t: jnp.concatenate([t, t], axis=1)
    return two(c), two(s1), two(s2)


def _in_proj(x, g1, w_in_bf, w2, b2, seq):
    n = x.shape[0]
    tm = min(IN_PROJ_ROWS, seq)
    per_seq = seq // tm
    cos, s1, s2 = _rope_tables(seq)
    row = lambda width: pl.BlockSpec((tm, width), lambda i: (i, 0))
    full = lambda a: pl.BlockSpec(a.shape, lambda i: (0,) * a.ndim)
    tab = pl.BlockSpec((tm, LANES), lambda i: (i % per_seq, 0))
    widths = [(A_WIDTH, BF16), (A_KV_WIDTH, BF16), (A_KV_WIDTH, BF16), (B_WIDTH, BF16), (B_WIDTH, BF16),
              (B_WIDTH, BF16), (C_WIDTH, F32), (C_WIDTH, F32), (C_WIDTH, BF16), (C_WIDTH, F32), (2 * C_WIDTH, F32)]
    return pl.pallas_call(
        _in_proj_kernel,
        grid=(n // tm,),
        in_specs=[row(D_MODEL), full(g1), full(w_in_bf), tab, tab, tab, full(w2), full(b2)],
        out_specs=[row(w) for w, _ in widths],
        out_shape=[jax.ShapeDtypeStruct((n, w), dt) for w, dt in widths],
        compiler_params=_cparams(("parallel",)),
        name="in_proj",
    )(x, g1, w_in_bf, cos, s1, s2, w2, b2)


A_QBLOCKS = 2


def _attn_a_kernel(sink_ref, q_ref, kp_ref, kc_ref, kn_ref, vp_ref, vc_ref, vn_ref, g_ref, o_ref, acc_ref, *, steps):
    i = pl.program_id(1)
    grp = A_HEADS // A_KV_HEADS
    k = jnp.concatenate([kp_ref[...], kc_ref[...], kn_ref[...]], axis=0)
    v = jnp.concatenate([vp_ref[...], vc_ref[...], vn_ref[...]], axis=0)
    qq = lax.broadcasted_iota(jnp.int32, (WINDOW, 3 * WINDOW), 0)
    kk = lax.broadcasted_iota(jnp.int32, (WINDOW, 3 * WINDOW), 1)
    band = jnp.abs(kk - WINDOW - qq) <= WINDOW
    head_row = lax.broadcasted_iota(jnp.int32, (grp * WINDOW, 1), 0) // WINDOW
    for blk in range(A_QBLOCKS):
        visible = band
        if blk == 0:
            visible = visible & jnp.logical_not((kk < WINDOW) & (i == 0))
        if blk == A_QBLOCKS - 1:
            visible = visible & jnp.logical_not((kk >= 2 * WINDOW) & (i == steps - 1))
        bias = jnp.concatenate([jnp.where(visible, 0.0, NEG_INF)] * grp, axis=0)
        rows = slice(blk * WINDOW, (blk + 1) * WINDOW)
        keys = slice(blk * WINDOW, (blk + 3) * WINDOW)
        for kv in range(A_KV_HEADS):
            heads = range(kv * grp, (kv + 1) * grp)
            q = jnp.concatenate([q_ref[rows, h * HEAD_DIM:(h + 1) * HEAD_DIM] for h in heads], axis=0)
            s = _nt_dot(q, k[keys, kv * HEAD_DIM:(kv + 1) * HEAD_DIM]) + bias
            sink = jnp.zeros((grp * WINDOW, 1), F32)
            for g_i, h in enumerate(heads):
                sink = jnp.where(head_row == g_i, sink_ref[h], sink)
            m = jnp.maximum(jnp.max(s, axis=-1, keepdims=True), sink)
            p = jnp.exp(s - m)
            den = jnp.sum(p, axis=-1, keepdims=True) + jnp.exp(sink - m)
            o = _dot(p.astype(BF16), v[keys, kv * HEAD_DIM:(kv + 1) * HEAD_DIM]) / den
            for g_i, h in enumerate(heads):
                acc_ref[rows, h * HEAD_DIM:(h + 1) * HEAD_DIM] = o[g_i * WINDOW:(g_i + 1) * WINDOW, :]
    o_ref[...] = _rms(acc_ref[...], g_ref[...]).astype(BF16)


def _attn_a(aq, ak, av, sink, g, bsz, seq):
    nb = seq // WINDOW
    assert nb % A_QBLOCKS == 0
    steps = nb // A_QBLOCKS
    qrows = A_QBLOCKS * WINDOW
    qspec = pl.BlockSpec((qrows, A_WIDTH), lambda b, i: (b * steps + i, 0))
    kprev = pl.BlockSpec((WINDOW, A_KV_WIDTH), lambda b, i: (b * nb + jnp.maximum(i * A_QBLOCKS - 1, 0), 0))
    kcur = pl.BlockSpec((qrows, A_KV_WIDTH), lambda b, i: (b * steps + i, 0))
    knext = pl.BlockSpec((WINDOW, A_KV_WIDTH), lambda b, i: (b * nb + jnp.minimum((i + 1) * A_QBLOCKS, nb - 1), 0))
    return pl.pallas_call(
        functools.partial(_attn_a_kernel, steps=steps),
        grid=(bsz, steps),
        in_specs=[pl.BlockSpec(memory_space=pltpu.SMEM), qspec, kprev, kcur, knext, kprev, kcur, knext,
                  pl.BlockSpec((1, A_WIDTH), lambda b, i: (0, 0))],
        out_specs=qspec,
        out_shape=jax.ShapeDtypeStruct((bsz * seq, A_WIDTH), BF16),
        scratch_shapes=[pltpu.VMEM((qrows, A_WIDTH), F32)],
        compiler_params=_cparams(("parallel", "parallel")),
        name="attn_a",
    )(sink, aq, ak, ak, ak, av, av, av, g)


B_GROUP = 8
B_KEYS = WIN_H * GRID_W


def _bias_table_kernel(rel_ref, o_ref):
    c = lax.broadcasted_iota(jnp.int32, (GRID_W, GRID_W), 0)
    w = lax.broadcasted_iota(jnp.int32, (GRID_W, GRID_W), 1)
    cstart = jnp.clip(c - WIN_W // 2, 0, GRID_W - WIN_W)
    colmask = (w >= cstart) & (w < cstart + WIN_W)
    col_off = jnp.clip(w - c + (WIN_W - 1), 0, 2 * WIN_W - 2)
    n_ro, n_co = 2 * WIN_H - 1, 2 * WIN_W - 1
    for h in range(B_HEADS):
        for ro in range(n_ro):
            def body(j, acc):
                return jnp.where(col_off == j, rel_ref[(h * n_ro + ro) * n_co + j], acc)
            t = lax.fori_loop(0, n_co, body, jnp.zeros((GRID_W, GRID_W), F32))
            t = jnp.where(colmask, t, NEG_INF)
            for p in range(WIN_H):
                kidx = ro - (WIN_H - 1) + p
                if 0 <= kidx < WIN_H:
                    o_ref[p, h, :, kidx * GRID_W:(kidx + 1) * GRID_W] = t


def _bias_table(rel_bias):
    return pl.pallas_call(
        _bias_table_kernel,
        in_specs=[pl.BlockSpec(memory_space=pltpu.SMEM)],
        out_shape=jax.ShapeDtypeStruct((WIN_H, B_HEADS, GRID_W, B_KEYS), F32),
        name="bias_table",
    )(rel_bias.reshape(-1)).reshape(WIN_H, B_HEADS * GRID_W, B_KEYS)


def _attn_b_kernel(q_ref, kp_ref, kc_ref, kn_ref, vp_ref, vc_ref, vn_ref, bias_ref, g_ref, o_ref,
                   kbuf, vbuf, acc_ref, *, rows):
    gidx = pl.program_id(1)
    blk = B_GROUP * GRID_W
    kbuf[0:blk, :] = kp_ref[...]
    kbuf[blk:2 * blk, :] = kc_ref[...]
    kbuf[2 * blk:3 * blk, :] = kn_ref[...]
    vbuf[0:blk, :] = vp_ref[...]
    vbuf[blk:2 * blk, :] = vc_ref[...]
    vbuf[2 * blk:3 * blk, :] = vn_ref[...]
    same_head = _same_head_mask(B_HEADS * GRID_W, GRID_W)
    for j in range(B_GROUP):
        r = gidx * B_GROUP + j
        start = jnp.clip(r - WIN_H // 2, 0, rows - WIN_H)
        pat = r - start
        loc = pl.multiple_of((start - gidx * B_GROUP + B_GROUP) * GRID_W, GRID_W)
        kw = kbuf[pl.ds(loc, B_KEYS), :]
        vw = vbuf[pl.ds(loc, B_KEYS), :]
        qj = q_ref[j * GRID_W:(j + 1) * GRID_W, :]
        q = jnp.where(same_head, jnp.concatenate([qj] * B_HEADS, axis=0), jnp.zeros((), BF16))
        s = _nt_dot(q, kw) + bias_ref[pat]
        m = jnp.max(s, axis=-1, keepdims=True)
        p = jnp.exp(s - m)
        den = jnp.sum(p, axis=-1, keepdims=True)
        o = jnp.where(same_head, _dot(p.astype(BF16), vw) / den, 0.0)
        acc_ref[j * GRID_W:(j + 1) * GRID_W, :] = sum(o[h * GRID_W:(h + 1) * GRID_W, :] for h in range(B_HEADS))
    o_ref[...] = _rms(acc_ref[...], g_ref[...]).astype(BF16)


def _attn_b(bq, bk, bv, bias, g, bsz, seq):
    rows = seq // GRID_W
    assert rows % B_GROUP == 0 and rows >= WIN_H
    ng = rows // B_GROUP
    blk = B_GROUP * GRID_W
    cur = pl.BlockSpec((blk, B_WIDTH), lambda b, i: (b * ng + i, 0))
    prev = pl.BlockSpec((blk, B_WIDTH), lambda b, i: (b * ng + jnp.maximum(i - 1, 0), 0))
    nxt = pl.BlockSpec((blk, B_WIDTH), lambda b, i: (b * ng + jnp.minimum(i + 1, ng - 1), 0))
    return pl.pallas_call(
        functools.partial(_attn_b_kernel, rows=rows),
        grid=(bsz, ng),
        in_specs=[cur, prev, cur, nxt, prev, cur, nxt,
                  pl.BlockSpec(bias.shape, lambda b, i: (0, 0, 0)),
                  pl.BlockSpec((1, B_WIDTH), lambda b, i: (0, 0))],
        out_specs=cur,
        out_shape=jax.ShapeDtypeStruct((bsz * seq, B_WIDTH), BF16),
        scratch_shapes=[pltpu.VMEM((3 * blk, B_WIDTH), BF16), pltpu.VMEM((3 * blk, B_WIDTH), BF16),
                        pltpu.VMEM((blk, B_WIDTH), F32)],
        compiler_params=_cparams(("parallel", "parallel")),
        name="attn_b",
    )(bq, bk, bk, bk, bv, bv, bv, bias, g)


def _gla_dir(q_ref, k_ref, v_ref, la_ref, o_ref, st_ref, reverse):
    tb = q_ref.shape[0]
    n_chunks = tb // C_CHUNK
    bi = lax.broadcasted_iota(jnp.int32, (tb, tb), 0)
    bj = lax.broadcasted_iota(jnp.int32, (tb, tb), 1)
    in_chunk = (bi // C_CHUNK) == (bj // C_CHUNK)
    cum_w = (in_chunk & ((bj >= bi) if reverse else (bj <= bi))).astype(BF16)
    la_hi, la_lo = _split_bf16(la_ref[...])
    b_all = _dot(cum_w, la_hi) + _dot(cum_w, la_lo)
    same_head = _same_head_mask(C_HEADS * C_CHUNK, C_CHUNK)
    ti = lax.broadcasted_iota(jnp.int32, (C_CHUNK, C_HEADS * C_CHUNK), 0)
    si = lax.broadcasted_iota(jnp.int32, (C_CHUNK, C_HEADS * C_CHUNK), 1) % C_CHUNK
    tri = (si >= ti) if reverse else (si <= ti)
    zero = jnp.zeros((), BF16)
    order = range(n_chunks - 1, -1, -1) if reverse else range(n_chunks)
    for c in order:
        rows = slice(c * C_CHUNK, (c + 1) * C_CHUNK)
        b = b_all[rows, :]
        b_tot = b[0:1, :] if reverse else b[C_CHUNK - 1:C_CHUNK, :]
        q_i = (q_ref[rows, :] * jnp.exp(b)).astype(BF16)
        kf = k_ref[rows, :]
        k_i = (kf * jnp.exp(-b)).astype(BF16)
        k_e = (kf * jnp.exp(b_tot - b)).astype(BF16)
        decay = jnp.exp(b_tot)
        v = v_ref[rows, :]
        k_bd = jnp.where(same_head, jnp.concatenate([k_i] * C_HEADS, axis=0), zero)
        v_bd = jnp.where(same_head, jnp.concatenate([v] * C_HEADS, axis=0), zero)
        a = jnp.where(tri, _nt_dot(q_i, k_bd), 0.0)
        st = st_ref[...]
        o_ref[rows, :] = _dot(a.astype(BF16), v_bd) + _nt_dot(q_i, st.astype(BF16))
        st_ref[...] = st * decay + jnp.where(same_head, _tn_dot(v, k_e), 0.0)


def _gla_kernel(qf_ref, kf_ref, vf_ref, laf_ref, qb_ref, kb_ref, vb_ref, lab_ref, of_ref, ob_ref, sf_ref, sb_ref):
    @pl.when(pl.program_id(1) == 0)
    def _():
        sf_ref[...] = jnp.zeros_like(sf_ref)
        sb_ref[...] = jnp.zeros_like(sb_ref)

    _gla_dir(qf_ref, kf_ref, vf_ref, laf_ref, of_ref, sf_ref, reverse=False)
    _gla_dir(qb_ref, kb_ref, vb_ref, lab_ref, ob_ref, sb_ref, reverse=True)


def _gla(cq, ck, cv, la, bsz, seq):
    tb = min(GLA_BLOCK, seq)
    nb = seq // tb
    fwd = lambda col: pl.BlockSpec((tb, C_WIDTH), lambda b, i: (b * nb + i, col))
    bwd = lambda col: pl.BlockSpec((tb, C_WIDTH), lambda b, i: (b * nb + nb - 1 - i, col))
    out = jax.ShapeDtypeStruct((bsz * seq, C_WIDTH), F32)
    state = pltpu.VMEM((C_WIDTH, C_WIDTH), F32)
    return pl.pallas_call(
        _gla_kernel,
        grid=(bsz, nb),
        in_specs=[fwd(0), fwd(0), fwd(0), fwd(0), bwd(0), bwd(0), bwd(0), bwd(1)],
        out_specs=[fwd(0), bwd(0)],
        out_shape=[out, out],
        scratch_shapes=[state, state],
        compiler_params=_cparams(("parallel", "arbitrary")),
        name="gla",
    )(cq, ck, cv, la, cq, ck, cv, la)


def _out_proj_kernel(x_ref, oa_ref, ob_ref, of_ref, obw_ref, cg_ref, cng_ref, wo_ref, g2_ref, wr_hi_ref, wr_lo_ref,
                     x1_ref, h2_ref, aff_ref):
    o = of_ref[...] + obw_ref[...]
    gi = lax.broadcasted_iota(jnp.int32, (C_WIDTH, C_WIDTH), 0) // HEAD_DIM
    gj = lax.broadcasted_iota(jnp.int32, (C_WIDTH, C_WIDTH), 1) // HEAD_DIM
    ones_bd = (gi == gj).astype(BF16)
    sq_hi, sq_lo = _split_bf16(o * o)
    ms = (_dot(sq_hi, ones_bd) + _dot(sq_lo, ones_bd)) * (1.0 / HEAD_DIM)
    cg = cg_ref[...]
    oc = (o * lax.rsqrt(ms + EPS) * cng_ref[...]) * (cg / (1.0 + jnp.exp(-cg)))
    y = _dot(oa_ref[...], wo_ref[0:A_WIDTH, :])
    y += _dot(ob_ref[...], wo_ref[A_WIDTH:A_WIDTH + B_WIDTH, :])
    y += _dot(oc.astype(BF16), wo_ref[A_WIDTH + B_WIDTH:, :])
    x1 = x_ref[...] + y
    x1_ref[...] = x1
    h2 = _rms(x1, g2_ref[...])
    h2_ref[...] = h2.astype(BF16)
    h_hi, h_lo = _split_bf16(h2)
    logits = _nt_dot(wr_hi_ref[...], h_hi) + _nt_dot(wr_hi_ref[...], h_lo) + _nt_dot(wr_lo_ref[...], h_hi)
    m = jnp.max(logits, axis=0, keepdims=True)
    p = jnp.exp(logits - m)
    aff_ref[...] = p / jnp.sum(p, axis=0, keepdims=True)


def _out_proj(x, oa, ob, o_f, o_b, cg, cng, wo_bf, g2, wr_hi, wr_lo):
    n = x.shape[0]
    tm = min(IN_PROJ_ROWS, n)
    row = lambda width: pl.BlockSpec((tm, width), lambda i: (i, 0))
    full = lambda a: pl.BlockSpec(a.shape, lambda i: (0,) * a.ndim)
    return pl.pallas_call(
        _out_proj_kernel,
        grid=(n // tm,),
        in_specs=[row(D_MODEL), row(A_WIDTH), row(B_WIDTH), row(C_WIDTH), row(C_WIDTH), row(C_WIDTH),
                  full(cng), full(wo_bf), full(g2), full(wr_hi), full(wr_lo)],
        out_specs=[row(D_MODEL), row(D_MODEL), pl.BlockSpec((N_EXPERTS, tm), lambda i: (0, i))],
        out_shape=[jax.ShapeDtypeStruct((n, D_MODEL), F32), jax.ShapeDtypeStruct((n, D_MODEL), BF16),
                   jax.ShapeDtypeStruct((N_EXPERTS, n), F32)],
        compiler_params=_cparams(("parallel",)),
        name="out_proj",
    )(x, oa, ob, o_f, o_b, cg, cng, wo_bf, g2, wr_hi, wr_lo)


def _route_kernel(aff_ref, pos_ref, offs_ref, *, cap):
    n = aff_ref.shape[1]
    nt = n // ROUTE_TILE
    bits = pltpu.bitcast(aff_ref[...], jnp.int32)

    def search(it, thr):
        cand = thr | jnp.left_shift(jnp.int32(1), 30 - it)
        cnt = jnp.sum(jnp.where(bits >= cand, 1.0, 0.0), axis=1, keepdims=True)
        return jnp.where(cnt >= cap, cand, thr)

    thr = lax.fori_loop(0, 31, search, jnp.zeros((N_EXPERTS, 1), jnp.int32))
    n_gt = jnp.sum(jnp.where(bits > thr, 1.0, 0.0), axis=1, keepdims=True)
    n_tie = cap - n_gt

    ui = lax.broadcasted_iota(jnp.int32, (ROUTE_TILE, ROUTE_TILE), 0)
    uj = lax.broadcasted_iota(jnp.int32, (ROUTE_TILE, ROUTE_TILE), 1)
    upper = (ui <= uj).astype(BF16)
    lane = lax.broadcasted_iota(jnp.int32, offs_ref.shape, 1)

    def tile(i, carry):
        c_gt, c_eq, offs = carry
        col = pl.multiple_of(i * ROUTE_TILE, ROUTE_TILE)
        b = pltpu.bitcast(aff_ref[:, pl.ds(col, ROUTE_TILE)], jnp.int32)
        gt = jnp.where(b > thr, 1.0, 0.0)
        eq = jnp.where(b == thr, 1.0, 0.0)
        inc = _dot(jnp.concatenate([gt, eq], axis=0).astype(BF16), upper)
        gt_before = c_gt + inc[:N_EXPERTS] - gt
        eq_before = c_eq + inc[N_EXPERTS:] - eq
        sel = (gt > 0.0) | ((eq > 0.0) & (eq_before < n_tie))
        pos = gt_before + jnp.minimum(eq_before, n_tie)
        pos_ref[:, pl.ds(col, ROUTE_TILE)] = jnp.where(sel, pos, -1.0).astype(jnp.int32)
        start = c_gt + jnp.minimum(c_eq, n_tie)
        offs = jnp.where(lane == i, start.astype(jnp.int32), offs)
        return (c_gt + jnp.sum(gt, axis=1, keepdims=True), c_eq + jnp.sum(eq, axis=1, keepdims=True), offs)

    zero = jnp.zeros((N_EXPERTS, 1), F32)
    offs = jnp.where(lane == nt, cap, 0).astype(jnp.int32)
    _, _, offs = lax.fori_loop(0, nt, tile, (zero, zero, offs))
    offs_ref[...] = offs


def _route(aff_t, cap):
    n = aff_t.shape[1]
    nt = n // ROUTE_TILE
    return pl.pallas_call(
        functools.partial(_route_kernel, cap=cap),
        out_shape=[jax.ShapeDtypeStruct((N_EXPERTS, n), jnp.int32),
                   jax.ShapeDtypeStruct((N_EXPERTS, nt + 1), jnp.int32)],
        compiler_params=pltpu.CompilerParams(vmem_limit_bytes=VMEM_LIMIT),
        name="route",
    )(aff_t)


def _align(v):
    return (v // MOE_ALIGN) * MOE_ALIGN


def _moe_passes(offs_ref, t, subs, x0, s):
    k = jnp.int32(1)
    for g in range(MOE_PAIR):
        need = offs_ref[x0 + g, t * subs + s + 1] - _align(offs_ref[x0 + g, t * subs + s])
        k = jnp.maximum(k, (need + MOE_WIN - 1) // MOE_WIN)
    return k


def _moe_windows(offs_ref, pos_ref, t, subs, x0, s, j):
    slot = lax.broadcasted_iota(jnp.int32, (MOE_WIN, ROUTE_TILE), 0)
    hits, rels = [], []
    for g in range(MOE_PAIR):
        x = x0 + g
        ws = _align(offs_ref[x, t * subs + s]) + j * MOE_WIN
        pos = pos_ref[pl.ds(x, 1), s * ROUTE_TILE:(s + 1) * ROUTE_TILE]
        hits.append((pos - ws) == slot)
        rels.append(pl.multiple_of(ws - _align(offs_ref[x, t * subs]), MOE_ALIGN))
    onehot = jnp.concatenate([jnp.where(h, 1.0, 0.0) for h in hits], axis=0).astype(BF16)
    return hits, rels, onehot


def _moe_chunks(offs_ref, t, subs, x):
    base = _align(offs_ref[x, t * subs])
    return base, (offs_ref[x, (t + 1) * subs] - base + MOE_CHUNK - 1) // MOE_CHUNK


def _moe_gather_kernel(offs_ref, h_ref, pos_ref, gate_ref, xe_ref, stage_ref, carry_ref, sem):
    t = pl.program_id(0)
    steps = pl.num_programs(1)
    e0 = pl.program_id(1) * MOE_STEP
    tt = h_ref.shape[0]
    subs = tt // ROUTE_TILE
    k = t * steps + pl.program_id(1)
    slot = k % 2
    x_ref = stage_ref.at[slot]

    def off(x, s):
        return offs_ref[x, t * subs + s]

    def copies(kk, half, act):
        step_t, step_q = kk // steps, kk % steps
        for g in range(MOE_STEP):
            x = step_q * MOE_STEP + g
            base, n_chunks = _moe_chunks(offs_ref, step_t, subs, x)

            def body(c, carry):
                r0 = pl.multiple_of(c * MOE_CHUNK, MOE_CHUNK)
                act(pltpu.make_async_copy(stage_ref.at[half, g, pl.ds(r0, MOE_CHUNK), :],
                                          xe_ref.at[x, pl.ds(pl.multiple_of(base + r0, MOE_ALIGN), MOE_CHUNK), :],
                                          sem.at[half]))
                return carry

            lax.fori_loop(0, n_chunks, body, 0)

    @pl.when(k >= 2)
    def _():
        copies(k - 2, slot, lambda cp: cp.wait())

    @pl.when(t == 0)
    def _():
        carry_ref[pl.ds(e0, MOE_STEP)] = jnp.zeros((MOE_STEP, MOE_ALIGN, XE_WIDTH), BF16)
        cap = xe_ref.shape[1] - MOE_CHUNK
        x_ref[0, 0:MOE_CHUNK, :] = jnp.zeros((MOE_CHUNK, XE_WIDTH), BF16)
        pads = [pltpu.make_async_copy(x_ref.at[0, 0:MOE_CHUNK, :], xe_ref.at[e0 + g, cap:cap + MOE_CHUNK, :],
                                      sem.at[slot]) for g in range(MOE_STEP)]
        for p in pads:
            p.start()
        for p in pads:
            p.wait()

    for g in range(MOE_STEP):
        x_ref[g, 0:MOE_ALIGN, :] = carry_ref[e0 + g]

    row = lax.broadcasted_iota(jnp.int32, (MOE_WIN, 1), 0)
    lane = lax.broadcasted_iota(jnp.int32, (MOE_WIN, XE_WIDTH - D_MODEL), 1)

    def gather_pass(pair, s, j):
        x0 = e0 + pair * MOE_PAIR
        cols = slice(s * ROUTE_TILE, (s + 1) * ROUTE_TILE)
        hits, rels, onehot = _moe_windows(offs_ref, pos_ref, t, subs, x0, s, j)
        r = _dot(onehot, h_ref[cols, :])
        for g in range(MOE_PAIR):
            gate = jnp.sum(jnp.where(hits[g], gate_ref[pl.ds(x0 + g, 1), cols], 0.0), axis=1, keepdims=True)
            g_hi = gate.astype(BF16).astype(F32)
            extra = jnp.where(lane == 0, g_hi, jnp.where(lane == 1, gate - g_hi, 0.0))
            new = jnp.concatenate([r[g * MOE_WIN:(g + 1) * MOE_WIN, :], extra], axis=1)
            ws = _align(off(x0 + g, s)) + j * MOE_WIN
            own = row >= off(x0 + g, s) - ws
            buf = pair * MOE_PAIR + g
            if isinstance(j, int):
                head = pl.ds(rels[g], MOE_ALIGN)
                x_ref[buf, head, :] = jnp.where(own[:MOE_ALIGN], new[:MOE_ALIGN],
                                                x_ref[buf, head, :].astype(F32)).astype(BF16)
                x_ref[buf, pl.ds(rels[g] + MOE_ALIGN, MOE_WIN - MOE_ALIGN), :] = new[MOE_ALIGN:].astype(BF16)
            else:
                own = own & (row < off(x0 + g, s + 1) - ws)
                win = pl.ds(rels[g], MOE_WIN)
                x_ref[buf, win, :] = jnp.where(own, new, x_ref[buf, win, :].astype(F32)).astype(BF16)

    for pair in range(MOE_STEP // MOE_PAIR):
        for s in range(subs):
            gather_pass(pair, s, 0)
    for pair in range(MOE_STEP // MOE_PAIR):
        for s in range(subs):
            lax.fori_loop(1, _moe_passes(offs_ref, t, subs, e0 + pair * MOE_PAIR, s),
                          lambda j, carry: (gather_pass(pair, s, j), carry)[1], 0)

    for g in range(MOE_STEP):
        x = e0 + g
        base, n_chunks = _moe_chunks(offs_ref, t, subs, x)
        last_group = pl.multiple_of(_align(off(x, subs)) - base, MOE_ALIGN)
        tail = x_ref[g, pl.ds(last_group, MOE_ALIGN), :].astype(F32)
        tail = jnp.where(row[:MOE_ALIGN] < off(x, subs) - base - last_group, tail, 0.0).astype(BF16)
        x_ref[g, pl.ds(last_group, MOE_ALIGN), :] = tail
        carry_ref[x] = tail

        def fill(i, carry):
            x_ref[g, pl.ds(pl.multiple_of(i * MOE_ALIGN, MOE_ALIGN), MOE_ALIGN), :] = jnp.zeros((MOE_ALIGN, XE_WIDTH), BF16)
            return carry

        lax.fori_loop(last_group // MOE_ALIGN + 1, n_chunks * MOE_CHUNK // MOE_ALIGN, fill, 0)

    copies(k, slot, lambda cp: cp.start())

    @pl.when(k == pl.num_programs(0) * steps - 1)
    def _():
        copies(k - 1, 1 - slot, lambda cp: cp.wait())
        copies(k, slot, lambda cp: cp.wait())


def _moe_ffn_kernel(x_ref, wg_ref, wu_ref, wd_ref, y_ref, wg_bf, wu_bf, wd_bf):
    j = pl.program_id(1)
    last = pl.num_programs(1) - 1

    @pl.when(j == 0)
    def _():
        wg_bf[...] = wg_ref[0].astype(BF16)
        wu_bf[...] = wu_ref[0].astype(BF16)
        wd_bf[...] = wd_ref[0].astype(BF16)

    @pl.when(j < last)
    def _():
        x = x_ref[0, :, 0:D_MODEL]
        gate = jnp.sum(x_ref[0, :, D_MODEL:XE_WIDTH].astype(F32), axis=1, keepdims=True)
        hg = _dot(x, wg_bf[...])
        hu = _dot(x, wu_bf[...])
        act = (hg / (1.0 + jnp.exp(-hg))) * hu * gate
        y_ref[0] = _dot(act.astype(BF16), wd_bf[...]).astype(BF16)

    @pl.when(j == last)
    def _():
        y_ref[0] = jnp.zeros(y_ref.shape[1:], BF16)


def _moe_combine_kernel(offs_ref, x1_ref, pos_ref, y_ref, fg_ref, o_ref, y_buf, sem, *, final_norm):
    t = pl.program_id(0)
    q = pl.program_id(1)
    steps = pl.num_programs(1)
    e0 = q * MOE_STEP
    tt = x1_ref.shape[0]
    subs = tt // ROUTE_TILE
    k = t * steps + q
    slot = k % 2

    def fetch(kk, half, act):
        step_t, step_q = kk // steps, kk % steps
        for g in range(MOE_STEP):
            x = step_q * MOE_STEP + g
            base, n_chunks = _moe_chunks(offs_ref, step_t, subs, x)

            def body(c, carry):
                r0 = pl.multiple_of(c * MOE_CHUNK, MOE_CHUNK)
                act(pltpu.make_async_copy(y_ref.at[x, pl.ds(pl.multiple_of(base + r0, MOE_ALIGN), MOE_CHUNK), :],
                                          y_buf.at[half, g, pl.ds(r0, MOE_CHUNK), :], sem.at[half]))
                return carry

            lax.fori_loop(0, n_chunks, body, 0)

    @pl.when(k == 0)
    def _():
        y_buf[...] = jnp.zeros_like(y_buf)
        fetch(k, slot, lambda cp: cp.start())

    @pl.when(k + 1 < pl.num_programs(0) * steps)
    def _():
        fetch(k + 1, 1 - slot, lambda cp: cp.start())

    fetch(k, slot, lambda cp: cp.wait())

    @pl.when(q == 0)
    def _():
        o_ref[...] = x1_ref[...]

    def combine_pass(pair, s, j):
        cols = slice(s * ROUTE_TILE, (s + 1) * ROUTE_TILE)
        _, rels, onehot = _moe_windows(offs_ref, pos_ref, t, subs, e0 + pair * MOE_PAIR, s, j)
        ycat = jnp.concatenate([y_buf[slot, pair * MOE_PAIR + g, pl.ds(rels[g], MOE_WIN), :]
                                for g in range(MOE_PAIR)], axis=0)
        o_ref[cols, :] += _tn_dot(onehot, ycat)

    for pair in range(MOE_STEP // MOE_PAIR):
        for s in range(subs):
            combine_pass(pair, s, 0)
    for pair in range(MOE_STEP // MOE_PAIR):
        for s in range(subs):
            lax.fori_loop(1, _moe_passes(offs_ref, t, subs, e0 + pair * MOE_PAIR, s),
                          lambda j, carry: (combine_pass(pair, s, j), carry)[1], 0)

    if final_norm:
        @pl.when(q == pl.num_programs(1) - 1)
        def _():
            o_ref[...] = _rms(o_ref[...], fg_ref[...])


def _moe(offs, h2, x1, pos, gate, wg, wu, wd, final_g, final_norm, cap):
    n = h2.shape[0]
    tt, tc = min(MOE_TILE, n), min(MOE_COMBINE_TILE, n)
    assert cap % MOE_CHUNK == 0 and n % tt == 0 and n % tc == 0
    steps = N_EXPERTS // MOE_STEP
    stage_rows = lambda tile: tile + MOE_ALIGN + 3 * MOE_WIN
    tok = lambda tile: pl.BlockSpec((tile, D_MODEL), lambda t, q, offs: (t, 0))
    per_tok = lambda tile: pl.BlockSpec((N_EXPERTS, tile), lambda t, q, offs: (0, t))
    hbm = pl.BlockSpec(memory_space=pl.ANY)
    sems = ("arbitrary", "arbitrary")

    xe = pl.pallas_call(
        _moe_gather_kernel,
        grid_spec=pltpu.PrefetchScalarGridSpec(
            num_scalar_prefetch=1, grid=(n // tt, steps),
            in_specs=[tok(tt), per_tok(tt), per_tok(tt)], out_specs=hbm,
            scratch_shapes=[pltpu.VMEM((2, MOE_STEP, stage_rows(tt), XE_WIDTH), BF16),
                            pltpu.VMEM((N_EXPERTS, MOE_ALIGN, XE_WIDTH), BF16), pltpu.SemaphoreType.DMA((2,))],
        ),
        out_shape=jax.ShapeDtypeStruct((N_EXPERTS, cap + MOE_CHUNK, XE_WIDTH), BF16),
        compiler_params=_cparams(sems),
        name="moe_gather",
    )(offs, h2, pos, gate)

    fb = min(MOE_FFN_ROWS, cap)
    nb = cap // fb
    wspec = pl.BlockSpec((1, D_MODEL, D_MODEL), lambda e, j: (e, 0, 0))
    w_bf = pltpu.VMEM((D_MODEL, D_MODEL), BF16)
    y = pl.pallas_call(
        _moe_ffn_kernel,
        grid=(N_EXPERTS, nb + 1),
        in_specs=[pl.BlockSpec((1, fb, XE_WIDTH), lambda e, j: (e, jnp.minimum(j, nb - 1), 0)), wspec, wspec, wspec],
        out_specs=pl.BlockSpec((1, fb, D_MODEL), lambda e, j: (e, j, 0)),
        out_shape=jax.ShapeDtypeStruct((N_EXPERTS, cap + fb, D_MODEL), BF16),
        scratch_shapes=[w_bf, w_bf, w_bf],
        compiler_params=_cparams(("parallel", "arbitrary")),
        name="moe_ffn",
    )(xe, wg, wu, wd)

    return pl.pallas_call(
        functools.partial(_moe_combine_kernel, final_norm=final_norm),
        grid_spec=pltpu.PrefetchScalarGridSpec(
            num_scalar_prefetch=1, grid=(n // tc, steps),
            in_specs=[tok(tc), per_tok(tc), hbm, pl.BlockSpec((1, D_MODEL), lambda t, q, offs: (0, 0))],
            out_specs=tok(tc),
            scratch_shapes=[pltpu.VMEM((2, MOE_STEP, stage_rows(tc), D_MODEL), BF16),
                            pltpu.SemaphoreType.DMA((2,))],
        ),
        out_shape=jax.ShapeDtypeStruct((n, D_MODEL), F32),
        compiler_params=_cparams(sems),
        name="moe_combine",
    )(offs, x1, pos, y, final_g)


def _prep_layer(l, norm1_g, w_in, a_sink, a_norm_g, b_rel_bias, b_norm_g, c_alpha_w2_f, c_alpha_b_f,
                c_alpha_w2_b, c_alpha_b_b, c_norm_g, w_out, norm2_g, w_router, w_gate, w_up, w_down):
    z = jnp.zeros((C_LOWRANK, C_WIDTH), F32)
    w2 = jnp.concatenate([jnp.concatenate([c_alpha_w2_f[l], z], axis=1),
                          jnp.concatenate([z, c_alpha_w2_b[l]], axis=1)], axis=0)
    b2 = jnp.concatenate([c_alpha_b_f[l], c_alpha_b_b[l]])[None, :]
    wr_t = w_router[l].T
    wr_hi = wr_t.astype(BF16)
    wr_lo = (wr_t - wr_hi.astype(F32)).astype(BF16)
    return dict(
        g1=norm1_g[l][None, :], w_in=w_in[l].astype(BF16), w2=w2, b2=b2,
        sink=a_sink[l], a_g=a_norm_g[l][None, :], bias=_bias_table(b_rel_bias[l]), b_g=b_norm_g[l][None, :],
        c_g=c_norm_g[l][None, :], w_out=w_out[l].astype(BF16), g2=norm2_g[l][None, :],
        wr_hi=wr_hi, wr_lo=wr_lo, wg=w_gate[l], wu=w_up[l], wd=w_down[l])


def _layer(x, p, bsz, seq, final_g, final_norm):
    n = bsz * seq
    aq, ak, av, bq, bk, bv, cq, ck, cv, cg, la = _in_proj(x, p["g1"], p["w_in"], p["w2"], p["b2"], seq)
    oa = _attn_a(aq, ak, av, p["sink"], p["a_g"], bsz, seq)
    ob = _attn_b(bq, bk, bv, p["bias"], p["b_g"], bsz, seq)
    o_f, o_b = _gla(cq, ck, cv, la, bsz, seq)
    x1, h2, aff = _out_proj(x, oa, ob, o_f, o_b, cg, p["c_g"], p["w_out"], p["g2"], p["wr_hi"], p["wr_lo"])
    cap = EC_CAPACITY * n // N_EXPERTS
    pos, offs = _route(aff, cap)
    return _moe(offs, h2, x1, pos, aff, p["wg"], p["wu"], p["wd"], final_g, final_norm, cap)


def _trunk(x, layers, final_g):
    bsz, seq, _ = x.shape
    y = x.reshape(bsz * seq, D_MODEL)
    for l, p in enumerate(layers):
        y = _layer(y, p, bsz, seq, final_g, final_norm=(l == len(layers) - 1))
    return y.reshape(bsz, seq, D_MODEL)


def kernel(x_prompt, x_sample, norm1_g, w_in, a_sink, a_norm_g, b_rel_bias, b_norm_g, c_alpha_w2_f, c_alpha_b_f,
           c_alpha_w2_b, c_alpha_b_b, c_norm_g, w_out, norm2_g, w_router, w_gate, w_up, w_down, final_g):
    depth = w_in.shape[0]
    layers = [_prep_layer(l, norm1_g, w_in, a_sink, a_norm_g, b_rel_bias, b_norm_g, c_alpha_w2_f, c_alpha_b_f,
                          c_alpha_w2_b, c_alpha_b_b, c_norm_g, w_out, norm2_g, w_router, w_gate, w_up, w_down)
              for l in range(depth)]
    fg = final_g[None, :]
    return _trunk(x_prompt, layers, fg), _trunk(x_sample, layers, fg)
```

```python
import functools

import jax
import jax.numpy as jnp
from jax import lax
from jax.experimental import pallas as pl
from jax.experimental.pallas import tpu as pltpu

F32 = jnp.float32
BF16 = jnp.bfloat16

D_MODEL = 1024
HEAD_DIM = 64
A_WIDTH = 512
A_HEADS = 8
A_KV_HEADS = 2
A_KV_WIDTH = 128
WINDOW = 128
ROPE_THETA = 500000.0
ROPE_DIM = 16
B_WIDTH = 256
B_HEADS = 4
GRID_W = 64
WIN_H = 8
WIN_W = 16
C_WIDTH = 256
C_HEADS = 4
C_LOWRANK = 16
C_TAU = 16.0
C_CHUNK = 64
N_EXPERTS = 16
EC_CAPACITY = 2
EPS = 1e-6
NEG_INF = -1e30
IN_WIDTH = 2592

_OFF_AQ, _OFF_AK, _OFF_AV = 0, 512, 640
_OFF_BQ, _OFF_BK, _OFF_BV = 768, 1024, 1280
_OFF_CQ, _OFF_CK, _OFF_CV, _OFF_CG = 1536, 1792, 2048, 2304
_OFF_LR = 2560

LANES = 128
VMEM_LIMIT = 56 * 1024 * 1024

IN_PROJ_ROWS = 512
GLA_BLOCK = 512
ROUTE_TILE = 256
MOE_TILE = 2048
MOE_COMBINE_TILE = 1024
MOE_STEP = 4
MOE_PAIR = 4
MOE_WIN = 128
MOE_ALIGN = 16
MOE_CHUNK = 128
MOE_FFN_ROWS = 1024
XE_WIDTH = D_MODEL + LANES


def _cparams(sem):
    return pltpu.CompilerParams(dimension_semantics=sem, vmem_limit_bytes=VMEM_LIMIT)


def _nt_dot(a, b):
    return lax.dot_general(a, b, (((1,), (1,)), ((), ())), preferred_element_type=F32)


def _tn_dot(a, b):
    return lax.dot_general(a, b, (((0,), (0,)), ((), ())), preferred_element_type=F32)


def _dot(a, b):
    return jnp.dot(a, b, preferred_element_type=F32)


def _rms(x, g):
    return x * lax.rsqrt(jnp.mean(x * x, axis=-1, keepdims=True) + EPS) * g


def _same_head_mask(n_rows, rows_per_head):
    width = (n_rows // rows_per_head) * HEAD_DIM
    ri = lax.broadcasted_iota(jnp.int32, (n_rows, width), 0) // rows_per_head
    li = lax.broadcasted_iota(jnp.int32, (n_rows, width), 1) // HEAD_DIM
    return ri == li


def _split_bf16(x):
    hi = x.astype(BF16)
    return hi, (x - hi.astype(F32)).astype(BF16)


def _in_proj_kernel(x_ref, g_ref, w_ref, cos_ref, s1_ref, s2_ref, w2_ref, b2_ref,
                    aq_ref, ak_ref, av_ref, bq_ref, bk_ref, bv_ref,
                    cq_ref, ck_ref, cv_ref, cg_ref, la_ref):
    h = _rms(x_ref[...], g_ref[...]).astype(BF16)
    y = _dot(h, w_ref[...])

    def proj(lo, width):
        return y[:, lo:lo + width]

    cos, s1, s2 = cos_ref[...], s1_ref[...], s2_ref[...]

    def rope(t):
        return t * cos + pltpu.roll(t, LANES - ROPE_DIM // 2, 1) * s1 + pltpu.roll(t, ROPE_DIM // 2, 1) * s2

    scale = HEAD_DIM ** -0.5
    aq = proj(_OFF_AQ, A_WIDTH)
    for c in range(A_WIDTH // LANES):
        aq_ref[:, c * LANES:(c + 1) * LANES] = (rope(aq[:, c * LANES:(c + 1) * LANES]) * scale).astype(BF16)
    ak_ref[...] = rope(proj(_OFF_AK, A_KV_WIDTH)).astype(BF16)
    av_ref[...] = proj(_OFF_AV, A_KV_WIDTH).astype(BF16)
    bq_ref[...] = (proj(_OFF_BQ, B_WIDTH) * scale).astype(BF16)
    bk_ref[...] = proj(_OFF_BK, B_WIDTH).astype(BF16)
    bv_ref[...] = proj(_OFF_BV, B_WIDTH).astype(BF16)
    cq_ref[...] = proj(_OFF_CQ, C_WIDTH) * scale
    ck_ref[...] = proj(_OFF_CK, C_WIDTH)
    cv_ref[...] = proj(_OFF_CV, C_WIDTH).astype(BF16)
    cg_ref[...] = proj(_OFF_CG, C_WIDTH)
    lr = proj(_OFF_LR, 2 * C_LOWRANK)
    z = jnp.dot(lr, w2_ref[...], preferred_element_type=F32, precision=lax.Precision.HIGHEST) + b2_ref[...]
    la_ref[...] = (jnp.minimum(z, 0.0) - jnp.log(1.0 + jnp.exp(-jnp.abs(z)))) * (1.0 / C_TAU)


def _rope_tables(seq):
    half = ROPE_DIM // 2
    inv = jnp.power(jnp.float32(ROPE_THETA), -jnp.arange(half, dtype=F32) * (2.0 / ROPE_DIM))
    ang = jnp.arange(seq, dtype=F32)[:, None] * inv[None, :]
    cos, sin = jnp.cos(ang), jnp.sin(ang)
    ones = jnp.ones((seq, HEAD_DIM - ROPE_DIM), F32)
    zeros = jnp.zeros((seq, HEAD_DIM - half), F32)
    c = jnp.concatenate([cos, cos, ones], axis=1)
    s1 = jnp.concatenate([-sin, zeros], axis=1)
    s2 = jnp.concatenate([jnp.zeros((seq, half), F32), sin, jnp.zeros((seq, HEAD_DIM - ROPE_DIM), F32)], axis=1)
    two = lambda t: jnp.concatenate([t, t], axis=1)
    return two(c), two(s1), two(s2)


def _in_proj(x, g1, w_in_bf, layer, w2, b2, seq):
    n = x.shape[0]
    tm = min(IN_PROJ_ROWS, seq)
    per_seq = seq // tm
    cos, s1, s2 = _rope_tables(seq)
    row = lambda width: pl.BlockSpec((tm, width), lambda i: (i, 0))
    full = lambda a: pl.BlockSpec(a.shape, lambda i: (0,) * a.ndim)
    tab = pl.BlockSpec((tm, LANES), lambda i: (i % per_seq, 0))
    widths = [(A_WIDTH, BF16), (A_KV_WIDTH, BF16), (A_KV_WIDTH, BF16), (B_WIDTH, BF16), (B_WIDTH, BF16),
              (B_WIDTH, BF16), (C_WIDTH, F32), (C_WIDTH, F32), (C_WIDTH, BF16), (C_WIDTH, F32), (2 * C_WIDTH, F32)]
    return pl.pallas_call(
        _in_proj_kernel,
        grid=(n // tm,),
        in_specs=[row(D_MODEL), full(g1), pl.BlockSpec((None, D_MODEL, IN_WIDTH), lambda i: (layer, 0, 0)),
                  tab, tab, tab, full(w2), full(b2)],
        out_specs=[row(w) for w, _ in widths],
        out_shape=[jax.ShapeDtypeStruct((n, w), dt) for w, dt in widths],
        compiler_params=_cparams(("parallel",)),
        name="in_proj",
    )(x, g1, w_in_bf, cos, s1, s2, w2, b2)


A_QBLOCKS = 2


def _attn_a_kernel(sink_ref, q_ref, kp_ref, kc_ref, kn_ref, vp_ref, vc_ref, vn_ref, g_ref, o_ref, acc_ref, *, steps):
    i = pl.program_id(1)
    grp = A_HEADS // A_KV_HEADS
    k = jnp.concatenate([kp_ref[...], kc_ref[...], kn_ref[...]], axis=0)
    v = jnp.concatenate([vp_ref[...], vc_ref[...], vn_ref[...]], axis=0)
    qq = lax.broadcasted_iota(jnp.int32, (WINDOW, 3 * WINDOW), 0)
    kk = lax.broadcasted_iota(jnp.int32, (WINDOW, 3 * WINDOW), 1)
    band = jnp.abs(kk - WINDOW - qq) <= WINDOW
    head_row = lax.broadcasted_iota(jnp.int32, (grp * WINDOW, 1), 0) // WINDOW
    for blk in range(A_QBLOCKS):
        visible = band
        if blk == 0:
            visible = visible & jnp.logical_not((kk < WINDOW) & (i == 0))
        if blk == A_QBLOCKS - 1:
            visible = visible & jnp.logical_not((kk >= 2 * WINDOW) & (i == steps - 1))
        bias = jnp.concatenate([jnp.where(visible, 0.0, NEG_INF)] * grp, axis=0)
        rows = slice(blk * WINDOW, (blk + 1) * WINDOW)
        keys = slice(blk * WINDOW, (blk + 3) * WINDOW)
        for kv in range(A_KV_HEADS):
            heads = range(kv * grp, (kv + 1) * grp)
            q = jnp.concatenate([q_ref[rows, h * HEAD_DIM:(h + 1) * HEAD_DIM] for h in heads], axis=0)
            s = _nt_dot(q, k[keys, kv * HEAD_DIM:(kv + 1) * HEAD_DIM]) + bias
            sink = jnp.zeros((grp * WINDOW, 1), F32)
            for g_i, h in enumerate(heads):
                sink = jnp.where(head_row == g_i, sink_ref[h], sink)
            m = jnp.maximum(jnp.max(s, axis=-1, keepdims=True), sink)
            p = jnp.exp(s - m)
            den = jnp.sum(p, axis=-1, keepdims=True) + jnp.exp(sink - m)
            o = _dot(p.astype(BF16), v[keys, kv * HEAD_DIM:(kv + 1) * HEAD_DIM]) / den
            for g_i, h in enumerate(heads):
                acc_ref[rows, h * HEAD_DIM:(h + 1) * HEAD_DIM] = o[g_i * WINDOW:(g_i + 1) * WINDOW, :]
    o_ref[...] = _rms(acc_ref[...], g_ref[...]).astype(BF16)


def _attn_a(aq, ak, av, sink, g, bsz, seq):
    nb = seq // WINDOW
    assert nb % A_QBLOCKS == 0
    steps = nb // A_QBLOCKS
    qrows = A_QBLOCKS * WINDOW
    qspec = pl.BlockSpec((qrows, A_WIDTH), lambda b, i: (b * steps + i, 0))
    kprev = pl.BlockSpec((WINDOW, A_KV_WIDTH), lambda b, i: (b * nb + jnp.maximum(i * A_QBLOCKS - 1, 0), 0))
    kcur = pl.BlockSpec((qrows, A_KV_WIDTH), lambda b, i: (b * steps + i, 0))
    knext = pl.BlockSpec((WINDOW, A_KV_WIDTH), lambda b, i: (b * nb + jnp.minimum((i + 1) * A_QBLOCKS, nb - 1), 0))
    return pl.pallas_call(
        functools.partial(_attn_a_kernel, steps=steps),
        grid=(bsz, steps),
        in_specs=[pl.BlockSpec(memory_space=pltpu.SMEM), qspec, kprev, kcur, knext, kprev, kcur, knext,
                  pl.BlockSpec((1, A_WIDTH), lambda b, i: (0, 0))],
        out_specs=qspec,
        out_shape=jax.ShapeDtypeStruct((bsz * seq, A_WIDTH), BF16),
        scratch_shapes=[pltpu.VMEM((qrows, A_WIDTH), F32)],
        compiler_params=_cparams(("parallel", "parallel")),
        name="attn_a",
    )(sink, aq, ak, ak, ak, av, av, av, g)


B_GROUP = 8
B_KEYS = WIN_H * GRID_W


def _bias_table_kernel(rel_ref, o_ref):
    c = lax.broadcasted_iota(jnp.int32, (GRID_W, GRID_W), 0)
    w = lax.broadcasted_iota(jnp.int32, (GRID_W, GRID_W), 1)
    cstart = jnp.clip(c - WIN_W // 2, 0, GRID_W - WIN_W)
    colmask = (w >= cstart) & (w < cstart + WIN_W)
    col_off = jnp.clip(w - c + (WIN_W - 1), 0, 2 * WIN_W - 2)
    n_ro, n_co = 2 * WIN_H - 1, 2 * WIN_W - 1
    for h in range(B_HEADS):
        for ro in range(n_ro):
            def body(j, acc):
                return jnp.where(col_off == j, rel_ref[(h * n_ro + ro) * n_co + j], acc)
            t = lax.fori_loop(0, n_co, body, jnp.zeros((GRID_W, GRID_W), F32))
            t = jnp.where(colmask, t, NEG_INF)
            for p in range(WIN_H):
                kidx = ro - (WIN_H - 1) + p
                if 0 <= kidx < WIN_H:
                    o_ref[p, h, :, kidx * GRID_W:(kidx + 1) * GRID_W] = t


def _bias_table(rel_bias):
    return pl.pallas_call(
        _bias_table_kernel,
        in_specs=[pl.BlockSpec(memory_space=pltpu.SMEM)],
        out_shape=jax.ShapeDtypeStruct((WIN_H, B_HEADS, GRID_W, B_KEYS), F32),
        name="bias_table",
    )(rel_bias.reshape(-1)).reshape(WIN_H, B_HEADS * GRID_W, B_KEYS)


def _attn_b_kernel(q_ref, kp_ref, kc_ref, kn_ref, vp_ref, vc_ref, vn_ref, bias_ref, g_ref, o_ref,
                   kbuf, vbuf, acc_ref, *, rows):
    gidx = pl.program_id(1)
    blk = B_GROUP * GRID_W
    kbuf[0:blk, :] = kp_ref[...]
    kbuf[blk:2 * blk, :] = kc_ref[...]
    kbuf[2 * blk:3 * blk, :] = kn_ref[...]
    vbuf[0:blk, :] = vp_ref[...]
    vbuf[blk:2 * blk, :] = vc_ref[...]
    vbuf[2 * blk:3 * blk, :] = vn_ref[...]
    same_head = _same_head_mask(B_HEADS * GRID_W, GRID_W)
    for j in range(B_GROUP):
        r = gidx * B_GROUP + j
        start = jnp.clip(r - WIN_H // 2, 0, rows - WIN_H)
        pat = r - start
        loc = pl.multiple_of((start - gidx * B_GROUP + B_GROUP) * GRID_W, GRID_W)
        kw = kbuf[pl.ds(loc, B_KEYS), :]
        vw = vbuf[pl.ds(loc, B_KEYS), :]
        qj = q_ref[j * GRID_W:(j + 1) * GRID_W, :]
        q = jnp.where(same_head, jnp.concatenate([qj] * B_HEADS, axis=0), jnp.zeros((), BF16))
        s = _nt_dot(q, kw) + bias_ref[pat]
        m = jnp.max(s, axis=-1, keepdims=True)
        p = jnp.exp(s - m)
        den = jnp.sum(p, axis=-1, keepdims=True)
        o = jnp.where(same_head, _dot(p.astype(BF16), vw) / den, 0.0)
        acc_ref[j * GRID_W:(j + 1) * GRID_W, :] = sum(o[h * GRID_W:(h + 1) * GRID_W, :] for h in range(B_HEADS))
    o_ref[...] = _rms(acc_ref[...], g_ref[...]).astype(BF16)


def _attn_b(bq, bk, bv, bias, g, bsz, seq):
    rows = seq // GRID_W
    assert rows % B_GROUP == 0 and rows >= WIN_H
    ng = rows // B_GROUP
    blk = B_GROUP * GRID_W
    cur = pl.BlockSpec((blk, B_WIDTH), lambda b, i: (b * ng + i, 0))
    prev = pl.BlockSpec((blk, B_WIDTH), lambda b, i: (b * ng + jnp.maximum(i - 1, 0), 0))
    nxt = pl.BlockSpec((blk, B_WIDTH), lambda b, i: (b * ng + jnp.minimum(i + 1, ng - 1), 0))
    return pl.pallas_call(
        functools.partial(_attn_b_kernel, rows=rows),
        grid=(bsz, ng),
        in_specs=[cur, prev, cur, nxt, prev, cur, nxt,
                  pl.BlockSpec(bias.shape, lambda b, i: (0, 0, 0)),
                  pl.BlockSpec((1, B_WIDTH), lambda b, i: (0, 0))],
        out_specs=cur,
        out_shape=jax.ShapeDtypeStruct((bsz * seq, B_WIDTH), BF16),
        scratch_shapes=[pltpu.VMEM((3 * blk, B_WIDTH), BF16), pltpu.VMEM((3 * blk, B_WIDTH), BF16),
                        pltpu.VMEM((blk, B_WIDTH), F32)],
        compiler_params=_cparams(("parallel", "parallel")),
        name="attn_b",
    )(bq, bk, bk, bk, bv, bv, bv, bias, g)


def _gla_dir(q_ref, k_ref, v_ref, la_ref, o_ref, st_ref, reverse):
    tb = q_ref.shape[0]
    n_chunks = tb // C_CHUNK
    bi = lax.broadcasted_iota(jnp.int32, (tb, tb), 0)
    bj = lax.broadcasted_iota(jnp.int32, (tb, tb), 1)
    in_chunk = (bi // C_CHUNK) == (bj // C_CHUNK)
    cum_w = (in_chunk & ((bj >= bi) if reverse else (bj <= bi))).astype(BF16)
    la_hi, la_lo = _split_bf16(la_ref[...])
    b_all = _dot(cum_w, la_hi) + _dot(cum_w, la_lo)
    same_head = _same_head_mask(C_HEADS * C_CHUNK, C_CHUNK)
    ti = lax.broadcasted_iota(jnp.int32, (C_CHUNK, C_HEADS * C_CHUNK), 0)
    si = lax.broadcasted_iota(jnp.int32, (C_CHUNK, C_HEADS * C_CHUNK), 1) % C_CHUNK
    tri = (si >= ti) if reverse else (si <= ti)
    zero = jnp.zeros((), BF16)
    order = range(n_chunks - 1, -1, -1) if reverse else range(n_chunks)
    for c in order:
        rows = slice(c * C_CHUNK, (c + 1) * C_CHUNK)
        b = b_all[rows, :]
        b_tot = b[0:1, :] if reverse else b[C_CHUNK - 1:C_CHUNK, :]
        q_i = (q_ref[rows, :] * jnp.exp(b)).astype(BF16)
        kf = k_ref[rows, :]
        k_i = (kf * jnp.exp(-b)).astype(BF16)
        k_e = (kf * jnp.exp(b_tot - b)).astype(BF16)
        decay = jnp.exp(b_tot)
        v = v_ref[rows, :]
        k_bd = jnp.where(same_head, jnp.concatenate([k_i] * C_HEADS, axis=0), zero)
        v_bd = jnp.where(same_head, jnp.concatenate([v] * C_HEADS, axis=0), zero)
        a = jnp.where(tri, _nt_dot(q_i, k_bd), 0.0)
        st = st_ref[...]
        o_ref[rows, :] = _dot(a.astype(BF16), v_bd) + _nt_dot(q_i, st.astype(BF16))
        st_ref[...] = st * decay + jnp.where(same_head, _tn_dot(v, k_e), 0.0)


def _gla_kernel(qf_ref, kf_ref, vf_ref, laf_ref, qb_ref, kb_ref, vb_ref, lab_ref, of_ref, ob_ref, sf_ref, sb_ref):
    @pl.when(pl.program_id(1) == 0)
    def _():
        sf_ref[...] = jnp.zeros_like(sf_ref)
        sb_ref[...] = jnp.zeros_like(sb_ref)

    _gla_dir(qf_ref, kf_ref, vf_ref, laf_ref, of_ref, sf_ref, reverse=False)
    _gla_dir(qb_ref, kb_ref, vb_ref, lab_ref, ob_ref, sb_ref, reverse=True)


def _gla(cq, ck, cv, la, bsz, seq):
    tb = min(GLA_BLOCK, seq)
    nb = seq // tb
    fwd = lambda col: pl.BlockSpec((tb, C_WIDTH), lambda b, i: (b * nb + i, col))
    bwd = lambda col: pl.BlockSpec((tb, C_WIDTH), lambda b, i: (b * nb + nb - 1 - i, col))
    out = jax.ShapeDtypeStruct((bsz * seq, C_WIDTH), F32)
    state = pltpu.VMEM((C_WIDTH, C_WIDTH), F32)
    return pl.pallas_call(
        _gla_kernel,
        grid=(bsz, nb),
        in_specs=[fwd(0), fwd(0), fwd(0), fwd(0), bwd(0), bwd(0), bwd(0), bwd(1)],
        out_specs=[fwd(0), bwd(0)],
        out_shape=[out, out],
        scratch_shapes=[state, state],
        compiler_params=_cparams(("parallel", "arbitrary")),
        name="gla",
    )(cq, ck, cv, la, cq, ck, cv, la)


def _out_proj_kernel(x_ref, oa_ref, ob_ref, of_ref, obw_ref, cg_ref, cng_ref, wo_ref, g2_ref, wr_ref,
                     x1_ref, h2_ref, aff_ref, mix_ref):
    o = of_ref[...] + obw_ref[...]
    gi = lax.broadcasted_iota(jnp.int32, (C_WIDTH, C_WIDTH), 0) // HEAD_DIM
    gj = lax.broadcasted_iota(jnp.int32, (C_WIDTH, C_WIDTH), 1) // HEAD_DIM
    ones_bd = (gi == gj).astype(BF16)
    sq_hi, sq_lo = _split_bf16(o * o)
    ms = (_dot(sq_hi, ones_bd) + _dot(sq_lo, ones_bd)) * (1.0 / HEAD_DIM)
    cg = cg_ref[...]
    oc = (o * lax.rsqrt(ms + EPS) * cng_ref[...]) * (cg / (1.0 + jnp.exp(-cg)))
    mix_ref[:, 0:A_WIDTH] = oa_ref[...]
    mix_ref[:, A_WIDTH:A_WIDTH + B_WIDTH] = ob_ref[...]
    mix_ref[:, A_WIDTH + B_WIDTH:] = oc.astype(BF16)
    x1 = x_ref[...] + _dot(mix_ref[...], wo_ref[...])
    x1_ref[...] = x1
    h2 = _rms(x1, g2_ref[...]).astype(BF16)
    h2_ref[...] = h2
    logits = _dot(h2, wr_ref[...])
    lane = lax.broadcasted_iota(jnp.int32, logits.shape, 1)
    logits = jnp.where(lane < N_EXPERTS, logits, NEG_INF)
    p = jnp.exp(logits - jnp.max(logits, axis=1, keepdims=True))
    aff = p / jnp.sum(p, axis=1, keepdims=True)
    aff_ref[...] = aff.T[0:N_EXPERTS, :]


def _out_proj(x, oa, ob, o_f, o_b, cg, cng, wo_bf, layer, g2, wr_cat):
    n = x.shape[0]
    tm = min(IN_PROJ_ROWS, n)
    row = lambda width: pl.BlockSpec((tm, width), lambda i: (i, 0))
    full = lambda a: pl.BlockSpec(a.shape, lambda i: (0,) * a.ndim)
    return pl.pallas_call(
        _out_proj_kernel,
        grid=(n // tm,),
        in_specs=[row(D_MODEL), row(A_WIDTH), row(B_WIDTH), row(C_WIDTH), row(C_WIDTH), row(C_WIDTH),
                  full(cng), pl.BlockSpec((None, D_MODEL, D_MODEL), lambda i: (layer, 0, 0)), full(g2), full(wr_cat)],
        out_specs=[row(D_MODEL), row(D_MODEL), pl.BlockSpec((N_EXPERTS, tm), lambda i: (0, i))],
        out_shape=[jax.ShapeDtypeStruct((n, D_MODEL), F32), jax.ShapeDtypeStruct((n, D_MODEL), BF16),
                   jax.ShapeDtypeStruct((N_EXPERTS, n), F32)],
        scratch_shapes=[pltpu.VMEM((tm, D_MODEL), BF16)],
        compiler_params=_cparams(("parallel",)),
        name="out_proj",
    )(x, oa, ob, o_f, o_b, cg, cng, wo_bf, g2, wr_cat)


def _route_kernel(aff_ref, pos_ref, offs_ref, *, cap):
    n = aff_ref.shape[1]
    nt = n // ROUTE_TILE
    bits = pltpu.bitcast(aff_ref[...], jnp.int32)

    def search(it, thr):
        cand = thr | jnp.left_shift(jnp.int32(1), 30 - it)
        cnt = jnp.sum(jnp.where(bits >= cand, 1.0, 0.0), axis=1, keepdims=True)
        return jnp.where(cnt >= cap, cand, thr)

    thr = lax.fori_loop(0, 31, search, jnp.zeros((N_EXPERTS, 1), jnp.int32))
    n_gt = jnp.sum(jnp.where(bits > thr, 1.0, 0.0), axis=1, keepdims=True)
    n_tie = cap - n_gt

    ui = lax.broadcasted_iota(jnp.int32, (ROUTE_TILE, ROUTE_TILE), 0)
    uj = lax.broadcasted_iota(jnp.int32, (ROUTE_TILE, ROUTE_TILE), 1)
    upper = (ui <= uj).astype(BF16)
    lane = lax.broadcasted_iota(jnp.int32, offs_ref.shape, 1)

    def tile(i, carry):
        c_gt, c_eq, offs = carry
        col = pl.multiple_of(i * ROUTE_TILE, ROUTE_TILE)
        b = pltpu.bitcast(aff_ref[:, pl.ds(col, ROUTE_TILE)], jnp.int32)
        gt = jnp.where(b > thr, 1.0, 0.0)
        eq = jnp.where(b == thr, 1.0, 0.0)
        inc = _dot(jnp.concatenate([gt, eq], axis=0).astype(BF16), upper)
        gt_before = c_gt + inc[:N_EXPERTS] - gt
        eq_before = c_eq + inc[N_EXPERTS:] - eq
        sel = (gt > 0.0) | ((eq > 0.0) & (eq_before < n_tie))
        pos = gt_before + jnp.minimum(eq_before, n_tie)
        pos_ref[:, pl.ds(col, ROUTE_TILE)] = jnp.where(sel, pos, -1.0).astype(jnp.int32)
        start = c_gt + jnp.minimum(c_eq, n_tie)
        offs = jnp.where(lane == i, start.astype(jnp.int32), offs)
        return (c_gt + jnp.sum(gt, axis=1, keepdims=True), c_eq + jnp.sum(eq, axis=1, keepdims=True), offs)

    zero = jnp.zeros((N_EXPERTS, 1), F32)
    offs = jnp.where(lane == nt, cap, 0).astype(jnp.int32)
    _, _, offs = lax.fori_loop(0, nt, tile, (zero, zero, offs))
    offs_ref[...] = offs


def _route(aff_t, cap):
    n = aff_t.shape[1]
    nt = n // ROUTE_TILE
    return pl.pallas_call(
        functools.partial(_route_kernel, cap=cap),
        out_shape=[jax.ShapeDtypeStruct((N_EXPERTS, n), jnp.int32),
                   jax.ShapeDtypeStruct((N_EXPERTS, nt + 1), jnp.int32)],
        compiler_params=pltpu.CompilerParams(vmem_limit_bytes=VMEM_LIMIT),
        name="route",
    )(aff_t)


def _align(v):
    return (v // MOE_ALIGN) * MOE_ALIGN


def _moe_passes(offs_ref, t, subs, x0, s):
    k = jnp.int32(1)
    for g in range(MOE_PAIR):
        need = offs_ref[x0 + g, t * subs + s + 1] - _align(offs_ref[x0 + g, t * subs + s])
        k = jnp.maximum(k, (need + MOE_WIN - 1) // MOE_WIN)
    return k


def _moe_windows(offs_ref, pos_ref, t, subs, x0, s, j):
    slot = lax.broadcasted_iota(jnp.int32, (MOE_WIN, ROUTE_TILE), 0)
    hits, rels = [], []
    for g in range(MOE_PAIR):
        x = x0 + g
        ws = _align(offs_ref[x, t * subs + s]) + j * MOE_WIN
        pos = pos_ref[pl.ds(x, 1), s * ROUTE_TILE:(s + 1) * ROUTE_TILE]
        hits.append((pos - ws) == slot)
        rels.append(pl.multiple_of(ws - _align(offs_ref[x, t * subs]), MOE_ALIGN))
    onehot = jnp.concatenate([jnp.where(h, 1.0, 0.0) for h in hits], axis=0).astype(BF16)
    return hits, rels, onehot


def _moe_chunks(offs_ref, t, subs, x):
    base = _align(offs_ref[x, t * subs])
    return base, (offs_ref[x, (t + 1) * subs] - base + MOE_CHUNK - 1) // MOE_CHUNK


def _moe_gather_kernel(offs_ref, h_ref, pos_ref, gate_ref, xe_ref, stage_ref, carry_ref, sem):
    t = pl.program_id(0)
    steps = pl.num_programs(1)
    e0 = pl.program_id(1) * MOE_STEP
    tt = h_ref.shape[0]
    subs = tt // ROUTE_TILE
    k = t * steps + pl.program_id(1)
    slot = k % 2
    x_ref = stage_ref.at[slot]

    def off(x, s):
        return offs_ref[x, t * subs + s]

    def copies(kk, half, act):
        step_t, step_q = kk // steps, kk % steps
        for g in range(MOE_STEP):
            x = step_q * MOE_STEP + g
            base, n_chunks = _moe_chunks(offs_ref, step_t, subs, x)

            def body(c, carry):
                r0 = pl.multiple_of(c * MOE_CHUNK, MOE_CHUNK)
                act(pltpu.make_async_copy(stage_ref.at[half, g, pl.ds(r0, MOE_CHUNK), :],
                                          xe_ref.at[x, pl.ds(pl.multiple_of(base + r0, MOE_ALIGN), MOE_CHUNK), :],
                                          sem.at[half]))
                return carry

            lax.fori_loop(0, n_chunks, body, 0)

    @pl.when(k >= 2)
    def _():
        copies(k - 2, slot, lambda cp: cp.wait())

    @pl.when(t == 0)
    def _():
        carry_ref[pl.ds(e0, MOE_STEP)] = jnp.zeros((MOE_STEP, MOE_ALIGN, XE_WIDTH), BF16)
        cap = xe_ref.shape[1] - MOE_CHUNK
        x_ref[0, 0:MOE_CHUNK, :] = jnp.zeros((MOE_CHUNK, XE_WIDTH), BF16)
        pads = [pltpu.make_async_copy(x_ref.at[0, 0:MOE_CHUNK, :], xe_ref.at[e0 + g, cap:cap + MOE_CHUNK, :],
                                      sem.at[slot]) for g in range(MOE_STEP)]
        for p in pads:
            p.start()
        for p in pads:
            p.wait()

    for g in range(MOE_STEP):
        x_ref[g, 0:MOE_ALIGN, :] = carry_ref[e0 + g]

    row = lax.broadcasted_iota(jnp.int32, (MOE_WIN, 1), 0)
    lane = lax.broadcasted_iota(jnp.int32, (MOE_WIN, XE_WIDTH - D_MODEL), 1)

    def gather_pass(pair, s, j):
        x0 = e0 + pair * MOE_PAIR
        cols = slice(s * ROUTE_TILE, (s + 1) * ROUTE_TILE)
        hits, rels, onehot = _moe_windows(offs_ref, pos_ref, t, subs, x0, s, j)
        r = _dot(onehot, h_ref[cols, :])
        for g in range(MOE_PAIR):
            gate = jnp.sum(jnp.where(hits[g], gate_ref[pl.ds(x0 + g, 1), cols], 0.0), axis=1, keepdims=True)
            g_hi = gate.astype(BF16).astype(F32)
            extra = jnp.where(lane == 0, g_hi, jnp.where(lane == 1, gate - g_hi, 0.0))
            new = jnp.concatenate([r[g * MOE_WIN:(g + 1) * MOE_WIN, :], extra], axis=1)
            ws = _align(off(x0 + g, s)) + j * MOE_WIN
            own = row >= off(x0 + g, s) - ws
            buf = pair * MOE_PAIR + g
            if isinstance(j, int):
                head = pl.ds(rels[g], MOE_ALIGN)
                x_ref[buf, head, :] = jnp.where(own[:MOE_ALIGN], new[:MOE_ALIGN],
                                                x_ref[buf, head, :].astype(F32)).astype(BF16)
                x_ref[buf, pl.ds(rels[g] + MOE_ALIGN, MOE_WIN - MOE_ALIGN), :] = new[MOE_ALIGN:].astype(BF16)
            else:
                own = own & (row < off(x0 + g, s + 1) - ws)
                win = pl.ds(rels[g], MOE_WIN)
                x_ref[buf, win, :] = jnp.where(own, new, x_ref[buf, win, :].astype(F32)).astype(BF16)

    for pair in range(MOE_STEP // MOE_PAIR):
        for s in range(subs):
            gather_pass(pair, s, 0)
    for pair in range(MOE_STEP // MOE_PAIR):
        for s in range(subs):
            lax.fori_loop(1, _moe_passes(offs_ref, t, subs, e0 + pair * MOE_PAIR, s),
                          lambda j, carry: (gather_pass(pair, s, j), carry)[1], 0)

    for g in range(MOE_STEP):
        x = e0 + g
        base, n_chunks = _moe_chunks(offs_ref, t, subs, x)
        last_group = pl.multiple_of(_align(off(x, subs)) - base, MOE_ALIGN)
        tail = x_ref[g, pl.ds(last_group, MOE_ALIGN), :].astype(F32)
        tail = jnp.where(row[:MOE_ALIGN] < off(x, subs) - base - last_group, tail, 0.0).astype(BF16)
        x_ref[g, pl.ds(last_group, MOE_ALIGN), :] = tail
        carry_ref[x] = tail

        def fill(i, carry):
            x_ref[g, pl.ds(pl.multiple_of(i * MOE_ALIGN, MOE_ALIGN), MOE_ALIGN), :] = jnp.zeros((MOE_ALIGN, XE_WIDTH), BF16)
            return carry

        lax.fori_loop(last_group // MOE_ALIGN + 1, n_chunks * MOE_CHUNK // MOE_ALIGN, fill, 0)

    copies(k, slot, lambda cp: cp.start())

    @pl.when(k == pl.num_programs(0) * steps - 1)
    def _():
        copies(k - 1, 1 - slot, lambda cp: cp.wait())
        copies(k, slot, lambda cp: cp.wait())


def _moe_ffn_kernel(x_ref, wg_ref, wu_ref, wd_ref, y_ref, wg_bf, wu_bf, wd_bf):
    j = pl.program_id(1)
    last = pl.num_programs(1) - 1

    @pl.when(j == 0)
    def _():
        wg_bf[...] = wg_ref[...].astype(BF16)
        wu_bf[...] = wu_ref[...].astype(BF16)
        wd_bf[...] = wd_ref[...].astype(BF16)

    @pl.when(j < last)
    def _():
        x = x_ref[0, :, 0:D_MODEL]
        gate = jnp.sum(x_ref[0, :, D_MODEL:XE_WIDTH].astype(F32), axis=1, keepdims=True)
        hg = _dot(x, wg_bf[...])
        hu = _dot(x, wu_bf[...])
        act = (hg / (1.0 + jnp.exp(-hg))) * hu * gate
        y_ref[0] = _dot(act.astype(BF16), wd_bf[...]).astype(BF16)

    @pl.when(j == last)
    def _():
        y_ref[0] = jnp.zeros(y_ref.shape[1:], BF16)


def _moe_combine_kernel(offs_ref, x1_ref, pos_ref, y_ref, fg_ref, o_ref, y_buf, sem, *, final_norm):
    t = pl.program_id(0)
    q = pl.program_id(1)
    steps = pl.num_programs(1)
    e0 = q * MOE_STEP
    tt = x1_ref.shape[0]
    subs = tt // ROUTE_TILE
    k = t * steps + q
    slot = k % 2

    def fetch(kk, half, act):
        step_t, step_q = kk // steps, kk % steps
        for g in range(MOE_STEP):
            x = step_q * MOE_STEP + g
            base, n_chunks = _moe_chunks(offs_ref, step_t, subs, x)

            def body(c, carry):
                r0 = pl.multiple_of(c * MOE_CHUNK, MOE_CHUNK)
                act(pltpu.make_async_copy(y_ref.at[x, pl.ds(pl.multiple_of(base + r0, MOE_ALIGN), MOE_CHUNK), :],
                                          y_buf.at[half, g, pl.ds(r0, MOE_CHUNK), :], sem.at[half]))
                return carry

            lax.fori_loop(0, n_chunks, body, 0)

    @pl.when(k == 0)
    def _():
        y_buf[...] = jnp.zeros_like(y_buf)
        fetch(k, slot, lambda cp: cp.start())

    @pl.when(k + 1 < pl.num_programs(0) * steps)
    def _():
        fetch(k + 1, 1 - slot, lambda cp: cp.start())

    fetch(k, slot, lambda cp: cp.wait())

    @pl.when(q == 0)
    def _():
        o_ref[...] = x1_ref[...]

    def combine_pass(pair, s, j):
        cols = slice(s * ROUTE_TILE, (s + 1) * ROUTE_TILE)
        _, rels, onehot = _moe_windows(offs_ref, pos_ref, t, subs, e0 + pair * MOE_PAIR, s, j)
        ycat = jnp.concatenate([y_buf[slot, pair * MOE_PAIR + g, pl.ds(rels[g], MOE_WIN), :]
                                for g in range(MOE_PAIR)], axis=0)
        o_ref[cols, :] += _tn_dot(onehot, ycat)

    for pair in range(MOE_STEP // MOE_PAIR):
        for s in range(subs):
            combine_pass(pair, s, 0)
    for pair in range(MOE_STEP // MOE_PAIR):
        for s in range(subs):
            lax.fori_loop(1, _moe_passes(offs_ref, t, subs, e0 + pair * MOE_PAIR, s),
                          lambda j, carry: (combine_pass(pair, s, j), carry)[1], 0)

    if final_norm:
        @pl.when(q == pl.num_programs(1) - 1)
        def _():
            o_ref[...] = _rms(o_ref[...], fg_ref[...])


def _moe(offs, h2, x1, pos, gate, wg, wu, wd, layer, final_g, final_norm, cap):
    n = h2.shape[0]
    tt, tc = min(MOE_TILE, n), min(MOE_COMBINE_TILE, n)
    assert cap % MOE_CHUNK == 0 and n % tt == 0 and n % tc == 0
    steps = N_EXPERTS // MOE_STEP
    stage_rows = lambda tile: tile + MOE_ALIGN + 3 * MOE_WIN
    tok = lambda tile: pl.BlockSpec((tile, D_MODEL), lambda t, q, offs: (t, 0))
    per_tok = lambda tile: pl.BlockSpec((N_EXPERTS, tile), lambda t, q, offs: (0, t))
    hbm = pl.BlockSpec(memory_space=pl.ANY)
    sems = ("arbitrary", "arbitrary")

    xe = pl.pallas_call(
        _moe_gather_kernel,
        grid_spec=pltpu.PrefetchScalarGridSpec(
            num_scalar_prefetch=1, grid=(n // tt, steps),
            in_specs=[tok(tt), per_tok(tt), per_tok(tt)], out_specs=hbm,
            scratch_shapes=[pltpu.VMEM((2, MOE_STEP, stage_rows(tt), XE_WIDTH), BF16),
                            pltpu.VMEM((N_EXPERTS, MOE_ALIGN, XE_WIDTH), BF16), pltpu.SemaphoreType.DMA((2,))],
        ),
        out_shape=jax.ShapeDtypeStruct((N_EXPERTS, cap + MOE_CHUNK, XE_WIDTH), BF16),
        compiler_params=_cparams(sems),
        name="moe_gather",
    )(offs, h2, pos, gate)

    fb = min(MOE_FFN_ROWS, cap)
    nb = cap // fb
    wspec = pl.BlockSpec((None, None, D_MODEL, D_MODEL), lambda e, j: (layer, e, 0, 0))
    w_bf = pltpu.VMEM((D_MODEL, D_MODEL), BF16)
    y = pl.pallas_call(
        _moe_ffn_kernel,
        grid=(N_EXPERTS, nb + 1),
        in_specs=[pl.BlockSpec((1, fb, XE_WIDTH), lambda e, j: (e, jnp.minimum(j, nb - 1), 0)), wspec, wspec, wspec],
        out_specs=pl.BlockSpec((1, fb, D_MODEL), lambda e, j: (e, j, 0)),
        out_shape=jax.ShapeDtypeStruct((N_EXPERTS, cap + fb, D_MODEL), BF16),
        scratch_shapes=[w_bf, w_bf, w_bf],
        compiler_params=_cparams(("parallel", "arbitrary")),
        name="moe_ffn",
    )(xe, wg, wu, wd)

    return pl.pallas_call(
        functools.partial(_moe_combine_kernel, final_norm=final_norm),
        grid_spec=pltpu.PrefetchScalarGridSpec(
            num_scalar_prefetch=1, grid=(n // tc, steps),
            in_specs=[tok(tc), per_tok(tc), hbm, pl.BlockSpec((1, D_MODEL), lambda t, q, offs: (0, 0))],
            out_specs=tok(tc),
            scratch_shapes=[pltpu.VMEM((2, MOE_STEP, stage_rows(tc), D_MODEL), BF16),
                            pltpu.SemaphoreType.DMA((2,))],
        ),
        out_shape=jax.ShapeDtypeStruct((n, D_MODEL), F32),
        compiler_params=_cparams(sems),
        name="moe_combine",
    )(offs, x1, pos, y, final_g)


def _prep_layer(l, norm1_g, w_in, a_sink, a_norm_g, b_rel_bias, b_norm_g, c_alpha_w2_f, c_alpha_b_f,
                c_alpha_w2_b, c_alpha_b_b, c_norm_g, w_out, norm2_g, w_router, w_gate, w_up, w_down):
    z = jnp.zeros((C_LOWRANK, C_WIDTH), F32)
    w2 = jnp.concatenate([jnp.concatenate([c_alpha_w2_f[l], z], axis=1),
                          jnp.concatenate([z, c_alpha_w2_b[l]], axis=1)], axis=0)
    b2 = jnp.concatenate([c_alpha_b_f[l], c_alpha_b_b[l]])[None, :]
    wr_cat = jnp.pad(w_router[l].astype(BF16), ((0, 0), (0, LANES - N_EXPERTS)))
    return dict(
        layer=l, g1=norm1_g[l][None, :], w_in=w_in.astype(BF16), w2=w2, b2=b2,
        sink=a_sink[l], a_g=a_norm_g[l][None, :], bias=_bias_table(b_rel_bias[l]), b_g=b_norm_g[l][None, :],
        c_g=c_norm_g[l][None, :], w_out=w_out.astype(BF16), g2=norm2_g[l][None, :],
        wr_cat=wr_cat, wg=w_gate, wu=w_up, wd=w_down)


def _layer(x, p, bsz, seq, final_g, final_norm):
    n = bsz * seq
    l = p["layer"]
    aq, ak, av, bq, bk, bv, cq, ck, cv, cg, la = _in_proj(x, p["g1"], p["w_in"], l, p["w2"], p["b2"], seq)
    oa = _attn_a(aq, ak, av, p["sink"], p["a_g"], bsz, seq)
    ob = _attn_b(bq, bk, bv, p["bias"], p["b_g"], bsz, seq)
    o_f, o_b = _gla(cq, ck, cv, la, bsz, seq)
    x1, h2, aff = _out_proj(x, oa, ob, o_f, o_b, cg, p["c_g"], p["w_out"], l, p["g2"], p["wr_cat"])
    cap = EC_CAPACITY * n // N_EXPERTS
    pos, offs = _route(aff, cap)
    return _moe(offs, h2, x1, pos, aff, p["wg"], p["wu"], p["wd"], l, final_g, final_norm, cap)


def _trunk(x, layers, final_g):
    bsz, seq, _ = x.shape
    y = x.reshape(bsz * seq, D_MODEL)
    for l, p in enumerate(layers):
        y = _layer(y, p, bsz, seq, final_g, final_norm=(l == len(layers) - 1))
    return y.reshape(bsz, seq, D_MODEL)


def kernel(x_prompt, x_sample, norm1_g, w_in, a_sink, a_norm_g, b_rel_bias, b_norm_g, c_alpha_w2_f, c_alpha_b_f,
           c_alpha_w2_b, c_alpha_b_b, c_norm_g, w_out, norm2_g, w_router, w_gate, w_up, w_down, final_g):
    depth = w_in.shape[0]
    layers = [_prep_layer(l, norm1_g, w_in, a_sink, a_norm_g, b_rel_bias, b_norm_g, c_alpha_w2_f, c_alpha_b_f,
                          c_alpha_w2_b, c_alpha_b_b, c_norm_g, w_out, norm2_g, w_router, w_gate, w_up, w_down)
              for l in range(depth)]
    fg = final_g[None, :]
    return _trunk(x_prompt, layers, fg), _trunk(x_sample, layers, fg)
```

```python
import functools

import jax
import jax.numpy as jnp
from jax import lax
from jax.experimental import pallas as pl
from jax.experimental.pallas import tpu as pltpu

F32 = jnp.float32
BF16 = jnp.bfloat16

D_MODEL = 1024
HEAD_DIM = 64
A_WIDTH = 512
A_HEADS = 8
A_KV_HEADS = 2
A_KV_WIDTH = 128
WINDOW = 128
ROPE_THETA = 500000.0
ROPE_DIM = 16
B_WIDTH = 256
B_HEADS = 4
GRID_W = 64
WIN_H = 8
WIN_W = 16
C_WIDTH = 256
C_HEADS = 4
C_LOWRANK = 16
C_TAU = 16.0
C_CHUNK = 64
N_EXPERTS = 16
EC_CAPACITY = 2
EPS = 1e-6
NEG_INF = -1e30
IN_WIDTH = 2592

_OFF_AQ, _OFF_AK, _OFF_AV = 0, 512, 640
_OFF_BQ, _OFF_BK, _OFF_BV = 768, 1024, 1280
_OFF_CQ, _OFF_CK, _OFF_CV, _OFF_CG = 1536, 1792, 2048, 2304
_OFF_LR = 2560

LANES = 128
VMEM_LIMIT = 56 * 1024 * 1024

IN_PROJ_ROWS = 512
GLA_BLOCK = 512
ROUTE_TILE = 256
MOE_TILE = 2048
MOE_COMBINE_TILE = 1024
MOE_STEP = 4
MOE_PAIR = 4
MOE_WIN = 64
MOE_ALIGN = 16
MOE_CHUNK = 128
MOE_FFN_ROWS = 1024
XE_WIDTH = D_MODEL + LANES


def _cparams(sem):
    return pltpu.CompilerParams(dimension_semantics=sem, vmem_limit_bytes=VMEM_LIMIT)


def _nt_dot(a, b):
    return lax.dot_general(a, b, (((1,), (1,)), ((), ())), preferred_element_type=F32)


def _tn_dot(a, b):
    return lax.dot_general(a, b, (((0,), (0,)), ((), ())), preferred_element_type=F32)


def _dot(a, b):
    return jnp.dot(a, b, preferred_element_type=F32)


def _rms(x, g):
    return x * lax.rsqrt(jnp.mean(x * x, axis=-1, keepdims=True) + EPS) * g


def _same_head_mask(n_rows, rows_per_head):
    width = (n_rows // rows_per_head) * HEAD_DIM
    ri = lax.broadcasted_iota(jnp.int32, (n_rows, width), 0) // rows_per_head
    li = lax.broadcasted_iota(jnp.int32, (n_rows, width), 1) // HEAD_DIM
    return ri == li


def _split_bf16(x):
    hi = x.astype(BF16)
    return hi, (x - hi.astype(F32)).astype(BF16)


def _in_proj_kernel(x_ref, g_ref, w_ref, cos_ref, s1_ref, s2_ref, w2_ref, b2_ref,
                    aq_ref, ak_ref, av_ref, bq_ref, bk_ref, bv_ref,
                    cq_ref, ck_ref, cv_ref, cg_ref, la_ref):
    h = _rms(x_ref[...], g_ref[...]).astype(BF16)
    y = _dot(h, w_ref[...])

    def proj(lo, width):
        return y[:, lo:lo + width]

    cos, s1, s2 = cos_ref[...], s1_ref[...], s2_ref[...]

    def rope(t):
        return t * cos + pltpu.roll(t, LANES - ROPE_DIM // 2, 1) * s1 + pltpu.roll(t, ROPE_DIM // 2, 1) * s2

    scale = HEAD_DIM ** -0.5
    aq = proj(_OFF_AQ, A_WIDTH)
    for c in range(A_WIDTH // LANES):
        aq_ref[:, c * LANES:(c + 1) * LANES] = (rope(aq[:, c * LANES:(c + 1) * LANES]) * scale).astype(BF16)
    ak_ref[...] = rope(proj(_OFF_AK, A_KV_WIDTH)).astype(BF16)
    av_ref[...] = proj(_OFF_AV, A_KV_WIDTH).astype(BF16)
    bq_ref[...] = (proj(_OFF_BQ, B_WIDTH) * scale).astype(BF16)
    bk_ref[...] = proj(_OFF_BK, B_WIDTH).astype(BF16)
    bv_ref[...] = proj(_OFF_BV, B_WIDTH).astype(BF16)
    cq_ref[...] = proj(_OFF_CQ, C_WIDTH) * scale
    ck_ref[...] = proj(_OFF_CK, C_WIDTH)
    cv_ref[...] = proj(_OFF_CV, C_WIDTH).astype(BF16)
    cg_ref[...] = proj(_OFF_CG, C_WIDTH)
    lr = proj(_OFF_LR, 2 * C_LOWRANK)
    z = jnp.dot(lr, w2_ref[...], preferred_element_type=F32, precision=lax.Precision.HIGHEST) + b2_ref[...]
    la_ref[...] = (jnp.minimum(z, 0.0) - jnp.log(1.0 + jnp.exp(-jnp.abs(z)))) * (1.0 / C_TAU)


def _rope_tables(seq):
    half = ROPE_DIM // 2
    inv = jnp.power(jnp.float32(ROPE_THETA), -jnp.arange(half, dtype=F32) * (2.0 / ROPE_DIM))
    ang = jnp.arange(seq, dtype=F32)[:, None] * inv[None, :]
    cos, sin = jnp.cos(ang), jnp.sin(ang)
    ones = jnp.ones((seq, HEAD_DIM - ROPE_DIM), F32)
    zeros = jnp.zeros((seq, HEAD_DIM - half), F32)
    c = jnp.concatenate([cos, cos, ones], axis=1)
    s1 = jnp.concatenate([-sin, zeros], axis=1)
    s2 = jnp.concatenate([jnp.zeros((seq, half), F32), sin, jnp.zeros((seq, HEAD_DIM - ROPE_DIM), F32)], axis=1)
    two = lambda t: jnp.concatenate([t, t], axis=1)
    return two(c), two(s1), two(s2)


def _in_proj(x, g1, w_in_bf, layer, w2, b2, seq):
    n = x.shape[0]
    tm = min(IN_PROJ_ROWS, seq)
    per_seq = seq // tm
    cos, s1, s2 = _rope_tables(seq)
    row = lambda width: pl.BlockSpec((tm, width), lambda i: (i, 0))
    full = lambda a: pl.BlockSpec(a.shape, lambda i: (0,) * a.ndim)
    tab = pl.BlockSpec((tm, LANES), lambda i: (i % per_seq, 0))
    widths = [(A_WIDTH, BF16), (A_KV_WIDTH, BF16), (A_KV_WIDTH, BF16), (B_WIDTH, BF16), (B_WIDTH, BF16),
              (B_WIDTH, BF16), (C_WIDTH, F32), (C_WIDTH, F32), (C_WIDTH, BF16), (C_WIDTH, F32), (2 * C_WIDTH, F32)]
    return pl.pallas_call(
        _in_proj_kernel,
        grid=(n // tm,),
        in_specs=[row(D_MODEL), full(g1), pl.BlockSpec((None, D_MODEL, IN_WIDTH), lambda i: (layer, 0, 0)),
                  tab, tab, tab, full(w2), full(b2)],
        out_specs=[row(w) for w, _ in widths],
        out_shape=[jax.ShapeDtypeStruct((n, w), dt) for w, dt in widths],
        compiler_params=_cparams(("parallel",)),
        name="in_proj",
    )(x, g1, w_in_bf, cos, s1, s2, w2, b2)


A_QBLOCKS = 2


def _attn_a_kernel(sink_ref, q_ref, kp_ref, kc_ref, kn_ref, vp_ref, vc_ref, vn_ref, g_ref, o_ref, acc_ref, *, steps):
    i = pl.program_id(1)
    grp = A_HEADS // A_KV_HEADS
    k = jnp.concatenate([kp_ref[...], kc_ref[...], kn_ref[...]], axis=0)
    v = jnp.concatenate([vp_ref[...], vc_ref[...], vn_ref[...]], axis=0)
    qq = lax.broadcasted_iota(jnp.int32, (WINDOW, 3 * WINDOW), 0)
    kk = lax.broadcasted_iota(jnp.int32, (WINDOW, 3 * WINDOW), 1)
    band = jnp.abs(kk - WINDOW - qq) <= WINDOW
    head_row = lax.broadcasted_iota(jnp.int32, (grp * WINDOW, 1), 0) // WINDOW
    for blk in range(A_QBLOCKS):
        visible = band
        if blk == 0:
            visible = visible & jnp.logical_not((kk < WINDOW) & (i == 0))
        if blk == A_QBLOCKS - 1:
            visible = visible & jnp.logical_not((kk >= 2 * WINDOW) & (i == steps - 1))
        bias = jnp.concatenate([jnp.where(visible, 0.0, NEG_INF)] * grp, axis=0)
        rows = slice(blk * WINDOW, (blk + 1) * WINDOW)
        keys = slice(blk * WINDOW, (blk + 3) * WINDOW)
        for kv in range(A_KV_HEADS):
            heads = range(kv * grp, (kv + 1) * grp)
            q = jnp.concatenate([q_ref[rows, h * HEAD_DIM:(h + 1) * HEAD_DIM] for h in heads], axis=0)
            s = _nt_dot(q, k[keys, kv * HEAD_DIM:(kv + 1) * HEAD_DIM]) + bias
            sink = jnp.zeros((grp * WINDOW, 1), F32)
            for g_i, h in enumerate(heads):
                sink = jnp.where(head_row == g_i, sink_ref[h], sink)
            m = jnp.maximum(jnp.max(s, axis=-1, keepdims=True), sink)
            p = jnp.exp(s - m)
            den = jnp.sum(p, axis=-1, keepdims=True) + jnp.exp(sink - m)
            o = _dot(p.astype(BF16), v[keys, kv * HEAD_DIM:(kv + 1) * HEAD_DIM]) / den
            for g_i, h in enumerate(heads):
                acc_ref[rows, h * HEAD_DIM:(h + 1) * HEAD_DIM] = o[g_i * WINDOW:(g_i + 1) * WINDOW, :]
    o_ref[...] = _rms(acc_ref[...], g_ref[...]).astype(BF16)


def _attn_a(aq, ak, av, sink, g, bsz, seq):
    nb = seq // WINDOW
    assert nb % A_QBLOCKS == 0
    steps = nb // A_QBLOCKS
    qrows = A_QBLOCKS * WINDOW
    qspec = pl.BlockSpec((qrows, A_WIDTH), lambda b, i: (b * steps + i, 0))
    kprev = pl.BlockSpec((WINDOW, A_KV_WIDTH), lambda b, i: (b * nb + jnp.maximum(i * A_QBLOCKS - 1, 0), 0))
    kcur = pl.BlockSpec((qrows, A_KV_WIDTH), lambda b, i: (b * steps + i, 0))
    knext = pl.BlockSpec((WINDOW, A_KV_WIDTH), lambda b, i: (b * nb + jnp.minimum((i + 1) * A_QBLOCKS, nb - 1), 0))
    return pl.pallas_call(
        functools.partial(_attn_a_kernel, steps=steps),
        grid=(bsz, steps),
        in_specs=[pl.BlockSpec(memory_space=pltpu.SMEM), qspec, kprev, kcur, knext, kprev, kcur, knext,
                  pl.BlockSpec((1, A_WIDTH), lambda b, i: (0, 0))],
        out_specs=qspec,
        out_shape=jax.ShapeDtypeStruct((bsz * seq, A_WIDTH), BF16),
        scratch_shapes=[pltpu.VMEM((qrows, A_WIDTH), F32)],
        compiler_params=_cparams(("parallel", "parallel")),
        name="attn_a",
    )(sink, aq, ak, ak, ak, av, av, av, g)


B_GROUP = 8
B_KEYS = WIN_H * GRID_W


def _bias_table_kernel(rel_ref, o_ref):
    c = lax.broadcasted_iota(jnp.int32, (GRID_W, GRID_W), 0)
    w = lax.broadcasted_iota(jnp.int32, (GRID_W, GRID_W), 1)
    cstart = jnp.clip(c - WIN_W // 2, 0, GRID_W - WIN_W)
    colmask = (w >= cstart) & (w < cstart + WIN_W)
    col_off = jnp.clip(w - c + (WIN_W - 1), 0, 2 * WIN_W - 2)
    n_ro, n_co = 2 * WIN_H - 1, 2 * WIN_W - 1
    for h in range(B_HEADS):
        for ro in range(n_ro):
            def body(j, acc):
                return jnp.where(col_off == j, rel_ref[(h * n_ro + ro) * n_co + j], acc)
            t = lax.fori_loop(0, n_co, body, jnp.zeros((GRID_W, GRID_W), F32))
            t = jnp.where(colmask, t, NEG_INF)
            for p in range(WIN_H):
                kidx = ro - (WIN_H - 1) + p
                if 0 <= kidx < WIN_H:
                    o_ref[p, h, :, kidx * GRID_W:(kidx + 1) * GRID_W] = t


def _bias_table(rel_bias):
    return pl.pallas_call(
        _bias_table_kernel,
        in_specs=[pl.BlockSpec(memory_space=pltpu.SMEM)],
        out_shape=jax.ShapeDtypeStruct((WIN_H, B_HEADS, GRID_W, B_KEYS), F32),
        name="bias_table",
    )(rel_bias.reshape(-1)).reshape(WIN_H, B_HEADS * GRID_W, B_KEYS)


def _attn_b_kernel(q_ref, kp_ref, kc_ref, kn_ref, vp_ref, vc_ref, vn_ref, bias_ref, g_ref, o_ref,
                   kbuf, vbuf, acc_ref, *, rows):
    gidx = pl.program_id(1)
    blk = B_GROUP * GRID_W
    kbuf[0:blk, :] = kp_ref[...]
    kbuf[blk:2 * blk, :] = kc_ref[...]
    kbuf[2 * blk:3 * blk, :] = kn_ref[...]
    vbuf[0:blk, :] = vp_ref[...]
    vbuf[blk:2 * blk, :] = vc_ref[...]
    vbuf[2 * blk:3 * blk, :] = vn_ref[...]
    same_head = _same_head_mask(B_HEADS * GRID_W, GRID_W)
    for j in range(B_GROUP):
        r = gidx * B_GROUP + j
        start = jnp.clip(r - WIN_H // 2, 0, rows - WIN_H)
        pat = r - start
        loc = pl.multiple_of((start - gidx * B_GROUP + B_GROUP) * GRID_W, GRID_W)
        kw = kbuf[pl.ds(loc, B_KEYS), :]
        vw = vbuf[pl.ds(loc, B_KEYS), :]
        qj = q_ref[j * GRID_W:(j + 1) * GRID_W, :]
        q = jnp.where(same_head, jnp.concatenate([qj] * B_HEADS, axis=0), jnp.zeros((), BF16))
        s = _nt_dot(q, kw) + bias_ref[pat]
        m = jnp.max(s, axis=-1, keepdims=True)
        p = jnp.exp(s - m)
        den = jnp.sum(p, axis=-1, keepdims=True)
        o = jnp.where(same_head, _dot(p.astype(BF16), vw) / den, 0.0)
        acc_ref[j * GRID_W:(j + 1) * GRID_W, :] = sum(o[h * GRID_W:(h + 1) * GRID_W, :] for h in range(B_HEADS))
    o_ref[...] = _rms(acc_ref[...], g_ref[...]).astype(BF16)


def _attn_b(bq, bk, bv, bias, g, bsz, seq):
    rows = seq // GRID_W
    assert rows % B_GROUP == 0 and rows >= WIN_H
    ng = rows // B_GROUP
    blk = B_GROUP * GRID_W
    cur = pl.BlockSpec((blk, B_WIDTH), lambda b, i: (b * ng + i, 0))
    prev = pl.BlockSpec((blk, B_WIDTH), lambda b, i: (b * ng + jnp.maximum(i - 1, 0), 0))
    nxt = pl.BlockSpec((blk, B_WIDTH), lambda b, i: (b * ng + jnp.minimum(i + 1, ng - 1), 0))
    return pl.pallas_call(
        functools.partial(_attn_b_kernel, rows=rows),
        grid=(bsz, ng),
        in_specs=[cur, prev, cur, nxt, prev, cur, nxt,
                  pl.BlockSpec(bias.shape, lambda b, i: (0, 0, 0)),
                  pl.BlockSpec((1, B_WIDTH), lambda b, i: (0, 0))],
        out_specs=cur,
        out_shape=jax.ShapeDtypeStruct((bsz * seq, B_WIDTH), BF16),
        scratch_shapes=[pltpu.VMEM((3 * blk, B_WIDTH), BF16), pltpu.VMEM((3 * blk, B_WIDTH), BF16),
                        pltpu.VMEM((blk, B_WIDTH), F32)],
        compiler_params=_cparams(("parallel", "parallel")),
        name="attn_b",
    )(bq, bk, bk, bk, bv, bv, bv, bias, g)


def _gla_decay(la_ref, reverse):
    tb = la_ref.shape[0]
    bi = lax.broadcasted_iota(jnp.int32, (tb, tb), 0)
    bj = lax.broadcasted_iota(jnp.int32, (tb, tb), 1)
    in_chunk = (bi // C_CHUNK) == (bj // C_CHUNK)
    cum_w = (in_chunk & ((bj >= bi) if reverse else (bj <= bi))).astype(BF16)
    la_hi, la_lo = _split_bf16(la_ref[...])
    return _dot(cum_w, la_hi) + _dot(cum_w, la_lo)


def _gla_chunk(q_ref, k_ref, v_ref, o_ref, b_all, st, c, reverse):
    same_head = _same_head_mask(C_HEADS * C_CHUNK, C_CHUNK)
    ti = lax.broadcasted_iota(jnp.int32, (C_CHUNK, C_HEADS * C_CHUNK), 0)
    si = lax.broadcasted_iota(jnp.int32, (C_CHUNK, C_HEADS * C_CHUNK), 1) % C_CHUNK
    tri = (si >= ti) if reverse else (si <= ti)
    zero = jnp.zeros((), BF16)
    rows = slice(c * C_CHUNK, (c + 1) * C_CHUNK)
    b = b_all[rows, :]
    b_tot = b[0:1, :] if reverse else b[C_CHUNK - 1:C_CHUNK, :]
    q_i = (q_ref[rows, :] * jnp.exp(b)).astype(BF16)
    kf = k_ref[rows, :]
    k_i = (kf * jnp.exp(-b)).astype(BF16)
    k_e = (kf * jnp.exp(b_tot - b)).astype(BF16)
    v = v_ref[rows, :]
    k_bd = jnp.where(same_head, jnp.concatenate([k_i] * C_HEADS, axis=0), zero)
    v_bd = jnp.where(same_head, jnp.concatenate([v] * C_HEADS, axis=0), zero)
    a = jnp.where(tri, _nt_dot(q_i, k_bd), 0.0)
    o_ref[rows, :] = _dot(a.astype(BF16), v_bd) + _nt_dot(q_i, st.astype(BF16))
    return st * jnp.exp(b_tot) + jnp.where(same_head, _tn_dot(v, k_e), 0.0)


def _gla_kernel(qf_ref, kf_ref, vf_ref, laf_ref, qb_ref, kb_ref, vb_ref, lab_ref, of_ref, ob_ref, sf_ref, sb_ref):
    @pl.when(pl.program_id(1) == 0)
    def _():
        sf_ref[...] = jnp.zeros_like(sf_ref)
        sb_ref[...] = jnp.zeros_like(sb_ref)

    n_chunks = qf_ref.shape[0] // C_CHUNK
    b_f = _gla_decay(laf_ref, reverse=False)
    b_b = _gla_decay(lab_ref, reverse=True)
    s_f, s_b = sf_ref[...], sb_ref[...]
    for c in range(n_chunks):
        s_f = _gla_chunk(qf_ref, kf_ref, vf_ref, of_ref, b_f, s_f, c, reverse=False)
        s_b = _gla_chunk(qb_ref, kb_ref, vb_ref, ob_ref, b_b, s_b, n_chunks - 1 - c, reverse=True)
    sf_ref[...] = s_f
    sb_ref[...] = s_b


def _gla(cq, ck, cv, la, bsz, seq):
    tb = min(GLA_BLOCK, seq)
    nb = seq // tb
    fwd = lambda col: pl.BlockSpec((tb, C_WIDTH), lambda b, i: (b * nb + i, col))
    bwd = lambda col: pl.BlockSpec((tb, C_WIDTH), lambda b, i: (b * nb + nb - 1 - i, col))
    out = jax.ShapeDtypeStruct((bsz * seq, C_WIDTH), F32)
    state = pltpu.VMEM((C_WIDTH, C_WIDTH), F32)
    return pl.pallas_call(
        _gla_kernel,
        grid=(bsz, nb),
        in_specs=[fwd(0), fwd(0), fwd(0), fwd(0), bwd(0), bwd(0), bwd(0), bwd(1)],
        out_specs=[fwd(0), bwd(0)],
        out_shape=[out, out],
        scratch_shapes=[state, state],
        compiler_params=_cparams(("parallel", "arbitrary")),
        name="gla",
    )(cq, ck, cv, la, cq, ck, cv, la)


def _out_proj_kernel(x_ref, oa_ref, ob_ref, of_ref, obw_ref, cg_ref, cng_ref, wo_ref, g2_ref, wr_ref,
                     x1_ref, h2_ref, aff_ref, mix_ref):
    o = of_ref[...] + obw_ref[...]
    gi = lax.broadcasted_iota(jnp.int32, (C_WIDTH, C_WIDTH), 0) // HEAD_DIM
    gj = lax.broadcasted_iota(jnp.int32, (C_WIDTH, C_WIDTH), 1) // HEAD_DIM
    ones_bd = (gi == gj).astype(BF16)
    sq_hi, sq_lo = _split_bf16(o * o)
    ms = (_dot(sq_hi, ones_bd) + _dot(sq_lo, ones_bd)) * (1.0 / HEAD_DIM)
    cg = cg_ref[...]
    oc = (o * lax.rsqrt(ms + EPS) * cng_ref[...]) * (cg / (1.0 + jnp.exp(-cg)))
    mix_ref[:, 0:A_WIDTH] = oa_ref[...]
    mix_ref[:, A_WIDTH:A_WIDTH + B_WIDTH] = ob_ref[...]
    mix_ref[:, A_WIDTH + B_WIDTH:] = oc.astype(BF16)
    x1 = x_ref[...] + _dot(mix_ref[...], wo_ref[...])
    x1_ref[...] = x1
    h2 = _rms(x1, g2_ref[...]).astype(BF16)
    h2_ref[...] = h2
    logits = _dot(h2, wr_ref[...])
    lane = lax.broadcasted_iota(jnp.int32, logits.shape, 1)
    logits = jnp.where(lane < N_EXPERTS, logits, NEG_INF)
    p = jnp.exp(logits - jnp.max(logits, axis=1, keepdims=True))
    aff = p / jnp.sum(p, axis=1, keepdims=True)
    aff_ref[...] = aff.T[0:N_EXPERTS, :]


def _out_proj(x, oa, ob, o_f, o_b, cg, cng, wo_bf, layer, g2, wr_cat):
    n = x.shape[0]
    tm = min(IN_PROJ_ROWS, n)
    row = lambda width: pl.BlockSpec((tm, width), lambda i: (i, 0))
    full = lambda a: pl.BlockSpec(a.shape, lambda i: (0,) * a.ndim)
    return pl.pallas_call(
        _out_proj_kernel,
        grid=(n // tm,),
        in_specs=[row(D_MODEL), row(A_WIDTH), row(B_WIDTH), row(C_WIDTH), row(C_WIDTH), row(C_WIDTH),
                  full(cng), pl.BlockSpec((None, D_MODEL, D_MODEL), lambda i: (layer, 0, 0)), full(g2), full(wr_cat)],
        out_specs=[row(D_MODEL), row(D_MODEL), pl.BlockSpec((N_EXPERTS, tm), lambda i: (0, i))],
        out_shape=[jax.ShapeDtypeStruct((n, D_MODEL), F32), jax.ShapeDtypeStruct((n, D_MODEL), BF16),
                   jax.ShapeDtypeStruct((N_EXPERTS, n), F32)],
        scratch_shapes=[pltpu.VMEM((tm, D_MODEL), BF16)],
        compiler_params=_cparams(("parallel",)),
        name="out_proj",
    )(x, oa, ob, o_f, o_b, cg, cng, wo_bf, g2, wr_cat)


def _route_kernel(aff_ref, pos_ref, offs_ref, *, cap):
    n = aff_ref.shape[1]
    nt = n // ROUTE_TILE
    bits = pltpu.bitcast(aff_ref[...], jnp.int32)

    def search(it, thr):
        cand = thr | jnp.left_shift(jnp.int32(1), 30 - it)
        cnt = jnp.sum(jnp.where(bits >= cand, 1.0, 0.0), axis=1, keepdims=True)
        return jnp.where(cnt >= cap, cand, thr)

    thr = lax.fori_loop(0, 31, search, jnp.zeros((N_EXPERTS, 1), jnp.int32))
    n_gt = jnp.sum(jnp.where(bits > thr, 1.0, 0.0), axis=1, keepdims=True)
    n_tie = cap - n_gt

    ui = lax.broadcasted_iota(jnp.int32, (ROUTE_TILE, ROUTE_TILE), 0)
    uj = lax.broadcasted_iota(jnp.int32, (ROUTE_TILE, ROUTE_TILE), 1)
    upper = (ui <= uj).astype(BF16)
    lane = lax.broadcasted_iota(jnp.int32, offs_ref.shape, 1)

    def tile(i, carry):
        c_gt, c_eq, offs = carry
        col = pl.multiple_of(i * ROUTE_TILE, ROUTE_TILE)
        b = pltpu.bitcast(aff_ref[:, pl.ds(col, ROUTE_TILE)], jnp.int32)
        gt = jnp.where(b > thr, 1.0, 0.0)
        eq = jnp.where(b == thr, 1.0, 0.0)
        inc = _dot(jnp.concatenate([gt, eq], axis=0).astype(BF16), upper)
        gt_before = c_gt + inc[:N_EXPERTS] - gt
        eq_before = c_eq + inc[N_EXPERTS:] - eq
        sel = (gt > 0.0) | ((eq > 0.0) & (eq_before < n_tie))
        pos = gt_before + jnp.minimum(eq_before, n_tie)
        pos_ref[:, pl.ds(col, ROUTE_TILE)] = jnp.where(sel, pos, -1.0).astype(jnp.int32)
        start = c_gt + jnp.minimum(c_eq, n_tie)
        offs = jnp.where(lane == i, start.astype(jnp.int32), offs)
        return (c_gt + jnp.sum(gt, axis=1, keepdims=True), c_eq + jnp.sum(eq, axis=1, keepdims=True), offs)

    zero = jnp.zeros((N_EXPERTS, 1), F32)
    offs = jnp.where(lane == nt, cap, 0).astype(jnp.int32)
    _, _, offs = lax.fori_loop(0, nt, tile, (zero, zero, offs))
    offs_ref[...] = offs


def _route(aff_t, cap):
    n = aff_t.shape[1]
    nt = n // ROUTE_TILE
    return pl.pallas_call(
        functools.partial(_route_kernel, cap=cap),
        out_shape=[jax.ShapeDtypeStruct((N_EXPERTS, n), jnp.int32),
                   jax.ShapeDtypeStruct((N_EXPERTS, nt + 1), jnp.int32)],
        compiler_params=pltpu.CompilerParams(vmem_limit_bytes=VMEM_LIMIT),
        name="route",
    )(aff_t)


def _align(v):
    return (v // MOE_ALIGN) * MOE_ALIGN


def _moe_passes(offs_ref, t, subs, x0, s):
    k = jnp.int32(1)
    for g in range(MOE_PAIR):
        need = offs_ref[x0 + g, t * subs + s + 1] - _align(offs_ref[x0 + g, t * subs + s])
        k = jnp.maximum(k, (need + MOE_WIN - 1) // MOE_WIN)
    return k


def _moe_windows(offs_ref, pos_ref, t, subs, x0, s, j):
    slot = lax.broadcasted_iota(jnp.int32, (MOE_WIN, ROUTE_TILE), 0)
    hits, rels = [], []
    for g in range(MOE_PAIR):
        x = x0 + g
        ws = _align(offs_ref[x, t * subs + s]) + j * MOE_WIN
        pos = pos_ref[pl.ds(x, 1), s * ROUTE_TILE:(s + 1) * ROUTE_TILE]
        hits.append((pos - ws) == slot)
        rels.append(pl.multiple_of(ws - _align(offs_ref[x, t * subs]), MOE_ALIGN))
    onehot = jnp.concatenate([jnp.where(h, 1.0, 0.0) for h in hits], axis=0).astype(BF16)
    return hits, rels, onehot


def _moe_chunks(offs_ref, t, subs, x):
    base = _align(offs_ref[x, t * subs])
    return base, (offs_ref[x, (t + 1) * subs] - base + MOE_CHUNK - 1) // MOE_CHUNK


def _moe_gather_kernel(offs_ref, h_ref, pos_ref, gate_ref, xe_ref, stage_ref, carry_ref, sem):
    t = pl.program_id(0)
    steps = pl.num_programs(1)
    e0 = pl.program_id(1) * MOE_STEP
    tt = h_ref.shape[0]
    subs = tt // ROUTE_TILE
    k = t * steps + pl.program_id(1)
    slot = k % 2
    x_ref = stage_ref.at[slot]

    def off(x, s):
        return offs_ref[x, t * subs + s]

    def copies(kk, half, act):
        step_t, step_q = kk // steps, kk % steps
        for g in range(MOE_STEP):
            x = step_q * MOE_STEP + g
            base, n_chunks = _moe_chunks(offs_ref, step_t, subs, x)

            def body(c, carry):
                r0 = pl.multiple_of(c * MOE_CHUNK, MOE_CHUNK)
                act(pltpu.make_async_copy(stage_ref.at[half, g, pl.ds(r0, MOE_CHUNK), :],
                                          xe_ref.at[x, pl.ds(pl.multiple_of(base + r0, MOE_ALIGN), MOE_CHUNK), :],
                                          sem.at[half]))
                return carry

            lax.fori_loop(0, n_chunks, body, 0)

    @pl.when(k >= 2)
    def _():
        copies(k - 2, slot, lambda cp: cp.wait())

    @pl.when(t == 0)
    def _():
        carry_ref[pl.ds(e0, MOE_STEP)] = jnp.zeros((MOE_STEP, MOE_ALIGN, XE_WIDTH), BF16)
        cap = xe_ref.shape[1] - MOE_CHUNK
        x_ref[0, 0:MOE_CHUNK, :] = jnp.zeros((MOE_CHUNK, XE_WIDTH), BF16)
        pads = [pltpu.make_async_copy(x_ref.at[0, 0:MOE_CHUNK, :], xe_ref.at[e0 + g, cap:cap + MOE_CHUNK, :],
                                      sem.at[slot]) for g in range(MOE_STEP)]
        for p in pads:
            p.start()
        for p in pads:
            p.wait()

    for g in range(MOE_STEP):
        x_ref[g, 0:MOE_ALIGN, :] = carry_ref[e0 + g]

    row = lax.broadcasted_iota(jnp.int32, (MOE_WIN, 1), 0)
    lane = lax.broadcasted_iota(jnp.int32, (MOE_WIN, XE_WIDTH - D_MODEL), 1)

    def gather_pass(pair, s, j):
        x0 = e0 + pair * MOE_PAIR
        cols = slice(s * ROUTE_TILE, (s + 1) * ROUTE_TILE)
        hits, rels, onehot = _moe_windows(offs_ref, pos_ref, t, subs, x0, s, j)
        r = _dot(onehot, h_ref[cols, :])
        for g in range(MOE_PAIR):
            gate = jnp.sum(jnp.where(hits[g], gate_ref[pl.ds(x0 + g, 1), cols], 0.0), axis=1, keepdims=True)
            g_hi = gate.astype(BF16).astype(F32)
            extra = jnp.where(lane == 0, g_hi, jnp.where(lane == 1, gate - g_hi, 0.0))
            new = jnp.concatenate([r[g * MOE_WIN:(g + 1) * MOE_WIN, :], extra], axis=1)
            ws = _align(off(x0 + g, s)) + j * MOE_WIN
            own = row >= off(x0 + g, s) - ws
            buf = pair * MOE_PAIR + g
            if isinstance(j, int):
                head = pl.ds(rels[g], MOE_ALIGN)
                x_ref[buf, head, :] = jnp.where(own[:MOE_ALIGN], new[:MOE_ALIGN],
                                                x_ref[buf, head, :].astype(F32)).astype(BF16)
                x_ref[buf, pl.ds(rels[g] + MOE_ALIGN, MOE_WIN - MOE_ALIGN), :] = new[MOE_ALIGN:].astype(BF16)
            else:
                own = own & (row < off(x0 + g, s + 1) - ws)
                win = pl.ds(rels[g], MOE_WIN)
                x_ref[buf, win, :] = jnp.where(own, new, x_ref[buf, win, :].astype(F32)).astype(BF16)

    for pair in range(MOE_STEP // MOE_PAIR):
        for s in range(subs):
            gather_pass(pair, s, 0)
    for pair in range(MOE_STEP // MOE_PAIR):
        for s in range(subs):
            lax.fori_loop(1, _moe_passes(offs_ref, t, subs, e0 + pair * MOE_PAIR, s),
                          lambda j, carry: (gather_pass(pair, s, j), carry)[1], 0)

    for g in range(MOE_STEP):
        x = e0 + g
        base, n_chunks = _moe_chunks(offs_ref, t, subs, x)
        last_group = pl.multiple_of(_align(off(x, subs)) - base, MOE_ALIGN)
        tail = x_ref[g, pl.ds(last_group, MOE_ALIGN), :].astype(F32)
        tail = jnp.where(row[:MOE_ALIGN] < off(x, subs) - base - last_group, tail, 0.0).astype(BF16)
        x_ref[g, pl.ds(last_group, MOE_ALIGN), :] = tail
        carry_ref[x] = tail

        def fill(i, carry):
            x_ref[g, pl.ds(pl.multiple_of(i * MOE_ALIGN, MOE_ALIGN), MOE_ALIGN), :] = jnp.zeros((MOE_ALIGN, XE_WIDTH), BF16)
            return carry

        lax.fori_loop(last_group // MOE_ALIGN + 1, n_chunks * MOE_CHUNK // MOE_ALIGN, fill, 0)

    copies(k, slot, lambda cp: cp.start())

    @pl.when(k == pl.num_programs(0) * steps - 1)
    def _():
        copies(k - 1, 1 - slot, lambda cp: cp.wait())
        copies(k, slot, lambda cp: cp.wait())


def _moe_ffn_kernel(x_ref, wg_ref, wu_ref, wd_ref, y_ref, wg_bf, wu_bf, wd_bf):
    j = pl.program_id(1)

    @pl.when(j == 0)
    def _():
        wg_bf[...] = wg_ref[...].astype(BF16)
        wu_bf[...] = wu_ref[...].astype(BF16)
        wd_bf[...] = wd_ref[...].astype(BF16)
        y_ref[0] = jnp.zeros(y_ref.shape[1:], BF16)

    @pl.when(j > 0)
    def _():
        x = x_ref[0, :, 0:D_MODEL]
        gate = jnp.sum(x_ref[0, :, D_MODEL:XE_WIDTH].astype(F32), axis=1, keepdims=True)
        hg = _dot(x, wg_bf[...])
        hu = _dot(x, wu_bf[...])
        act = (hg / (1.0 + jnp.exp(-hg))) * hu * gate
        y_ref[0] = _dot(act.astype(BF16), wd_bf[...]).astype(BF16)


def _moe_combine_kernel(offs_ref, x1_ref, pos_ref, y_ref, fg_ref, o_ref, y_buf, sem, *, final_norm):
    t = pl.program_id(0)
    q = pl.program_id(1)
    steps = pl.num_programs(1)
    e0 = q * MOE_STEP
    tt = x1_ref.shape[0]
    subs = tt // ROUTE_TILE
    k = t * steps + q
    slot = k % 2

    def fetch(kk, half, act):
        step_t, step_q = kk // steps, kk % steps
        for g in range(MOE_STEP):
            x = step_q * MOE_STEP + g
            base, n_chunks = _moe_chunks(offs_ref, step_t, subs, x)

            def body(c, carry):
                r0 = pl.multiple_of(c * MOE_CHUNK, MOE_CHUNK)
                act(pltpu.make_async_copy(y_ref.at[x, pl.ds(pl.multiple_of(base + r0, MOE_ALIGN), MOE_CHUNK), :],
                                          y_buf.at[half, g, pl.ds(r0, MOE_CHUNK), :], sem.at[half]))
                return carry

            lax.fori_loop(0, n_chunks, body, 0)

    @pl.when(k == 0)
    def _():
        y_buf[...] = jnp.zeros_like(y_buf)
        fetch(k, slot, lambda cp: cp.start())

    @pl.when(k + 1 < pl.num_programs(0) * steps)
    def _():
        fetch(k + 1, 1 - slot, lambda cp: cp.start())

    fetch(k, slot, lambda cp: cp.wait())

    @pl.when(q == 0)
    def _():
        o_ref[...] = x1_ref[...]

    def combine_pass(pair, s, j):
        cols = slice(s * ROUTE_TILE, (s + 1) * ROUTE_TILE)
        _, rels, onehot = _moe_windows(offs_ref, pos_ref, t, subs, e0 + pair * MOE_PAIR, s, j)
        ycat = jnp.concatenate([y_buf[slot, pair * MOE_PAIR + g, pl.ds(rels[g], MOE_WIN), :]
                                for g in range(MOE_PAIR)], axis=0)
        o_ref[cols, :] += _tn_dot(onehot, ycat)

    for pair in range(MOE_STEP // MOE_PAIR):
        for s in range(subs):
            combine_pass(pair, s, 0)
    for pair in range(MOE_STEP // MOE_PAIR):
        for s in range(subs):
            lax.fori_loop(1, _moe_passes(offs_ref, t, subs, e0 + pair * MOE_PAIR, s),
                          lambda j, carry: (combine_pass(pair, s, j), carry)[1], 0)

    if final_norm:
        @pl.when(q == pl.num_programs(1) - 1)
        def _():
            o_ref[...] = _rms(o_ref[...], fg_ref[...])


def _moe(offs, h2, x1, pos, gate, wg, wu, wd, layer, final_g, final_norm, cap):
    n = h2.shape[0]
    tt, tc = min(MOE_TILE, n), min(MOE_COMBINE_TILE, n)
    assert cap % MOE_CHUNK == 0 and n % tt == 0 and n % tc == 0
    steps = N_EXPERTS // MOE_STEP
    stage_rows = lambda tile: tile + MOE_ALIGN + MOE_WIN * pl.cdiv(ROUTE_TILE + MOE_ALIGN, MOE_WIN)
    tok = lambda tile: pl.BlockSpec((tile, D_MODEL), lambda t, q, offs: (t, 0))
    per_tok = lambda tile: pl.BlockSpec((N_EXPERTS, tile), lambda t, q, offs: (0, t))
    hbm = pl.BlockSpec(memory_space=pl.ANY)
    sems = ("arbitrary", "arbitrary")

    xe = pl.pallas_call(
        _moe_gather_kernel,
        grid_spec=pltpu.PrefetchScalarGridSpec(
            num_scalar_prefetch=1, grid=(n // tt, steps),
            in_specs=[tok(tt), per_tok(tt), per_tok(tt)], out_specs=hbm,
            scratch_shapes=[pltpu.VMEM((2, MOE_STEP, stage_rows(tt), XE_WIDTH), BF16),
                            pltpu.VMEM((N_EXPERTS, MOE_ALIGN, XE_WIDTH), BF16), pltpu.SemaphoreType.DMA((2,))],
        ),
        out_shape=jax.ShapeDtypeStruct((N_EXPERTS, cap + MOE_CHUNK, XE_WIDTH), BF16),
        compiler_params=_cparams(sems),
        name="moe_gather",
    )(offs, h2, pos, gate)

    fb = min(MOE_FFN_ROWS, cap)
    nb = cap // fb
    wspec = pl.BlockSpec((None, None, D_MODEL, D_MODEL), lambda e, j: (layer, e, 0, 0))
    w_bf = pltpu.VMEM((D_MODEL, D_MODEL), BF16)
    y = pl.pallas_call(
        _moe_ffn_kernel,
        grid=(N_EXPERTS, nb + 1),
        in_specs=[pl.BlockSpec((1, fb, XE_WIDTH), lambda e, j: (e, jnp.maximum(j - 1, 0), 0)), wspec, wspec, wspec],
        out_specs=pl.BlockSpec((1, fb, D_MODEL), lambda e, j: (e, jnp.where(j == 0, nb, j - 1), 0)),
        out_shape=jax.ShapeDtypeStruct((N_EXPERTS, cap + fb, D_MODEL), BF16),
        scratch_shapes=[w_bf, w_bf, w_bf],
        compiler_params=_cparams(("parallel", "arbitrary")),
        name="moe_ffn",
    )(xe, wg, wu, wd)

    return pl.pallas_call(
        functools.partial(_moe_combine_kernel, final_norm=final_norm),
        grid_spec=pltpu.PrefetchScalarGridSpec(
            num_scalar_prefetch=1, grid=(n // tc, steps),
            in_specs=[tok(tc), per_tok(tc), hbm, pl.BlockSpec((1, D_MODEL), lambda t, q, offs: (0, 0))],
            out_specs=tok(tc),
            scratch_shapes=[pltpu.VMEM((2, MOE_STEP, stage_rows(tc), D_MODEL), BF16),
                            pltpu.SemaphoreType.DMA((2,))],
        ),
        out_shape=jax.ShapeDtypeStruct((n, D_MODEL), F32),
        compiler_params=_cparams(sems),
        name="moe_combine",
    )(offs, x1, pos, y, final_g)


def _prep_layer(l, norm1_g, w_in, a_sink, a_norm_g, b_rel_bias, b_norm_g, c_alpha_w2_f, c_alpha_b_f,
                c_alpha_w2_b, c_alpha_b_b, c_norm_g, w_out, norm2_g, w_router, w_gate, w_up, w_down):
    z = jnp.zeros((C_LOWRANK, C_WIDTH), F32)
    w2 = jnp.concatenate([jnp.concatenate([c_alpha_w2_f[l], z], axis=1),
                          jnp.concatenate([z, c_alpha_w2_b[l]], axis=1)], axis=0)
    b2 = jnp.concatenate([c_alpha_b_f[l], c_alpha_b_b[l]])[None, :]
    wr_cat = jnp.pad(w_router[l].astype(BF16), ((0, 0), (0, LANES - N_EXPERTS)))
    return dict(
        layer=l, g1=norm1_g[l][None, :], w_in=w_in.astype(BF16), w2=w2, b2=b2,
        sink=a_sink[l], a_g=a_norm_g[l][None, :], bias=_bias_table(b_rel_bias[l]), b_g=b_norm_g[l][None, :],
        c_g=c_norm_g[l][None, :], w_out=w_out.astype(BF16), g2=norm2_g[l][None, :],
        wr_cat=wr_cat, wg=w_gate, wu=w_up, wd=w_down)


def _layer(x, p, bsz, seq, final_g, final_norm):
    n = bsz * seq
    l = p["layer"]
    aq, ak, av, bq, bk, bv, cq, ck, cv, cg, la = _in_proj(x, p["g1"], p["w_in"], l, p["w2"], p["b2"], seq)
    oa = _attn_a(aq, ak, av, p["sink"], p["a_g"], bsz, seq)
    ob = _attn_b(bq, bk, bv, p["bias"], p["b_g"], bsz, seq)
    o_f, o_b = _gla(cq, ck, cv, la, bsz, seq)
    x1, h2, aff = _out_proj(x, oa, ob, o_f, o_b, cg, p["c_g"], p["w_out"], l, p["g2"], p["wr_cat"])
    cap = EC_CAPACITY * n // N_EXPERTS
    pos, offs = _route(aff, cap)
    return _moe(offs, h2, x1, pos, aff, p["wg"], p["wu"], p["wd"], l, final_g, final_norm, cap)


def _trunk(x, layers, final_g):
    bsz, seq, _ = x.shape
    y = x.reshape(bsz * seq, D_MODEL)
    for l, p in enumerate(layers):
        y = _layer(y, p, bsz, seq, final_g, final_norm=(l == len(layers) - 1))
    return y.reshape(bsz, seq, D_MODEL)


def kernel(x_prompt, x_sample, norm1_g, w_in, a_sink, a_norm_g, b_rel_bias, b_norm_g, c_alpha_w2_f, c_alpha_b_f,
           c_alpha_w2_b, c_alpha_b_b, c_norm_g, w_out, norm2_g, w_router, w_gate, w_up, w_down, final_g):
    depth = w_in.shape[0]
    layers = [_prep_layer(l, norm1_g, w_in, a_sink, a_norm_g, b_rel_bias, b_norm_g, c_alpha_w2_f, c_alpha_b_f,
                          c_alpha_w2_b, c_alpha_b_b, c_norm_g, w_out, norm2_g, w_router, w_gate, w_up, w_down)
              for l in range(depth)]
    fg = final_g[None, :]
    return _trunk(x_prompt, layers, fg), _trunk(x_sample, layers, fg)
```

```python
import functools

import jax
import jax.numpy as jnp
from jax import lax
from jax.experimental import pallas as pl
from jax.experimental.pallas import tpu as pltpu

F32 = jnp.float32
BF16 = jnp.bfloat16

D_MODEL = 1024
HEAD_DIM = 64
A_WIDTH = 512
A_HEADS = 8
A_KV_HEADS = 2
A_KV_WIDTH = 128
WINDOW = 128
ROPE_THETA = 500000.0
ROPE_DIM = 16
B_WIDTH = 256
B_HEADS = 4
GRID_W = 64
WIN_H = 8
WIN_W = 16
C_WIDTH = 256
C_HEADS = 4
C_LOWRANK = 16
C_TAU = 16.0
C_CHUNK = 64
N_EXPERTS = 16
EC_CAPACITY = 2
EPS = 1e-6
NEG_INF = -1e30
IN_WIDTH = 2592

_OFF_AQ, _OFF_AK, _OFF_AV = 0, 512, 640
_OFF_BQ, _OFF_BK, _OFF_BV = 768, 1024, 1280
_OFF_CQ, _OFF_CK, _OFF_CV, _OFF_CG = 1536, 1792, 2048, 2304
_OFF_LR = 2560

LANES = 128
VMEM_LIMIT = 56 * 1024 * 1024

IN_PROJ_ROWS = 512
GLA_BLOCK = 512
ROUTE_TILE = 256
MOE_TILE = 2048
MOE_COMBINE_TILE = 1024
MOE_STEP = 4
MOE_PAIR = 4
MOE_WIN = 64
MOE_ALIGN = 16
MOE_CHUNK = 128
MOE_FFN_ROWS = 1024
XE_WIDTH = D_MODEL + LANES


def _cparams(sem):
    return pltpu.CompilerParams(dimension_semantics=sem, vmem_limit_bytes=VMEM_LIMIT)


def _nt_dot(a, b):
    return lax.dot_general(a, b, (((1,), (1,)), ((), ())), preferred_element_type=F32)


def _tn_dot(a, b):
    return lax.dot_general(a, b, (((0,), (0,)), ((), ())), preferred_element_type=F32)


def _dot(a, b):
    return jnp.dot(a, b, preferred_element_type=F32)


def _rms(x, g):
    return x * lax.rsqrt(jnp.mean(x * x, axis=-1, keepdims=True) + EPS) * g


def _same_head_mask(n_rows, rows_per_head):
    width = (n_rows // rows_per_head) * HEAD_DIM
    ri = lax.broadcasted_iota(jnp.int32, (n_rows, width), 0) // rows_per_head
    li = lax.broadcasted_iota(jnp.int32, (n_rows, width), 1) // HEAD_DIM
    return ri == li


def _split_bf16(x):
    hi = x.astype(BF16)
    return hi, (x - hi.astype(F32)).astype(BF16)


def _in_proj_kernel(x_ref, g_ref, w_ref, cos_ref, s1_ref, s2_ref, w2_ref, b2_ref,
                    aq_ref, ak_ref, av_ref, bq_ref, bk_ref, bv_ref,
                    cq_ref, ck_ref, cv_ref, cg_ref, la_ref):
    h = _rms(x_ref[...], g_ref[...]).astype(BF16)
    y = _dot(h, w_ref[...])

    def proj(lo, width):
        return y[:, lo:lo + width]

    cos, s1, s2 = cos_ref[...], s1_ref[...], s2_ref[...]

    def rope(t):
        return t * cos + pltpu.roll(t, LANES - ROPE_DIM // 2, 1) * s1 + pltpu.roll(t, ROPE_DIM // 2, 1) * s2

    scale = HEAD_DIM ** -0.5
    aq = proj(_OFF_AQ, A_WIDTH)
    for c in range(A_WIDTH // LANES):
        aq_ref[:, c * LANES:(c + 1) * LANES] = (rope(aq[:, c * LANES:(c + 1) * LANES]) * scale).astype(BF16)
    ak_ref[...] = rope(proj(_OFF_AK, A_KV_WIDTH)).astype(BF16)
    av_ref[...] = proj(_OFF_AV, A_KV_WIDTH).astype(BF16)
    bq_ref[...] = (proj(_OFF_BQ, B_WIDTH) * scale).astype(BF16)
    bk_ref[...] = proj(_OFF_BK, B_WIDTH).astype(BF16)
    bv_ref[...] = proj(_OFF_BV, B_WIDTH).astype(BF16)
    cq_ref[...] = proj(_OFF_CQ, C_WIDTH) * scale
    ck_ref[...] = proj(_OFF_CK, C_WIDTH)
    cv_ref[...] = proj(_OFF_CV, C_WIDTH).astype(BF16)
    cg_ref[...] = proj(_OFF_CG, C_WIDTH)
    z = _dot(proj(_OFF_LR, 2 * C_LOWRANK).astype(BF16), w2_ref[...]) + b2_ref[...]
    la_ref[...] = (jnp.minimum(z, 0.0) - jnp.log(1.0 + jnp.exp(-jnp.abs(z)))) * (1.0 / C_TAU)


def _rope_tables(seq):
    half = ROPE_DIM // 2
    inv = jnp.power(jnp.float32(ROPE_THETA), -jnp.arange(half, dtype=F32) * (2.0 / ROPE_DIM))
    ang = jnp.arange(seq, dtype=F32)[:, None] * inv[None, :]
    cos, sin = jnp.cos(ang), jnp.sin(ang)
    ones = jnp.ones((seq, HEAD_DIM - ROPE_DIM), F32)
    zeros = jnp.zeros((seq, HEAD_DIM - half), F32)
    c = jnp.concatenate([cos, cos, ones], axis=1)
    s1 = jnp.concatenate([-sin, zeros], axis=1)
    s2 = jnp.concatenate([jnp.zeros((seq, half), F32), sin, jnp.zeros((seq, HEAD_DIM - ROPE_DIM), F32)], axis=1)
    two = lambda t: jnp.concatenate([t, t], axis=1)
    return two(c), two(s1), two(s2)


def _in_proj(x, g1, w_in_bf, layer, w2, b2, seq):
    n = x.shape[0]
    tm = min(IN_PROJ_ROWS, seq)
    per_seq = seq // tm
    cos, s1, s2 = _rope_tables(seq)
    row = lambda width: pl.BlockSpec((tm, width), lambda i: (i, 0))
    full = lambda a: pl.BlockSpec(a.shape, lambda i: (0,) * a.ndim)
    tab = pl.BlockSpec((tm, LANES), lambda i: (i % per_seq, 0))
    widths = [(A_WIDTH, BF16), (A_KV_WIDTH, BF16), (A_KV_WIDTH, BF16), (B_WIDTH, BF16), (B_WIDTH, BF16),
              (B_WIDTH, BF16), (C_WIDTH, F32), (C_WIDTH, F32), (C_WIDTH, BF16), (C_WIDTH, F32), (2 * C_WIDTH, F32)]
    return pl.pallas_call(
        _in_proj_kernel,
        grid=(n // tm,),
        in_specs=[row(D_MODEL), full(g1), pl.BlockSpec((None, D_MODEL, IN_WIDTH), lambda i: (layer, 0, 0)),
                  tab, tab, tab, full(w2), full(b2)],
        out_specs=[row(w) for w, _ in widths],
        out_shape=[jax.ShapeDtypeStruct((n, w), dt) for w, dt in widths],
        compiler_params=_cparams(("parallel",)),
        name="in_proj",
    )(x, g1, w_in_bf, cos, s1, s2, w2, b2)


A_QBLOCKS = 4


def _attn_a_kernel(sink_ref, q_ref, kp_ref, kc_ref, kn_ref, vp_ref, vc_ref, vn_ref, g_ref, o_ref, acc_ref, *, steps):
    i = pl.program_id(1)
    grp = A_HEADS // A_KV_HEADS
    k = jnp.concatenate([kp_ref[...], kc_ref[...], kn_ref[...]], axis=0)
    v = jnp.concatenate([vp_ref[...], vc_ref[...], vn_ref[...]], axis=0)
    qq = lax.broadcasted_iota(jnp.int32, (WINDOW, 3 * WINDOW), 0)
    kk = lax.broadcasted_iota(jnp.int32, (WINDOW, 3 * WINDOW), 1)
    band = jnp.abs(kk - WINDOW - qq) <= WINDOW
    for blk in range(A_QBLOCKS):
        visible = band
        if blk == 0:
            visible = visible & jnp.logical_not((kk < WINDOW) & (i == 0))
        if blk == A_QBLOCKS - 1:
            visible = visible & jnp.logical_not((kk >= 2 * WINDOW) & (i == steps - 1))
        bias = jnp.where(visible, 0.0, NEG_INF)
        rows = slice(blk * WINDOW, (blk + 1) * WINDOW)
        keys = slice(blk * WINDOW, (blk + 3) * WINDOW)
        for kv in range(A_KV_HEADS):
            heads = range(kv * grp, (kv + 1) * grp)
            q = jnp.concatenate([q_ref[rows, h * HEAD_DIM:(h + 1) * HEAD_DIM] for h in heads], axis=0)
            s = _nt_dot(q, k[keys, kv * HEAD_DIM:(kv + 1) * HEAD_DIM])
            ps, dens = [], []
            for g_i, h in enumerate(heads):
                sh = s[g_i * WINDOW:(g_i + 1) * WINDOW, :] + bias
                sink = sink_ref[h]
                m = jnp.maximum(jnp.max(sh, axis=-1, keepdims=True), sink)
                p = jnp.exp(sh - m)
                dens.append(jnp.sum(p, axis=-1, keepdims=True) + jnp.exp(sink - m))
                ps.append(p.astype(BF16))
            o = _dot(jnp.concatenate(ps, axis=0), v[keys, kv * HEAD_DIM:(kv + 1) * HEAD_DIM])
            for g_i, h in enumerate(heads):
                acc_ref[rows, h * HEAD_DIM:(h + 1) * HEAD_DIM] = o[g_i * WINDOW:(g_i + 1) * WINDOW, :] / dens[g_i]
    o_ref[...] = _rms(acc_ref[...], g_ref[...]).astype(BF16)


def _attn_a(aq, ak, av, sink, g, bsz, seq):
    nb = seq // WINDOW
    assert nb % A_QBLOCKS == 0
    steps = nb // A_QBLOCKS
    qrows = A_QBLOCKS * WINDOW
    qspec = pl.BlockSpec((qrows, A_WIDTH), lambda b, i: (b * steps + i, 0))
    kprev = pl.BlockSpec((WINDOW, A_KV_WIDTH), lambda b, i: (b * nb + jnp.maximum(i * A_QBLOCKS - 1, 0), 0))
    kcur = pl.BlockSpec((qrows, A_KV_WIDTH), lambda b, i: (b * steps + i, 0))
    knext = pl.BlockSpec((WINDOW, A_KV_WIDTH), lambda b, i: (b * nb + jnp.minimum((i + 1) * A_QBLOCKS, nb - 1), 0))
    return pl.pallas_call(
        functools.partial(_attn_a_kernel, steps=steps),
        grid=(bsz, steps),
        in_specs=[pl.BlockSpec(memory_space=pltpu.SMEM), qspec, kprev, kcur, knext, kprev, kcur, knext,
                  pl.BlockSpec((1, A_WIDTH), lambda b, i: (0, 0))],
        out_specs=qspec,
        out_shape=jax.ShapeDtypeStruct((bsz * seq, A_WIDTH), BF16),
        scratch_shapes=[pltpu.VMEM((qrows, A_WIDTH), F32)],
        compiler_params=_cparams(("parallel", "parallel")),
        name="attn_a",
    )(sink, aq, ak, ak, ak, av, av, av, g)


B_GROUP = 16
B_KEYS = WIN_H * GRID_W


def _bias_table_kernel(rel_ref, o_ref):
    c = lax.broadcasted_iota(jnp.int32, (GRID_W, GRID_W), 0)
    w = lax.broadcasted_iota(jnp.int32, (GRID_W, GRID_W), 1)
    cstart = jnp.clip(c - WIN_W // 2, 0, GRID_W - WIN_W)
    colmask = (w >= cstart) & (w < cstart + WIN_W)
    col_off = jnp.clip(w - c + (WIN_W - 1), 0, 2 * WIN_W - 2)
    n_ro, n_co = 2 * WIN_H - 1, 2 * WIN_W - 1
    for h in range(B_HEADS):
        for ro in range(n_ro):
            def body(j, acc):
                return jnp.where(col_off == j, rel_ref[(h * n_ro + ro) * n_co + j], acc)
            t = lax.fori_loop(0, n_co, body, jnp.zeros((GRID_W, GRID_W), F32))
            t = jnp.where(colmask, t, NEG_INF)
            for p in range(WIN_H):
                kidx = ro - (WIN_H - 1) + p
                if 0 <= kidx < WIN_H:
                    o_ref[p, h, :, kidx * GRID_W:(kidx + 1) * GRID_W] = t


def _bias_table(rel_bias):
    return pl.pallas_call(
        _bias_table_kernel,
        in_specs=[pl.BlockSpec(memory_space=pltpu.SMEM)],
        out_shape=jax.ShapeDtypeStruct((WIN_H, B_HEADS, GRID_W, B_KEYS), F32),
        name="bias_table",
    )(rel_bias.reshape(-1)).reshape(WIN_H, B_HEADS * GRID_W, B_KEYS)


def _attn_b_kernel(q_ref, kp_ref, kc_ref, kn_ref, vp_ref, vc_ref, vn_ref, bias_ref, g_ref, o_ref,
                   kbuf, vbuf, acc_ref, *, rows):
    gidx = pl.program_id(1)
    blk = B_GROUP * GRID_W
    kbuf[0:blk, :] = kp_ref[...]
    kbuf[blk:2 * blk, :] = kc_ref[...]
    kbuf[2 * blk:3 * blk, :] = kn_ref[...]
    vbuf[0:blk, :] = vp_ref[...]
    vbuf[blk:2 * blk, :] = vc_ref[...]
    vbuf[2 * blk:3 * blk, :] = vn_ref[...]
    same_head = _same_head_mask(B_HEADS * GRID_W, GRID_W)
    for j in range(B_GROUP):
        r = gidx * B_GROUP + j
        start = jnp.clip(r - WIN_H // 2, 0, rows - WIN_H)
        pat = r - start
        loc = pl.multiple_of((start - gidx * B_GROUP + B_GROUP) * GRID_W, GRID_W)
        kw = kbuf[pl.ds(loc, B_KEYS), :]
        vw = vbuf[pl.ds(loc, B_KEYS), :]
        qj = q_ref[j * GRID_W:(j + 1) * GRID_W, :]
        q = jnp.where(same_head, jnp.concatenate([qj] * B_HEADS, axis=0), jnp.zeros((), BF16))
        s = _nt_dot(q, kw) + bias_ref[pat]
        m = jnp.max(s, axis=-1, keepdims=True)
        p = jnp.exp(s - m)
        den = jnp.sum(p, axis=-1, keepdims=True)
        o = jnp.where(same_head, _dot(p.astype(BF16), vw) / den, 0.0)
        acc_ref[j * GRID_W:(j + 1) * GRID_W, :] = sum(o[h * GRID_W:(h + 1) * GRID_W, :] for h in range(B_HEADS))
    o_ref[...] = _rms(acc_ref[...], g_ref[...]).astype(BF16)


def _attn_b(bq, bk, bv, bias, g, bsz, seq):
    rows = seq // GRID_W
    assert rows % B_GROUP == 0 and rows >= WIN_H
    ng = rows // B_GROUP
    blk = B_GROUP * GRID_W
    cur = pl.BlockSpec((blk, B_WIDTH), lambda b, i: (b * ng + i, 0))
    prev = pl.BlockSpec((blk, B_WIDTH), lambda b, i: (b * ng + jnp.maximum(i - 1, 0), 0))
    nxt = pl.BlockSpec((blk, B_WIDTH), lambda b, i: (b * ng + jnp.minimum(i + 1, ng - 1), 0))
    return pl.pallas_call(
        functools.partial(_attn_b_kernel, rows=rows),
        grid=(bsz, ng),
        in_specs=[cur, prev, cur, nxt, prev, cur, nxt,
                  pl.BlockSpec(bias.shape, lambda b, i: (0, 0, 0)),
                  pl.BlockSpec((1, B_WIDTH), lambda b, i: (0, 0))],
        out_specs=cur,
        out_shape=jax.ShapeDtypeStruct((bsz * seq, B_WIDTH), BF16),
        scratch_shapes=[pltpu.VMEM((3 * blk, B_WIDTH), BF16), pltpu.VMEM((3 * blk, B_WIDTH), BF16),
                        pltpu.VMEM((blk, B_WIDTH), F32)],
        compiler_params=_cparams(("parallel", "parallel")),
        name="attn_b",
    )(bq, bk, bk, bk, bv, bv, bv, bias, g)


def _gla_decay(la_ref, reverse):
    tb = la_ref.shape[0]
    bi = lax.broadcasted_iota(jnp.int32, (tb, tb), 0)
    bj = lax.broadcasted_iota(jnp.int32, (tb, tb), 1)
    in_chunk = (bi // C_CHUNK) == (bj // C_CHUNK)
    cum_w = (in_chunk & ((bj >= bi) if reverse else (bj <= bi))).astype(BF16)
    la_hi, la_lo = _split_bf16(la_ref[...])
    return _dot(cum_w, la_hi) + _dot(cum_w, la_lo)


def _gla_chunk(q_ref, k_ref, v_ref, o_ref, b_all, st, c, reverse):
    same_head = _same_head_mask(C_HEADS * C_CHUNK, C_CHUNK)
    ti = lax.broadcasted_iota(jnp.int32, (C_CHUNK, C_HEADS * C_CHUNK), 0)
    si = lax.broadcasted_iota(jnp.int32, (C_CHUNK, C_HEADS * C_CHUNK), 1) % C_CHUNK
    tri = (si >= ti) if reverse else (si <= ti)
    zero = jnp.zeros((), BF16)
    rows = slice(c * C_CHUNK, (c + 1) * C_CHUNK)
    b = b_all[rows, :]
    b_tot = b[0:1, :] if reverse else b[C_CHUNK - 1:C_CHUNK, :]
    q_i = (q_ref[rows, :] * jnp.exp(b)).astype(BF16)
    kf = k_ref[rows, :]
    k_i = (kf * jnp.exp(-b)).astype(BF16)
    k_e = (kf * jnp.exp(b_tot - b)).astype(BF16)
    v = v_ref[rows, :]
    k_bd = jnp.where(same_head, jnp.concatenate([k_i] * C_HEADS, axis=0), zero)
    v_bd = jnp.where(same_head, jnp.concatenate([v] * C_HEADS, axis=0), zero)
    a = jnp.where(tri, _nt_dot(q_i, k_bd), 0.0)
    o_ref[rows, :] = _dot(a.astype(BF16), v_bd) + _nt_dot(q_i, st.astype(BF16))
    return st * jnp.exp(b_tot) + jnp.where(same_head, _tn_dot(v, k_e), 0.0)


def _gla_kernel(qf_ref, kf_ref, vf_ref, laf_ref, qb_ref, kb_ref, vb_ref, lab_ref, of_ref, ob_ref, sf_ref, sb_ref):
    @pl.when(pl.program_id(1) == 0)
    def _():
        sf_ref[...] = jnp.zeros_like(sf_ref)
        sb_ref[...] = jnp.zeros_like(sb_ref)

    n_chunks = qf_ref.shape[0] // C_CHUNK
    b_f = _gla_decay(laf_ref, reverse=False)
    b_b = _gla_decay(lab_ref, reverse=True)
    s_f, s_b = sf_ref[...], sb_ref[...]
    for c in range(n_chunks):
        s_f = _gla_chunk(qf_ref, kf_ref, vf_ref, of_ref, b_f, s_f, c, reverse=False)
        s_b = _gla_chunk(qb_ref, kb_ref, vb_ref, ob_ref, b_b, s_b, n_chunks - 1 - c, reverse=True)
    sf_ref[...] = s_f
    sb_ref[...] = s_b


def _gla(cq, ck, cv, la, bsz, seq):
    tb = min(GLA_BLOCK, seq)
    nb = seq // tb
    fwd = lambda col: pl.BlockSpec((tb, C_WIDTH), lambda b, i: (b * nb + i, col))
    bwd = lambda col: pl.BlockSpec((tb, C_WIDTH), lambda b, i: (b * nb + nb - 1 - i, col))
    out = jax.ShapeDtypeStruct((bsz * seq, C_WIDTH), F32)
    state = pltpu.VMEM((C_WIDTH, C_WIDTH), F32)
    return pl.pallas_call(
        _gla_kernel,
        grid=(bsz, nb),
        in_specs=[fwd(0), fwd(0), fwd(0), fwd(0), bwd(0), bwd(0), bwd(0), bwd(1)],
        out_specs=[fwd(0), bwd(0)],
        out_shape=[out, out],
        scratch_shapes=[state, state],
        compiler_params=_cparams(("parallel", "arbitrary")),
        name="gla",
    )(cq, ck, cv, la, cq, ck, cv, la)


def _out_proj_kernel(x_ref, oa_ref, ob_ref, of_ref, obw_ref, cg_ref, cng_ref, wo_ref, g2_ref, wr_ref,
                     x1_ref, h2_ref, aff_ref, mix_ref):
    o = of_ref[...] + obw_ref[...]
    gi = lax.broadcasted_iota(jnp.int32, (C_WIDTH, C_WIDTH), 0) // HEAD_DIM
    gj = lax.broadcasted_iota(jnp.int32, (C_WIDTH, C_WIDTH), 1) // HEAD_DIM
    ones_bd = (gi == gj).astype(BF16)
    sq_hi, sq_lo = _split_bf16(o * o)
    ms = (_dot(sq_hi, ones_bd) + _dot(sq_lo, ones_bd)) * (1.0 / HEAD_DIM)
    cg = cg_ref[...]
    oc = (o * lax.rsqrt(ms + EPS) * cng_ref[...]) * (cg / (1.0 + jnp.exp(-cg)))
    mix_ref[:, 0:A_WIDTH] = oa_ref[...]
    mix_ref[:, A_WIDTH:A_WIDTH + B_WIDTH] = ob_ref[...]
    mix_ref[:, A_WIDTH + B_WIDTH:] = oc.astype(BF16)
    x1 = x_ref[...] + _dot(mix_ref[...], wo_ref[...])
    x1_ref[...] = x1
    h2 = _rms(x1, g2_ref[...]).astype(BF16)
    h2_ref[...] = h2
    logits = _dot(h2, wr_ref[...])
    lane = lax.broadcasted_iota(jnp.int32, logits.shape, 1)
    logits = jnp.where(lane < N_EXPERTS, logits, NEG_INF)
    p = jnp.exp(logits - jnp.max(logits, axis=1, keepdims=True))
    aff = p / jnp.sum(p, axis=1, keepdims=True)
    aff_ref[...] = aff.T[0:N_EXPERTS, :]


def _out_proj(x, oa, ob, o_f, o_b, cg, cng, wo_bf, layer, g2, wr_cat):
    n = x.shape[0]
    tm = min(IN_PROJ_ROWS, n)
    row = lambda width: pl.BlockSpec((tm, width), lambda i: (i, 0))
    full = lambda a: pl.BlockSpec(a.shape, lambda i: (0,) * a.ndim)
    return pl.pallas_call(
        _out_proj_kernel,
        grid=(n // tm,),
        in_specs=[row(D_MODEL), row(A_WIDTH), row(B_WIDTH), row(C_WIDTH), row(C_WIDTH), row(C_WIDTH),
                  full(cng), pl.BlockSpec((None, D_MODEL, D_MODEL), lambda i: (layer, 0, 0)), full(g2), full(wr_cat)],
        out_specs=[row(D_MODEL), row(D_MODEL), pl.BlockSpec((N_EXPERTS, tm), lambda i: (0, i))],
        out_shape=[jax.ShapeDtypeStruct((n, D_MODEL), F32), jax.ShapeDtypeStruct((n, D_MODEL), BF16),
                   jax.ShapeDtypeStruct((N_EXPERTS, n), F32)],
        scratch_shapes=[pltpu.VMEM((tm, D_MODEL), BF16)],
        compiler_params=_cparams(("parallel",)),
        name="out_proj",
    )(x, oa, ob, o_f, o_b, cg, cng, wo_bf, g2, wr_cat)


def _route_kernel(aff_ref, pos_ref, offs_ref, *, cap):
    n = aff_ref.shape[1]
    nt = n // ROUTE_TILE
    bits = pltpu.bitcast(aff_ref[...], jnp.int32)

    def search(it, thr):
        cand = thr | jnp.left_shift(jnp.int32(1), 30 - it)
        cnt = jnp.sum(jnp.where(bits >= cand, 1.0, 0.0), axis=1, keepdims=True)
        return jnp.where(cnt >= cap, cand, thr)

    thr = lax.fori_loop(0, 31, search, jnp.zeros((N_EXPERTS, 1), jnp.int32))
    n_gt = jnp.sum(jnp.where(bits > thr, 1.0, 0.0), axis=1, keepdims=True)
    n_tie = cap - n_gt

    ui = lax.broadcasted_iota(jnp.int32, (ROUTE_TILE, ROUTE_TILE), 0)
    uj = lax.broadcasted_iota(jnp.int32, (ROUTE_TILE, ROUTE_TILE), 1)
    upper = (ui <= uj).astype(BF16)
    lane = lax.broadcasted_iota(jnp.int32, offs_ref.shape, 1)

    def tile(i, carry):
        c_gt, c_eq, offs = carry
        col = pl.multiple_of(i * ROUTE_TILE, ROUTE_TILE)
        b = pltpu.bitcast(aff_ref[:, pl.ds(col, ROUTE_TILE)], jnp.int32)
        gt = jnp.where(b > thr, 1.0, 0.0)
        eq = jnp.where(b == thr, 1.0, 0.0)
        inc = _dot(jnp.concatenate([gt, eq], axis=0).astype(BF16), upper)
        gt_before = c_gt + inc[:N_EXPERTS] - gt
        eq_before = c_eq + inc[N_EXPERTS:] - eq
        sel = (gt > 0.0) | ((eq > 0.0) & (eq_before < n_tie))
        pos = gt_before + jnp.minimum(eq_before, n_tie)
        pos_ref[:, pl.ds(col, ROUTE_TILE)] = jnp.where(sel, pos, -1.0).astype(jnp.int32)
        start = c_gt + jnp.minimum(c_eq, n_tie)
        offs = jnp.where(lane == i, start.astype(jnp.int32), offs)
        return (c_gt + jnp.sum(gt, axis=1, keepdims=True), c_eq + jnp.sum(eq, axis=1, keepdims=True), offs)

    zero = jnp.zeros((N_EXPERTS, 1), F32)
    offs = jnp.where(lane == nt, cap, 0).astype(jnp.int32)
    _, _, offs = lax.fori_loop(0, nt, tile, (zero, zero, offs))
    offs_ref[...] = offs


def _route(aff_t, cap):
    n = aff_t.shape[1]
    nt = n // ROUTE_TILE
    return pl.pallas_call(
        functools.partial(_route_kernel, cap=cap),
        out_shape=[jax.ShapeDtypeStruct((N_EXPERTS, n), jnp.int32),
                   jax.ShapeDtypeStruct((N_EXPERTS, nt + 1), jnp.int32)],
        compiler_params=pltpu.CompilerParams(vmem_limit_bytes=VMEM_LIMIT),
        name="route",
    )(aff_t)


def _align(v):
    return (v // MOE_ALIGN) * MOE_ALIGN


def _moe_passes(offs_ref, t, subs, x0, s):
    k = jnp.int32(1)
    for g in range(MOE_PAIR):
        need = offs_ref[x0 + g, t * subs + s + 1] - _align(offs_ref[x0 + g, t * subs + s])
        k = jnp.maximum(k, (need + MOE_WIN - 1) // MOE_WIN)
    return k


def _moe_windows(offs_ref, pos_ref, t, subs, x0, s, j):
    slot = lax.broadcasted_iota(jnp.int32, (MOE_WIN, ROUTE_TILE), 0)
    hits, rels = [], []
    for g in range(MOE_PAIR):
        x = x0 + g
        ws = _align(offs_ref[x, t * subs + s]) + j * MOE_WIN
        pos = pos_ref[pl.ds(x, 1), s * ROUTE_TILE:(s + 1) * ROUTE_TILE]
        hits.append((pos - ws) == slot)
        rels.append(pl.multiple_of(ws - _align(offs_ref[x, t * subs]), MOE_ALIGN))
    onehot = jnp.concatenate([jnp.where(h, 1.0, 0.0) for h in hits], axis=0).astype(BF16)
    return hits, rels, onehot


def _moe_chunks(offs_ref, t, subs, x):
    base = _align(offs_ref[x, t * subs])
    return base, (offs_ref[x, (t + 1) * subs] - base + MOE_CHUNK - 1) // MOE_CHUNK


def _moe_gather_kernel(offs_ref, h_ref, pos_ref, gate_ref, xe_ref, stage_ref, carry_ref, sem):
    t = pl.program_id(0)
    steps = pl.num_programs(1)
    e0 = pl.program_id(1) * MOE_STEP
    tt = h_ref.shape[0]
    subs = tt // ROUTE_TILE
    k = t * steps + pl.program_id(1)
    slot = k % 2
    x_ref = stage_ref.at[slot]

    def off(x, s):
        return offs_ref[x, t * subs + s]

    def copies(kk, half, act):
        step_t, step_q = kk // steps, kk % steps
        for g in range(MOE_STEP):
            x = step_q * MOE_STEP + g
            base, n_chunks = _moe_chunks(offs_ref, step_t, subs, x)

            def body(c, carry):
                r0 = pl.multiple_of(c * MOE_CHUNK, MOE_CHUNK)
                act(pltpu.make_async_copy(stage_ref.at[half, g, pl.ds(r0, MOE_CHUNK), :],
                                          xe_ref.at[x, pl.ds(pl.multiple_of(base + r0, MOE_ALIGN), MOE_CHUNK), :],
                                          sem.at[half]))
                return carry

            lax.fori_loop(0, n_chunks, body, 0)

    @pl.when(k >= 2)
    def _():
        copies(k - 2, slot, lambda cp: cp.wait())

    @pl.when(t == 0)
    def _():
        carry_ref[pl.ds(e0, MOE_STEP)] = jnp.zeros((MOE_STEP, MOE_ALIGN, XE_WIDTH), BF16)
        cap = xe_ref.shape[1] - MOE_CHUNK
        x_ref[0, 0:MOE_CHUNK, :] = jnp.zeros((MOE_CHUNK, XE_WIDTH), BF16)
        pads = [pltpu.make_async_copy(x_ref.at[0, 0:MOE_CHUNK, :], xe_ref.at[e0 + g, cap:cap + MOE_CHUNK, :],
                                      sem.at[slot]) for g in range(MOE_STEP)]
        for p in pads:
            p.start()
        for p in pads:
            p.wait()

    for g in range(MOE_STEP):
        x_ref[g, 0:MOE_ALIGN, :] = carry_ref[e0 + g]

    row = lax.broadcasted_iota(jnp.int32, (MOE_WIN, 1), 0)
    lane = lax.broadcasted_iota(jnp.int32, (MOE_WIN, XE_WIDTH - D_MODEL), 1)

    def gather_pass(pair, s, j):
        x0 = e0 + pair * MOE_PAIR
        cols = slice(s * ROUTE_TILE, (s + 1) * ROUTE_TILE)
        hits, rels, onehot = _moe_windows(offs_ref, pos_ref, t, subs, x0, s, j)
        r = _dot(onehot, h_ref[cols, :])
        for g in range(MOE_PAIR):
            gate = jnp.sum(jnp.where(hits[g], gate_ref[pl.ds(x0 + g, 1), cols], 0.0), axis=1, keepdims=True)
            g_hi = gate.astype(BF16).astype(F32)
            extra = jnp.where(lane == 0, g_hi, jnp.where(lane == 1, gate - g_hi, 0.0))
            new = jnp.concatenate([r[g * MOE_WIN:(g + 1) * MOE_WIN, :], extra], axis=1)
            ws = _align(off(x0 + g, s)) + j * MOE_WIN
            own = row >= off(x0 + g, s) - ws
            buf = pair * MOE_PAIR + g
            if isinstance(j, int):
                head = pl.ds(rels[g], MOE_ALIGN)
                x_ref[buf, head, :] = jnp.where(own[:MOE_ALIGN], new[:MOE_ALIGN],
                                                x_ref[buf, head, :].astype(F32)).astype(BF16)
                x_ref[buf, pl.ds(rels[g] + MOE_ALIGN, MOE_WIN - MOE_ALIGN), :] = new[MOE_ALIGN:].astype(BF16)
            else:
                own = own & (row < off(x0 + g, s + 1) - ws)
                win = pl.ds(rels[g], MOE_WIN)
                x_ref[buf, win, :] = jnp.where(own, new, x_ref[buf, win, :].astype(F32)).astype(BF16)

    for pair in range(MOE_STEP // MOE_PAIR):
        for s in range(subs):
            gather_pass(pair, s, 0)
    for pair in range(MOE_STEP // MOE_PAIR):
        for s in range(subs):
            lax.fori_loop(1, _moe_passes(offs_ref, t, subs, e0 + pair * MOE_PAIR, s),
                          lambda j, carry: (gather_pass(pair, s, j), carry)[1], 0)

    for g in range(MOE_STEP):
        x = e0 + g
        base, n_chunks = _moe_chunks(offs_ref, t, subs, x)
        last_group = pl.multiple_of(_align(off(x, subs)) - base, MOE_ALIGN)
        tail = x_ref[g, pl.ds(last_group, MOE_ALIGN), :].astype(F32)
        tail = jnp.where(row[:MOE_ALIGN] < off(x, subs) - base - last_group, tail, 0.0).astype(BF16)
        x_ref[g, pl.ds(last_group, MOE_ALIGN), :] = tail
        carry_ref[x] = tail

        def fill(i, carry):
            x_ref[g, pl.ds(pl.multiple_of(i * MOE_ALIGN, MOE_ALIGN), MOE_ALIGN), :] = jnp.zeros((MOE_ALIGN, XE_WIDTH), BF16)
            return carry

        lax.fori_loop(last_group // MOE_ALIGN + 1, n_chunks * MOE_CHUNK // MOE_ALIGN, fill, 0)

    copies(k, slot, lambda cp: cp.start())

    @pl.when(k == pl.num_programs(0) * steps - 1)
    def _():
        copies(k - 1, 1 - slot, lambda cp: cp.wait())
        copies(k, slot, lambda cp: cp.wait())


def _moe_ffn_kernel(x_ref, wg_ref, wu_ref, wd_ref, y_ref, wg_bf, wu_bf, wd_bf):
    j = pl.program_id(1)

    @pl.when(j == 0)
    def _():
        wg_bf[...] = wg_ref[...].astype(BF16)
        wu_bf[...] = wu_ref[...].astype(BF16)
        wd_bf[...] = wd_ref[...].astype(BF16)
        y_ref[0] = jnp.zeros(y_ref.shape[1:], BF16)

    @pl.when(j > 0)
    def _():
        x = x_ref[0, :, 0:D_MODEL]
        gate = jnp.sum(x_ref[0, :, D_MODEL:XE_WIDTH].astype(F32), axis=1, keepdims=True)
        hg = _dot(x, wg_bf[...])
        hu = _dot(x, wu_bf[...])
        act = (hg / (1.0 + jnp.exp(-hg))) * hu * gate
        y_ref[0] = _dot(act.astype(BF16), wd_bf[...]).astype(BF16)


def _moe_combine_kernel(offs_ref, x1_ref, pos_ref, y_ref, fg_ref, o_ref, y_buf, sem, *, final_norm):
    t = pl.program_id(0)
    q = pl.program_id(1)
    steps = pl.num_programs(1)
    e0 = q * MOE_STEP
    tt = x1_ref.shape[0]
    subs = tt // ROUTE_TILE
    k = t * steps + q
    slot = k % 2

    def fetch(kk, half, act):
        step_t, step_q = kk // steps, kk % steps
        for g in range(MOE_STEP):
            x = step_q * MOE_STEP + g
            base, n_chunks = _moe_chunks(offs_ref, step_t, subs, x)

            def body(c, carry):
                r0 = pl.multiple_of(c * MOE_CHUNK, MOE_CHUNK)
                act(pltpu.make_async_copy(y_ref.at[x, pl.ds(pl.multiple_of(base + r0, MOE_ALIGN), MOE_CHUNK), :],
                                          y_buf.at[half, g, pl.ds(r0, MOE_CHUNK), :], sem.at[half]))
                return carry

            lax.fori_loop(0, n_chunks, body, 0)

    @pl.when(k == 0)
    def _():
        y_buf[...] = jnp.zeros_like(y_buf)
        fetch(k, slot, lambda cp: cp.start())

    @pl.when(k + 1 < pl.num_programs(0) * steps)
    def _():
        fetch(k + 1, 1 - slot, lambda cp: cp.start())

    fetch(k, slot, lambda cp: cp.wait())

    @pl.when(q == 0)
    def _():
        o_ref[...] = x1_ref[...]

    def combine_pass(pair, s, j):
        cols = slice(s * ROUTE_TILE, (s + 1) * ROUTE_TILE)
        _, rels, onehot = _moe_windows(offs_ref, pos_ref, t, subs, e0 + pair * MOE_PAIR, s, j)
        ycat = jnp.concatenate([y_buf[slot, pair * MOE_PAIR + g, pl.ds(rels[g], MOE_WIN), :]
                                for g in range(MOE_PAIR)], axis=0)
        o_ref[cols, :] += _tn_dot(onehot, ycat)

    for pair in range(MOE_STEP // MOE_PAIR):
        for s in range(subs):
            combine_pass(pair, s, 0)
    for pair in range(MOE_STEP // MOE_PAIR):
        for s in range(subs):
            lax.fori_loop(1, _moe_passes(offs_ref, t, subs, e0 + pair * MOE_PAIR, s),
                          lambda j, carry: (combine_pass(pair, s, j), carry)[1], 0)

    if final_norm:
        @pl.when(q == pl.num_programs(1) - 1)
        def _():
            o_ref[...] = _rms(o_ref[...], fg_ref[...])


def _moe(offs, h2, x1, pos, gate, wg, wu, wd, layer, final_g, final_norm, cap):
    n = h2.shape[0]
    tt, tc = min(MOE_TILE, n), min(MOE_COMBINE_TILE, n)
    assert cap % MOE_CHUNK == 0 and n % tt == 0 and n % tc == 0
    steps = N_EXPERTS // MOE_STEP
    stage_rows = lambda tile: tile + MOE_ALIGN + MOE_WIN * pl.cdiv(ROUTE_TILE + MOE_ALIGN, MOE_WIN)
    tok = lambda tile: pl.BlockSpec((tile, D_MODEL), lambda t, q, offs: (t, 0))
    per_tok = lambda tile: pl.BlockSpec((N_EXPERTS, tile), lambda t, q, offs: (0, t))
    hbm = pl.BlockSpec(memory_space=pl.ANY)
    sems = ("arbitrary", "arbitrary")

    xe = pl.pallas_call(
        _moe_gather_kernel,
        grid_spec=pltpu.PrefetchScalarGridSpec(
            num_scalar_prefetch=1, grid=(n // tt, steps),
            in_specs=[tok(tt), per_tok(tt), per_tok(tt)], out_specs=hbm,
            scratch_shapes=[pltpu.VMEM((2, MOE_STEP, stage_rows(tt), XE_WIDTH), BF16),
                            pltpu.VMEM((N_EXPERTS, MOE_ALIGN, XE_WIDTH), BF16), pltpu.SemaphoreType.DMA((2,))],
        ),
        out_shape=jax.ShapeDtypeStruct((N_EXPERTS, cap + MOE_CHUNK, XE_WIDTH), BF16),
        compiler_params=_cparams(sems),
        name="moe_gather",
    )(offs, h2, pos, gate)

    fb = min(MOE_FFN_ROWS, cap)
    nb = cap // fb
    wspec = pl.BlockSpec((None, None, D_MODEL, D_MODEL), lambda e, j: (layer, e, 0, 0))
    w_bf = pltpu.VMEM((D_MODEL, D_MODEL), BF16)
    y = pl.pallas_call(
        _moe_ffn_kernel,
        grid=(N_EXPERTS, nb + 1),
        in_specs=[pl.BlockSpec((1, fb, XE_WIDTH), lambda e, j: (e, jnp.maximum(j - 1, 0), 0)), wspec, wspec, wspec],
        out_specs=pl.BlockSpec((1, fb, D_MODEL), lambda e, j: (e, jnp.where(j == 0, nb, j - 1), 0)),
        out_shape=jax.ShapeDtypeStruct((N_EXPERTS, cap + fb, D_MODEL), BF16),
        scratch_shapes=[w_bf, w_bf, w_bf],
        compiler_params=_cparams(("parallel", "arbitrary")),
        name="moe_ffn",
    )(xe, wg, wu, wd)

    return pl.pallas_call(
        functools.partial(_moe_combine_kernel, final_norm=final_norm),
        grid_spec=pltpu.PrefetchScalarGridSpec(
            num_scalar_prefetch=1, grid=(n // tc, steps),
            in_specs=[tok(tc), per_tok(tc), hbm, pl.BlockSpec((1, D_MODEL), lambda t, q, offs: (0, 0))],
            out_specs=tok(tc),
            scratch_shapes=[pltpu.VMEM((2, MOE_STEP, stage_rows(tc), D_MODEL), BF16),
                            pltpu.SemaphoreType.DMA((2,))],
        ),
        out_shape=jax.ShapeDtypeStruct((n, D_MODEL), F32),
        compiler_params=_cparams(sems),
        name="moe_combine",
    )(offs, x1, pos, y, final_g)


def _prep_layer(l, norm1_g, w_in, a_sink, a_norm_g, b_rel_bias, b_norm_g, c_alpha_w2_f, c_alpha_b_f,
                c_alpha_w2_b, c_alpha_b_b, c_norm_g, w_out, norm2_g, w_router, w_gate, w_up, w_down):
    z = jnp.zeros((C_LOWRANK, C_WIDTH), F32)
    w2 = jnp.concatenate([jnp.concatenate([c_alpha_w2_f[l], z], axis=1),
                          jnp.concatenate([z, c_alpha_w2_b[l]], axis=1)], axis=0).astype(BF16)
    b2 = jnp.concatenate([c_alpha_b_f[l], c_alpha_b_b[l]])[None, :]
    wr_cat = jnp.pad(w_router[l].astype(BF16), ((0, 0), (0, LANES - N_EXPERTS)))
    return dict(
        layer=l, g1=norm1_g[l][None, :], w_in=w_in.astype(BF16), w2=w2, b2=b2,
        sink=a_sink[l], a_g=a_norm_g[l][None, :], bias=_bias_table(b_rel_bias[l]), b_g=b_norm_g[l][None, :],
        c_g=c_norm_g[l][None, :], w_out=w_out.astype(BF16), g2=norm2_g[l][None, :],
        wr_cat=wr_cat, wg=w_gate, wu=w_up, wd=w_down)


def _layer(x, p, bsz, seq, final_g, final_norm):
    n = bsz * seq
    l = p["layer"]
    aq, ak, av, bq, bk, bv, cq, ck, cv, cg, la = _in_proj(x, p["g1"], p["w_in"], l, p["w2"], p["b2"], seq)
    oa = _attn_a(aq, ak, av, p["sink"], p["a_g"], bsz, seq)
    ob = _attn_b(bq, bk, bv, p["bias"], p["b_g"], bsz, seq)
    o_f, o_b = _gla(cq, ck, cv, la, bsz, seq)
    x1, h2, aff = _out_proj(x, oa, ob, o_f, o_b, cg, p["c_g"], p["w_out"], l, p["g2"], p["wr_cat"])
    cap = EC_CAPACITY * n // N_EXPERTS
    pos, offs = _route(aff, cap)
    return _moe(offs, h2, x1, pos, aff, p["wg"], p["wu"], p["wd"], l, final_g, final_norm, cap)


def _trunk(x, layers, final_g):
    bsz, seq, _ = x.shape
    y = x.reshape(bsz * seq, D_MODEL)
    for l, p in enumerate(layers):
        y = _layer(y, p, bsz, seq, final_g, final_norm=(l == len(layers) - 1))
    return y.reshape(bsz, seq, D_MODEL)


def kernel(x_prompt, x_sample, norm1_g, w_in, a_sink, a_norm_g, b_rel_bias, b_norm_g, c_alpha_w2_f, c_alpha_b_f,
           c_alpha_w2_b, c_alpha_b_b, c_norm_g, w_out, norm2_g, w_router, w_gate, w_up, w_down, final_g):
    depth = w_in.shape[0]
    layers = [_prep_layer(l, norm1_g, w_in, a_sink, a_norm_g, b_rel_bias, b_norm_g, c_alpha_w2_f, c_alpha_b_f,
                          c_alpha_w2_b, c_alpha_b_b, c_norm_g, w_out, norm2_g, w_router, w_gate, w_up, w_down)
              for l in range(depth)]
    fg = final_g[None, :]
    return _trunk(x_prompt, layers, fg), _trunk(x_sample, layers, fg)
```

```python
import functools

import jax
import jax.numpy as jnp
from jax import lax
from jax.experimental import pallas as pl
from jax.experimental.pallas import tpu as pltpu

F32 = jnp.float32
BF16 = jnp.bfloat16

D_MODEL = 1024
HEAD_DIM = 64
A_WIDTH = 512
A_HEADS = 8
A_KV_HEADS = 2
A_KV_WIDTH = 128
WINDOW = 128
ROPE_THETA = 500000.0
ROPE_DIM = 16
B_WIDTH = 256
B_HEADS = 4
GRID_W = 64
WIN_H = 8
WIN_W = 16
C_WIDTH = 256
C_HEADS = 4
C_LOWRANK = 16
C_TAU = 16.0
C_CHUNK = 64
N_EXPERTS = 16
EC_CAPACITY = 2
EPS = 1e-6
NEG_INF = -1e30
IN_WIDTH = 2592

_OFF_AQ, _OFF_AK, _OFF_AV = 0, 512, 640
_OFF_BQ, _OFF_BK, _OFF_BV = 768, 1024, 1280
_OFF_CQ, _OFF_CK, _OFF_CV, _OFF_CG = 1536, 1792, 2048, 2304
_OFF_LR = 2560

LANES = 128
VMEM_LIMIT = 56 * 1024 * 1024

IN_PROJ_ROWS = 1024
OUT_PROJ_ROWS = 1024
GLA_BLOCK = 1024
GLA_CUM_ROWS = 256
ROUTE_TILE = 256
MOE_TILE = 2048
MOE_COMBINE_TILE = 1024
MOE_STEP = 4
MOE_PAIR = 4
MOE_WIN = 64
MOE_ALIGN = 16
MOE_CHUNK = 128
MOE_FFN_ROWS = 1024
XE_WIDTH = D_MODEL + LANES


def _cparams(sem):
    return pltpu.CompilerParams(dimension_semantics=sem, vmem_limit_bytes=VMEM_LIMIT)


def _nt_dot(a, b):
    return lax.dot_general(a, b, (((1,), (1,)), ((), ())), preferred_element_type=F32)


def _tn_dot(a, b):
    return lax.dot_general(a, b, (((0,), (0,)), ((), ())), preferred_element_type=F32)


def _dot(a, b):
    return jnp.dot(a, b, preferred_element_type=F32)


def _rms(x, g):
    return x * lax.rsqrt(jnp.mean(x * x, axis=-1, keepdims=True) + EPS) * g


def _same_head_mask(n_rows, rows_per_head):
    width = (n_rows // rows_per_head) * HEAD_DIM
    ri = lax.broadcasted_iota(jnp.int32, (n_rows, width), 0) // rows_per_head
    li = lax.broadcasted_iota(jnp.int32, (n_rows, width), 1) // HEAD_DIM
    return ri == li


def _split_bf16(x):
    hi = x.astype(BF16)
    return hi, (x - hi.astype(F32)).astype(BF16)


def _in_proj_kernel(x_ref, g_ref, w_ref, cos_ref, s1_ref, s2_ref, w2_ref, b2_ref,
                    aq_ref, ak_ref, av_ref, bq_ref, bk_ref, bv_ref,
                    cq_ref, ck_ref, cv_ref, cg_ref, la_ref):
    h = _rms(x_ref[...], g_ref[...]).astype(BF16)
    y = _dot(h, w_ref[...])

    def proj(lo, width):
        return y[:, lo:lo + width]

    cos, s1, s2 = cos_ref[...], s1_ref[...], s2_ref[...]

    def rope(t):
        return t * cos + pltpu.roll(t, LANES - ROPE_DIM // 2, 1) * s1 + pltpu.roll(t, ROPE_DIM // 2, 1) * s2

    scale = HEAD_DIM ** -0.5
    aq = proj(_OFF_AQ, A_WIDTH)
    for c in range(A_WIDTH // LANES):
        aq_ref[:, c * LANES:(c + 1) * LANES] = (rope(aq[:, c * LANES:(c + 1) * LANES]) * scale).astype(BF16)
    ak_ref[...] = rope(proj(_OFF_AK, A_KV_WIDTH)).astype(BF16)
    av_ref[...] = proj(_OFF_AV, A_KV_WIDTH).astype(BF16)
    bq_ref[...] = (proj(_OFF_BQ, B_WIDTH) * scale).astype(BF16)
    bk_ref[...] = proj(_OFF_BK, B_WIDTH).astype(BF16)
    bv_ref[...] = proj(_OFF_BV, B_WIDTH).astype(BF16)
    cq_ref[...] = proj(_OFF_CQ, C_WIDTH) * scale
    ck_ref[...] = proj(_OFF_CK, C_WIDTH)
    cv_ref[...] = proj(_OFF_CV, C_WIDTH).astype(BF16)
    cg_ref[...] = proj(_OFF_CG, C_WIDTH)
    z = _dot(proj(_OFF_LR, 2 * C_LOWRANK).astype(BF16), w2_ref[...]) + b2_ref[...]
    la_ref[...] = (jnp.minimum(z, 0.0) - jnp.log(1.0 + jnp.exp(-jnp.abs(z)))) * (1.0 / C_TAU)


def _rope_tables(seq):
    half = ROPE_DIM // 2
    inv = jnp.power(jnp.float32(ROPE_THETA), -jnp.arange(half, dtype=F32) * (2.0 / ROPE_DIM))
    ang = jnp.arange(seq, dtype=F32)[:, None] * inv[None, :]
    cos, sin = jnp.cos(ang), jnp.sin(ang)
    ones = jnp.ones((seq, HEAD_DIM - ROPE_DIM), F32)
    zeros = jnp.zeros((seq, HEAD_DIM - half), F32)
    c = jnp.concatenate([cos, cos, ones], axis=1)
    s1 = jnp.concatenate([-sin, zeros], axis=1)
    s2 = jnp.concatenate([jnp.zeros((seq, half), F32), sin, jnp.zeros((seq, HEAD_DIM - ROPE_DIM), F32)], axis=1)
    two = lambda t: jnp.concatenate([t, t], axis=1)
    return two(c), two(s1), two(s2)


def _in_proj(x, g1, w_in_bf, layer, w2, b2, seq):
    n = x.shape[0]
    tm = min(IN_PROJ_ROWS, seq)
    per_seq = seq // tm
    cos, s1, s2 = _rope_tables(seq)
    row = lambda width: pl.BlockSpec((tm, width), lambda i: (i, 0))
    full = lambda a: pl.BlockSpec(a.shape, lambda i: (0,) * a.ndim)
    tab = pl.BlockSpec((tm, LANES), lambda i: (i % per_seq, 0))
    widths = [(A_WIDTH, BF16), (A_KV_WIDTH, BF16), (A_KV_WIDTH, BF16), (B_WIDTH, BF16), (B_WIDTH, BF16),
              (B_WIDTH, BF16), (C_WIDTH, F32), (C_WIDTH, F32), (C_WIDTH, BF16), (C_WIDTH, F32), (2 * C_WIDTH, F32)]
    return pl.pallas_call(
        _in_proj_kernel,
        grid=(n // tm,),
        in_specs=[row(D_MODEL), full(g1), pl.BlockSpec((None, D_MODEL, IN_WIDTH), lambda i: (layer, 0, 0)),
                  tab, tab, tab, full(w2), full(b2)],
        out_specs=[row(w) for w, _ in widths],
        out_shape=[jax.ShapeDtypeStruct((n, w), dt) for w, dt in widths],
        compiler_params=_cparams(("parallel",)),
        name="in_proj",
    )(x, g1, w_in_bf, cos, s1, s2, w2, b2)


A_QBLOCKS = 4


def _attn_a_kernel(sink_ref, q_ref, kp_ref, kc_ref, kn_ref, vp_ref, vc_ref, vn_ref, g_ref, o_ref, acc_ref, *, steps):
    i = pl.program_id(1)
    grp = A_HEADS // A_KV_HEADS
    k = jnp.concatenate([kp_ref[...], kc_ref[...], kn_ref[...]], axis=0)
    v = jnp.concatenate([vp_ref[...], vc_ref[...], vn_ref[...]], axis=0)
    qq = lax.broadcasted_iota(jnp.int32, (WINDOW, 3 * WINDOW), 0)
    kk = lax.broadcasted_iota(jnp.int32, (WINDOW, 3 * WINDOW), 1)
    band = jnp.abs(kk - WINDOW - qq) <= WINDOW
    for blk in range(A_QBLOCKS):
        visible = band
        if blk == 0:
            visible = visible & jnp.logical_not((kk < WINDOW) & (i == 0))
        if blk == A_QBLOCKS - 1:
            visible = visible & jnp.logical_not((kk >= 2 * WINDOW) & (i == steps - 1))
        bias = jnp.where(visible, 0.0, NEG_INF)
        rows = slice(blk * WINDOW, (blk + 1) * WINDOW)
        keys = slice(blk * WINDOW, (blk + 3) * WINDOW)
        for kv in range(A_KV_HEADS):
            heads = range(kv * grp, (kv + 1) * grp)
            q = jnp.concatenate([q_ref[rows, h * HEAD_DIM:(h + 1) * HEAD_DIM] for h in heads], axis=0)
            s = _nt_dot(q, k[keys, kv * HEAD_DIM:(kv + 1) * HEAD_DIM])
            ps, dens = [], []
            for g_i, h in enumerate(heads):
                sh = s[g_i * WINDOW:(g_i + 1) * WINDOW, :] + bias
                sink = sink_ref[h]
                m = jnp.maximum(jnp.max(sh, axis=-1, keepdims=True), sink)
                p = jnp.exp(sh - m)
                dens.append(jnp.sum(p, axis=-1, keepdims=True) + jnp.exp(sink - m))
                ps.append(p.astype(BF16))
            o = _dot(jnp.concatenate(ps, axis=0), v[keys, kv * HEAD_DIM:(kv + 1) * HEAD_DIM])
            for g_i, h in enumerate(heads):
                acc_ref[rows, h * HEAD_DIM:(h + 1) * HEAD_DIM] = o[g_i * WINDOW:(g_i + 1) * WINDOW, :] / dens[g_i]
    o_ref[...] = _rms(acc_ref[...], g_ref[...]).astype(BF16)


def _attn_a(aq, ak, av, sink, g, bsz, seq):
    nb = seq // WINDOW
    assert nb % A_QBLOCKS == 0
    steps = nb // A_QBLOCKS
    qrows = A_QBLOCKS * WINDOW
    qspec = pl.BlockSpec((qrows, A_WIDTH), lambda b, i: (b * steps + i, 0))
    kprev = pl.BlockSpec((WINDOW, A_KV_WIDTH), lambda b, i: (b * nb + jnp.maximum(i * A_QBLOCKS - 1, 0), 0))
    kcur = pl.BlockSpec((qrows, A_KV_WIDTH), lambda b, i: (b * steps + i, 0))
    knext = pl.BlockSpec((WINDOW, A_KV_WIDTH), lambda b, i: (b * nb + jnp.minimum((i + 1) * A_QBLOCKS, nb - 1), 0))
    return pl.pallas_call(
        functools.partial(_attn_a_kernel, steps=steps),
        grid=(bsz, steps),
        in_specs=[pl.BlockSpec(memory_space=pltpu.SMEM), qspec, kprev, kcur, knext, kprev, kcur, knext,
                  pl.BlockSpec((1, A_WIDTH), lambda b, i: (0, 0))],
        out_specs=qspec,
        out_shape=jax.ShapeDtypeStruct((bsz * seq, A_WIDTH), BF16),
        scratch_shapes=[pltpu.VMEM((qrows, A_WIDTH), F32)],
        compiler_params=_cparams(("parallel", "parallel")),
        name="attn_a",
    )(sink, aq, ak, ak, ak, av, av, av, g)


B_GROUP = 16
B_KEYS = WIN_H * GRID_W


def _bias_table_kernel(rel_ref, o_ref):
    c = lax.broadcasted_iota(jnp.int32, (GRID_W, GRID_W), 0)
    w = lax.broadcasted_iota(jnp.int32, (GRID_W, GRID_W), 1)
    cstart = jnp.clip(c - WIN_W // 2, 0, GRID_W - WIN_W)
    colmask = (w >= cstart) & (w < cstart + WIN_W)
    col_off = jnp.clip(w - c + (WIN_W - 1), 0, 2 * WIN_W - 2)
    n_ro, n_co = 2 * WIN_H - 1, 2 * WIN_W - 1
    for h in range(B_HEADS):
        for ro in range(n_ro):
            def body(j, acc):
                return jnp.where(col_off == j, rel_ref[(h * n_ro + ro) * n_co + j], acc)
            t = lax.fori_loop(0, n_co, body, jnp.zeros((GRID_W, GRID_W), F32))
            t = jnp.where(colmask, t, NEG_INF)
            for p in range(WIN_H):
                kidx = ro - (WIN_H - 1) + p
                if 0 <= kidx < WIN_H:
                    o_ref[p, h, :, kidx * GRID_W:(kidx + 1) * GRID_W] = t


def _bias_table(rel_bias):
    return pl.pallas_call(
        _bias_table_kernel,
        in_specs=[pl.BlockSpec(memory_space=pltpu.SMEM)],
        out_shape=jax.ShapeDtypeStruct((WIN_H, B_HEADS, GRID_W, B_KEYS), F32),
        name="bias_table",
    )(rel_bias.reshape(-1)).reshape(WIN_H, B_HEADS * GRID_W, B_KEYS)


def _attn_b_kernel(q_ref, kp_ref, kc_ref, kn_ref, vp_ref, vc_ref, vn_ref, bias_ref, g_ref, o_ref,
                   kbuf, vbuf, acc_ref, *, rows):
    gidx = pl.program_id(1)
    blk = B_GROUP * GRID_W
    kbuf[0:blk, :] = kp_ref[...]
    kbuf[blk:2 * blk, :] = kc_ref[...]
    kbuf[2 * blk:3 * blk, :] = kn_ref[...]
    vbuf[0:blk, :] = vp_ref[...]
    vbuf[blk:2 * blk, :] = vc_ref[...]
    vbuf[2 * blk:3 * blk, :] = vn_ref[...]
    same_head = _same_head_mask(B_HEADS * GRID_W, GRID_W)
    for j in range(B_GROUP):
        r = gidx * B_GROUP + j
        start = jnp.clip(r - WIN_H // 2, 0, rows - WIN_H)
        pat = r - start
        loc = pl.multiple_of((start - gidx * B_GROUP + B_GROUP) * GRID_W, GRID_W)
        kw = kbuf[pl.ds(loc, B_KEYS), :]
        vw = vbuf[pl.ds(loc, B_KEYS), :]
        qj = q_ref[j * GRID_W:(j + 1) * GRID_W, :]
        q = jnp.where(same_head, jnp.concatenate([qj] * B_HEADS, axis=0), jnp.zeros((), BF16))
        s = _nt_dot(q, kw) + bias_ref[pat]
        m = jnp.max(s, axis=-1, keepdims=True)
        p = jnp.exp(s - m)
        den = jnp.sum(p, axis=-1, keepdims=True)
        o = jnp.where(same_head, _dot(p.astype(BF16), vw) / den, 0.0)
        acc_ref[j * GRID_W:(j + 1) * GRID_W, :] = sum(o[h * GRID_W:(h + 1) * GRID_W, :] for h in range(B_HEADS))
    o_ref[...] = _rms(acc_ref[...], g_ref[...]).astype(BF16)


def _attn_b(bq, bk, bv, bias, g, bsz, seq):
    rows = seq // GRID_W
    assert rows % B_GROUP == 0 and rows >= WIN_H
    ng = rows // B_GROUP
    blk = B_GROUP * GRID_W
    cur = pl.BlockSpec((blk, B_WIDTH), lambda b, i: (b * ng + i, 0))
    prev = pl.BlockSpec((blk, B_WIDTH), lambda b, i: (b * ng + jnp.maximum(i - 1, 0), 0))
    nxt = pl.BlockSpec((blk, B_WIDTH), lambda b, i: (b * ng + jnp.minimum(i + 1, ng - 1), 0))
    return pl.pallas_call(
        functools.partial(_attn_b_kernel, rows=rows),
        grid=(bsz, ng),
        in_specs=[cur, prev, cur, nxt, prev, cur, nxt,
                  pl.BlockSpec(bias.shape, lambda b, i: (0, 0, 0)),
                  pl.BlockSpec((1, B_WIDTH), lambda b, i: (0, 0))],
        out_specs=cur,
        out_shape=jax.ShapeDtypeStruct((bsz * seq, B_WIDTH), BF16),
        scratch_shapes=[pltpu.VMEM((3 * blk, B_WIDTH), BF16), pltpu.VMEM((3 * blk, B_WIDTH), BF16),
                        pltpu.VMEM((blk, B_WIDTH), F32)],
        compiler_params=_cparams(("parallel", "parallel")),
        name="attn_b",
    )(bq, bk, bk, bk, bv, bv, bv, bias, g)


def _gla_decay(la_ref, reverse):
    seg = min(GLA_CUM_ROWS, la_ref.shape[0])
    bi = lax.broadcasted_iota(jnp.int32, (seg, seg), 0)
    bj = lax.broadcasted_iota(jnp.int32, (seg, seg), 1)
    in_chunk = (bi // C_CHUNK) == (bj // C_CHUNK)
    cum_w = (in_chunk & ((bj >= bi) if reverse else (bj <= bi))).astype(BF16)
    parts = []
    for r in range(0, la_ref.shape[0], seg):
        la_hi, la_lo = _split_bf16(la_ref[r:r + seg, :])
        parts.append(_dot(cum_w, la_hi) + _dot(cum_w, la_lo))
    return parts[0] if len(parts) == 1 else jnp.concatenate(parts, axis=0)


def _gla_chunk(q_ref, k_ref, v_ref, o_ref, b_all, st, c, reverse):
    same_head = _same_head_mask(C_HEADS * C_CHUNK, C_CHUNK)
    ti = lax.broadcasted_iota(jnp.int32, (C_CHUNK, C_HEADS * C_CHUNK), 0)
    si = lax.broadcasted_iota(jnp.int32, (C_CHUNK, C_HEADS * C_CHUNK), 1) % C_CHUNK
    tri = (si >= ti) if reverse else (si <= ti)
    zero = jnp.zeros((), BF16)
    rows = slice(c * C_CHUNK, (c + 1) * C_CHUNK)
    b = b_all[rows, :]
    b_tot = b[0:1, :] if reverse else b[C_CHUNK - 1:C_CHUNK, :]
    q_i = (q_ref[rows, :] * jnp.exp(b)).astype(BF16)
    kf = k_ref[rows, :]
    k_i = (kf * jnp.exp(-b)).astype(BF16)
    k_e = (kf * jnp.exp(b_tot - b)).astype(BF16)
    v = v_ref[rows, :]
    k_bd = jnp.where(same_head, jnp.concatenate([k_i] * C_HEADS, axis=0), zero)
    v_bd = jnp.where(same_head, jnp.concatenate([v] * C_HEADS, axis=0), zero)
    a = jnp.where(tri, _nt_dot(q_i, k_bd), 0.0)
    o_ref[rows, :] = _dot(a.astype(BF16), v_bd) + _nt_dot(q_i, st.astype(BF16))
    return st * jnp.exp(b_tot) + jnp.where(same_head, _tn_dot(v, k_e), 0.0)


def _gla_kernel(qf_ref, kf_ref, vf_ref, laf_ref, qb_ref, kb_ref, vb_ref, lab_ref, of_ref, ob_ref, sf_ref, sb_ref):
    @pl.when(pl.program_id(1) == 0)
    def _():
        sf_ref[...] = jnp.zeros_like(sf_ref)
        sb_ref[...] = jnp.zeros_like(sb_ref)

    n_chunks = qf_ref.shape[0] // C_CHUNK
    b_f = _gla_decay(laf_ref, reverse=False)
    b_b = _gla_decay(lab_ref, reverse=True)
    s_f, s_b = sf_ref[...], sb_ref[...]
    for c in range(n_chunks):
        s_f = _gla_chunk(qf_ref, kf_ref, vf_ref, of_ref, b_f, s_f, c, reverse=False)
        s_b = _gla_chunk(qb_ref, kb_ref, vb_ref, ob_ref, b_b, s_b, n_chunks - 1 - c, reverse=True)
    sf_ref[...] = s_f
    sb_ref[...] = s_b


def _gla(cq, ck, cv, la, bsz, seq):
    tb = min(GLA_BLOCK, seq)
    nb = seq // tb
    fwd = lambda col: pl.BlockSpec((tb, C_WIDTH), lambda b, i: (b * nb + i, col))
    bwd = lambda col: pl.BlockSpec((tb, C_WIDTH), lambda b, i: (b * nb + nb - 1 - i, col))
    out = jax.ShapeDtypeStruct((bsz * seq, C_WIDTH), F32)
    state = pltpu.VMEM((C_WIDTH, C_WIDTH), F32)
    return pl.pallas_call(
        _gla_kernel,
        grid=(bsz, nb),
        in_specs=[fwd(0), fwd(0), fwd(0), fwd(0), bwd(0), bwd(0), bwd(0), bwd(1)],
        out_specs=[fwd(0), bwd(0)],
        out_shape=[out, out],
        scratch_shapes=[state, state],
        compiler_params=_cparams(("parallel", "arbitrary")),
        name="gla",
    )(cq, ck, cv, la, cq, ck, cv, la)


def _out_proj_kernel(x_ref, oa_ref, ob_ref, of_ref, obw_ref, cg_ref, cng_ref, wo_ref, g2_ref, wr_ref,
                     x1_ref, h2_ref, aff_ref, mix_ref):
    gi = lax.broadcasted_iota(jnp.int32, (C_WIDTH, C_WIDTH), 0) // HEAD_DIM
    gj = lax.broadcasted_iota(jnp.int32, (C_WIDTH, C_WIDTH), 1) // HEAD_DIM
    ones_bd = (gi == gj).astype(BF16)
    half = x_ref.shape[0] // 2
    for rows in (slice(0, half), slice(half, 2 * half)):
        o = of_ref[rows, :] + obw_ref[rows, :]
        sq_hi, sq_lo = _split_bf16(o * o)
        ms = (_dot(sq_hi, ones_bd) + _dot(sq_lo, ones_bd)) * (1.0 / HEAD_DIM)
        cg = cg_ref[rows, :]
        oc = (o * lax.rsqrt(ms + EPS) * cng_ref[...]) * (cg / (1.0 + jnp.exp(-cg)))
        mix_ref[rows, 0:A_WIDTH] = oa_ref[rows, :]
        mix_ref[rows, A_WIDTH:A_WIDTH + B_WIDTH] = ob_ref[rows, :]
        mix_ref[rows, A_WIDTH + B_WIDTH:] = oc.astype(BF16)
        x1 = x_ref[rows, :] + _dot(mix_ref[rows, :], wo_ref[...])
        x1_ref[rows, :] = x1
        h2 = _rms(x1, g2_ref[...]).astype(BF16)
        h2_ref[rows, :] = h2
        logits = _dot(h2, wr_ref[...])
        lane = lax.broadcasted_iota(jnp.int32, logits.shape, 1)
        logits = jnp.where(lane < N_EXPERTS, logits, NEG_INF)
        p = jnp.exp(logits - jnp.max(logits, axis=1, keepdims=True))
        aff = p / jnp.sum(p, axis=1, keepdims=True)
        aff_ref[:, rows] = aff.T[0:N_EXPERTS, :]


def _out_proj(x, oa, ob, o_f, o_b, cg, cng, wo_bf, layer, g2, wr_cat):
    n = x.shape[0]
    tm = min(OUT_PROJ_ROWS, n)
    row = lambda width: pl.BlockSpec((tm, width), lambda i: (i, 0))
    full = lambda a: pl.BlockSpec(a.shape, lambda i: (0,) * a.ndim)
    return pl.pallas_call(
        _out_proj_kernel,
        grid=(n // tm,),
        in_specs=[row(D_MODEL), row(A_WIDTH), row(B_WIDTH), row(C_WIDTH), row(C_WIDTH), row(C_WIDTH),
                  full(cng), pl.BlockSpec((None, D_MODEL, D_MODEL), lambda i: (layer, 0, 0)), full(g2), full(wr_cat)],
        out_specs=[row(D_MODEL), row(D_MODEL), pl.BlockSpec((N_EXPERTS, tm), lambda i: (0, i))],
        out_shape=[jax.ShapeDtypeStruct((n, D_MODEL), F32), jax.ShapeDtypeStruct((n, D_MODEL), BF16),
                   jax.ShapeDtypeStruct((N_EXPERTS, n), F32)],
        scratch_shapes=[pltpu.VMEM((tm, D_MODEL), BF16)],
        compiler_params=_cparams(("parallel",)),
        name="out_proj",
    )(x, oa, ob, o_f, o_b, cg, cng, wo_bf, g2, wr_cat)


def _route_kernel(aff_ref, pos_ref, offs_ref, *, cap):
    n = aff_ref.shape[1]
    nt = n // ROUTE_TILE
    bits = pltpu.bitcast(aff_ref[...], jnp.int32)

    def search(it, thr):
        cand = thr | jnp.left_shift(jnp.int32(1), 30 - it)
        cnt = jnp.sum(jnp.where(bits >= cand, 1.0, 0.0), axis=1, keepdims=True)
        return jnp.where(cnt >= cap, cand, thr)

    thr = lax.fori_loop(0, 31, search, jnp.zeros((N_EXPERTS, 1), jnp.int32))
    n_gt = jnp.sum(jnp.where(bits > thr, 1.0, 0.0), axis=1, keepdims=True)
    n_tie = cap - n_gt

    ui = lax.broadcasted_iota(jnp.int32, (ROUTE_TILE, ROUTE_TILE), 0)
    uj = lax.broadcasted_iota(jnp.int32, (ROUTE_TILE, ROUTE_TILE), 1)
    upper = (ui <= uj).astype(BF16)
    lane = lax.broadcasted_iota(jnp.int32, offs_ref.shape, 1)

    def tile(i, carry):
        c_gt, c_eq, offs = carry
        col = pl.multiple_of(i * ROUTE_TILE, ROUTE_TILE)
        b = pltpu.bitcast(aff_ref[:, pl.ds(col, ROUTE_TILE)], jnp.int32)
        gt = jnp.where(b > thr, 1.0, 0.0)
        eq = jnp.where(b == thr, 1.0, 0.0)
        inc = _dot(jnp.concatenate([gt, eq], axis=0).astype(BF16), upper)
        gt_before = c_gt + inc[:N_EXPERTS] - gt
        eq_before = c_eq + inc[N_EXPERTS:] - eq
        sel = (gt > 0.0) | ((eq > 0.0) & (eq_before < n_tie))
        pos = gt_before + jnp.minimum(eq_before, n_tie)
        pos_ref[:, pl.ds(col, ROUTE_TILE)] = jnp.where(sel, pos, -1.0).astype(jnp.int32)
        start = c_gt + jnp.minimum(c_eq, n_tie)
        offs = jnp.where(lane == i, start.astype(jnp.int32), offs)
        return (c_gt + jnp.sum(gt, axis=1, keepdims=True), c_eq + jnp.sum(eq, axis=1, keepdims=True), offs)

    zero = jnp.zeros((N_EXPERTS, 1), F32)
    offs = jnp.where(lane == nt, cap, 0).astype(jnp.int32)
    _, _, offs = lax.fori_loop(0, nt, tile, (zero, zero, offs))
    offs_ref[...] = offs


def _route(aff_t, cap):
    n = aff_t.shape[1]
    nt = n // ROUTE_TILE
    return pl.pallas_call(
        functools.partial(_route_kernel, cap=cap),
        out_shape=[jax.ShapeDtypeStruct((N_EXPERTS, n), jnp.int32),
                   jax.ShapeDtypeStruct((N_EXPERTS, nt + 1), jnp.int32)],
        compiler_params=pltpu.CompilerParams(vmem_limit_bytes=VMEM_LIMIT),
        name="route",
    )(aff_t)


def _align(v):
    return (v // MOE_ALIGN) * MOE_ALIGN


def _moe_passes(offs_ref, t, subs, x0, s):
    k = jnp.int32(1)
    for g in range(MOE_PAIR):
        need = offs_ref[x0 + g, t * subs + s + 1] - _align(offs_ref[x0 + g, t * subs + s])
        k = jnp.maximum(k, (need + MOE_WIN - 1) // MOE_WIN)
    return k


def _moe_windows(offs_ref, pos_ref, t, subs, x0, s, j):
    slot = lax.broadcasted_iota(jnp.int32, (MOE_WIN, ROUTE_TILE), 0)
    hits, rels = [], []
    for g in range(MOE_PAIR):
        x = x0 + g
        ws = _align(offs_ref[x, t * subs + s]) + j * MOE_WIN
        pos = pos_ref[pl.ds(x, 1), s * ROUTE_TILE:(s + 1) * ROUTE_TILE]
        hits.append((pos - ws) == slot)
        rels.append(pl.multiple_of(ws - _align(offs_ref[x, t * subs]), MOE_ALIGN))
    onehot = jnp.concatenate([jnp.where(h, 1.0, 0.0) for h in hits], axis=0).astype(BF16)
    return hits, rels, onehot


def _moe_chunks(offs_ref, t, subs, x):
    base = _align(offs_ref[x, t * subs])
    return base, (offs_ref[x, (t + 1) * subs] - base + MOE_CHUNK - 1) // MOE_CHUNK


def _moe_gather_kernel(offs_ref, h_ref, pos_ref, gate_ref, xe_ref, stage_ref, carry_ref, sem):
    t = pl.program_id(0)
    steps = pl.num_programs(1)
    e0 = pl.program_id(1) * MOE_STEP
    tt = h_ref.shape[0]
    subs = tt // ROUTE_TILE
    k = t * steps + pl.program_id(1)
    slot = k % 2
    x_ref = stage_ref.at[slot]

    def off(x, s):
        return offs_ref[x, t * subs + s]

    def copies(kk, half, act):
        step_t, step_q = kk // steps, kk % steps
        for g in range(MOE_STEP):
            x = step_q * MOE_STEP + g
            base, n_chunks = _moe_chunks(offs_ref, step_t, subs, x)

            def body(c, carry):
                r0 = pl.multiple_of(c * MOE_CHUNK, MOE_CHUNK)
                act(pltpu.make_async_copy(stage_ref.at[half, g, pl.ds(r0, MOE_CHUNK), :],
                                          xe_ref.at[x, pl.ds(pl.multiple_of(base + r0, MOE_ALIGN), MOE_CHUNK), :],
                                          sem.at[half]))
                return carry

            lax.fori_loop(0, n_chunks, body, 0)

    @pl.when(k >= 2)
    def _():
        copies(k - 2, slot, lambda cp: cp.wait())

    @pl.when(t == 0)
    def _():
        carry_ref[pl.ds(e0, MOE_STEP)] = jnp.zeros((MOE_STEP, MOE_ALIGN, XE_WIDTH), BF16)
        cap = xe_ref.shape[1] - MOE_CHUNK
        x_ref[0, 0:MOE_CHUNK, :] = jnp.zeros((MOE_CHUNK, XE_WIDTH), BF16)
        pads = [pltpu.make_async_copy(x_ref.at[0, 0:MOE_CHUNK, :], xe_ref.at[e0 + g, cap:cap + MOE_CHUNK, :],
                                      sem.at[slot]) for g in range(MOE_STEP)]
        for p in pads:
            p.start()
        for p in pads:
            p.wait()

    for g in range(MOE_STEP):
        x_ref[g, 0:MOE_ALIGN, :] = carry_ref[e0 + g]

    row = lax.broadcasted_iota(jnp.int32, (MOE_WIN, 1), 0)
    lane = lax.broadcasted_iota(jnp.int32, (MOE_WIN, XE_WIDTH - D_MODEL), 1)

    def gather_pass(pair, s, j):
        x0 = e0 + pair * MOE_PAIR
        cols = slice(s * ROUTE_TILE, (s + 1) * ROUTE_TILE)
        hits, rels, onehot = _moe_windows(offs_ref, pos_ref, t, subs, x0, s, j)
        r = _dot(onehot, h_ref[cols, :])
        for g in range(MOE_PAIR):
            gate = jnp.sum(jnp.where(hits[g], gate_ref[pl.ds(x0 + g, 1), cols], 0.0), axis=1, keepdims=True)
            g_hi = gate.astype(BF16).astype(F32)
            extra = jnp.where(lane == 0, g_hi, jnp.where(lane == 1, gate - g_hi, 0.0))
            new = jnp.concatenate([r[g * MOE_WIN:(g + 1) * MOE_WIN, :], extra], axis=1)
            ws = _align(off(x0 + g, s)) + j * MOE_WIN
            own = row >= off(x0 + g, s) - ws
            buf = pair * MOE_PAIR + g
            if isinstance(j, int):
                head = pl.ds(rels[g], MOE_ALIGN)
                x_ref[buf, head, :] = jnp.where(own[:MOE_ALIGN], new[:MOE_ALIGN],
                                                x_ref[buf, head, :].astype(F32)).astype(BF16)
                x_ref[buf, pl.ds(rels[g] + MOE_ALIGN, MOE_WIN - MOE_ALIGN), :] = new[MOE_ALIGN:].astype(BF16)
            else:
                own = own & (row < off(x0 + g, s + 1) - ws)
                win = pl.ds(rels[g], MOE_WIN)
                x_ref[buf, win, :] = jnp.where(own, new, x_ref[buf, win, :].astype(F32)).astype(BF16)

    for pair in range(MOE_STEP // MOE_PAIR):
        for s in range(subs):
            gather_pass(pair, s, 0)
    for pair in range(MOE_STEP // MOE_PAIR):
        for s in range(subs):
            lax.fori_loop(1, _moe_passes(offs_ref, t, subs, e0 + pair * MOE_PAIR, s),
                          lambda j, carry: (gather_pass(pair, s, j), carry)[1], 0)

    for g in range(MOE_STEP):
        x = e0 + g
        base, n_chunks = _moe_chunks(offs_ref, t, subs, x)
        last_group = pl.multiple_of(_align(off(x, subs)) - base, MOE_ALIGN)
        tail = x_ref[g, pl.ds(last_group, MOE_ALIGN), :].astype(F32)
        tail = jnp.where(row[:MOE_ALIGN] < off(x, subs) - base - last_group, tail, 0.0).astype(BF16)
        x_ref[g, pl.ds(last_group, MOE_ALIGN), :] = tail
        carry_ref[x] = tail

        def fill(i, carry):
            x_ref[g, pl.ds(pl.multiple_of(i * MOE_ALIGN, MOE_ALIGN), MOE_ALIGN), :] = jnp.zeros((MOE_ALIGN, XE_WIDTH), BF16)
            return carry

        lax.fori_loop(last_group // MOE_ALIGN + 1, n_chunks * MOE_CHUNK // MOE_ALIGN, fill, 0)

    copies(k, slot, lambda cp: cp.start())

    @pl.when(k == pl.num_programs(0) * steps - 1)
    def _():
        copies(k - 1, 1 - slot, lambda cp: cp.wait())
        copies(k, slot, lambda cp: cp.wait())


def _moe_ffn_kernel(x_ref, wg_ref, wu_ref, wd_ref, y_ref, wg_bf, wu_bf, wd_bf):
    j = pl.program_id(1)

    @pl.when(j == 0)
    def _():
        wg_bf[...] = wg_ref[...].astype(BF16)
        wu_bf[...] = wu_ref[...].astype(BF16)
        wd_bf[...] = wd_ref[...].astype(BF16)
        y_ref[0] = jnp.zeros(y_ref.shape[1:], BF16)

    @pl.when(j > 0)
    def _():
        x = x_ref[0, :, 0:D_MODEL]
        gate = jnp.sum(x_ref[0, :, D_MODEL:XE_WIDTH].astype(F32), axis=1, keepdims=True)
        hg = _dot(x, wg_bf[...])
        hu = _dot(x, wu_bf[...])
        act = (hg / (1.0 + jnp.exp(-hg))) * hu * gate
        y_ref[0] = _dot(act.astype(BF16), wd_bf[...]).astype(BF16)


def _moe_combine_kernel(offs_ref, x1_ref, pos_ref, y_ref, fg_ref, o_ref, y_buf, sem, *, final_norm):
    t = pl.program_id(0)
    q = pl.program_id(1)
    steps = pl.num_programs(1)
    e0 = q * MOE_STEP
    tt = x1_ref.shape[0]
    subs = tt // ROUTE_TILE
    k = t * steps + q
    slot = k % 2

    def fetch(kk, half, act):
        step_t, step_q = kk // steps, kk % steps
        for g in range(MOE_STEP):
            x = step_q * MOE_STEP + g
            base, n_chunks = _moe_chunks(offs_ref, step_t, subs, x)

            def body(c, carry):
                r0 = pl.multiple_of(c * MOE_CHUNK, MOE_CHUNK)
                act(pltpu.make_async_copy(y_ref.at[x, pl.ds(pl.multiple_of(base + r0, MOE_ALIGN), MOE_CHUNK), :],
                                          y_buf.at[half, g, pl.ds(r0, MOE_CHUNK), :], sem.at[half]))
                return carry

            lax.fori_loop(0, n_chunks, body, 0)

    @pl.when(k == 0)
    def _():
        y_buf[...] = jnp.zeros_like(y_buf)
        fetch(k, slot, lambda cp: cp.start())

    @pl.when(k + 1 < pl.num_programs(0) * steps)
    def _():
        fetch(k + 1, 1 - slot, lambda cp: cp.start())

    fetch(k, slot, lambda cp: cp.wait())

    @pl.when(q == 0)
    def _():
        o_ref[...] = x1_ref[...]

    def combine_pass(pair, s, j):
        cols = slice(s * ROUTE_TILE, (s + 1) * ROUTE_TILE)
        _, rels, onehot = _moe_windows(offs_ref, pos_ref, t, subs, e0 + pair * MOE_PAIR, s, j)
        ycat = jnp.concatenate([y_buf[slot, pair * MOE_PAIR + g, pl.ds(rels[g], MOE_WIN), :]
                                for g in range(MOE_PAIR)], axis=0)
        o_ref[cols, :] += _tn_dot(onehot, ycat)

    for pair in range(MOE_STEP // MOE_PAIR):
        for s in range(subs):
            combine_pass(pair, s, 0)
    for pair in range(MOE_STEP // MOE_PAIR):
        for s in range(subs):
            lax.fori_loop(1, _moe_passes(offs_ref, t, subs, e0 + pair * MOE_PAIR, s),
                          lambda j, carry: (combine_pass(pair, s, j), carry)[1], 0)

    if final_norm:
        @pl.when(q == pl.num_programs(1) - 1)
        def _():
            o_ref[...] = _rms(o_ref[...], fg_ref[...])


def _moe(offs, h2, x1, pos, gate, wg, wu, wd, layer, final_g, final_norm, cap):
    n = h2.shape[0]
    tt, tc = min(MOE_TILE, n), min(MOE_COMBINE_TILE, n)
    assert cap % MOE_CHUNK == 0 and n % tt == 0 and n % tc == 0
    steps = N_EXPERTS // MOE_STEP
    stage_rows = lambda tile: tile + MOE_ALIGN + MOE_WIN * pl.cdiv(ROUTE_TILE + MOE_ALIGN, MOE_WIN)
    tok = lambda tile: pl.BlockSpec((tile, D_MODEL), lambda t, q, offs: (t, 0))
    per_tok = lambda tile: pl.BlockSpec((N_EXPERTS, tile), lambda t, q, offs: (0, t))
    hbm = pl.BlockSpec(memory_space=pl.ANY)
    sems = ("arbitrary", "arbitrary")

    xe = pl.pallas_call(
        _moe_gather_kernel,
        grid_spec=pltpu.PrefetchScalarGridSpec(
            num_scalar_prefetch=1, grid=(n // tt, steps),
            in_specs=[tok(tt), per_tok(tt), per_tok(tt)], out_specs=hbm,
            scratch_shapes=[pltpu.VMEM((2, MOE_STEP, stage_rows(tt), XE_WIDTH), BF16),
                            pltpu.VMEM((N_EXPERTS, MOE_ALIGN, XE_WIDTH), BF16), pltpu.SemaphoreType.DMA((2,))],
        ),
        out_shape=jax.ShapeDtypeStruct((N_EXPERTS, cap + MOE_CHUNK, XE_WIDTH), BF16),
        compiler_params=_cparams(sems),
        name="moe_gather",
    )(offs, h2, pos, gate)

    fb = min(MOE_FFN_ROWS, cap)
    nb = cap // fb
    wspec = pl.BlockSpec((None, None, D_MODEL, D_MODEL), lambda e, j: (layer, e, 0, 0))
    w_bf = pltpu.VMEM((D_MODEL, D_MODEL), BF16)
    y = pl.pallas_call(
        _moe_ffn_kernel,
        grid=(N_EXPERTS, nb + 1),
        in_specs=[pl.BlockSpec((1, fb, XE_WIDTH), lambda e, j: (e, jnp.maximum(j - 1, 0), 0)), wspec, wspec, wspec],
        out_specs=pl.BlockSpec((1, fb, D_MODEL), lambda e, j: (e, jnp.where(j == 0, nb, j - 1), 0)),
        out_shape=jax.ShapeDtypeStruct((N_EXPERTS, cap + fb, D_MODEL), BF16),
        scratch_shapes=[w_bf, w_bf, w_bf],
        compiler_params=_cparams(("parallel", "arbitrary")),
        name="moe_ffn",
    )(xe, wg, wu, wd)

    return pl.pallas_call(
        functools.partial(_moe_combine_kernel, final_norm=final_norm),
        grid_spec=pltpu.PrefetchScalarGridSpec(
            num_scalar_prefetch=1, grid=(n // tc, steps),
            in_specs=[tok(tc), per_tok(tc), hbm, pl.BlockSpec((1, D_MODEL), lambda t, q, offs: (0, 0))],
            out_specs=tok(tc),
            scratch_shapes=[pltpu.VMEM((2, MOE_STEP, stage_rows(tc), D_MODEL), BF16),
                            pltpu.SemaphoreType.DMA((2,))],
        ),
        out_shape=jax.ShapeDtypeStruct((n, D_MODEL), F32),
        compiler_params=_cparams(sems),
        name="moe_combine",
    )(offs, x1, pos, y, final_g)


def _prep_layer(l, norm1_g, w_in, a_sink, a_norm_g, b_rel_bias, b_norm_g, c_alpha_w2_f, c_alpha_b_f,
                c_alpha_w2_b, c_alpha_b_b, c_norm_g, w_out, norm2_g, w_router, w_gate, w_up, w_down):
    z = jnp.zeros((C_LOWRANK, C_WIDTH), F32)
    w2 = jnp.concatenate([jnp.concatenate([c_alpha_w2_f[l], z], axis=1),
                          jnp.concatenate([z, c_alpha_w2_b[l]], axis=1)], axis=0).astype(BF16)
    b2 = jnp.concatenate([c_alpha_b_f[l], c_alpha_b_b[l]])[None, :]
    wr_cat = jnp.pad(w_router[l].astype(BF16), ((0, 0), (0, LANES - N_EXPERTS)))
    return dict(
        layer=l, g1=norm1_g[l][None, :], w_in=w_in.astype(BF16), w2=w2, b2=b2,
        sink=a_sink[l], a_g=a_norm_g[l][None, :], bias=_bias_table(b_rel_bias[l]), b_g=b_norm_g[l][None, :],
        c_g=c_norm_g[l][None, :], w_out=w_out.astype(BF16), g2=norm2_g[l][None, :],
        wr_cat=wr_cat, wg=w_gate, wu=w_up, wd=w_down)


def _layer(x, p, bsz, seq, final_g, final_norm):
    n = bsz * seq
    l = p["layer"]
    aq, ak, av, bq, bk, bv, cq, ck, cv, cg, la = _in_proj(x, p["g1"], p["w_in"], l, p["w2"], p["b2"], seq)
    oa = _attn_a(aq, ak, av, p["sink"], p["a_g"], bsz, seq)
    ob = _attn_b(bq, bk, bv, p["bias"], p["b_g"], bsz, seq)
    o_f, o_b = _gla(cq, ck, cv, la, bsz, seq)
    x1, h2, aff = _out_proj(x, oa, ob, o_f, o_b, cg, p["c_g"], p["w_out"], l, p["g2"], p["wr_cat"])
    cap = EC_CAPACITY * n // N_EXPERTS
    pos, offs = _route(aff, cap)
    return _moe(offs, h2, x1, pos, aff, p["wg"], p["wu"], p["wd"], l, final_g, final_norm, cap)


def _trunk(x, layers, final_g):
    bsz, seq, _ = x.shape
    y = x.reshape(bsz * seq, D_MODEL)
    for l, p in enumerate(layers):
        y = _layer(y, p, bsz, seq, final_g, final_norm=(l == len(layers) - 1))
    return y.reshape(bsz, seq, D_MODEL)


def kernel(x_prompt, x_sample, norm1_g, w_in, a_sink, a_norm_g, b_rel_bias, b_norm_g, c_alpha_w2_f, c_alpha_b_f,
           c_alpha_w2_b, c_alpha_b_b, c_norm_g, w_out, norm2_g, w_router, w_gate, w_up, w_down, final_g):
    depth = w_in.shape[0]
    layers = [_prep_layer(l, norm1_g, w_in, a_sink, a_norm_g, b_rel_bias, b_norm_g, c_alpha_w2_f, c_alpha_b_f,
                          c_alpha_w2_b, c_alpha_b_b, c_norm_g, w_out, norm2_g, w_router, w_gate, w_up, w_down)
              for l in range(depth)]
    fg = final_g[None, :]
    return _trunk(x_prompt, layers, fg), _trunk(x_sample, layers, fg)
```

```python
import functools

import jax
import jax.numpy as jnp
from jax import lax
from jax.experimental import pallas as pl
from jax.experimental.pallas import tpu as pltpu

F32 = jnp.float32
BF16 = jnp.bfloat16

D_MODEL = 1024
HEAD_DIM = 64
A_WIDTH = 512
A_HEADS = 8
A_KV_HEADS = 2
A_KV_WIDTH = 128
WINDOW = 128
ROPE_THETA = 500000.0
ROPE_DIM = 16
B_WIDTH = 256
B_HEADS = 4
GRID_W = 64
WIN_H = 8
WIN_W = 16
C_WIDTH = 256
C_HEADS = 4
C_LOWRANK = 16
C_TAU = 16.0
C_CHUNK = 64
N_EXPERTS = 16
EC_CAPACITY = 2
EPS = 1e-6
NEG_INF = -1e30
IN_WIDTH = 2592

_OFF_AQ, _OFF_AK, _OFF_AV = 0, 512, 640
_OFF_BQ, _OFF_BK, _OFF_BV = 768, 1024, 1280
_OFF_CQ, _OFF_CK, _OFF_CV, _OFF_CG = 1536, 1792, 2048, 2304
_OFF_LR = 2560

LANES = 128
VMEM_LIMIT = 56 * 1024 * 1024

IN_PROJ_ROWS = 1024
OUT_PROJ_ROWS = 1024
GLA_BLOCK = 1024
GLA_CUM_ROWS = 256
ROUTE_TILE = 256
MOE_TILE = 2048
MOE_COMBINE_TILE = 1024
MOE_STEP = 4
MOE_PAIR = 4
MOE_WIN = 64
MOE_COMBINE_WIN = 128
MOE_ALIGN = 16
MOE_CHUNK = 128
MOE_FFN_ROWS = 1024
XE_WIDTH = D_MODEL + LANES


def _cparams(sem):
    return pltpu.CompilerParams(dimension_semantics=sem, vmem_limit_bytes=VMEM_LIMIT)


def _nt_dot(a, b):
    return lax.dot_general(a, b, (((1,), (1,)), ((), ())), preferred_element_type=F32)


def _tn_dot(a, b):
    return lax.dot_general(a, b, (((0,), (0,)), ((), ())), preferred_element_type=F32)


def _dot(a, b):
    return jnp.dot(a, b, preferred_element_type=F32)


def _rms(x, g):
    return x * lax.rsqrt(jnp.mean(x * x, axis=-1, keepdims=True) + EPS) * g


def _same_head_mask(n_rows, rows_per_head):
    width = (n_rows // rows_per_head) * HEAD_DIM
    ri = lax.broadcasted_iota(jnp.int32, (n_rows, width), 0) // rows_per_head
    li = lax.broadcasted_iota(jnp.int32, (n_rows, width), 1) // HEAD_DIM
    return ri == li


def _split_bf16(x):
    hi = x.astype(BF16)
    return hi, (x - hi.astype(F32)).astype(BF16)


def _in_proj_kernel(x_ref, g_ref, w_ref, cos_ref, s1_ref, s2_ref, w2_ref, b2_ref,
                    aq_ref, ak_ref, av_ref, bq_ref, bk_ref, bv_ref,
                    cq_ref, ck_ref, cv_ref, cg_ref, la_ref):
    h = _rms(x_ref[...], g_ref[...]).astype(BF16)
    y = _dot(h, w_ref[...])

    def proj(lo, width):
        return y[:, lo:lo + width]

    cos, s1, s2 = cos_ref[...], s1_ref[...], s2_ref[...]

    def rope(t):
        return t * cos + pltpu.roll(t, LANES - ROPE_DIM // 2, 1) * s1 + pltpu.roll(t, ROPE_DIM // 2, 1) * s2

    scale = HEAD_DIM ** -0.5
    aq = proj(_OFF_AQ, A_WIDTH)
    for c in range(A_WIDTH // LANES):
        aq_ref[:, c * LANES:(c + 1) * LANES] = (rope(aq[:, c * LANES:(c + 1) * LANES]) * scale).astype(BF16)
    ak_ref[...] = rope(proj(_OFF_AK, A_KV_WIDTH)).astype(BF16)
    av_ref[...] = proj(_OFF_AV, A_KV_WIDTH).astype(BF16)
    bq_ref[...] = (proj(_OFF_BQ, B_WIDTH) * scale).astype(BF16)
    bk_ref[...] = proj(_OFF_BK, B_WIDTH).astype(BF16)
    bv_ref[...] = proj(_OFF_BV, B_WIDTH).astype(BF16)
    cq_ref[...] = proj(_OFF_CQ, C_WIDTH) * scale
    ck_ref[...] = proj(_OFF_CK, C_WIDTH)
    cv_ref[...] = proj(_OFF_CV, C_WIDTH).astype(BF16)
    cg_ref[...] = proj(_OFF_CG, C_WIDTH)
    z = _dot(proj(_OFF_LR, 2 * C_LOWRANK).astype(BF16), w2_ref[...]) + b2_ref[...]
    la_ref[...] = (jnp.minimum(z, 0.0) - jnp.log(1.0 + jnp.exp(-jnp.abs(z)))) * (1.0 / C_TAU)


def _rope_tables(seq):
    half = ROPE_DIM // 2
    inv = jnp.power(jnp.float32(ROPE_THETA), -jnp.arange(half, dtype=F32) * (2.0 / ROPE_DIM))
    ang = jnp.arange(seq, dtype=F32)[:, None] * inv[None, :]
    cos, sin = jnp.cos(ang), jnp.sin(ang)
    ones = jnp.ones((seq, HEAD_DIM - ROPE_DIM), F32)
    zeros = jnp.zeros((seq, HEAD_DIM - half), F32)
    c = jnp.concatenate([cos, cos, ones], axis=1)
    s1 = jnp.concatenate([-sin, zeros], axis=1)
    s2 = jnp.concatenate([jnp.zeros((seq, half), F32), sin, jnp.zeros((seq, HEAD_DIM - ROPE_DIM), F32)], axis=1)
    two = lambda t: jnp.concatenate([t, t], axis=1)
    return two(c), two(s1), two(s2)


def _in_proj(x, g1, w_in_bf, layer, w2, b2, seq, rope):
    n = x.shape[0]
    tm = min(IN_PROJ_ROWS, seq)
    per_seq = seq // tm
    cos, s1, s2 = rope
    row = lambda width: pl.BlockSpec((tm, width), lambda i: (i, 0))
    full = lambda a: pl.BlockSpec(a.shape, lambda i: (0,) * a.ndim)
    tab = pl.BlockSpec((tm, LANES), lambda i: (i % per_seq, 0))
    widths = [(A_WIDTH, BF16), (A_KV_WIDTH, BF16), (A_KV_WIDTH, BF16), (B_WIDTH, BF16), (B_WIDTH, BF16),
              (B_WIDTH, BF16), (C_WIDTH, F32), (C_WIDTH, F32), (C_WIDTH, BF16), (C_WIDTH, F32), (2 * C_WIDTH, F32)]
    return pl.pallas_call(
        _in_proj_kernel,
        grid=(n // tm,),
        in_specs=[row(D_MODEL), full(g1), pl.BlockSpec((None, D_MODEL, IN_WIDTH), lambda i: (layer, 0, 0)),
                  tab, tab, tab, full(w2), full(b2)],
        out_specs=[row(w) for w, _ in widths],
        out_shape=[jax.ShapeDtypeStruct((n, w), dt) for w, dt in widths],
        compiler_params=_cparams(("parallel",)),
        name="in_proj",
    )(x, g1, w_in_bf, cos, s1, s2, w2, b2)


A_QBLOCKS = 4


def _attn_a_kernel(sink_ref, q_ref, kp_ref, kc_ref, kn_ref, vp_ref, vc_ref, vn_ref, g_ref, o_ref, acc_ref, *, steps):
    i = pl.program_id(1)
    grp = A_HEADS // A_KV_HEADS
    k = jnp.concatenate([kp_ref[...], kc_ref[...], kn_ref[...]], axis=0)
    v = jnp.concatenate([vp_ref[...], vc_ref[...], vn_ref[...]], axis=0)
    qq = lax.broadcasted_iota(jnp.int32, (WINDOW, 3 * WINDOW), 0)
    kk = lax.broadcasted_iota(jnp.int32, (WINDOW, 3 * WINDOW), 1)
    band = jnp.abs(kk - WINDOW - qq) <= WINDOW
    for blk in range(A_QBLOCKS):
        visible = band
        if blk == 0:
            visible = visible & jnp.logical_not((kk < WINDOW) & (i == 0))
        if blk == A_QBLOCKS - 1:
            visible = visible & jnp.logical_not((kk >= 2 * WINDOW) & (i == steps - 1))
        bias = jnp.where(visible, 0.0, NEG_INF)
        rows = slice(blk * WINDOW, (blk + 1) * WINDOW)
        keys = slice(blk * WINDOW, (blk + 3) * WINDOW)
        for kv in range(A_KV_HEADS):
            heads = range(kv * grp, (kv + 1) * grp)
            q = jnp.concatenate([q_ref[rows, h * HEAD_DIM:(h + 1) * HEAD_DIM] for h in heads], axis=0)
            s = _nt_dot(q, k[keys, kv * HEAD_DIM:(kv + 1) * HEAD_DIM])
            ps, dens = [], []
            for g_i, h in enumerate(heads):
                sh = s[g_i * WINDOW:(g_i + 1) * WINDOW, :] + bias
                sink = sink_ref[h]
                m = jnp.maximum(jnp.max(sh, axis=-1, keepdims=True), sink)
                p = jnp.exp(sh - m)
                dens.append(jnp.sum(p, axis=-1, keepdims=True) + jnp.exp(sink - m))
                ps.append(p.astype(BF16))
            o = _dot(jnp.concatenate(ps, axis=0), v[keys, kv * HEAD_DIM:(kv + 1) * HEAD_DIM])
            for g_i, h in enumerate(heads):
                acc_ref[rows, h * HEAD_DIM:(h + 1) * HEAD_DIM] = o[g_i * WINDOW:(g_i + 1) * WINDOW, :] / dens[g_i]
    o_ref[...] = _rms(acc_ref[...], g_ref[...]).astype(BF16)


def _attn_a(aq, ak, av, sink, g, bsz, seq):
    nb = seq // WINDOW
    assert nb % A_QBLOCKS == 0
    steps = nb // A_QBLOCKS
    qrows = A_QBLOCKS * WINDOW
    qspec = pl.BlockSpec((qrows, A_WIDTH), lambda b, i: (b * steps + i, 0))
    kprev = pl.BlockSpec((WINDOW, A_KV_WIDTH), lambda b, i: (b * nb + jnp.maximum(i * A_QBLOCKS - 1, 0), 0))
    kcur = pl.BlockSpec((qrows, A_KV_WIDTH), lambda b, i: (b * steps + i, 0))
    knext = pl.BlockSpec((WINDOW, A_KV_WIDTH), lambda b, i: (b * nb + jnp.minimum((i + 1) * A_QBLOCKS, nb - 1), 0))
    return pl.pallas_call(
        functools.partial(_attn_a_kernel, steps=steps),
        grid=(bsz, steps),
        in_specs=[pl.BlockSpec(memory_space=pltpu.SMEM), qspec, kprev, kcur, knext, kprev, kcur, knext,
                  pl.BlockSpec((1, A_WIDTH), lambda b, i: (0, 0))],
        out_specs=qspec,
        out_shape=jax.ShapeDtypeStruct((bsz * seq, A_WIDTH), BF16),
        scratch_shapes=[pltpu.VMEM((qrows, A_WIDTH), F32)],
        compiler_params=_cparams(("parallel", "parallel")),
        name="attn_a",
    )(sink, aq, ak, ak, ak, av, av, av, g)


B_GROUP = 16
B_KEYS = WIN_H * GRID_W


def _bias_table_kernel(rel_ref, o_ref):
    c = lax.broadcasted_iota(jnp.int32, (GRID_W, GRID_W), 0)
    w = lax.broadcasted_iota(jnp.int32, (GRID_W, GRID_W), 1)
    cstart = jnp.clip(c - WIN_W // 2, 0, GRID_W - WIN_W)
    colmask = (w >= cstart) & (w < cstart + WIN_W)
    col_off = jnp.clip(w - c + (WIN_W - 1), 0, 2 * WIN_W - 2)
    n_ro, n_co = 2 * WIN_H - 1, 2 * WIN_W - 1
    for h in range(B_HEADS):
        for ro in range(n_ro):
            def body(j, acc):
                return jnp.where(col_off == j, rel_ref[(h * n_ro + ro) * n_co + j], acc)
            t = lax.fori_loop(0, n_co, body, jnp.zeros((GRID_W, GRID_W), F32))
            t = jnp.where(colmask, t, NEG_INF)
            for p in range(WIN_H):
                kidx = ro - (WIN_H - 1) + p
                if 0 <= kidx < WIN_H:
                    o_ref[p, h, :, kidx * GRID_W:(kidx + 1) * GRID_W] = t


def _bias_table(rel_bias):
    return pl.pallas_call(
        _bias_table_kernel,
        in_specs=[pl.BlockSpec(memory_space=pltpu.SMEM)],
        out_shape=jax.ShapeDtypeStruct((WIN_H, B_HEADS, GRID_W, B_KEYS), F32),
        name="bias_table",
    )(rel_bias.reshape(-1)).reshape(WIN_H, B_HEADS * GRID_W, B_KEYS)


def _attn_b_kernel(q_ref, kp_ref, kc_ref, kn_ref, vp_ref, vc_ref, vn_ref, bias_ref, g_ref, o_ref,
                   kbuf, vbuf, acc_ref, *, rows):
    gidx = pl.program_id(1)
    blk = B_GROUP * GRID_W
    kbuf[0:blk, :] = kp_ref[...]
    kbuf[blk:2 * blk, :] = kc_ref[...]
    kbuf[2 * blk:3 * blk, :] = kn_ref[...]
    vbuf[0:blk, :] = vp_ref[...]
    vbuf[blk:2 * blk, :] = vc_ref[...]
    vbuf[2 * blk:3 * blk, :] = vn_ref[...]
    same_head = _same_head_mask(B_HEADS * GRID_W, GRID_W)
    for j in range(B_GROUP):
        r = gidx * B_GROUP + j
        start = jnp.clip(r - WIN_H // 2, 0, rows - WIN_H)
        pat = r - start
        loc = pl.multiple_of((start - gidx * B_GROUP + B_GROUP) * GRID_W, GRID_W)
        kw = kbuf[pl.ds(loc, B_KEYS), :]
        vw = vbuf[pl.ds(loc, B_KEYS), :]
        qj = q_ref[j * GRID_W:(j + 1) * GRID_W, :]
        q = jnp.where(same_head, jnp.concatenate([qj] * B_HEADS, axis=0), jnp.zeros((), BF16))
        s = _nt_dot(q, kw) + bias_ref[pat]
        m = jnp.max(s, axis=-1, keepdims=True)
        p = jnp.exp(s - m)
        den = jnp.sum(p, axis=-1, keepdims=True)
        o = jnp.where(same_head, _dot(p.astype(BF16), vw) / den, 0.0)
        acc_ref[j * GRID_W:(j + 1) * GRID_W, :] = sum(o[h * GRID_W:(h + 1) * GRID_W, :] for h in range(B_HEADS))
    o_ref[...] = _rms(acc_ref[...], g_ref[...]).astype(BF16)


def _attn_b(bq, bk, bv, bias, g, bsz, seq):
    rows = seq // GRID_W
    assert rows % B_GROUP == 0 and rows >= WIN_H
    ng = rows // B_GROUP
    blk = B_GROUP * GRID_W
    cur = pl.BlockSpec((blk, B_WIDTH), lambda b, i: (b * ng + i, 0))
    prev = pl.BlockSpec((blk, B_WIDTH), lambda b, i: (b * ng + jnp.maximum(i - 1, 0), 0))
    nxt = pl.BlockSpec((blk, B_WIDTH), lambda b, i: (b * ng + jnp.minimum(i + 1, ng - 1), 0))
    return pl.pallas_call(
        functools.partial(_attn_b_kernel, rows=rows),
        grid=(bsz, ng),
        in_specs=[cur, prev, cur, nxt, prev, cur, nxt,
                  pl.BlockSpec(bias.shape, lambda b, i: (0, 0, 0)),
                  pl.BlockSpec((1, B_WIDTH), lambda b, i: (0, 0))],
        out_specs=cur,
        out_shape=jax.ShapeDtypeStruct((bsz * seq, B_WIDTH), BF16),
        scratch_shapes=[pltpu.VMEM((3 * blk, B_WIDTH), BF16), pltpu.VMEM((3 * blk, B_WIDTH), BF16),
                        pltpu.VMEM((blk, B_WIDTH), F32)],
        compiler_params=_cparams(("parallel", "parallel")),
        name="attn_b",
    )(bq, bk, bk, bk, bv, bv, bv, bias, g)


def _gla_decay(la_ref, reverse):
    seg = min(GLA_CUM_ROWS, la_ref.shape[0])
    bi = lax.broadcasted_iota(jnp.int32, (seg, seg), 0)
    bj = lax.broadcasted_iota(jnp.int32, (seg, seg), 1)
    in_chunk = (bi // C_CHUNK) == (bj // C_CHUNK)
    cum_w = (in_chunk & ((bj >= bi) if reverse else (bj <= bi))).astype(BF16)
    parts = []
    for r in range(0, la_ref.shape[0], seg):
        la_hi, la_lo = _split_bf16(la_ref[r:r + seg, :])
        parts.append(_dot(cum_w, la_hi) + _dot(cum_w, la_lo))
    return parts[0] if len(parts) == 1 else jnp.concatenate(parts, axis=0)


def _gla_chunk(q_ref, k_ref, v_ref, o_ref, b_all, st, c, reverse):
    same_head = _same_head_mask(C_HEADS * C_CHUNK, C_CHUNK)
    ti = lax.broadcasted_iota(jnp.int32, (C_CHUNK, C_HEADS * C_CHUNK), 0)
    si = lax.broadcasted_iota(jnp.int32, (C_CHUNK, C_HEADS * C_CHUNK), 1) % C_CHUNK
    tri = (si >= ti) if reverse else (si <= ti)
    zero = jnp.zeros((), BF16)
    rows = slice(c * C_CHUNK, (c + 1) * C_CHUNK)
    b = b_all[rows, :]
    b_tot = b[0:1, :] if reverse else b[C_CHUNK - 1:C_CHUNK, :]
    q_i = (q_ref[rows, :] * jnp.exp(b)).astype(BF16)
    kf = k_ref[rows, :]
    k_i = (kf * jnp.exp(-b)).astype(BF16)
    k_e = (kf * jnp.exp(b_tot - b)).astype(BF16)
    v = v_ref[rows, :]
    k_bd = jnp.where(same_head, jnp.concatenate([k_i] * C_HEADS, axis=0), zero)
    v_bd = jnp.where(same_head, jnp.concatenate([v] * C_HEADS, axis=0), zero)
    a = jnp.where(tri, _nt_dot(q_i, k_bd), 0.0)
    o_ref[rows, :] = _dot(a.astype(BF16), v_bd) + _nt_dot(q_i, st.astype(BF16))
    return st * jnp.exp(b_tot) + jnp.where(same_head, _tn_dot(v, k_e), 0.0)


def _gla_kernel(qf_ref, kf_ref, vf_ref, laf_ref, qb_ref, kb_ref, vb_ref, lab_ref, of_ref, ob_ref, sf_ref, sb_ref):
    @pl.when(pl.program_id(1) == 0)
    def _():
        sf_ref[...] = jnp.zeros_like(sf_ref)
        sb_ref[...] = jnp.zeros_like(sb_ref)

    n_chunks = qf_ref.shape[0] // C_CHUNK
    b_f = _gla_decay(laf_ref, reverse=False)
    b_b = _gla_decay(lab_ref, reverse=True)
    s_f, s_b = sf_ref[...], sb_ref[...]
    for c in range(n_chunks):
        s_f = _gla_chunk(qf_ref, kf_ref, vf_ref, of_ref, b_f, s_f, c, reverse=False)
        s_b = _gla_chunk(qb_ref, kb_ref, vb_ref, ob_ref, b_b, s_b, n_chunks - 1 - c, reverse=True)
    sf_ref[...] = s_f
    sb_ref[...] = s_b


def _gla(cq, ck, cv, la, bsz, seq):
    tb = min(GLA_BLOCK, seq)
    nb = seq // tb
    fwd = lambda col: pl.BlockSpec((tb, C_WIDTH), lambda b, i: (b * nb + i, col))
    bwd = lambda col: pl.BlockSpec((tb, C_WIDTH), lambda b, i: (b * nb + nb - 1 - i, col))
    out = jax.ShapeDtypeStruct((bsz * seq, C_WIDTH), F32)
    state = pltpu.VMEM((C_WIDTH, C_WIDTH), F32)
    return pl.pallas_call(
        _gla_kernel,
        grid=(bsz, nb),
        in_specs=[fwd(0), fwd(0), fwd(0), fwd(0), bwd(0), bwd(0), bwd(0), bwd(1)],
        out_specs=[fwd(0), bwd(0)],
        out_shape=[out, out],
        scratch_shapes=[state, state],
        compiler_params=_cparams(("parallel", "arbitrary")),
        name="gla",
    )(cq, ck, cv, la, cq, ck, cv, la)


def _out_proj_kernel(x_ref, oa_ref, ob_ref, of_ref, obw_ref, cg_ref, cng_ref, wo_ref, g2_ref, wr_ref,
                     x1_ref, h2_ref, aff_ref, mix_ref):
    gi = lax.broadcasted_iota(jnp.int32, (C_WIDTH, C_WIDTH), 0) // HEAD_DIM
    gj = lax.broadcasted_iota(jnp.int32, (C_WIDTH, C_WIDTH), 1) // HEAD_DIM
    ones_bd = (gi == gj).astype(BF16)
    half = x_ref.shape[0] // 2
    for rows in (slice(0, half), slice(half, 2 * half)):
        o = of_ref[rows, :] + obw_ref[rows, :]
        sq_hi, sq_lo = _split_bf16(o * o)
        ms = (_dot(sq_hi, ones_bd) + _dot(sq_lo, ones_bd)) * (1.0 / HEAD_DIM)
        cg = cg_ref[rows, :]
        oc = (o * lax.rsqrt(ms + EPS) * cng_ref[...]) * (cg / (1.0 + jnp.exp(-cg)))
        mix_ref[rows, 0:A_WIDTH] = oa_ref[rows, :]
        mix_ref[rows, A_WIDTH:A_WIDTH + B_WIDTH] = ob_ref[rows, :]
        mix_ref[rows, A_WIDTH + B_WIDTH:] = oc.astype(BF16)
        x1 = x_ref[rows, :] + _dot(mix_ref[rows, :], wo_ref[...])
        x1_ref[rows, :] = x1
        h2 = _rms(x1, g2_ref[...]).astype(BF16)
        h2_ref[rows, :] = h2
        logits = _dot(h2, wr_ref[...])
        lane = lax.broadcasted_iota(jnp.int32, logits.shape, 1)
        logits = jnp.where(lane < N_EXPERTS, logits, NEG_INF)
        p = jnp.exp(logits - jnp.max(logits, axis=1, keepdims=True))
        aff = p / jnp.sum(p, axis=1, keepdims=True)
        aff_ref[:, rows] = aff.T[0:N_EXPERTS, :]


def _out_proj(x, oa, ob, o_f, o_b, cg, cng, wo_bf, layer, g2, wr_cat):
    n = x.shape[0]
    tm = min(OUT_PROJ_ROWS, n)
    row = lambda width: pl.BlockSpec((tm, width), lambda i: (i, 0))
    full = lambda a: pl.BlockSpec(a.shape, lambda i: (0,) * a.ndim)
    return pl.pallas_call(
        _out_proj_kernel,
        grid=(n // tm,),
        in_specs=[row(D_MODEL), row(A_WIDTH), row(B_WIDTH), row(C_WIDTH), row(C_WIDTH), row(C_WIDTH),
                  full(cng), pl.BlockSpec((None, D_MODEL, D_MODEL), lambda i: (layer, 0, 0)), full(g2), full(wr_cat)],
        out_specs=[row(D_MODEL), row(D_MODEL), pl.BlockSpec((N_EXPERTS, tm), lambda i: (0, i))],
        out_shape=[jax.ShapeDtypeStruct((n, D_MODEL), F32), jax.ShapeDtypeStruct((n, D_MODEL), BF16),
                   jax.ShapeDtypeStruct((N_EXPERTS, n), F32)],
        scratch_shapes=[pltpu.VMEM((tm, D_MODEL), BF16)],
        compiler_params=_cparams(("parallel",)),
        name="out_proj",
    )(x, oa, ob, o_f, o_b, cg, cng, wo_bf, g2, wr_cat)


def _route_kernel(aff_ref, pos_ref, offs_ref, *, cap):
    n = aff_ref.shape[1]
    nt = n // ROUTE_TILE
    bits = pltpu.bitcast(aff_ref[...], jnp.int32)

    def search(it, thr):
        cand = thr | jnp.left_shift(jnp.int32(1), 30 - it)
        cnt = jnp.sum(jnp.where(bits >= cand, 1.0, 0.0), axis=1, keepdims=True)
        return jnp.where(cnt >= cap, cand, thr)

    thr = lax.fori_loop(0, 31, search, jnp.zeros((N_EXPERTS, 1), jnp.int32))
    n_gt = jnp.sum(jnp.where(bits > thr, 1.0, 0.0), axis=1, keepdims=True)
    n_tie = cap - n_gt

    ui = lax.broadcasted_iota(jnp.int32, (ROUTE_TILE, ROUTE_TILE), 0)
    uj = lax.broadcasted_iota(jnp.int32, (ROUTE_TILE, ROUTE_TILE), 1)
    upper = (ui <= uj).astype(BF16)
    lane = lax.broadcasted_iota(jnp.int32, offs_ref.shape, 1)

    def tile(i, carry):
        c_gt, c_eq, offs = carry
        col = pl.multiple_of(i * ROUTE_TILE, ROUTE_TILE)
        b = pltpu.bitcast(aff_ref[:, pl.ds(col, ROUTE_TILE)], jnp.int32)
        gt = jnp.where(b > thr, 1.0, 0.0)
        eq = jnp.where(b == thr, 1.0, 0.0)
        inc = _dot(jnp.concatenate([gt, eq], axis=0).astype(BF16), upper)
        gt_before = c_gt + inc[:N_EXPERTS] - gt
        eq_before = c_eq + inc[N_EXPERTS:] - eq
        sel = (gt > 0.0) | ((eq > 0.0) & (eq_before < n_tie))
        pos = gt_before + jnp.minimum(eq_before, n_tie)
        pos_ref[:, pl.ds(col, ROUTE_TILE)] = jnp.where(sel, pos, -1.0).astype(jnp.int32)
        start = c_gt + jnp.minimum(c_eq, n_tie)
        offs = jnp.where(lane == i, start.astype(jnp.int32), offs)
        return (c_gt + jnp.sum(gt, axis=1, keepdims=True), c_eq + jnp.sum(eq, axis=1, keepdims=True), offs)

    zero = jnp.zeros((N_EXPERTS, 1), F32)
    offs = jnp.where(lane == nt, cap, 0).astype(jnp.int32)
    _, _, offs = lax.fori_loop(0, nt, tile, (zero, zero, offs))
    offs_ref[...] = offs


def _route(aff_t, cap):
    n = aff_t.shape[1]
    nt = n // ROUTE_TILE
    return pl.pallas_call(
        functools.partial(_route_kernel, cap=cap),
        out_shape=[jax.ShapeDtypeStruct((N_EXPERTS, n), jnp.int32),
                   jax.ShapeDtypeStruct((N_EXPERTS, nt + 1), jnp.int32)],
        compiler_params=pltpu.CompilerParams(vmem_limit_bytes=VMEM_LIMIT),
        name="route",
    )(aff_t)


def _align(v):
    return (v // MOE_ALIGN) * MOE_ALIGN


def _moe_passes(offs_ref, t, subs, x0, s, win):
    k = jnp.int32(1)
    for g in range(MOE_PAIR):
        need = offs_ref[x0 + g, t * subs + s + 1] - _align(offs_ref[x0 + g, t * subs + s])
        k = jnp.maximum(k, (need + win - 1) // win)
    return k


def _moe_windows(offs_ref, pos_ref, t, subs, x0, s, j, win):
    slot = lax.broadcasted_iota(jnp.int32, (win, ROUTE_TILE), 0)
    hits, rels = [], []
    for g in range(MOE_PAIR):
        x = x0 + g
        ws = _align(offs_ref[x, t * subs + s]) + j * win
        pos = pos_ref[pl.ds(x, 1), s * ROUTE_TILE:(s + 1) * ROUTE_TILE]
        hits.append((pos - ws) == slot)
        rels.append(pl.multiple_of(ws - _align(offs_ref[x, t * subs]), MOE_ALIGN))
    onehot = jnp.concatenate([jnp.where(h, 1.0, 0.0) for h in hits], axis=0).astype(BF16)
    return hits, rels, onehot


def _moe_chunks(offs_ref, t, subs, x):
    base = _align(offs_ref[x, t * subs])
    return base, (offs_ref[x, (t + 1) * subs] - base + MOE_CHUNK - 1) // MOE_CHUNK


def _moe_gather_kernel(offs_ref, h_ref, pos_ref, gate_ref, xe_ref, stage_ref, carry_ref, sem):
    t = pl.program_id(0)
    steps = pl.num_programs(1)
    e0 = pl.program_id(1) * MOE_STEP
    tt = h_ref.shape[0]
    subs = tt // ROUTE_TILE
    k = t * steps + pl.program_id(1)
    slot = k % 2
    x_ref = stage_ref.at[slot]

    def off(x, s):
        return offs_ref[x, t * subs + s]

    def copies(kk, half, act):
        step_t, step_q = kk // steps, kk % steps
        for g in range(MOE_STEP):
            x = step_q * MOE_STEP + g
            base, n_chunks = _moe_chunks(offs_ref, step_t, subs, x)

            def body(c, carry):
                r0 = pl.multiple_of(c * MOE_CHUNK, MOE_CHUNK)
                act(pltpu.make_async_copy(stage_ref.at[half, g, pl.ds(r0, MOE_CHUNK), :],
                                          xe_ref.at[x, pl.ds(pl.multiple_of(base + r0, MOE_ALIGN), MOE_CHUNK), :],
                                          sem.at[half]))
                return carry

            lax.fori_loop(0, n_chunks, body, 0)

    @pl.when(k >= 2)
    def _():
        copies(k - 2, slot, lambda cp: cp.wait())

    @pl.when(t == 0)
    def _():
        carry_ref[pl.ds(e0, MOE_STEP)] = jnp.zeros((MOE_STEP, MOE_ALIGN, XE_WIDTH), BF16)
        cap = xe_ref.shape[1] - MOE_CHUNK
        x_ref[0, 0:MOE_CHUNK, :] = jnp.zeros((MOE_CHUNK, XE_WIDTH), BF16)
        pads = [pltpu.make_async_copy(x_ref.at[0, 0:MOE_CHUNK, :], xe_ref.at[e0 + g, cap:cap + MOE_CHUNK, :],
                                      sem.at[slot]) for g in range(MOE_STEP)]
        for p in pads:
            p.start()
        for p in pads:
            p.wait()

    for g in range(MOE_STEP):
        x_ref[g, 0:MOE_ALIGN, :] = carry_ref[e0 + g]

    row = lax.broadcasted_iota(jnp.int32, (MOE_WIN, 1), 0)
    lane = lax.broadcasted_iota(jnp.int32, (MOE_WIN, XE_WIDTH - D_MODEL), 1)

    def gather_pass(pair, s, j):
        x0 = e0 + pair * MOE_PAIR
        cols = slice(s * ROUTE_TILE, (s + 1) * ROUTE_TILE)
        hits, rels, onehot = _moe_windows(offs_ref, pos_ref, t, subs, x0, s, j, MOE_WIN)
        r = _dot(onehot, h_ref[cols, :])
        for g in range(MOE_PAIR):
            gate = jnp.sum(jnp.where(hits[g], gate_ref[pl.ds(x0 + g, 1), cols], 0.0), axis=1, keepdims=True)
            g_hi = gate.astype(BF16).astype(F32)
            extra = jnp.where(lane == 0, g_hi, jnp.where(lane == 1, gate - g_hi, 0.0))
            new = jnp.concatenate([r[g * MOE_WIN:(g + 1) * MOE_WIN, :], extra], axis=1)
            ws = _align(off(x0 + g, s)) + j * MOE_WIN
            own = row >= off(x0 + g, s) - ws
            buf = pair * MOE_PAIR + g
            if isinstance(j, int):
                head = pl.ds(rels[g], MOE_ALIGN)
                x_ref[buf, head, :] = jnp.where(own[:MOE_ALIGN], new[:MOE_ALIGN],
                                                x_ref[buf, head, :].astype(F32)).astype(BF16)
                x_ref[buf, pl.ds(rels[g] + MOE_ALIGN, MOE_WIN - MOE_ALIGN), :] = new[MOE_ALIGN:].astype(BF16)
            else:
                own = own & (row < off(x0 + g, s + 1) - ws)
                win = pl.ds(rels[g], MOE_WIN)
                x_ref[buf, win, :] = jnp.where(own, new, x_ref[buf, win, :].astype(F32)).astype(BF16)

    for pair in range(MOE_STEP // MOE_PAIR):
        for s in range(subs):
            gather_pass(pair, s, 0)
    for pair in range(MOE_STEP // MOE_PAIR):
        for s in range(subs):
            lax.fori_loop(1, _moe_passes(offs_ref, t, subs, e0 + pair * MOE_PAIR, s, MOE_WIN),
                          lambda j, carry: (gather_pass(pair, s, j), carry)[1], 0)

    for g in range(MOE_STEP):
        x = e0 + g
        base, n_chunks = _moe_chunks(offs_ref, t, subs, x)
        last_group = pl.multiple_of(_align(off(x, subs)) - base, MOE_ALIGN)
        tail = x_ref[g, pl.ds(last_group, MOE_ALIGN), :].astype(F32)
        tail = jnp.where(row[:MOE_ALIGN] < off(x, subs) - base - last_group, tail, 0.0).astype(BF16)
        x_ref[g, pl.ds(last_group, MOE_ALIGN), :] = tail
        carry_ref[x] = tail

        def fill(i, carry):
            x_ref[g, pl.ds(pl.multiple_of(i * MOE_ALIGN, MOE_ALIGN), MOE_ALIGN), :] = jnp.zeros((MOE_ALIGN, XE_WIDTH), BF16)
            return carry

        lax.fori_loop(last_group // MOE_ALIGN + 1, n_chunks * MOE_CHUNK // MOE_ALIGN, fill, 0)

    copies(k, slot, lambda cp: cp.start())

    @pl.when(k == pl.num_programs(0) * steps - 1)
    def _():
        copies(k - 1, 1 - slot, lambda cp: cp.wait())
        copies(k, slot, lambda cp: cp.wait())


def _moe_ffn_kernel(x_ref, wg_ref, wu_ref, wd_ref, y_ref, wg_bf, wu_bf, wd_bf):
    j = pl.program_id(1)

    @pl.when(j == 0)
    def _():
        wg_bf[...] = wg_ref[...].astype(BF16)
        wu_bf[...] = wu_ref[...].astype(BF16)
        wd_bf[...] = wd_ref[...].astype(BF16)
        y_ref[0] = jnp.zeros(y_ref.shape[1:], BF16)

    @pl.when(j > 0)
    def _():
        x = x_ref[0, :, 0:D_MODEL]
        gate = jnp.sum(x_ref[0, :, D_MODEL:XE_WIDTH].astype(F32), axis=1, keepdims=True)
        hg = _dot(x, wg_bf[...])
        hu = _dot(x, wu_bf[...])
        act = (hg / (1.0 + jnp.exp(-hg))) * hu * gate
        y_ref[0] = _dot(act.astype(BF16), wd_bf[...]).astype(BF16)


def _moe_combine_kernel(offs_ref, x1_ref, pos_ref, y_ref, fg_ref, o_ref, y_buf, sem, *, final_norm):
    t = pl.program_id(0)
    q = pl.program_id(1)
    steps = pl.num_programs(1)
    e0 = q * MOE_STEP
    tt = x1_ref.shape[0]
    subs = tt // ROUTE_TILE
    k = t * steps + q
    slot = k % 2

    def fetch(kk, half, act):
        step_t, step_q = kk // steps, kk % steps
        for g in range(MOE_STEP):
            x = step_q * MOE_STEP + g
            base, n_chunks = _moe_chunks(offs_ref, step_t, subs, x)

            def body(c, carry):
                r0 = pl.multiple_of(c * MOE_CHUNK, MOE_CHUNK)
                act(pltpu.make_async_copy(y_ref.at[x, pl.ds(pl.multiple_of(base + r0, MOE_ALIGN), MOE_CHUNK), :],
                                          y_buf.at[half, g, pl.ds(r0, MOE_CHUNK), :], sem.at[half]))
                return carry

            lax.fori_loop(0, n_chunks, body, 0)

    @pl.when(k == 0)
    def _():
        y_buf[...] = jnp.zeros_like(y_buf)
        fetch(k, slot, lambda cp: cp.start())

    @pl.when(k + 1 < pl.num_programs(0) * steps)
    def _():
        fetch(k + 1, 1 - slot, lambda cp: cp.start())

    fetch(k, slot, lambda cp: cp.wait())

    @pl.when(q == 0)
    def _():
        o_ref[...] = x1_ref[...]

    def combine_pass(pair, s, j):
        cols = slice(s * ROUTE_TILE, (s + 1) * ROUTE_TILE)
        _, rels, onehot = _moe_windows(offs_ref, pos_ref, t, subs, e0 + pair * MOE_PAIR, s, j, MOE_COMBINE_WIN)
        ycat = jnp.concatenate([y_buf[slot, pair * MOE_PAIR + g, pl.ds(rels[g], MOE_COMBINE_WIN), :]
                                for g in range(MOE_PAIR)], axis=0)
        o_ref[cols, :] += _tn_dot(onehot, ycat)

    for pair in range(MOE_STEP // MOE_PAIR):
        for s in range(subs):
            combine_pass(pair, s, 0)
    for pair in range(MOE_STEP // MOE_PAIR):
        for s in range(subs):
            lax.fori_loop(1, _moe_passes(offs_ref, t, subs, e0 + pair * MOE_PAIR, s, MOE_COMBINE_WIN),
                          lambda j, carry: (combine_pass(pair, s, j), carry)[1], 0)

    if final_norm:
        @pl.when(q == pl.num_programs(1) - 1)
        def _():
            o_ref[...] = _rms(o_ref[...], fg_ref[...])


def _moe(offs, h2, x1, pos, gate, wg, wu, wd, layer, final_g, final_norm, cap):
    n = h2.shape[0]
    tt, tc = min(MOE_TILE, n), min(MOE_COMBINE_TILE, n)
    assert cap % MOE_CHUNK == 0 and n % tt == 0 and n % tc == 0
    steps = N_EXPERTS // MOE_STEP
    stage_rows = lambda tile, win: tile + MOE_ALIGN + win * pl.cdiv(ROUTE_TILE + MOE_ALIGN, win)
    tok = lambda tile: pl.BlockSpec((tile, D_MODEL), lambda t, q, offs: (t, 0))
    per_tok = lambda tile: pl.BlockSpec((N_EXPERTS, tile), lambda t, q, offs: (0, t))
    hbm = pl.BlockSpec(memory_space=pl.ANY)
    sems = ("arbitrary", "arbitrary")

    xe = pl.pallas_call(
        _moe_gather_kernel,
        grid_spec=pltpu.PrefetchScalarGridSpec(
            num_scalar_prefetch=1, grid=(n // tt, steps),
            in_specs=[tok(tt), per_tok(tt), per_tok(tt)], out_specs=hbm,
            scratch_shapes=[pltpu.VMEM((2, MOE_STEP, stage_rows(tt, MOE_WIN), XE_WIDTH), BF16),
                            pltpu.VMEM((N_EXPERTS, MOE_ALIGN, XE_WIDTH), BF16), pltpu.SemaphoreType.DMA((2,))],
        ),
        out_shape=jax.ShapeDtypeStruct((N_EXPERTS, cap + MOE_CHUNK, XE_WIDTH), BF16),
        compiler_params=_cparams(sems),
        name="moe_gather",
    )(offs, h2, pos, gate)

    fb = min(MOE_FFN_ROWS, cap)
    nb = cap // fb
    wspec = pl.BlockSpec((None, None, D_MODEL, D_MODEL), lambda e, j: (layer, e, 0, 0))
    w_bf = pltpu.VMEM((D_MODEL, D_MODEL), BF16)
    y = pl.pallas_call(
        _moe_ffn_kernel,
        grid=(N_EXPERTS, nb + 1),
        in_specs=[pl.BlockSpec((1, fb, XE_WIDTH), lambda e, j: (e, jnp.maximum(j - 1, 0), 0)), wspec, wspec, wspec],
        out_specs=pl.BlockSpec((1, fb, D_MODEL), lambda e, j: (e, jnp.where(j == 0, nb, j - 1), 0)),
        out_shape=jax.ShapeDtypeStruct((N_EXPERTS, cap + fb, D_MODEL), BF16),
        scratch_shapes=[w_bf, w_bf, w_bf],
        compiler_params=_cparams(("parallel", "arbitrary")),
        name="moe_ffn",
    )(xe, wg, wu, wd)

    return pl.pallas_call(
        functools.partial(_moe_combine_kernel, final_norm=final_norm),
        grid_spec=pltpu.PrefetchScalarGridSpec(
            num_scalar_prefetch=1, grid=(n // tc, steps),
            in_specs=[tok(tc), per_tok(tc), hbm, pl.BlockSpec((1, D_MODEL), lambda t, q, offs: (0, 0))],
            out_specs=tok(tc),
            scratch_shapes=[pltpu.VMEM((2, MOE_STEP, stage_rows(tc, MOE_COMBINE_WIN), D_MODEL), BF16),
                            pltpu.SemaphoreType.DMA((2,))],
        ),
        out_shape=jax.ShapeDtypeStruct((n, D_MODEL), F32),
        compiler_params=_cparams(sems),
        name="moe_combine",
    )(offs, x1, pos, y, final_g)


def _prep_layer(l, norm1_g, w_in, a_sink, a_norm_g, b_rel_bias, b_norm_g, c_alpha_w2_f, c_alpha_b_f,
                c_alpha_w2_b, c_alpha_b_b, c_norm_g, w_out, norm2_g, w_router, w_gate, w_up, w_down):
    z = jnp.zeros((C_LOWRANK, C_WIDTH), F32)
    w2 = jnp.concatenate([jnp.concatenate([c_alpha_w2_f[l], z], axis=1),
                          jnp.concatenate([z, c_alpha_w2_b[l]], axis=1)], axis=0).astype(BF16)
    b2 = jnp.concatenate([c_alpha_b_f[l], c_alpha_b_b[l]])[None, :]
    wr_cat = jnp.pad(w_router[l].astype(BF16), ((0, 0), (0, LANES - N_EXPERTS)))
    return dict(
        layer=l, g1=norm1_g[l][None, :], w_in=w_in.astype(BF16), w2=w2, b2=b2,
        sink=a_sink[l], a_g=a_norm_g[l][None, :], bias=_bias_table(b_rel_bias[l]), b_g=b_norm_g[l][None, :],
        c_g=c_norm_g[l][None, :], w_out=w_out.astype(BF16), g2=norm2_g[l][None, :],
        wr_cat=wr_cat, wg=w_gate, wu=w_up, wd=w_down)


def _layer(x, p, bsz, seq, rope, final_g, final_norm):
    n = bsz * seq
    l = p["layer"]
    aq, ak, av, bq, bk, bv, cq, ck, cv, cg, la = _in_proj(x, p["g1"], p["w_in"], l, p["w2"], p["b2"], seq, rope)
    oa = _attn_a(aq, ak, av, p["sink"], p["a_g"], bsz, seq)
    ob = _attn_b(bq, bk, bv, p["bias"], p["b_g"], bsz, seq)
    o_f, o_b = _gla(cq, ck, cv, la, bsz, seq)
    x1, h2, aff = _out_proj(x, oa, ob, o_f, o_b, cg, p["c_g"], p["w_out"], l, p["g2"], p["wr_cat"])
    cap = EC_CAPACITY * n // N_EXPERTS
    pos, offs = _route(aff, cap)
    return _moe(offs, h2, x1, pos, aff, p["wg"], p["wu"], p["wd"], l, final_g, final_norm, cap)


def _trunk(x, layers, rope, final_g):
    bsz, seq, _ = x.shape
    y = x.reshape(bsz * seq, D_MODEL)
    for l, p in enumerate(layers):
        y = _layer(y, p, bsz, seq, rope, final_g, final_norm=(l == len(layers) - 1))
    return y.reshape(bsz, seq, D_MODEL)


def kernel(x_prompt, x_sample, norm1_g, w_in, a_sink, a_norm_g, b_rel_bias, b_norm_g, c_alpha_w2_f, c_alpha_b_f,
           c_alpha_w2_b, c_alpha_b_b, c_norm_g, w_out, norm2_g, w_router, w_gate, w_up, w_down, final_g):
    depth = w_in.shape[0]
    layers = [_prep_layer(l, norm1_g, w_in, a_sink, a_norm_g, b_rel_bias, b_norm_g, c_alpha_w2_f, c_alpha_b_f,
                          c_alpha_w2_b, c_alpha_b_b, c_norm_g, w_out, norm2_g, w_router, w_gate, w_up, w_down)
              for l in range(depth)]
    fg = final_g[None, :]
    rope = _rope_tables(max(x_prompt.shape[1], x_sample.shape[1]))
    return _trunk(x_prompt, layers, rope, fg), _trunk(x_sample, layers, rope, fg)
```

```python
import functools

import jax
import jax.numpy as jnp
from jax import lax
from jax.experimental import pallas as pl
from jax.experimental.pallas import tpu as pltpu

F32 = jnp.float32
BF16 = jnp.bfloat16

D_MODEL = 1024
HEAD_DIM = 64
A_WIDTH = 512
A_HEADS = 8
A_KV_HEADS = 2
A_KV_WIDTH = 128
WINDOW = 128
ROPE_THETA = 500000.0
ROPE_DIM = 16
B_WIDTH = 256
B_HEADS = 4
GRID_W = 64
WIN_H = 8
WIN_W = 16
C_WIDTH = 256
C_HEADS = 4
C_LOWRANK = 16
C_TAU = 16.0
C_CHUNK = 64
N_EXPERTS = 16
EC_CAPACITY = 2
EPS = 1e-6
NEG_INF = -1e30
IN_WIDTH = 2592

_OFF_AQ, _OFF_AK, _OFF_AV = 0, 512, 640
_OFF_BQ, _OFF_BK, _OFF_BV = 768, 1024, 1280
_OFF_CQ, _OFF_CK, _OFF_CV, _OFF_CG = 1536, 1792, 2048, 2304
_OFF_LR = 2560

LANES = 128
VMEM_LIMIT = 56 * 1024 * 1024

IN_PROJ_ROWS = 1024
OUT_PROJ_ROWS = 1024
GLA_BLOCK = 1024
GLA_CUM_ROWS = 256
ROUTE_TILE = 256
MOE_TILE = 2048
MOE_COMBINE_TILE = 1024
MOE_STEP = 4
MOE_PAIR = 4
MOE_WIN = 64
MOE_COMBINE_WIN = 128
MOE_ALIGN = 16
MOE_CHUNK = 128
MOE_FFN_ROWS = 1024
XE_WIDTH = D_MODEL + LANES


def _cparams(sem):
    return pltpu.CompilerParams(dimension_semantics=sem, vmem_limit_bytes=VMEM_LIMIT)


def _nt_dot(a, b):
    return lax.dot_general(a, b, (((1,), (1,)), ((), ())), preferred_element_type=F32)


def _tn_dot(a, b):
    return lax.dot_general(a, b, (((0,), (0,)), ((), ())), preferred_element_type=F32)


def _dot(a, b):
    return jnp.dot(a, b, preferred_element_type=F32)


def _rms(x, g):
    return x * lax.rsqrt(jnp.mean(x * x, axis=-1, keepdims=True) + EPS) * g


def _same_head_mask(n_rows, rows_per_head):
    width = (n_rows // rows_per_head) * HEAD_DIM
    ri = lax.broadcasted_iota(jnp.int32, (n_rows, width), 0) // rows_per_head
    li = lax.broadcasted_iota(jnp.int32, (n_rows, width), 1) // HEAD_DIM
    return ri == li


def _split_bf16(x):
    hi = x.astype(BF16)
    return hi, (x - hi.astype(F32)).astype(BF16)


def _in_proj_kernel(x_ref, g_ref, w_ref, cos_ref, s1_ref, s2_ref, w2_ref, b2_ref,
                    aq_ref, ak_ref, av_ref, bq_ref, bk_ref, bv_ref,
                    cq_ref, ck_ref, cv_ref, cg_ref, la_ref):
    h = _rms(x_ref[...], g_ref[...]).astype(BF16)
    y = _dot(h, w_ref[...])

    def proj(lo, width):
        return y[:, lo:lo + width]

    cos, s1, s2 = cos_ref[...], s1_ref[...], s2_ref[...]

    def rope(t):
        return t * cos + pltpu.roll(t, LANES - ROPE_DIM // 2, 1) * s1 + pltpu.roll(t, ROPE_DIM // 2, 1) * s2

    scale = HEAD_DIM ** -0.5
    aq = proj(_OFF_AQ, A_WIDTH)
    for c in range(A_WIDTH // LANES):
        aq_ref[:, c * LANES:(c + 1) * LANES] = (rope(aq[:, c * LANES:(c + 1) * LANES]) * scale).astype(BF16)
    ak_ref[...] = rope(proj(_OFF_AK, A_KV_WIDTH)).astype(BF16)
    av_ref[...] = proj(_OFF_AV, A_KV_WIDTH).astype(BF16)
    bq_ref[...] = (proj(_OFF_BQ, B_WIDTH) * scale).astype(BF16)
    bk_ref[...] = proj(_OFF_BK, B_WIDTH).astype(BF16)
    bv_ref[...] = proj(_OFF_BV, B_WIDTH).astype(BF16)
    cq_ref[...] = proj(_OFF_CQ, C_WIDTH) * scale
    ck_ref[...] = proj(_OFF_CK, C_WIDTH)
    cv_ref[...] = proj(_OFF_CV, C_WIDTH).astype(BF16)
    cg_ref[...] = proj(_OFF_CG, C_WIDTH)
    z = _dot(proj(_OFF_LR, 2 * C_LOWRANK).astype(BF16), w2_ref[...]) + b2_ref[...]
    la_ref[...] = (jnp.minimum(z, 0.0) - jnp.log(1.0 + jnp.exp(-jnp.abs(z)))) * (1.0 / C_TAU)


def _rope_tables(seq):
    half = ROPE_DIM // 2
    inv = jnp.power(jnp.float32(ROPE_THETA), -jnp.arange(half, dtype=F32) * (2.0 / ROPE_DIM))
    ang = jnp.arange(seq, dtype=F32)[:, None] * inv[None, :]
    cos, sin = jnp.cos(ang), jnp.sin(ang)
    ones = jnp.ones((seq, HEAD_DIM - ROPE_DIM), F32)
    zeros = jnp.zeros((seq, HEAD_DIM - half), F32)
    c = jnp.concatenate([cos, cos, ones], axis=1)
    s1 = jnp.concatenate([-sin, zeros], axis=1)
    s2 = jnp.concatenate([jnp.zeros((seq, half), F32), sin, jnp.zeros((seq, HEAD_DIM - ROPE_DIM), F32)], axis=1)
    two = lambda t: jnp.concatenate([t, t], axis=1)
    return two(c), two(s1), two(s2)


def _in_proj(x, g1, w_in_bf, layer, w2, b2, seq, rope):
    n = x.shape[0]
    tm = min(IN_PROJ_ROWS, seq)
    per_seq = seq // tm
    cos, s1, s2 = rope
    row = lambda width: pl.BlockSpec((tm, width), lambda i: (i, 0))
    full = lambda a: pl.BlockSpec(a.shape, lambda i: (0,) * a.ndim)
    tab = pl.BlockSpec((tm, LANES), lambda i: (i % per_seq, 0))
    widths = [(A_WIDTH, BF16), (A_KV_WIDTH, BF16), (A_KV_WIDTH, BF16), (B_WIDTH, BF16), (B_WIDTH, BF16),
              (B_WIDTH, BF16), (C_WIDTH, F32), (C_WIDTH, F32), (C_WIDTH, BF16), (C_WIDTH, F32), (2 * C_WIDTH, F32)]
    return pl.pallas_call(
        _in_proj_kernel,
        grid=(n // tm,),
        in_specs=[row(D_MODEL), full(g1), pl.BlockSpec((None, D_MODEL, IN_WIDTH), lambda i: (layer, 0, 0)),
                  tab, tab, tab, full(w2), full(b2)],
        out_specs=[row(w) for w, _ in widths],
        out_shape=[jax.ShapeDtypeStruct((n, w), dt) for w, dt in widths],
        compiler_params=_cparams(("parallel",)),
        name="in_proj",
    )(x, g1, w_in_bf, cos, s1, s2, w2, b2)


A_QBLOCKS = 4


def _attn_a_kernel(sink_ref, q_ref, kp_ref, kc_ref, kn_ref, vp_ref, vc_ref, vn_ref, g_ref, o_ref, acc_ref, *, steps):
    i = pl.program_id(1)
    grp = A_HEADS // A_KV_HEADS
    k = jnp.concatenate([kp_ref[...], kc_ref[...], kn_ref[...]], axis=0)
    v = jnp.concatenate([vp_ref[...], vc_ref[...], vn_ref[...]], axis=0)
    qq = lax.broadcasted_iota(jnp.int32, (WINDOW, 3 * WINDOW), 0)
    kk = lax.broadcasted_iota(jnp.int32, (WINDOW, 3 * WINDOW), 1)
    band = jnp.abs(kk - WINDOW - qq) <= WINDOW
    for blk in range(A_QBLOCKS):
        visible = band
        if blk == 0:
            visible = visible & jnp.logical_not((kk < WINDOW) & (i == 0))
        if blk == A_QBLOCKS - 1:
            visible = visible & jnp.logical_not((kk >= 2 * WINDOW) & (i == steps - 1))
        bias = jnp.where(visible, 0.0, NEG_INF)
        rows = slice(blk * WINDOW, (blk + 1) * WINDOW)
        keys = slice(blk * WINDOW, (blk + 3) * WINDOW)
        for kv in range(A_KV_HEADS):
            heads = range(kv * grp, (kv + 1) * grp)
            q = jnp.concatenate([q_ref[rows, h * HEAD_DIM:(h + 1) * HEAD_DIM] for h in heads], axis=0)
            s = _nt_dot(q, k[keys, kv * HEAD_DIM:(kv + 1) * HEAD_DIM])
            ps, dens = [], []
            for g_i, h in enumerate(heads):
                sh = s[g_i * WINDOW:(g_i + 1) * WINDOW, :] + bias
                sink = sink_ref[h]
                m = jnp.maximum(jnp.max(sh, axis=-1, keepdims=True), sink)
                p = jnp.exp(sh - m)
                dens.append(jnp.sum(p, axis=-1, keepdims=True) + jnp.exp(sink - m))
                ps.append(p.astype(BF16))
            o = _dot(jnp.concatenate(ps, axis=0), v[keys, kv * HEAD_DIM:(kv + 1) * HEAD_DIM])
            for g_i, h in enumerate(heads):
                acc_ref[rows, h * HEAD_DIM:(h + 1) * HEAD_DIM] = o[g_i * WINDOW:(g_i + 1) * WINDOW, :] / dens[g_i]
    o_ref[...] = _rms(acc_ref[...], g_ref[...]).astype(BF16)


def _attn_a(aq, ak, av, sink, g, bsz, seq):
    nb = seq // WINDOW
    assert nb % A_QBLOCKS == 0
    steps = nb // A_QBLOCKS
    qrows = A_QBLOCKS * WINDOW
    qspec = pl.BlockSpec((qrows, A_WIDTH), lambda b, i: (b * steps + i, 0))
    kprev = pl.BlockSpec((WINDOW, A_KV_WIDTH), lambda b, i: (b * nb + jnp.maximum(i * A_QBLOCKS - 1, 0), 0))
    kcur = pl.BlockSpec((qrows, A_KV_WIDTH), lambda b, i: (b * steps + i, 0))
    knext = pl.BlockSpec((WINDOW, A_KV_WIDTH), lambda b, i: (b * nb + jnp.minimum((i + 1) * A_QBLOCKS, nb - 1), 0))
    return pl.pallas_call(
        functools.partial(_attn_a_kernel, steps=steps),
        grid=(bsz, steps),
        in_specs=[pl.BlockSpec(memory_space=pltpu.SMEM), qspec, kprev, kcur, knext, kprev, kcur, knext,
                  pl.BlockSpec((1, A_WIDTH), lambda b, i: (0, 0))],
        out_specs=qspec,
        out_shape=jax.ShapeDtypeStruct((bsz * seq, A_WIDTH), BF16),
        scratch_shapes=[pltpu.VMEM((qrows, A_WIDTH), F32)],
        compiler_params=_cparams(("parallel", "parallel")),
        name="attn_a",
    )(sink, aq, ak, ak, ak, av, av, av, g)


B_GROUP = 16
B_KEYS = WIN_H * GRID_W


def _bias_table_kernel(rel_ref, o_ref):
    c = lax.broadcasted_iota(jnp.int32, (GRID_W, GRID_W), 0)
    w = lax.broadcasted_iota(jnp.int32, (GRID_W, GRID_W), 1)
    cstart = jnp.clip(c - WIN_W // 2, 0, GRID_W - WIN_W)
    colmask = (w >= cstart) & (w < cstart + WIN_W)
    col_off = jnp.clip(w - c + (WIN_W - 1), 0, 2 * WIN_W - 2)
    n_ro, n_co = 2 * WIN_H - 1, 2 * WIN_W - 1
    for h in range(B_HEADS):
        for ro in range(n_ro):
            def body(j, acc):
                return jnp.where(col_off == j, rel_ref[(h * n_ro + ro) * n_co + j], acc)
            t = lax.fori_loop(0, n_co, body, jnp.zeros((GRID_W, GRID_W), F32))
            t = jnp.where(colmask, t, NEG_INF)
            for p in range(WIN_H):
                kidx = ro - (WIN_H - 1) + p
                if 0 <= kidx < WIN_H:
                    o_ref[p, h, :, kidx * GRID_W:(kidx + 1) * GRID_W] = t


def _bias_table(rel_bias):
    return pl.pallas_call(
        _bias_table_kernel,
        in_specs=[pl.BlockSpec(memory_space=pltpu.SMEM)],
        out_shape=jax.ShapeDtypeStruct((WIN_H, B_HEADS, GRID_W, B_KEYS), F32),
        name="bias_table",
    )(rel_bias.reshape(-1)).reshape(WIN_H, B_HEADS * GRID_W, B_KEYS)


def _attn_b_kernel(q_ref, kp_ref, kc_ref, kn_ref, vp_ref, vc_ref, vn_ref, bias_ref, g_ref, o_ref,
                   kbuf, vbuf, acc_ref, *, rows):
    gidx = pl.program_id(1)
    blk = B_GROUP * GRID_W
    kbuf[0:blk, :] = kp_ref[...]
    kbuf[blk:2 * blk, :] = kc_ref[...]
    kbuf[2 * blk:3 * blk, :] = kn_ref[...]
    vbuf[0:blk, :] = vp_ref[...]
    vbuf[blk:2 * blk, :] = vc_ref[...]
    vbuf[2 * blk:3 * blk, :] = vn_ref[...]
    same_head = _same_head_mask(B_HEADS * GRID_W, GRID_W)
    for j in range(B_GROUP):
        r = gidx * B_GROUP + j
        start = jnp.clip(r - WIN_H // 2, 0, rows - WIN_H)
        pat = r - start
        loc = pl.multiple_of((start - gidx * B_GROUP + B_GROUP) * GRID_W, GRID_W)
        kw = kbuf[pl.ds(loc, B_KEYS), :]
        vw = vbuf[pl.ds(loc, B_KEYS), :]
        qj = q_ref[j * GRID_W:(j + 1) * GRID_W, :]
        q = jnp.where(same_head, jnp.concatenate([qj] * B_HEADS, axis=0), jnp.zeros((), BF16))
        s = _nt_dot(q, kw) + bias_ref[pat]
        m = jnp.max(s, axis=-1, keepdims=True)
        p = jnp.exp(s - m)
        den = jnp.sum(p, axis=-1, keepdims=True)
        o = jnp.where(same_head, _dot(p.astype(BF16), vw) / den, 0.0)
        acc_ref[j * GRID_W:(j + 1) * GRID_W, :] = sum(o[h * GRID_W:(h + 1) * GRID_W, :] for h in range(B_HEADS))
    o_ref[...] = _rms(acc_ref[...], g_ref[...]).astype(BF16)


def _attn_b(bq, bk, bv, bias, g, bsz, seq):
    rows = seq // GRID_W
    assert rows % B_GROUP == 0 and rows >= WIN_H
    ng = rows // B_GROUP
    blk = B_GROUP * GRID_W
    cur = pl.BlockSpec((blk, B_WIDTH), lambda b, i: (b * ng + i, 0))
    prev = pl.BlockSpec((blk, B_WIDTH), lambda b, i: (b * ng + jnp.maximum(i - 1, 0), 0))
    nxt = pl.BlockSpec((blk, B_WIDTH), lambda b, i: (b * ng + jnp.minimum(i + 1, ng - 1), 0))
    return pl.pallas_call(
        functools.partial(_attn_b_kernel, rows=rows),
        grid=(bsz, ng),
        in_specs=[cur, prev, cur, nxt, prev, cur, nxt,
                  pl.BlockSpec(bias.shape, lambda b, i: (0, 0, 0)),
                  pl.BlockSpec((1, B_WIDTH), lambda b, i: (0, 0))],
        out_specs=cur,
        out_shape=jax.ShapeDtypeStruct((bsz * seq, B_WIDTH), BF16),
        scratch_shapes=[pltpu.VMEM((3 * blk, B_WIDTH), BF16), pltpu.VMEM((3 * blk, B_WIDTH), BF16),
                        pltpu.VMEM((blk, B_WIDTH), F32)],
        compiler_params=_cparams(("parallel", "parallel")),
        name="attn_b",
    )(bq, bk, bk, bk, bv, bv, bv, bias, g)


def _gla_decay(la_ref, reverse):
    seg = min(GLA_CUM_ROWS, la_ref.shape[0])
    bi = lax.broadcasted_iota(jnp.int32, (seg, seg), 0)
    bj = lax.broadcasted_iota(jnp.int32, (seg, seg), 1)
    in_chunk = (bi // C_CHUNK) == (bj // C_CHUNK)
    cum_w = (in_chunk & ((bj >= bi) if reverse else (bj <= bi))).astype(BF16)
    parts = []
    for r in range(0, la_ref.shape[0], seg):
        la_hi, la_lo = _split_bf16(la_ref[r:r + seg, :])
        parts.append(_dot(cum_w, la_hi) + _dot(cum_w, la_lo))
    return parts[0] if len(parts) == 1 else jnp.concatenate(parts, axis=0)


def _gla_chunk(q_ref, k_ref, v_ref, o_ref, b_all, st, c, reverse):
    same_head = _same_head_mask(C_HEADS * C_CHUNK, C_CHUNK)
    ti = lax.broadcasted_iota(jnp.int32, (C_CHUNK, C_HEADS * C_CHUNK), 0)
    si = lax.broadcasted_iota(jnp.int32, (C_CHUNK, C_HEADS * C_CHUNK), 1) % C_CHUNK
    tri = (si >= ti) if reverse else (si <= ti)
    zero = jnp.zeros((), BF16)
    rows = slice(c * C_CHUNK, (c + 1) * C_CHUNK)
    b = b_all[rows, :]
    b_tot = b[0:1, :] if reverse else b[C_CHUNK - 1:C_CHUNK, :]
    q_i = (q_ref[rows, :] * jnp.exp(b)).astype(BF16)
    kf = k_ref[rows, :]
    k_i = (kf * jnp.exp(-b)).astype(BF16)
    k_e = (kf * jnp.exp(b_tot - b)).astype(BF16)
    v = v_ref[rows, :]
    k_bd = jnp.where(same_head, jnp.concatenate([k_i] * C_HEADS, axis=0), zero)
    v_bd = jnp.where(same_head, jnp.concatenate([v] * C_HEADS, axis=0), zero)
    a = jnp.where(tri, _nt_dot(q_i, k_bd), 0.0)
    o_ref[rows, :] = _dot(a.astype(BF16), v_bd) + _nt_dot(q_i, st.astype(BF16))
    return st * jnp.exp(b_tot) + jnp.where(same_head, _tn_dot(v, k_e), 0.0)


def _gla_kernel(qf_ref, kf_ref, vf_ref, laf_ref, qb_ref, kb_ref, vb_ref, lab_ref, of_ref, ob_ref, sf_ref, sb_ref):
    @pl.when(pl.program_id(1) == 0)
    def _():
        sf_ref[...] = jnp.zeros_like(sf_ref)
        sb_ref[...] = jnp.zeros_like(sb_ref)

    n_chunks = qf_ref.shape[0] // C_CHUNK
    b_f = _gla_decay(laf_ref, reverse=False)
    b_b = _gla_decay(lab_ref, reverse=True)
    s_f, s_b = sf_ref[...], sb_ref[...]
    for c in range(n_chunks):
        s_f = _gla_chunk(qf_ref, kf_ref, vf_ref, of_ref, b_f, s_f, c, reverse=False)
        s_b = _gla_chunk(qb_ref, kb_ref, vb_ref, ob_ref, b_b, s_b, n_chunks - 1 - c, reverse=True)
    sf_ref[...] = s_f
    sb_ref[...] = s_b


def _gla(cq, ck, cv, la, bsz, seq):
    tb = min(GLA_BLOCK, seq)
    nb = seq // tb
    fwd = lambda col: pl.BlockSpec((tb, C_WIDTH), lambda b, i: (b * nb + i, col))
    bwd = lambda col: pl.BlockSpec((tb, C_WIDTH), lambda b, i: (b * nb + nb - 1 - i, col))
    out = jax.ShapeDtypeStruct((bsz * seq, C_WIDTH), F32)
    state = pltpu.VMEM((C_WIDTH, C_WIDTH), F32)
    return pl.pallas_call(
        _gla_kernel,
        grid=(bsz, nb),
        in_specs=[fwd(0), fwd(0), fwd(0), fwd(0), bwd(0), bwd(0), bwd(0), bwd(1)],
        out_specs=[fwd(0), bwd(0)],
        out_shape=[out, out],
        scratch_shapes=[state, state],
        compiler_params=_cparams(("parallel", "arbitrary")),
        name="gla",
    )(cq, ck, cv, la, cq, ck, cv, la)


def _out_proj_kernel(x_ref, oa_ref, ob_ref, of_ref, obw_ref, cg_ref, cng_ref, wo_ref, g2_ref, wr_ref,
                     x1_ref, h2_ref, aff_ref, mix_ref):
    gi = lax.broadcasted_iota(jnp.int32, (C_WIDTH, C_WIDTH), 0) // HEAD_DIM
    gj = lax.broadcasted_iota(jnp.int32, (C_WIDTH, C_WIDTH), 1) // HEAD_DIM
    ones_bd = (gi == gj).astype(BF16)
    half = x_ref.shape[0] // 2
    for rows in (slice(0, half), slice(half, 2 * half)):
        o = of_ref[rows, :] + obw_ref[rows, :]
        sq_hi, sq_lo = _split_bf16(o * o)
        ms = (_dot(sq_hi, ones_bd) + _dot(sq_lo, ones_bd)) * (1.0 / HEAD_DIM)
        cg = cg_ref[rows, :]
        oc = (o * lax.rsqrt(ms + EPS) * cng_ref[...]) * (cg / (1.0 + jnp.exp(-cg)))
        mix_ref[rows, 0:A_WIDTH] = oa_ref[rows, :]
        mix_ref[rows, A_WIDTH:A_WIDTH + B_WIDTH] = ob_ref[rows, :]
        mix_ref[rows, A_WIDTH + B_WIDTH:] = oc.astype(BF16)
        x1 = x_ref[rows, :] + _dot(mix_ref[rows, :], wo_ref[...])
        x1_ref[rows, :] = x1
        h2 = _rms(x1, g2_ref[...]).astype(BF16)
        h2_ref[rows, :] = h2
        logits = _dot(h2, wr_ref[...])
        lane = lax.broadcasted_iota(jnp.int32, logits.shape, 1)
        logits = jnp.where(lane < N_EXPERTS, logits, NEG_INF)
        p = jnp.exp(logits - jnp.max(logits, axis=1, keepdims=True))
        aff = p / jnp.sum(p, axis=1, keepdims=True)
        aff_ref[:, rows] = aff.T[0:N_EXPERTS, :]


def _out_proj(x, oa, ob, o_f, o_b, cg, cng, wo_bf, layer, g2, wr_cat):
    n = x.shape[0]
    tm = min(OUT_PROJ_ROWS, n)
    row = lambda width: pl.BlockSpec((tm, width), lambda i: (i, 0))
    full = lambda a: pl.BlockSpec(a.shape, lambda i: (0,) * a.ndim)
    return pl.pallas_call(
        _out_proj_kernel,
        grid=(n // tm,),
        in_specs=[row(D_MODEL), row(A_WIDTH), row(B_WIDTH), row(C_WIDTH), row(C_WIDTH), row(C_WIDTH),
                  full(cng), pl.BlockSpec((None, D_MODEL, D_MODEL), lambda i: (layer, 0, 0)), full(g2), full(wr_cat)],
        out_specs=[row(D_MODEL), row(D_MODEL), pl.BlockSpec((N_EXPERTS, tm), lambda i: (0, i))],
        out_shape=[jax.ShapeDtypeStruct((n, D_MODEL), F32), jax.ShapeDtypeStruct((n, D_MODEL), BF16),
                   jax.ShapeDtypeStruct((N_EXPERTS, n), F32)],
        scratch_shapes=[pltpu.VMEM((tm, D_MODEL), BF16)],
        compiler_params=_cparams(("parallel",)),
        name="out_proj",
    )(x, oa, ob, o_f, o_b, cg, cng, wo_bf, g2, wr_cat)


def _route_kernel(aff_ref, pos_ref, offs_ref, *, cap):
    n = aff_ref.shape[1]
    nt = n // ROUTE_TILE
    bits = pltpu.bitcast(aff_ref[...], jnp.int32)

    def search(it, thr):
        cand = thr | jnp.left_shift(jnp.int32(1), 30 - it)
        cnt = jnp.sum(jnp.where(bits >= cand, 1.0, 0.0), axis=1, keepdims=True)
        return jnp.where(cnt >= cap, cand, thr)

    thr = lax.fori_loop(0, 31, search, jnp.zeros((N_EXPERTS, 1), jnp.int32))
    n_gt = jnp.sum(jnp.where(bits > thr, 1.0, 0.0), axis=1, keepdims=True)
    n_tie = cap - n_gt

    ui = lax.broadcasted_iota(jnp.int32, (ROUTE_TILE, ROUTE_TILE), 0)
    uj = lax.broadcasted_iota(jnp.int32, (ROUTE_TILE, ROUTE_TILE), 1)
    upper = (ui <= uj).astype(BF16)
    lane = lax.broadcasted_iota(jnp.int32, offs_ref.shape, 1)

    def tile(i, carry):
        c_gt, c_eq, offs = carry
        col = pl.multiple_of(i * ROUTE_TILE, ROUTE_TILE)
        b = pltpu.bitcast(aff_ref[:, pl.ds(col, ROUTE_TILE)], jnp.int32)
        gt = jnp.where(b > thr, 1.0, 0.0)
        eq = jnp.where(b == thr, 1.0, 0.0)
        inc = _dot(jnp.concatenate([gt, eq], axis=0).astype(BF16), upper)
        gt_before = c_gt + inc[:N_EXPERTS] - gt
        eq_before = c_eq + inc[N_EXPERTS:] - eq
        sel = (gt > 0.0) | ((eq > 0.0) & (eq_before < n_tie))
        pos = gt_before + jnp.minimum(eq_before, n_tie)
        pos_ref[:, pl.ds(col, ROUTE_TILE)] = jnp.where(sel, pos, -1.0).astype(jnp.int32)
        start = c_gt + jnp.minimum(c_eq, n_tie)
        offs = jnp.where(lane == i, start.astype(jnp.int32), offs)
        return (c_gt + jnp.sum(gt, axis=1, keepdims=True), c_eq + jnp.sum(eq, axis=1, keepdims=True), offs)

    zero = jnp.zeros((N_EXPERTS, 1), F32)
    offs = jnp.where(lane == nt, cap, 0).astype(jnp.int32)
    _, _, offs = lax.fori_loop(0, nt, tile, (zero, zero, offs), unroll=4 if nt % 4 == 0 else 1)
    offs_ref[...] = offs


def _route(aff_t, cap):
    n = aff_t.shape[1]
    nt = n // ROUTE_TILE
    return pl.pallas_call(
        functools.partial(_route_kernel, cap=cap),
        out_shape=[jax.ShapeDtypeStruct((N_EXPERTS, n), jnp.int32),
                   jax.ShapeDtypeStruct((N_EXPERTS, nt + 1), jnp.int32)],
        compiler_params=pltpu.CompilerParams(vmem_limit_bytes=VMEM_LIMIT),
        name="route",
    )(aff_t)


def _align(v):
    return (v // MOE_ALIGN) * MOE_ALIGN


def _moe_passes(offs_ref, t, subs, x0, s, win):
    k = jnp.int32(1)
    for g in range(MOE_PAIR):
        need = offs_ref[x0 + g, t * subs + s + 1] - _align(offs_ref[x0 + g, t * subs + s])
        k = jnp.maximum(k, (need + win - 1) // win)
    return k


def _moe_windows(offs_ref, pos_ref, t, subs, x0, s, j, win):
    slot = lax.broadcasted_iota(jnp.int32, (win, ROUTE_TILE), 0)
    hits, rels = [], []
    for g in range(MOE_PAIR):
        x = x0 + g
        ws = _align(offs_ref[x, t * subs + s]) + j * win
        pos = pos_ref[pl.ds(x, 1), s * ROUTE_TILE:(s + 1) * ROUTE_TILE]
        hits.append((pos - ws) == slot)
        rels.append(pl.multiple_of(ws - _align(offs_ref[x, t * subs]), MOE_ALIGN))
    onehot = jnp.concatenate([jnp.where(h, 1.0, 0.0) for h in hits], axis=0).astype(BF16)
    return hits, rels, onehot


def _moe_chunks(offs_ref, t, subs, x):
    base = _align(offs_ref[x, t * subs])
    return base, (offs_ref[x, (t + 1) * subs] - base + MOE_CHUNK - 1) // MOE_CHUNK


def _moe_gather_kernel(offs_ref, h_ref, pos_ref, gate_ref, xe_ref, stage_ref, carry_ref, sem):
    t = pl.program_id(0)
    steps = pl.num_programs(1)
    e0 = pl.program_id(1) * MOE_STEP
    tt = h_ref.shape[0]
    subs = tt // ROUTE_TILE
    k = t * steps + pl.program_id(1)
    slot = k % 2
    x_ref = stage_ref.at[slot]

    def off(x, s):
        return offs_ref[x, t * subs + s]

    def copies(kk, half, act):
        step_t, step_q = kk // steps, kk % steps
        for g in range(MOE_STEP):
            x = step_q * MOE_STEP + g
            base, n_chunks = _moe_chunks(offs_ref, step_t, subs, x)

            def body(c, carry):
                r0 = pl.multiple_of(c * MOE_CHUNK, MOE_CHUNK)
                act(pltpu.make_async_copy(stage_ref.at[half, g, pl.ds(r0, MOE_CHUNK), :],
                                          xe_ref.at[x, pl.ds(pl.multiple_of(base + r0, MOE_ALIGN), MOE_CHUNK), :],
                                          sem.at[half]))
                return carry

            lax.fori_loop(0, n_chunks, body, 0)

    @pl.when(k >= 2)
    def _():
        copies(k - 2, slot, lambda cp: cp.wait())

    @pl.when(t == 0)
    def _():
        carry_ref[pl.ds(e0, MOE_STEP)] = jnp.zeros((MOE_STEP, MOE_ALIGN, XE_WIDTH), BF16)
        cap = xe_ref.shape[1] - MOE_CHUNK
        x_ref[0, 0:MOE_CHUNK, :] = jnp.zeros((MOE_CHUNK, XE_WIDTH), BF16)
        pads = [pltpu.make_async_copy(x_ref.at[0, 0:MOE_CHUNK, :], xe_ref.at[e0 + g, cap:cap + MOE_CHUNK, :],
                                      sem.at[slot]) for g in range(MOE_STEP)]
        for p in pads:
            p.start()
        for p in pads:
            p.wait()

    for g in range(MOE_STEP):
        x_ref[g, 0:MOE_ALIGN, :] = carry_ref[e0 + g]

    row = lax.broadcasted_iota(jnp.int32, (MOE_WIN, 1), 0)
    lane = lax.broadcasted_iota(jnp.int32, (MOE_WIN, XE_WIDTH - D_MODEL), 1)

    def gather_pass(pair, s, j):
        x0 = e0 + pair * MOE_PAIR
        cols = slice(s * ROUTE_TILE, (s + 1) * ROUTE_TILE)
        hits, rels, onehot = _moe_windows(offs_ref, pos_ref, t, subs, x0, s, j, MOE_WIN)
        r = _dot(onehot, h_ref[cols, :])
        for g in range(MOE_PAIR):
            gate = jnp.sum(jnp.where(hits[g], gate_ref[pl.ds(x0 + g, 1), cols], 0.0), axis=1, keepdims=True)
            g_hi = gate.astype(BF16).astype(F32)
            extra = jnp.where(lane == 0, g_hi, jnp.where(lane == 1, gate - g_hi, 0.0))
            new = jnp.concatenate([r[g * MOE_WIN:(g + 1) * MOE_WIN, :], extra], axis=1)
            ws = _align(off(x0 + g, s)) + j * MOE_WIN
            own = row >= off(x0 + g, s) - ws
            buf = pair * MOE_PAIR + g
            if isinstance(j, int):
                head = pl.ds(rels[g], MOE_ALIGN)
                x_ref[buf, head, :] = jnp.where(own[:MOE_ALIGN], new[:MOE_ALIGN],
                                                x_ref[buf, head, :].astype(F32)).astype(BF16)
                x_ref[buf, pl.ds(rels[g] + MOE_ALIGN, MOE_WIN - MOE_ALIGN), :] = new[MOE_ALIGN:].astype(BF16)
            else:
                own = own & (row < off(x0 + g, s + 1) - ws)
                win = pl.ds(rels[g], MOE_WIN)
                x_ref[buf, win, :] = jnp.where(own, new, x_ref[buf, win, :].astype(F32)).astype(BF16)

    for pair in range(MOE_STEP // MOE_PAIR):
        for s in range(subs):
            gather_pass(pair, s, 0)
    for pair in range(MOE_STEP // MOE_PAIR):
        for s in range(subs):
            lax.fori_loop(1, _moe_passes(offs_ref, t, subs, e0 + pair * MOE_PAIR, s, MOE_WIN),
                          lambda j, carry: (gather_pass(pair, s, j), carry)[1], 0)

    for g in range(MOE_STEP):
        x = e0 + g
        base, n_chunks = _moe_chunks(offs_ref, t, subs, x)
        last_group = pl.multiple_of(_align(off(x, subs)) - base, MOE_ALIGN)
        tail = x_ref[g, pl.ds(last_group, MOE_ALIGN), :].astype(F32)
        tail = jnp.where(row[:MOE_ALIGN] < off(x, subs) - base - last_group, tail, 0.0).astype(BF16)
        x_ref[g, pl.ds(last_group, MOE_ALIGN), :] = tail
        carry_ref[x] = tail

        def fill(i, carry):
            x_ref[g, pl.ds(pl.multiple_of(i * MOE_ALIGN, MOE_ALIGN), MOE_ALIGN), :] = jnp.zeros((MOE_ALIGN, XE_WIDTH), BF16)
            return carry

        lax.fori_loop(last_group // MOE_ALIGN + 1, n_chunks * MOE_CHUNK // MOE_ALIGN, fill, 0)

    copies(k, slot, lambda cp: cp.start())

    @pl.when(k == pl.num_programs(0) * steps - 1)
    def _():
        copies(k - 1, 1 - slot, lambda cp: cp.wait())
        copies(k, slot, lambda cp: cp.wait())


def _moe_ffn_kernel(x_ref, wg_ref, wu_ref, wd_ref, y_ref, wg_bf, wu_bf, wd_bf):
    j = pl.program_id(1)

    @pl.when(j == 0)
    def _():
        wg_bf[...] = wg_ref[...].astype(BF16)
        wu_bf[...] = wu_ref[...].astype(BF16)
        wd_bf[...] = wd_ref[...].astype(BF16)
        y_ref[0] = jnp.zeros(y_ref.shape[1:], BF16)

    @pl.when(j > 0)
    def _():
        x = x_ref[0, :, 0:D_MODEL]
        gate = jnp.sum(x_ref[0, :, D_MODEL:XE_WIDTH].astype(F32), axis=1, keepdims=True)
        hg = _dot(x, wg_bf[...])
        hu = _dot(x, wu_bf[...])
        act = (hg / (1.0 + jnp.exp(-hg))) * hu * gate
        y_ref[0] = _dot(act.astype(BF16), wd_bf[...]).astype(BF16)


def _moe_combine_kernel(offs_ref, x1_ref, pos_ref, y_ref, fg_ref, o_ref, y_buf, sem, *, final_norm):
    t = pl.program_id(0)
    q = pl.program_id(1)
    steps = pl.num_programs(1)
    e0 = q * MOE_STEP
    tt = x1_ref.shape[0]
    subs = tt // ROUTE_TILE
    k = t * steps + q
    slot = k % 2

    def fetch(kk, half, act):
        step_t, step_q = kk // steps, kk % steps
        for g in range(MOE_STEP):
            x = step_q * MOE_STEP + g
            base, n_chunks = _moe_chunks(offs_ref, step_t, subs, x)

            def body(c, carry):
                r0 = pl.multiple_of(c * MOE_CHUNK, MOE_CHUNK)
                act(pltpu.make_async_copy(y_ref.at[x, pl.ds(pl.multiple_of(base + r0, MOE_ALIGN), MOE_CHUNK), :],
                                          y_buf.at[half, g, pl.ds(r0, MOE_CHUNK), :], sem.at[half]))
                return carry

            lax.fori_loop(0, n_chunks, body, 0)

    @pl.when(k == 0)
    def _():
        y_buf[...] = jnp.zeros_like(y_buf)
        fetch(k, slot, lambda cp: cp.start())

    @pl.when(k + 1 < pl.num_programs(0) * steps)
    def _():
        fetch(k + 1, 1 - slot, lambda cp: cp.start())

    fetch(k, slot, lambda cp: cp.wait())

    @pl.when(q == 0)
    def _():
        o_ref[...] = x1_ref[...]

    def combine_pass(pair, s, j):
        cols = slice(s * ROUTE_TILE, (s + 1) * ROUTE_TILE)
        _, rels, onehot = _moe_windows(offs_ref, pos_ref, t, subs, e0 + pair * MOE_PAIR, s, j, MOE_COMBINE_WIN)
        ycat = jnp.concatenate([y_buf[slot, pair * MOE_PAIR + g, pl.ds(rels[g], MOE_COMBINE_WIN), :]
                                for g in range(MOE_PAIR)], axis=0)
        o_ref[cols, :] += _tn_dot(onehot, ycat)

    for pair in range(MOE_STEP // MOE_PAIR):
        for s in range(subs):
            combine_pass(pair, s, 0)
    for pair in range(MOE_STEP // MOE_PAIR):
        for s in range(subs):
            lax.fori_loop(1, _moe_passes(offs_ref, t, subs, e0 + pair * MOE_PAIR, s, MOE_COMBINE_WIN),
                          lambda j, carry: (combine_pass(pair, s, j), carry)[1], 0)

    if final_norm:
        @pl.when(q == pl.num_programs(1) - 1)
        def _():
            o_ref[...] = _rms(o_ref[...], fg_ref[...])


def _moe(offs, h2, x1, pos, gate, wg, wu, wd, layer, final_g, final_norm, cap):
    n = h2.shape[0]
    tt, tc = min(MOE_TILE, n), min(MOE_COMBINE_TILE, n)
    assert cap % MOE_CHUNK == 0 and n % tt == 0 and n % tc == 0
    steps = N_EXPERTS // MOE_STEP
    stage_rows = lambda tile, win: tile + MOE_ALIGN + win * pl.cdiv(ROUTE_TILE + MOE_ALIGN, win)
    tok = lambda tile: pl.BlockSpec((tile, D_MODEL), lambda t, q, offs: (t, 0))
    per_tok = lambda tile: pl.BlockSpec((N_EXPERTS, tile), lambda t, q, offs: (0, t))
    hbm = pl.BlockSpec(memory_space=pl.ANY)
    sems = ("arbitrary", "arbitrary")

    xe = pl.pallas_call(
        _moe_gather_kernel,
        grid_spec=pltpu.PrefetchScalarGridSpec(
            num_scalar_prefetch=1, grid=(n // tt, steps),
            in_specs=[tok(tt), per_tok(tt), per_tok(tt)], out_specs=hbm,
            scratch_shapes=[pltpu.VMEM((2, MOE_STEP, stage_rows(tt, MOE_WIN), XE_WIDTH), BF16),
                            pltpu.VMEM((N_EXPERTS, MOE_ALIGN, XE_WIDTH), BF16), pltpu.SemaphoreType.DMA((2,))],
        ),
        out_shape=jax.ShapeDtypeStruct((N_EXPERTS, cap + MOE_CHUNK, XE_WIDTH), BF16),
        compiler_params=_cparams(sems),
        name="moe_gather",
    )(offs, h2, pos, gate)

    fb = min(MOE_FFN_ROWS, cap)
    nb = cap // fb
    wspec = pl.BlockSpec((None, None, D_MODEL, D_MODEL), lambda e, j: (layer, e, 0, 0))
    w_bf = pltpu.VMEM((D_MODEL, D_MODEL), BF16)
    y = pl.pallas_call(
        _moe_ffn_kernel,
        grid=(N_EXPERTS, nb + 1),
        in_specs=[pl.BlockSpec((1, fb, XE_WIDTH), lambda e, j: (e, jnp.maximum(j - 1, 0), 0)), wspec, wspec, wspec],
        out_specs=pl.BlockSpec((1, fb, D_MODEL), lambda e, j: (e, jnp.where(j == 0, nb, j - 1), 0)),
        out_shape=jax.ShapeDtypeStruct((N_EXPERTS, cap + fb, D_MODEL), BF16),
        scratch_shapes=[w_bf, w_bf, w_bf],
        compiler_params=_cparams(("parallel", "arbitrary")),
        name="moe_ffn",
    )(xe, wg, wu, wd)

    return pl.pallas_call(
        functools.partial(_moe_combine_kernel, final_norm=final_norm),
        grid_spec=pltpu.PrefetchScalarGridSpec(
            num_scalar_prefetch=1, grid=(n // tc, steps),
            in_specs=[tok(tc), per_tok(tc), hbm, pl.BlockSpec((1, D_MODEL), lambda t, q, offs: (0, 0))],
            out_specs=tok(tc),
            scratch_shapes=[pltpu.VMEM((2, MOE_STEP, stage_rows(tc, MOE_COMBINE_WIN), D_MODEL), BF16),
                            pltpu.SemaphoreType.DMA((2,))],
        ),
        out_shape=jax.ShapeDtypeStruct((n, D_MODEL), F32),
        compiler_params=_cparams(sems),
        name="moe_combine",
    )(offs, x1, pos, y, final_g)


def _prep_layer(l, norm1_g, w_in, a_sink, a_norm_g, b_rel_bias, b_norm_g, c_alpha_w2_f, c_alpha_b_f,
                c_alpha_w2_b, c_alpha_b_b, c_norm_g, w_out, norm2_g, w_router, w_gate, w_up, w_down):
    z = jnp.zeros((C_LOWRANK, C_WIDTH), F32)
    w2 = jnp.concatenate([jnp.concatenate([c_alpha_w2_f[l], z], axis=1),
                          jnp.concatenate([z, c_alpha_w2_b[l]], axis=1)], axis=0).astype(BF16)
    b2 = jnp.concatenate([c_alpha_b_f[l], c_alpha_b_b[l]])[None, :]
    wr_cat = jnp.pad(w_router[l].astype(BF16), ((0, 0), (0, LANES - N_EXPERTS)))
    return dict(
        layer=l, g1=norm1_g[l][None, :], w_in=w_in.astype(BF16), w2=w2, b2=b2,
        sink=a_sink[l], a_g=a_norm_g[l][None, :], bias=_bias_table(b_rel_bias[l]), b_g=b_norm_g[l][None, :],
        c_g=c_norm_g[l][None, :], w_out=w_out.astype(BF16), g2=norm2_g[l][None, :],
        wr_cat=wr_cat, wg=w_gate, wu=w_up, wd=w_down)


def _layer(x, p, bsz, seq, rope, final_g, final_norm):
    n = bsz * seq
    l = p["layer"]
    aq, ak, av, bq, bk, bv, cq, ck, cv, cg, la = _in_proj(x, p["g1"], p["w_in"], l, p["w2"], p["b2"], seq, rope)
    oa = _attn_a(aq, ak, av, p["sink"], p["a_g"], bsz, seq)
    ob = _attn_b(bq, bk, bv, p["bias"], p["b_g"], bsz, seq)
    o_f, o_b = _gla(cq, ck, cv, la, bsz, seq)
    x1, h2, aff = _out_proj(x, oa, ob, o_f, o_b, cg, p["c_g"], p["w_out"], l, p["g2"], p["wr_cat"])
    cap = EC_CAPACITY * n // N_EXPERTS
    pos, offs = _route(aff, cap)
    return _moe(offs, h2, x1, pos, aff, p["wg"], p["wu"], p["wd"], l, final_g, final_norm, cap)


def _trunk(x, layers, rope, final_g):
    bsz, seq, _ = x.shape
    y = x.reshape(bsz * seq, D_MODEL)
    for l, p in enumerate(layers):
        y = _layer(y, p, bsz, seq, rope, final_g, final_norm=(l == len(layers) - 1))
    return y.reshape(bsz, seq, D_MODEL)


def kernel(x_prompt, x_sample, norm1_g, w_in, a_sink, a_norm_g, b_rel_bias, b_norm_g, c_alpha_w2_f, c_alpha_b_f,
           c_alpha_w2_b, c_alpha_b_b, c_norm_g, w_out, norm2_g, w_router, w_gate, w_up, w_down, final_g):
    depth = w_in.shape[0]
    layers = [_prep_layer(l, norm1_g, w_in, a_sink, a_norm_g, b_rel_bias, b_norm_g, c_alpha_w2_f, c_alpha_b_f,
                          c_alpha_w2_b, c_alpha_b_b, c_norm_g, w_out, norm2_g, w_router, w_gate, w_up, w_down)
              for l in range(depth)]
    fg = final_g[None, :]
    rope = _rope_tables(max(x_prompt.shape[1], x_sample.shape[1]))
    return _trunk(x_prompt, layers, rope, fg), _trunk(x_sample, layers, rope, fg)
```

```python
import functools

import jax
import jax.numpy as jnp
from jax import lax
from jax.experimental import pallas as pl
from jax.experimental.pallas import tpu as pltpu

F32 = jnp.float32
BF16 = jnp.bfloat16

D_MODEL = 1024
HEAD_DIM = 64
A_WIDTH = 512
A_HEADS = 8
A_KV_HEADS = 2
A_KV_WIDTH = 128
WINDOW = 128
ROPE_THETA = 500000.0
ROPE_DIM = 16
B_WIDTH = 256
B_HEADS = 4
GRID_W = 64
WIN_H = 8
WIN_W = 16
C_WIDTH = 256
C_HEADS = 4
C_LOWRANK = 16
C_TAU = 16.0
C_CHUNK = 64
N_EXPERTS = 16
EC_CAPACITY = 2
EPS = 1e-6
NEG_INF = -1e30
IN_WIDTH = 2592

_OFF_AQ, _OFF_AK, _OFF_AV = 0, 512, 640
_OFF_BQ, _OFF_BK, _OFF_BV = 768, 1024, 1280
_OFF_CQ, _OFF_CK, _OFF_CV, _OFF_CG = 1536, 1792, 2048, 2304
_OFF_LR = 2560

LANES = 128
VMEM_LIMIT = 56 * 1024 * 1024

IN_PROJ_ROWS = 1024
OUT_PROJ_ROWS = 1024
GLA_BLOCK = 1024
GLA_CUM_ROWS = 256
ROUTE_TILE = 256
MOE_TILE = 2048
MOE_COMBINE_TILE = 1024
MOE_STEP = 4
MOE_PAIR = 4
MOE_WIN = 64
MOE_COMBINE_WIN = 128
MOE_ALIGN = 16
MOE_CHUNK = 128
MOE_FFN_ROWS = 1024
XE_WIDTH = D_MODEL + LANES


def _cparams(sem):
    return pltpu.CompilerParams(dimension_semantics=sem, vmem_limit_bytes=VMEM_LIMIT)


def _nt_dot(a, b):
    return lax.dot_general(a, b, (((1,), (1,)), ((), ())), preferred_element_type=F32)


def _tn_dot(a, b):
    return lax.dot_general(a, b, (((0,), (0,)), ((), ())), preferred_element_type=F32)


def _dot(a, b):
    return jnp.dot(a, b, preferred_element_type=F32)


def _rms(x, g):
    return x * lax.rsqrt(jnp.mean(x * x, axis=-1, keepdims=True) + EPS) * g


def _same_head_mask(n_rows, rows_per_head):
    width = (n_rows // rows_per_head) * HEAD_DIM
    ri = lax.broadcasted_iota(jnp.int32, (n_rows, width), 0) // rows_per_head
    li = lax.broadcasted_iota(jnp.int32, (n_rows, width), 1) // HEAD_DIM
    return ri == li


def _split_bf16(x):
    hi = x.astype(BF16)
    return hi, (x - hi.astype(F32)).astype(BF16)


def _in_proj_kernel(x_ref, g_ref, w_ref, cos_ref, s1_ref, s2_ref, w2_ref, b2_ref,
                    aq_ref, ak_ref, av_ref, bq_ref, bk_ref, bv_ref,
                    cq_ref, ck_ref, cv_ref, cg_ref, la_ref):
    h = _rms(x_ref[...], g_ref[...]).astype(BF16)
    y = _dot(h, w_ref[...])

    def proj(lo, width):
        return y[:, lo:lo + width]

    cos, s1, s2 = cos_ref[...], s1_ref[...], s2_ref[...]

    def rope(t):
        return t * cos + pltpu.roll(t, LANES - ROPE_DIM // 2, 1) * s1 + pltpu.roll(t, ROPE_DIM // 2, 1) * s2

    scale = HEAD_DIM ** -0.5
    aq = proj(_OFF_AQ, A_WIDTH)
    for c in range(A_WIDTH // LANES):
        aq_ref[:, c * LANES:(c + 1) * LANES] = (rope(aq[:, c * LANES:(c + 1) * LANES]) * scale).astype(BF16)
    ak_ref[...] = rope(proj(_OFF_AK, A_KV_WIDTH)).astype(BF16)
    av_ref[...] = proj(_OFF_AV, A_KV_WIDTH).astype(BF16)
    bq_ref[...] = (proj(_OFF_BQ, B_WIDTH) * scale).astype(BF16)
    bk_ref[...] = proj(_OFF_BK, B_WIDTH).astype(BF16)
    bv_ref[...] = proj(_OFF_BV, B_WIDTH).astype(BF16)
    cq_ref[...] = proj(_OFF_CQ, C_WIDTH) * scale
    ck_ref[...] = proj(_OFF_CK, C_WIDTH)
    cv_ref[...] = proj(_OFF_CV, C_WIDTH).astype(BF16)
    cg_ref[...] = proj(_OFF_CG, C_WIDTH)
    z = _dot(proj(_OFF_LR, 2 * C_LOWRANK).astype(BF16), w2_ref[...]) + b2_ref[...]
    la_ref[...] = (jnp.minimum(z, 0.0) - jnp.log(1.0 + jnp.exp(-jnp.abs(z)))) * (1.0 / C_TAU)


def _rope_tables(seq):
    half = ROPE_DIM // 2
    inv = jnp.power(jnp.float32(ROPE_THETA), -jnp.arange(half, dtype=F32) * (2.0 / ROPE_DIM))
    d = jnp.arange(LANES) % HEAD_DIM
    ang = jnp.arange(seq, dtype=F32)[:, None] * inv[d % half][None, :]
    cos, sin = jnp.cos(ang), jnp.sin(ang)
    first, second = (d < half)[None, :], ((d >= half) & (d < ROPE_DIM))[None, :]
    return (jnp.where(first | second, cos, 1.0), jnp.where(first, -sin, 0.0), jnp.where(second, sin, 0.0))


def _in_proj(x, g1, w_in_bf, layer, w2, b2, seq, rope):
    n = x.shape[0]
    tm = min(IN_PROJ_ROWS, seq)
    per_seq = seq // tm
    cos, s1, s2 = rope
    row = lambda width: pl.BlockSpec((tm, width), lambda i: (i, 0))
    full = lambda a: pl.BlockSpec(a.shape, lambda i: (0,) * a.ndim)
    tab = pl.BlockSpec((tm, LANES), lambda i: (i % per_seq, 0))
    widths = [(A_WIDTH, BF16), (A_KV_WIDTH, BF16), (A_KV_WIDTH, BF16), (B_WIDTH, BF16), (B_WIDTH, BF16),
              (B_WIDTH, BF16), (C_WIDTH, F32), (C_WIDTH, F32), (C_WIDTH, BF16), (C_WIDTH, F32), (2 * C_WIDTH, F32)]
    return pl.pallas_call(
        _in_proj_kernel,
        grid=(n // tm,),
        in_specs=[row(D_MODEL), full(g1), pl.BlockSpec((None, D_MODEL, IN_WIDTH), lambda i: (layer, 0, 0)),
                  tab, tab, tab, full(w2), full(b2)],
        out_specs=[row(w) for w, _ in widths],
        out_shape=[jax.ShapeDtypeStruct((n, w), dt) for w, dt in widths],
        compiler_params=_cparams(("parallel",)),
        name="in_proj",
    )(x, g1, w_in_bf, cos, s1, s2, w2, b2)


A_QBLOCKS = 4


def _attn_a_kernel(sink_ref, q_ref, kp_ref, kc_ref, kn_ref, vp_ref, vc_ref, vn_ref, g_ref, o_ref, acc_ref, *, steps):
    i = pl.program_id(1)
    grp = A_HEADS // A_KV_HEADS
    k = jnp.concatenate([kp_ref[...], kc_ref[...], kn_ref[...]], axis=0)
    v = jnp.concatenate([vp_ref[...], vc_ref[...], vn_ref[...]], axis=0)
    qq = lax.broadcasted_iota(jnp.int32, (WINDOW, 3 * WINDOW), 0)
    kk = lax.broadcasted_iota(jnp.int32, (WINDOW, 3 * WINDOW), 1)
    band = jnp.abs(kk - WINDOW - qq) <= WINDOW
    for blk in range(A_QBLOCKS):
        visible = band
        if blk == 0:
            visible = visible & jnp.logical_not((kk < WINDOW) & (i == 0))
        if blk == A_QBLOCKS - 1:
            visible = visible & jnp.logical_not((kk >= 2 * WINDOW) & (i == steps - 1))
        bias = jnp.where(visible, 0.0, NEG_INF)
        rows = slice(blk * WINDOW, (blk + 1) * WINDOW)
        keys = slice(blk * WINDOW, (blk + 3) * WINDOW)
        for kv in range(A_KV_HEADS):
            heads = range(kv * grp, (kv + 1) * grp)
            q = jnp.concatenate([q_ref[rows, h * HEAD_DIM:(h + 1) * HEAD_DIM] for h in heads], axis=0)
            s = _nt_dot(q, k[keys, kv * HEAD_DIM:(kv + 1) * HEAD_DIM])
            ps, dens = [], []
            for g_i, h in enumerate(heads):
                sh = s[g_i * WINDOW:(g_i + 1) * WINDOW, :] + bias
                sink = sink_ref[h]
                m = jnp.maximum(jnp.max(sh, axis=-1, keepdims=True), sink)
                p = jnp.exp(sh - m)
                dens.append(jnp.sum(p, axis=-1, keepdims=True) + jnp.exp(sink - m))
                ps.append(p.astype(BF16))
            o = _dot(jnp.concatenate(ps, axis=0), v[keys, kv * HEAD_DIM:(kv + 1) * HEAD_DIM])
            for g_i, h in enumerate(heads):
                acc_ref[rows, h * HEAD_DIM:(h + 1) * HEAD_DIM] = o[g_i * WINDOW:(g_i + 1) * WINDOW, :] / dens[g_i]
    o_ref[...] = _rms(acc_ref[...], g_ref[...]).astype(BF16)


def _attn_a(aq, ak, av, sink, g, bsz, seq):
    nb = seq // WINDOW
    assert nb % A_QBLOCKS == 0
    steps = nb // A_QBLOCKS
    qrows = A_QBLOCKS * WINDOW
    qspec = pl.BlockSpec((qrows, A_WIDTH), lambda b, i: (b * steps + i, 0))
    kprev = pl.BlockSpec((WINDOW, A_KV_WIDTH), lambda b, i: (b * nb + jnp.maximum(i * A_QBLOCKS - 1, 0), 0))
    kcur = pl.BlockSpec((qrows, A_KV_WIDTH), lambda b, i: (b * steps + i, 0))
    knext = pl.BlockSpec((WINDOW, A_KV_WIDTH), lambda b, i: (b * nb + jnp.minimum((i + 1) * A_QBLOCKS, nb - 1), 0))
    return pl.pallas_call(
        functools.partial(_attn_a_kernel, steps=steps),
        grid=(bsz, steps),
        in_specs=[pl.BlockSpec(memory_space=pltpu.SMEM), qspec, kprev, kcur, knext, kprev, kcur, knext,
                  pl.BlockSpec((1, A_WIDTH), lambda b, i: (0, 0))],
        out_specs=qspec,
        out_shape=jax.ShapeDtypeStruct((bsz * seq, A_WIDTH), BF16),
        scratch_shapes=[pltpu.VMEM((qrows, A_WIDTH), F32)],
        compiler_params=_cparams(("parallel", "parallel")),
        name="attn_a",
    )(sink, aq, ak, ak, ak, av, av, av, g)


B_GROUP = 16
B_KEYS = WIN_H * GRID_W


def _bias_table_kernel(rel_ref, o_ref):
    c = lax.broadcasted_iota(jnp.int32, (GRID_W, GRID_W), 0)
    w = lax.broadcasted_iota(jnp.int32, (GRID_W, GRID_W), 1)
    cstart = jnp.clip(c - WIN_W // 2, 0, GRID_W - WIN_W)
    colmask = (w >= cstart) & (w < cstart + WIN_W)
    col_off = jnp.clip(w - c + (WIN_W - 1), 0, 2 * WIN_W - 2)
    n_ro, n_co = 2 * WIN_H - 1, 2 * WIN_W - 1
    for h in range(B_HEADS):
        for ro in range(n_ro):
            def body(j, acc):
                return jnp.where(col_off == j, rel_ref[(h * n_ro + ro) * n_co + j], acc)
            t = lax.fori_loop(0, n_co, body, jnp.zeros((GRID_W, GRID_W), F32))
            t = jnp.where(colmask, t, NEG_INF)
            for p in range(WIN_H):
                kidx = ro - (WIN_H - 1) + p
                if 0 <= kidx < WIN_H:
                    o_ref[p, h, :, kidx * GRID_W:(kidx + 1) * GRID_W] = t


def _bias_table(rel_bias):
    return pl.pallas_call(
        _bias_table_kernel,
        in_specs=[pl.BlockSpec(memory_space=pltpu.SMEM)],
        out_shape=jax.ShapeDtypeStruct((WIN_H, B_HEADS, GRID_W, B_KEYS), F32),
        name="bias_table",
    )(rel_bias.reshape(-1)).reshape(WIN_H, B_HEADS * GRID_W, B_KEYS)


def _attn_b_kernel(q_ref, kp_ref, kc_ref, kn_ref, vp_ref, vc_ref, vn_ref, bias_ref, g_ref, o_ref,
                   kbuf, vbuf, acc_ref, *, rows):
    gidx = pl.program_id(1)
    blk = B_GROUP * GRID_W
    kbuf[0:blk, :] = kp_ref[...]
    kbuf[blk:2 * blk, :] = kc_ref[...]
    kbuf[2 * blk:3 * blk, :] = kn_ref[...]
    vbuf[0:blk, :] = vp_ref[...]
    vbuf[blk:2 * blk, :] = vc_ref[...]
    vbuf[2 * blk:3 * blk, :] = vn_ref[...]
    same_head = _same_head_mask(B_HEADS * GRID_W, GRID_W)
    for j in range(B_GROUP):
        r = gidx * B_GROUP + j
        start = jnp.clip(r - WIN_H // 2, 0, rows - WIN_H)
        pat = r - start
        loc = pl.multiple_of((start - gidx * B_GROUP + B_GROUP) * GRID_W, GRID_W)
        kw = kbuf[pl.ds(loc, B_KEYS), :]
        vw = vbuf[pl.ds(loc, B_KEYS), :]
        qj = q_ref[j * GRID_W:(j + 1) * GRID_W, :]
        q = jnp.where(same_head, jnp.concatenate([qj] * B_HEADS, axis=0), jnp.zeros((), BF16))
        s = _nt_dot(q, kw) + bias_ref[pat]
        m = jnp.max(s, axis=-1, keepdims=True)
        p = jnp.exp(s - m)
        den = jnp.sum(p, axis=-1, keepdims=True)
        o = jnp.where(same_head, _dot(p.astype(BF16), vw) / den, 0.0)
        acc_ref[j * GRID_W:(j + 1) * GRID_W, :] = sum(o[h * GRID_W:(h + 1) * GRID_W, :] for h in range(B_HEADS))
    o_ref[...] = _rms(acc_ref[...], g_ref[...]).astype(BF16)


def _attn_b(bq, bk, bv, bias, g, bsz, seq):
    rows = seq // GRID_W
    assert rows % B_GROUP == 0 and rows >= WIN_H
    ng = rows // B_GROUP
    blk = B_GROUP * GRID_W
    cur = pl.BlockSpec((blk, B_WIDTH), lambda b, i: (b * ng + i, 0))
    prev = pl.BlockSpec((blk, B_WIDTH), lambda b, i: (b * ng + jnp.maximum(i - 1, 0), 0))
    nxt = pl.BlockSpec((blk, B_WIDTH), lambda b, i: (b * ng + jnp.minimum(i + 1, ng - 1), 0))
    return pl.pallas_call(
        functools.partial(_attn_b_kernel, rows=rows),
        grid=(bsz, ng),
        in_specs=[cur, prev, cur, nxt, prev, cur, nxt,
                  pl.BlockSpec(bias.shape, lambda b, i: (0, 0, 0)),
                  pl.BlockSpec((1, B_WIDTH), lambda b, i: (0, 0))],
        out_specs=cur,
        out_shape=jax.ShapeDtypeStruct((bsz * seq, B_WIDTH), BF16),
        scratch_shapes=[pltpu.VMEM((3 * blk, B_WIDTH), BF16), pltpu.VMEM((3 * blk, B_WIDTH), BF16),
                        pltpu.VMEM((blk, B_WIDTH), F32)],
        compiler_params=_cparams(("parallel", "parallel")),
        name="attn_b",
    )(bq, bk, bk, bk, bv, bv, bv, bias, g)


def _gla_decay(la_ref, reverse):
    seg = min(GLA_CUM_ROWS, la_ref.shape[0])
    bi = lax.broadcasted_iota(jnp.int32, (seg, seg), 0)
    bj = lax.broadcasted_iota(jnp.int32, (seg, seg), 1)
    in_chunk = (bi // C_CHUNK) == (bj // C_CHUNK)
    cum_w = (in_chunk & ((bj >= bi) if reverse else (bj <= bi))).astype(BF16)
    parts = []
    for r in range(0, la_ref.shape[0], seg):
        la_hi, la_lo = _split_bf16(la_ref[r:r + seg, :])
        parts.append(_dot(cum_w, la_hi) + _dot(cum_w, la_lo))
    return parts[0] if len(parts) == 1 else jnp.concatenate(parts, axis=0)


def _gla_chunk(q_ref, k_ref, v_ref, o_ref, b_all, st, c, reverse):
    same_head = _same_head_mask(C_HEADS * C_CHUNK, C_CHUNK)
    ti = lax.broadcasted_iota(jnp.int32, (C_CHUNK, C_HEADS * C_CHUNK), 0)
    si = lax.broadcasted_iota(jnp.int32, (C_CHUNK, C_HEADS * C_CHUNK), 1) % C_CHUNK
    tri = (si >= ti) if reverse else (si <= ti)
    zero = jnp.zeros((), BF16)
    rows = slice(c * C_CHUNK, (c + 1) * C_CHUNK)
    b = b_all[rows, :]
    b_tot = b[0:1, :] if reverse else b[C_CHUNK - 1:C_CHUNK, :]
    q_i = (q_ref[rows, :] * jnp.exp(b)).astype(BF16)
    kf = k_ref[rows, :]
    k_i = (kf * jnp.exp(-b)).astype(BF16)
    k_e = (kf * jnp.exp(b_tot - b)).astype(BF16)
    v = v_ref[rows, :]
    k_bd = jnp.where(same_head, jnp.concatenate([k_i] * C_HEADS, axis=0), zero)
    v_bd = jnp.where(same_head, jnp.concatenate([v] * C_HEADS, axis=0), zero)
    a = jnp.where(tri, _nt_dot(q_i, k_bd), 0.0)
    o_ref[rows, :] = _dot(a.astype(BF16), v_bd) + _nt_dot(q_i, st.astype(BF16))
    return st * jnp.exp(b_tot) + jnp.where(same_head, _tn_dot(v, k_e), 0.0)


def _gla_kernel(qf_ref, kf_ref, vf_ref, laf_ref, qb_ref, kb_ref, vb_ref, lab_ref, of_ref, ob_ref, sf_ref, sb_ref):
    @pl.when(pl.program_id(1) == 0)
    def _():
        sf_ref[...] = jnp.zeros_like(sf_ref)
        sb_ref[...] = jnp.zeros_like(sb_ref)

    n_chunks = qf_ref.shape[0] // C_CHUNK
    b_f = _gla_decay(laf_ref, reverse=False)
    b_b = _gla_decay(lab_ref, reverse=True)
    s_f, s_b = sf_ref[...], sb_ref[...]
    for c in range(n_chunks):
        s_f = _gla_chunk(qf_ref, kf_ref, vf_ref, of_ref, b_f, s_f, c, reverse=False)
        s_b = _gla_chunk(qb_ref, kb_ref, vb_ref, ob_ref, b_b, s_b, n_chunks - 1 - c, reverse=True)
    sf_ref[...] = s_f
    sb_ref[...] = s_b


def _gla(cq, ck, cv, la, bsz, seq):
    tb = min(GLA_BLOCK, seq)
    nb = seq // tb
    fwd = lambda col: pl.BlockSpec((tb, C_WIDTH), lambda b, i: (b * nb + i, col))
    bwd = lambda col: pl.BlockSpec((tb, C_WIDTH), lambda b, i: (b * nb + nb - 1 - i, col))
    out = jax.ShapeDtypeStruct((bsz * seq, C_WIDTH), F32)
    state = pltpu.VMEM((C_WIDTH, C_WIDTH), F32)
    return pl.pallas_call(
        _gla_kernel,
        grid=(bsz, nb),
        in_specs=[fwd(0), fwd(0), fwd(0), fwd(0), bwd(0), bwd(0), bwd(0), bwd(1)],
        out_specs=[fwd(0), bwd(0)],
        out_shape=[out, out],
        scratch_shapes=[state, state],
        compiler_params=_cparams(("parallel", "arbitrary")),
        name="gla",
    )(cq, ck, cv, la, cq, ck, cv, la)


def _out_proj_kernel(x_ref, oa_ref, ob_ref, of_ref, obw_ref, cg_ref, cng_ref, wo_ref, g2_ref, wr_ref,
                     x1_ref, h2_ref, aff_ref, mix_ref):
    gi = lax.broadcasted_iota(jnp.int32, (C_WIDTH, C_WIDTH), 0) // HEAD_DIM
    gj = lax.broadcasted_iota(jnp.int32, (C_WIDTH, C_WIDTH), 1) // HEAD_DIM
    ones_bd = (gi == gj).astype(BF16)
    half = x_ref.shape[0] // 2
    for rows in (slice(0, half), slice(half, 2 * half)):
        o = of_ref[rows, :] + obw_ref[rows, :]
        sq_hi, sq_lo = _split_bf16(o * o)
        ms = (_dot(sq_hi, ones_bd) + _dot(sq_lo, ones_bd)) * (1.0 / HEAD_DIM)
        cg = cg_ref[rows, :]
        oc = (o * lax.rsqrt(ms + EPS) * cng_ref[...]) * (cg / (1.0 + jnp.exp(-cg)))
        mix_ref[rows, 0:A_WIDTH] = oa_ref[rows, :]
        mix_ref[rows, A_WIDTH:A_WIDTH + B_WIDTH] = ob_ref[rows, :]
        mix_ref[rows, A_WIDTH + B_WIDTH:] = oc.astype(BF16)
        x1 = x_ref[rows, :] + _dot(mix_ref[rows, :], wo_ref[...])
        x1_ref[rows, :] = x1
        h2 = _rms(x1, g2_ref[...]).astype(BF16)
        h2_ref[rows, :] = h2
        logits = _dot(h2, wr_ref[...])
        lane = lax.broadcasted_iota(jnp.int32, logits.shape, 1)
        logits = jnp.where(lane < N_EXPERTS, logits, NEG_INF)
        p = jnp.exp(logits - jnp.max(logits, axis=1, keepdims=True))
        aff = p / jnp.sum(p, axis=1, keepdims=True)
        aff_ref[:, rows] = aff.T[0:N_EXPERTS, :]


def _out_proj(x, oa, ob, o_f, o_b, cg, cng, wo_bf, layer, g2, wr_cat):
    n = x.shape[0]
    tm = min(OUT_PROJ_ROWS, n)
    row = lambda width: pl.BlockSpec((tm, width), lambda i: (i, 0))
    full = lambda a: pl.BlockSpec(a.shape, lambda i: (0,) * a.ndim)
    return pl.pallas_call(
        _out_proj_kernel,
        grid=(n // tm,),
        in_specs=[row(D_MODEL), row(A_WIDTH), row(B_WIDTH), row(C_WIDTH), row(C_WIDTH), row(C_WIDTH),
                  full(cng), pl.BlockSpec((None, D_MODEL, D_MODEL), lambda i: (layer, 0, 0)), full(g2), full(wr_cat)],
        out_specs=[row(D_MODEL), row(D_MODEL), pl.BlockSpec((N_EXPERTS, tm), lambda i: (0, i))],
        out_shape=[jax.ShapeDtypeStruct((n, D_MODEL), F32), jax.ShapeDtypeStruct((n, D_MODEL), BF16),
                   jax.ShapeDtypeStruct((N_EXPERTS, n), F32)],
        scratch_shapes=[pltpu.VMEM((tm, D_MODEL), BF16)],
        compiler_params=_cparams(("parallel",)),
        name="out_proj",
    )(x, oa, ob, o_f, o_b, cg, cng, wo_bf, g2, wr_cat)


def _route_kernel(aff_ref, pos_ref, offs_ref, *, cap):
    n = aff_ref.shape[1]
    nt = n // ROUTE_TILE
    bits = pltpu.bitcast(aff_ref[...], jnp.int32)

    def search(it, thr):
        cand = thr | jnp.left_shift(jnp.int32(1), 30 - it)
        cnt = jnp.sum(jnp.where(bits >= cand, 1.0, 0.0), axis=1, keepdims=True)
        return jnp.where(cnt >= cap, cand, thr)

    thr = lax.fori_loop(0, 31, search, jnp.zeros((N_EXPERTS, 1), jnp.int32))
    n_gt = jnp.sum(jnp.where(bits > thr, 1.0, 0.0), axis=1, keepdims=True)
    n_tie = cap - n_gt

    ui = lax.broadcasted_iota(jnp.int32, (ROUTE_TILE, ROUTE_TILE), 0)
    uj = lax.broadcasted_iota(jnp.int32, (ROUTE_TILE, ROUTE_TILE), 1)
    upper = (ui <= uj).astype(BF16)
    lane = lax.broadcasted_iota(jnp.int32, offs_ref.shape, 1)

    def tile(i, carry):
        c_gt, c_eq, offs = carry
        col = pl.multiple_of(i * ROUTE_TILE, ROUTE_TILE)
        b = pltpu.bitcast(aff_ref[:, pl.ds(col, ROUTE_TILE)], jnp.int32)
        gt = jnp.where(b > thr, 1.0, 0.0)
        eq = jnp.where(b == thr, 1.0, 0.0)
        inc = _dot(jnp.concatenate([gt, eq], axis=0).astype(BF16), upper)
        gt_before = c_gt + inc[:N_EXPERTS] - gt
        eq_before = c_eq + inc[N_EXPERTS:] - eq
        sel = (gt > 0.0) | ((eq > 0.0) & (eq_before < n_tie))
        pos = gt_before + jnp.minimum(eq_before, n_tie)
        pos_ref[:, pl.ds(col, ROUTE_TILE)] = jnp.where(sel, pos, -1.0).astype(jnp.int32)
        start = c_gt + jnp.minimum(c_eq, n_tie)
        offs = jnp.where(lane == i, start.astype(jnp.int32), offs)
        return (c_gt + jnp.sum(gt, axis=1, keepdims=True), c_eq + jnp.sum(eq, axis=1, keepdims=True), offs)

    zero = jnp.zeros((N_EXPERTS, 1), F32)
    offs = jnp.where(lane == nt, cap, 0).astype(jnp.int32)
    _, _, offs = lax.fori_loop(0, nt, tile, (zero, zero, offs), unroll=4 if nt % 4 == 0 else 1)
    offs_ref[...] = offs


def _route(aff_t, cap):
    n = aff_t.shape[1]
    nt = n // ROUTE_TILE
    return pl.pallas_call(
        functools.partial(_route_kernel, cap=cap),
        out_shape=[jax.ShapeDtypeStruct((N_EXPERTS, n), jnp.int32),
                   jax.ShapeDtypeStruct((N_EXPERTS, nt + 1), jnp.int32)],
        compiler_params=pltpu.CompilerParams(vmem_limit_bytes=VMEM_LIMIT),
        name="route",
    )(aff_t)


def _align(v):
    return (v // MOE_ALIGN) * MOE_ALIGN


def _moe_passes(offs_ref, t, subs, x0, s, win):
    k = jnp.int32(1)
    for g in range(MOE_PAIR):
        need = offs_ref[x0 + g, t * subs + s + 1] - _align(offs_ref[x0 + g, t * subs + s])
        k = jnp.maximum(k, (need + win - 1) // win)
    return k


def _moe_windows(offs_ref, pos_ref, t, subs, x0, s, j, win):
    slot = lax.broadcasted_iota(jnp.int32, (win, ROUTE_TILE), 0)
    hits, rels = [], []
    for g in range(MOE_PAIR):
        x = x0 + g
        ws = _align(offs_ref[x, t * subs + s]) + j * win
        pos = pos_ref[pl.ds(x, 1), s * ROUTE_TILE:(s + 1) * ROUTE_TILE]
        hits.append((pos - ws) == slot)
        rels.append(pl.multiple_of(ws - _align(offs_ref[x, t * subs]), MOE_ALIGN))
    onehot = jnp.concatenate([jnp.where(h, 1.0, 0.0) for h in hits], axis=0).astype(BF16)
    return hits, rels, onehot


def _moe_chunks(offs_ref, t, subs, x):
    base = _align(offs_ref[x, t * subs])
    return base, (offs_ref[x, (t + 1) * subs] - base + MOE_CHUNK - 1) // MOE_CHUNK


def _moe_gather_kernel(offs_ref, h_ref, pos_ref, gate_ref, xe_ref, stage_ref, carry_ref, sem):
    t = pl.program_id(0)
    steps = pl.num_programs(1)
    e0 = pl.program_id(1) * MOE_STEP
    tt = h_ref.shape[0]
    subs = tt // ROUTE_TILE
    k = t * steps + pl.program_id(1)
    slot = k % 2
    x_ref = stage_ref.at[slot]

    def off(x, s):
        return offs_ref[x, t * subs + s]

    def copies(kk, half, act):
        step_t, step_q = kk // steps, kk % steps
        for g in range(MOE_STEP):
            x = step_q * MOE_STEP + g
            base, n_chunks = _moe_chunks(offs_ref, step_t, subs, x)

            def body(c, carry):
                r0 = pl.multiple_of(c * MOE_CHUNK, MOE_CHUNK)
                act(pltpu.make_async_copy(stage_ref.at[half, g, pl.ds(r0, MOE_CHUNK), :],
                                          xe_ref.at[x, pl.ds(pl.multiple_of(base + r0, MOE_ALIGN), MOE_CHUNK), :],
                                          sem.at[half]))
                return carry

            lax.fori_loop(0, n_chunks, body, 0)

    @pl.when(k >= 2)
    def _():
        copies(k - 2, slot, lambda cp: cp.wait())

    @pl.when(t == 0)
    def _():
        carry_ref[pl.ds(e0, MOE_STEP)] = jnp.zeros((MOE_STEP, MOE_ALIGN, XE_WIDTH), BF16)
        cap = xe_ref.shape[1] - MOE_CHUNK
        x_ref[0, 0:MOE_CHUNK, :] = jnp.zeros((MOE_CHUNK, XE_WIDTH), BF16)
        pads = [pltpu.make_async_copy(x_ref.at[0, 0:MOE_CHUNK, :], xe_ref.at[e0 + g, cap:cap + MOE_CHUNK, :],
                                      sem.at[slot]) for g in range(MOE_STEP)]
        for p in pads:
            p.start()
        for p in pads:
            p.wait()

    for g in range(MOE_STEP):
        x_ref[g, 0:MOE_ALIGN, :] = carry_ref[e0 + g]

    row = lax.broadcasted_iota(jnp.int32, (MOE_WIN, 1), 0)
    lane = lax.broadcasted_iota(jnp.int32, (MOE_WIN, XE_WIDTH - D_MODEL), 1)

    def gather_pass(pair, s, j):
        x0 = e0 + pair * MOE_PAIR
        cols = slice(s * ROUTE_TILE, (s + 1) * ROUTE_TILE)
        hits, rels, onehot = _moe_windows(offs_ref, pos_ref, t, subs, x0, s, j, MOE_WIN)
        r = _dot(onehot, h_ref[cols, :])
        for g in range(MOE_PAIR):
            gate = jnp.sum(jnp.where(hits[g], gate_ref[pl.ds(x0 + g, 1), cols], 0.0), axis=1, keepdims=True)
            g_hi = gate.astype(BF16).astype(F32)
            extra = jnp.where(lane == 0, g_hi, jnp.where(lane == 1, gate - g_hi, 0.0))
            new = jnp.concatenate([r[g * MOE_WIN:(g + 1) * MOE_WIN, :], extra], axis=1)
            ws = _align(off(x0 + g, s)) + j * MOE_WIN
            own = row >= off(x0 + g, s) - ws
            buf = pair * MOE_PAIR + g
            if isinstance(j, int):
                head = pl.ds(rels[g], MOE_ALIGN)
                x_ref[buf, head, :] = jnp.where(own[:MOE_ALIGN], new[:MOE_ALIGN],
                                                x_ref[buf, head, :].astype(F32)).astype(BF16)
                x_ref[buf, pl.ds(rels[g] + MOE_ALIGN, MOE_WIN - MOE_ALIGN), :] = new[MOE_ALIGN:].astype(BF16)
            else:
                own = own & (row < off(x0 + g, s + 1) - ws)
                win = pl.ds(rels[g], MOE_WIN)
                x_ref[buf, win, :] = jnp.where(own, new, x_ref[buf, win, :].astype(F32)).astype(BF16)

    for pair in range(MOE_STEP // MOE_PAIR):
        for s in range(subs):
            gather_pass(pair, s, 0)
    for pair in range(MOE_STEP // MOE_PAIR):
        for s in range(subs):
            lax.fori_loop(1, _moe_passes(offs_ref, t, subs, e0 + pair * MOE_PAIR, s, MOE_WIN),
                          lambda j, carry: (gather_pass(pair, s, j), carry)[1], 0)

    for g in range(MOE_STEP):
        x = e0 + g
        base, n_chunks = _moe_chunks(offs_ref, t, subs, x)
        last_group = pl.multiple_of(_align(off(x, subs)) - base, MOE_ALIGN)
        tail = x_ref[g, pl.ds(last_group, MOE_ALIGN), :].astype(F32)
        tail = jnp.where(row[:MOE_ALIGN] < off(x, subs) - base - last_group, tail, 0.0).astype(BF16)
        x_ref[g, pl.ds(last_group, MOE_ALIGN), :] = tail
        carry_ref[x] = tail

        def fill(i, carry):
            x_ref[g, pl.ds(pl.multiple_of(i * MOE_ALIGN, MOE_ALIGN), MOE_ALIGN), :] = jnp.zeros((MOE_ALIGN, XE_WIDTH), BF16)
            return carry

        lax.fori_loop(last_group // MOE_ALIGN + 1, n_chunks * MOE_CHUNK // MOE_ALIGN, fill, 0)

    copies(k, slot, lambda cp: cp.start())

    @pl.when(k == pl.num_programs(0) * steps - 1)
    def _():
        copies(k - 1, 1 - slot, lambda cp: cp.wait())
        copies(k, slot, lambda cp: cp.wait())


def _moe_ffn_kernel(x_ref, wg_ref, wu_ref, wd_ref, y_ref, wg_bf, wu_bf, wd_bf):
    j = pl.program_id(1)

    @pl.when(j == 0)
    def _():
        wg_bf[...] = wg_ref[...].astype(BF16)
        wu_bf[...] = wu_ref[...].astype(BF16)
        wd_bf[...] = wd_ref[...].astype(BF16)
        y_ref[0] = jnp.zeros(y_ref.shape[1:], BF16)

    @pl.when(j > 0)
    def _():
        x = x_ref[0, :, 0:D_MODEL]
        gate = jnp.sum(x_ref[0, :, D_MODEL:XE_WIDTH].astype(F32), axis=1, keepdims=True)
        hg = _dot(x, wg_bf[...])
        hu = _dot(x, wu_bf[...])
        act = (hg / (1.0 + jnp.exp(-hg))) * hu * gate
        y_ref[0] = _dot(act.astype(BF16), wd_bf[...]).astype(BF16)


def _moe_combine_kernel(offs_ref, x1_ref, pos_ref, y_ref, fg_ref, o_ref, y_buf, sem, *, final_norm):
    t = pl.program_id(0)
    q = pl.program_id(1)
    steps = pl.num_programs(1)
    e0 = q * MOE_STEP
    tt = x1_ref.shape[0]
    subs = tt // ROUTE_TILE
    k = t * steps + q
    slot = k % 2

    def fetch(kk, half, act):
        step_t, step_q = kk // steps, kk % steps
        for g in range(MOE_STEP):
            x = step_q * MOE_STEP + g
            base, n_chunks = _moe_chunks(offs_ref, step_t, subs, x)

            def body(c, carry):
                r0 = pl.multiple_of(c * MOE_CHUNK, MOE_CHUNK)
                act(pltpu.make_async_copy(y_ref.at[x, pl.ds(pl.multiple_of(base + r0, MOE_ALIGN), MOE_CHUNK), :],
                                          y_buf.at[half, g, pl.ds(r0, MOE_CHUNK), :], sem.at[half]))
                return carry

            lax.fori_loop(0, n_chunks, body, 0)

    @pl.when(k == 0)
    def _():
        y_buf[...] = jnp.zeros_like(y_buf)
        fetch(k, slot, lambda cp: cp.start())

    @pl.when(k + 1 < pl.num_programs(0) * steps)
    def _():
        fetch(k + 1, 1 - slot, lambda cp: cp.start())

    fetch(k, slot, lambda cp: cp.wait())

    @pl.when(q == 0)
    def _():
        o_ref[...] = x1_ref[...]

    def combine_pass(pair, s, j):
        cols = slice(s * ROUTE_TILE, (s + 1) * ROUTE_TILE)
        _, rels, onehot = _moe_windows(offs_ref, pos_ref, t, subs, e0 + pair * MOE_PAIR, s, j, MOE_COMBINE_WIN)
        ycat = jnp.concatenate([y_buf[slot, pair * MOE_PAIR + g, pl.ds(rels[g], MOE_COMBINE_WIN), :]
                                for g in range(MOE_PAIR)], axis=0)
        o_ref[cols, :] += _tn_dot(onehot, ycat)

    for pair in range(MOE_STEP // MOE_PAIR):
        for s in range(subs):
            combine_pass(pair, s, 0)
    for pair in range(MOE_STEP // MOE_PAIR):
        for s in range(subs):
            lax.fori_loop(1, _moe_passes(offs_ref, t, subs, e0 + pair * MOE_PAIR, s, MOE_COMBINE_WIN),
                          lambda j, carry: (combine_pass(pair, s, j), carry)[1], 0)

    if final_norm:
        @pl.when(q == pl.num_programs(1) - 1)
        def _():
            o_ref[...] = _rms(o_ref[...], fg_ref[...])


def _moe(offs, h2, x1, pos, gate, wg, wu, wd, layer, final_g, final_norm, cap):
    n = h2.shape[0]
    tt, tc = min(MOE_TILE, n), min(MOE_COMBINE_TILE, n)
    assert cap % MOE_CHUNK == 0 and n % tt == 0 and n % tc == 0
    steps = N_EXPERTS // MOE_STEP
    stage_rows = lambda tile, win: tile + MOE_ALIGN + win * pl.cdiv(ROUTE_TILE + MOE_ALIGN, win)
    tok = lambda tile: pl.BlockSpec((tile, D_MODEL), lambda t, q, offs: (t, 0))
    per_tok = lambda tile: pl.BlockSpec((N_EXPERTS, tile), lambda t, q, offs: (0, t))
    hbm = pl.BlockSpec(memory_space=pl.ANY)
    sems = ("arbitrary", "arbitrary")

    xe = pl.pallas_call(
        _moe_gather_kernel,
        grid_spec=pltpu.PrefetchScalarGridSpec(
            num_scalar_prefetch=1, grid=(n // tt, steps),
            in_specs=[tok(tt), per_tok(tt), per_tok(tt)], out_specs=hbm,
            scratch_shapes=[pltpu.VMEM((2, MOE_STEP, stage_rows(tt, MOE_WIN), XE_WIDTH), BF16),
                            pltpu.VMEM((N_EXPERTS, MOE_ALIGN, XE_WIDTH), BF16), pltpu.SemaphoreType.DMA((2,))],
        ),
        out_shape=jax.ShapeDtypeStruct((N_EXPERTS, cap + MOE_CHUNK, XE_WIDTH), BF16),
        compiler_params=_cparams(sems),
        name="moe_gather",
    )(offs, h2, pos, gate)

    fb = min(MOE_FFN_ROWS, cap)
    nb = cap // fb
    wspec = pl.BlockSpec((None, None, D_MODEL, D_MODEL), lambda e, j: (layer, e, 0, 0))
    w_bf = pltpu.VMEM((D_MODEL, D_MODEL), BF16)
    y = pl.pallas_call(
        _moe_ffn_kernel,
        grid=(N_EXPERTS, nb + 1),
        in_specs=[pl.BlockSpec((1, fb, XE_WIDTH), lambda e, j: (e, jnp.maximum(j - 1, 0), 0)), wspec, wspec, wspec],
        out_specs=pl.BlockSpec((1, fb, D_MODEL), lambda e, j: (e, jnp.where(j == 0, nb, j - 1), 0)),
        out_shape=jax.ShapeDtypeStruct((N_EXPERTS, cap + fb, D_MODEL), BF16),
        scratch_shapes=[w_bf, w_bf, w_bf],
        compiler_params=_cparams(("parallel", "arbitrary")),
        name="moe_ffn",
    )(xe, wg, wu, wd)

    return pl.pallas_call(
        functools.partial(_moe_combine_kernel, final_norm=final_norm),
        grid_spec=pltpu.PrefetchScalarGridSpec(
            num_scalar_prefetch=1, grid=(n // tc, steps),
            in_specs=[tok(tc), per_tok(tc), hbm, pl.BlockSpec((1, D_MODEL), lambda t, q, offs: (0, 0))],
            out_specs=tok(tc),
            scratch_shapes=[pltpu.VMEM((2, MOE_STEP, stage_rows(tc, MOE_COMBINE_WIN), D_MODEL), BF16),
                            pltpu.SemaphoreType.DMA((2,))],
        ),
        out_shape=jax.ShapeDtypeStruct((n, D_MODEL), F32),
        compiler_params=_cparams(sems),
        name="moe_combine",
    )(offs, x1, pos, y, final_g)


def _prep_layer(l, norm1_g, w_in, a_sink, a_norm_g, b_rel_bias, b_norm_g, c_alpha_w2_f, c_alpha_b_f,
                c_alpha_w2_b, c_alpha_b_b, c_norm_g, w_out, norm2_g, w_router, w_gate, w_up, w_down):
    z = jnp.zeros((C_LOWRANK, C_WIDTH), F32)
    w2 = jnp.concatenate([jnp.concatenate([c_alpha_w2_f[l], z], axis=1),
                          jnp.concatenate([z, c_alpha_w2_b[l]], axis=1)], axis=0).astype(BF16)
    b2 = jnp.concatenate([c_alpha_b_f[l], c_alpha_b_b[l]])[None, :]
    wr_cat = jnp.pad(w_router[l].astype(BF16), ((0, 0), (0, LANES - N_EXPERTS)))
    return dict(
        layer=l, g1=norm1_g[l][None, :], w_in=w_in.astype(BF16), w2=w2, b2=b2,
        sink=a_sink[l], a_g=a_norm_g[l][None, :], bias=_bias_table(b_rel_bias[l]), b_g=b_norm_g[l][None, :],
        c_g=c_norm_g[l][None, :], w_out=w_out.astype(BF16), g2=norm2_g[l][None, :],
        wr_cat=wr_cat, wg=w_gate, wu=w_up, wd=w_down)


def _layer(x, p, bsz, seq, rope, final_g, final_norm):
    n = bsz * seq
    l = p["layer"]
    aq, ak, av, bq, bk, bv, cq, ck, cv, cg, la = _in_proj(x, p["g1"], p["w_in"], l, p["w2"], p["b2"], seq, rope)
    oa = _attn_a(aq, ak, av, p["sink"], p["a_g"], bsz, seq)
    ob = _attn_b(bq, bk, bv, p["bias"], p["b_g"], bsz, seq)
    o_f, o_b = _gla(cq, ck, cv, la, bsz, seq)
    x1, h2, aff = _out_proj(x, oa, ob, o_f, o_b, cg, p["c_g"], p["w_out"], l, p["g2"], p["wr_cat"])
    cap = EC_CAPACITY * n // N_EXPERTS
    pos, offs = _route(aff, cap)
    return _moe(offs, h2, x1, pos, aff, p["wg"], p["wu"], p["wd"], l, final_g, final_norm, cap)


def _trunk(x, layers, rope, final_g):
    bsz, seq, _ = x.shape
    y = x.reshape(bsz * seq, D_MODEL)
    for l, p in enumerate(layers):
        y = _layer(y, p, bsz, seq, rope, final_g, final_norm=(l == len(layers) - 1))
    return y.reshape(bsz, seq, D_MODEL)


def kernel(x_prompt, x_sample, norm1_g, w_in, a_sink, a_norm_g, b_rel_bias, b_norm_g, c_alpha_w2_f, c_alpha_b_f,
           c_alpha_w2_b, c_alpha_b_b, c_norm_g, w_out, norm2_g, w_router, w_gate, w_up, w_down, final_g):
    depth = w_in.shape[0]
    layers = [_prep_layer(l, norm1_g, w_in, a_sink, a_norm_g, b_rel_bias, b_norm_g, c_alpha_w2_f, c_alpha_b_f,
                          c_alpha_w2_b, c_alpha_b_b, c_norm_g, w_out, norm2_g, w_router, w_gate, w_up, w_down)
              for l in range(depth)]
    fg = final_g[None, :]
    rope = _rope_tables(max(x_prompt.shape[1], x_sample.shape[1]))
    return _trunk(x_prompt, layers, rope, fg), _trunk(x_sample, layers, rope, fg)
```

```python
import functools

import jax
import jax.numpy as jnp
from jax import lax
from jax.experimental import pallas as pl
from jax.experimental.pallas import tpu as pltpu

F32 = jnp.float32
BF16 = jnp.bfloat16

D_MODEL = 1024
HEAD_DIM = 64
A_WIDTH = 512
A_HEADS = 8
A_KV_HEADS = 2
A_KV_WIDTH = 128
WINDOW = 128
ROPE_THETA = 500000.0
ROPE_DIM = 16
B_WIDTH = 256
B_HEADS = 4
GRID_W = 64
WIN_H = 8
WIN_W = 16
C_WIDTH = 256
C_HEADS = 4
C_LOWRANK = 16
C_TAU = 16.0
C_CHUNK = 64
N_EXPERTS = 16
EC_CAPACITY = 2
EPS = 1e-6
NEG_INF = -1e30
IN_WIDTH = 2592

_OFF_AQ, _OFF_AK, _OFF_AV = 0, 512, 640
_OFF_BQ, _OFF_BK, _OFF_BV = 768, 1024, 1280
_OFF_CQ, _OFF_CK, _OFF_CV, _OFF_CG = 1536, 1792, 2048, 2304
_OFF_LR = 2560

LANES = 128
VMEM_LIMIT = 56 * 1024 * 1024

IN_PROJ_ROWS = 1024
OUT_PROJ_ROWS = 1024
GLA_BLOCK = 1024
GLA_CUM_ROWS = 256
ROUTE_TILE = 256
MOE_TILE = 2048
MOE_COMBINE_TILE = 1024
MOE_STEP = 4
MOE_PAIR = 4
MOE_WIN = 64
MOE_COMBINE_WIN = 128
MOE_ALIGN = 16
MOE_CHUNK = 64
MOE_FFN_ROWS = 1024
XE_WIDTH = D_MODEL + LANES


def _cparams(sem):
    return pltpu.CompilerParams(dimension_semantics=sem, vmem_limit_bytes=VMEM_LIMIT)


def _nt_dot(a, b):
    return lax.dot_general(a, b, (((1,), (1,)), ((), ())), preferred_element_type=F32)


def _tn_dot(a, b):
    return lax.dot_general(a, b, (((0,), (0,)), ((), ())), preferred_element_type=F32)


def _dot(a, b):
    return jnp.dot(a, b, preferred_element_type=F32)


def _rms(x, g):
    return x * lax.rsqrt(jnp.mean(x * x, axis=-1, keepdims=True) + EPS) * g


def _same_head_mask(n_rows, rows_per_head):
    width = (n_rows // rows_per_head) * HEAD_DIM
    ri = lax.broadcasted_iota(jnp.int32, (n_rows, width), 0) // rows_per_head
    li = lax.broadcasted_iota(jnp.int32, (n_rows, width), 1) // HEAD_DIM
    return ri == li


def _split_bf16(x):
    hi = x.astype(BF16)
    return hi, (x - hi.astype(F32)).astype(BF16)


def _in_proj_kernel(x_ref, g_ref, w_ref, cos_ref, s1_ref, s2_ref, w2_ref, b2_ref,
                    aq_ref, ak_ref, av_ref, bq_ref, bk_ref, bv_ref,
                    cq_ref, ck_ref, cv_ref, cg_ref, la_ref):
    h = _rms(x_ref[...], g_ref[...]).astype(BF16)
    y = _dot(h, w_ref[...])

    def proj(lo, width):
        return y[:, lo:lo + width]

    cos, s1, s2 = cos_ref[...], s1_ref[...], s2_ref[...]

    def rope(t):
        return t * cos + pltpu.roll(t, LANES - ROPE_DIM // 2, 1) * s1 + pltpu.roll(t, ROPE_DIM // 2, 1) * s2

    scale = HEAD_DIM ** -0.5
    aq = proj(_OFF_AQ, A_WIDTH)
    for c in range(A_WIDTH // LANES):
        aq_ref[:, c * LANES:(c + 1) * LANES] = (rope(aq[:, c * LANES:(c + 1) * LANES]) * scale).astype(BF16)
    ak_ref[...] = rope(proj(_OFF_AK, A_KV_WIDTH)).astype(BF16)
    av_ref[...] = proj(_OFF_AV, A_KV_WIDTH).astype(BF16)
    bq_ref[...] = (proj(_OFF_BQ, B_WIDTH) * scale).astype(BF16)
    bk_ref[...] = proj(_OFF_BK, B_WIDTH).astype(BF16)
    bv_ref[...] = proj(_OFF_BV, B_WIDTH).astype(BF16)
    cq_ref[...] = proj(_OFF_CQ, C_WIDTH) * scale
    ck_ref[...] = proj(_OFF_CK, C_WIDTH)
    cv_ref[...] = proj(_OFF_CV, C_WIDTH).astype(BF16)
    cg_ref[...] = proj(_OFF_CG, C_WIDTH)
    z = _dot(proj(_OFF_LR, 2 * C_LOWRANK).astype(BF16), w2_ref[...]) + b2_ref[...]
    la_ref[...] = (jnp.minimum(z, 0.0) - jnp.log(1.0 + jnp.exp(-jnp.abs(z)))) * (1.0 / C_TAU)


def _rope_tables(seq):
    half = ROPE_DIM // 2
    inv = jnp.power(jnp.float32(ROPE_THETA), -jnp.arange(half, dtype=F32) * (2.0 / ROPE_DIM))
    d = jnp.arange(LANES) % HEAD_DIM
    ang = jnp.arange(seq, dtype=F32)[:, None] * inv[d % half][None, :]
    cos, sin = jnp.cos(ang), jnp.sin(ang)
    first, second = (d < half)[None, :], ((d >= half) & (d < ROPE_DIM))[None, :]
    return (jnp.where(first | second, cos, 1.0), jnp.where(first, -sin, 0.0), jnp.where(second, sin, 0.0))


def _in_proj(x, g1, w_in_bf, layer, w2, b2, seq, rope):
    n = x.shape[0]
    tm = min(IN_PROJ_ROWS, seq)
    per_seq = seq // tm
    cos, s1, s2 = rope
    row = lambda width: pl.BlockSpec((tm, width), lambda i: (i, 0))
    full = lambda a: pl.BlockSpec(a.shape, lambda i: (0,) * a.ndim)
    tab = pl.BlockSpec((tm, LANES), lambda i: (i % per_seq, 0))
    widths = [(A_WIDTH, BF16), (A_KV_WIDTH, BF16), (A_KV_WIDTH, BF16), (B_WIDTH, BF16), (B_WIDTH, BF16),
              (B_WIDTH, BF16), (C_WIDTH, F32), (C_WIDTH, F32), (C_WIDTH, BF16), (C_WIDTH, F32), (2 * C_WIDTH, F32)]
    return pl.pallas_call(
        _in_proj_kernel,
        grid=(n // tm,),
        in_specs=[row(D_MODEL), full(g1), pl.BlockSpec((None, D_MODEL, IN_WIDTH), lambda i: (layer, 0, 0)),
                  tab, tab, tab, full(w2), full(b2)],
        out_specs=[row(w) for w, _ in widths],
        out_shape=[jax.ShapeDtypeStruct((n, w), dt) for w, dt in widths],
        compiler_params=_cparams(("parallel",)),
        name="in_proj",
    )(x, g1, w_in_bf, cos, s1, s2, w2, b2)


A_QBLOCKS = 4


def _attn_a_kernel(sink_ref, q_ref, kp_ref, kc_ref, kn_ref, vp_ref, vc_ref, vn_ref, g_ref, o_ref, acc_ref, *, steps):
    i = pl.program_id(1)
    grp = A_HEADS // A_KV_HEADS
    k = jnp.concatenate([kp_ref[...], kc_ref[...], kn_ref[...]], axis=0)
    v = jnp.concatenate([vp_ref[...], vc_ref[...], vn_ref[...]], axis=0)
    qq = lax.broadcasted_iota(jnp.int32, (WINDOW, 3 * WINDOW), 0)
    kk = lax.broadcasted_iota(jnp.int32, (WINDOW, 3 * WINDOW), 1)
    band = jnp.abs(kk - WINDOW - qq) <= WINDOW
    for blk in range(A_QBLOCKS):
        visible = band
        if blk == 0:
            visible = visible & jnp.logical_not((kk < WINDOW) & (i == 0))
        if blk == A_QBLOCKS - 1:
            visible = visible & jnp.logical_not((kk >= 2 * WINDOW) & (i == steps - 1))
        bias = jnp.where(visible, 0.0, NEG_INF)
        rows = slice(blk * WINDOW, (blk + 1) * WINDOW)
        keys = slice(blk * WINDOW, (blk + 3) * WINDOW)
        for kv in range(A_KV_HEADS):
            heads = range(kv * grp, (kv + 1) * grp)
            q = jnp.concatenate([q_ref[rows, h * HEAD_DIM:(h + 1) * HEAD_DIM] for h in heads], axis=0)
            s = _nt_dot(q, k[keys, kv * HEAD_DIM:(kv + 1) * HEAD_DIM])
            ps, dens = [], []
            for g_i, h in enumerate(heads):
                sh = s[g_i * WINDOW:(g_i + 1) * WINDOW, :] + bias
                sink = sink_ref[h]
                m = jnp.maximum(jnp.max(sh, axis=-1, keepdims=True), sink)
                p = jnp.exp(sh - m)
                dens.append(jnp.sum(p, axis=-1, keepdims=True) + jnp.exp(sink - m))
                ps.append(p.astype(BF16))
            o = _dot(jnp.concatenate(ps, axis=0), v[keys, kv * HEAD_DIM:(kv + 1) * HEAD_DIM])
            for g_i, h in enumerate(heads):
                acc_ref[rows, h * HEAD_DIM:(h + 1) * HEAD_DIM] = o[g_i * WINDOW:(g_i + 1) * WINDOW, :] / dens[g_i]
    o_ref[...] = _rms(acc_ref[...], g_ref[...]).astype(BF16)


def _attn_a(aq, ak, av, sink, g, bsz, seq):
    nb = seq // WINDOW
    assert nb % A_QBLOCKS == 0
    steps = nb // A_QBLOCKS
    qrows = A_QBLOCKS * WINDOW
    qspec = pl.BlockSpec((qrows, A_WIDTH), lambda b, i: (b * steps + i, 0))
    kprev = pl.BlockSpec((WINDOW, A_KV_WIDTH), lambda b, i: (b * nb + jnp.maximum(i * A_QBLOCKS - 1, 0), 0))
    kcur = pl.BlockSpec((qrows, A_KV_WIDTH), lambda b, i: (b * steps + i, 0))
    knext = pl.BlockSpec((WINDOW, A_KV_WIDTH), lambda b, i: (b * nb + jnp.minimum((i + 1) * A_QBLOCKS, nb - 1), 0))
    return pl.pallas_call(
        functools.partial(_attn_a_kernel, steps=steps),
        grid=(bsz, steps),
        in_specs=[pl.BlockSpec(memory_space=pltpu.SMEM), qspec, kprev, kcur, knext, kprev, kcur, knext,
                  pl.BlockSpec((1, A_WIDTH), lambda b, i: (0, 0))],
        out_specs=qspec,
        out_shape=jax.ShapeDtypeStruct((bsz * seq, A_WIDTH), BF16),
        scratch_shapes=[pltpu.VMEM((qrows, A_WIDTH), F32)],
        compiler_params=_cparams(("parallel", "parallel")),
        name="attn_a",
    )(sink, aq, ak, ak, ak, av, av, av, g)


B_GROUP = 16
B_KEYS = WIN_H * GRID_W


def _bias_table_kernel(rel_ref, o_ref):
    c = lax.broadcasted_iota(jnp.int32, (GRID_W, GRID_W), 0)
    w = lax.broadcasted_iota(jnp.int32, (GRID_W, GRID_W), 1)
    cstart = jnp.clip(c - WIN_W // 2, 0, GRID_W - WIN_W)
    colmask = (w >= cstart) & (w < cstart + WIN_W)
    col_off = jnp.clip(w - c + (WIN_W - 1), 0, 2 * WIN_W - 2)
    n_ro, n_co = 2 * WIN_H - 1, 2 * WIN_W - 1
    for h in range(B_HEADS):
        for ro in range(n_ro):
            def body(j, acc):
                return jnp.where(col_off == j, rel_ref[(h * n_ro + ro) * n_co + j], acc)
            t = lax.fori_loop(0, n_co, body, jnp.zeros((GRID_W, GRID_W), F32))
            t = jnp.where(colmask, t, NEG_INF)
            for p in range(WIN_H):
                kidx = ro - (WIN_H - 1) + p
                if 0 <= kidx < WIN_H:
                    o_ref[p, h, :, kidx * GRID_W:(kidx + 1) * GRID_W] = t


def _bias_table(rel_bias):
    return pl.pallas_call(
        _bias_table_kernel,
        in_specs=[pl.BlockSpec(memory_space=pltpu.SMEM)],
        out_shape=jax.ShapeDtypeStruct((WIN_H, B_HEADS, GRID_W, B_KEYS), F32),
        name="bias_table",
    )(rel_bias.reshape(-1)).reshape(WIN_H, B_HEADS * GRID_W, B_KEYS)


def _attn_b_kernel(q_ref, kp_ref, kc_ref, kn_ref, vp_ref, vc_ref, vn_ref, bias_ref, g_ref, o_ref,
                   kbuf, vbuf, acc_ref, *, rows):
    gidx = pl.program_id(1)
    blk = B_GROUP * GRID_W
    kbuf[0:blk, :] = kp_ref[...]
    kbuf[blk:2 * blk, :] = kc_ref[...]
    kbuf[2 * blk:3 * blk, :] = kn_ref[...]
    vbuf[0:blk, :] = vp_ref[...]
    vbuf[blk:2 * blk, :] = vc_ref[...]
    vbuf[2 * blk:3 * blk, :] = vn_ref[...]
    same_head = _same_head_mask(B_HEADS * GRID_W, GRID_W)
    for j in range(B_GROUP):
        r = gidx * B_GROUP + j
        start = jnp.clip(r - WIN_H // 2, 0, rows - WIN_H)
        pat = r - start
        loc = pl.multiple_of((start - gidx * B_GROUP + B_GROUP) * GRID_W, GRID_W)
        kw = kbuf[pl.ds(loc, B_KEYS), :]
        vw = vbuf[pl.ds(loc, B_KEYS), :]
        qj = q_ref[j * GRID_W:(j + 1) * GRID_W, :]
        q = jnp.where(same_head, jnp.concatenate([qj] * B_HEADS, axis=0), jnp.zeros((), BF16))
        s = _nt_dot(q, kw) + bias_ref[pat]
        m = jnp.max(s, axis=-1, keepdims=True)
        p = jnp.exp(s - m)
        den = jnp.sum(p, axis=-1, keepdims=True)
        o = jnp.where(same_head, _dot(p.astype(BF16), vw) / den, 0.0)
        acc_ref[j * GRID_W:(j + 1) * GRID_W, :] = sum(o[h * GRID_W:(h + 1) * GRID_W, :] for h in range(B_HEADS))
    o_ref[...] = _rms(acc_ref[...], g_ref[...]).astype(BF16)


def _attn_b(bq, bk, bv, bias, g, bsz, seq):
    rows = seq // GRID_W
    assert rows % B_GROUP == 0 and rows >= WIN_H
    ng = rows // B_GROUP
    blk = B_GROUP * GRID_W
    cur = pl.BlockSpec((blk, B_WIDTH), lambda b, i: (b * ng + i, 0))
    prev = pl.BlockSpec((blk, B_WIDTH), lambda b, i: (b * ng + jnp.maximum(i - 1, 0), 0))
    nxt = pl.BlockSpec((blk, B_WIDTH), lambda b, i: (b * ng + jnp.minimum(i + 1, ng - 1), 0))
    return pl.pallas_call(
        functools.partial(_attn_b_kernel, rows=rows),
        grid=(bsz, ng),
        in_specs=[cur, prev, cur, nxt, prev, cur, nxt,
                  pl.BlockSpec(bias.shape, lambda b, i: (0, 0, 0)),
                  pl.BlockSpec((1, B_WIDTH), lambda b, i: (0, 0))],
        out_specs=cur,
        out_shape=jax.ShapeDtypeStruct((bsz * seq, B_WIDTH), BF16),
        scratch_shapes=[pltpu.VMEM((3 * blk, B_WIDTH), BF16), pltpu.VMEM((3 * blk, B_WIDTH), BF16),
                        pltpu.VMEM((blk, B_WIDTH), F32)],
        compiler_params=_cparams(("parallel", "parallel")),
        name="attn_b",
    )(bq, bk, bk, bk, bv, bv, bv, bias, g)


def _gla_decay(la_ref, reverse):
    seg = min(GLA_CUM_ROWS, la_ref.shape[0])
    bi = lax.broadcasted_iota(jnp.int32, (seg, seg), 0)
    bj = lax.broadcasted_iota(jnp.int32, (seg, seg), 1)
    in_chunk = (bi // C_CHUNK) == (bj // C_CHUNK)
    cum_w = (in_chunk & ((bj >= bi) if reverse else (bj <= bi))).astype(BF16)
    parts = []
    for r in range(0, la_ref.shape[0], seg):
        la_hi, la_lo = _split_bf16(la_ref[r:r + seg, :])
        parts.append(_dot(cum_w, la_hi) + _dot(cum_w, la_lo))
    return parts[0] if len(parts) == 1 else jnp.concatenate(parts, axis=0)


def _gla_chunk(q_ref, k_ref, v_ref, o_ref, b_all, st, c, reverse):
    same_head = _same_head_mask(C_HEADS * C_CHUNK, C_CHUNK)
    ti = lax.broadcasted_iota(jnp.int32, (C_CHUNK, C_HEADS * C_CHUNK), 0)
    si = lax.broadcasted_iota(jnp.int32, (C_CHUNK, C_HEADS * C_CHUNK), 1) % C_CHUNK
    tri = (si >= ti) if reverse else (si <= ti)
    zero = jnp.zeros((), BF16)
    rows = slice(c * C_CHUNK, (c + 1) * C_CHUNK)
    b = b_all[rows, :]
    b_tot = b[0:1, :] if reverse else b[C_CHUNK - 1:C_CHUNK, :]
    q_i = (q_ref[rows, :] * jnp.exp(b)).astype(BF16)
    kf = k_ref[rows, :]
    k_i = (kf * jnp.exp(-b)).astype(BF16)
    k_e = (kf * jnp.exp(b_tot - b)).astype(BF16)
    v = v_ref[rows, :]
    k_bd = jnp.where(same_head, jnp.concatenate([k_i] * C_HEADS, axis=0), zero)
    v_bd = jnp.where(same_head, jnp.concatenate([v] * C_HEADS, axis=0), zero)
    a = jnp.where(tri, _nt_dot(q_i, k_bd), 0.0)
    o_ref[rows, :] = _dot(a.astype(BF16), v_bd) + _nt_dot(q_i, st.astype(BF16))
    return st * jnp.exp(b_tot) + jnp.where(same_head, _tn_dot(v, k_e), 0.0)


def _gla_kernel(qf_ref, kf_ref, vf_ref, laf_ref, qb_ref, kb_ref, vb_ref, lab_ref, of_ref, ob_ref, sf_ref, sb_ref):
    @pl.when(pl.program_id(1) == 0)
    def _():
        sf_ref[...] = jnp.zeros_like(sf_ref)
        sb_ref[...] = jnp.zeros_like(sb_ref)

    n_chunks = qf_ref.shape[0] // C_CHUNK
    b_f = _gla_decay(laf_ref, reverse=False)
    b_b = _gla_decay(lab_ref, reverse=True)
    s_f, s_b = sf_ref[...], sb_ref[...]
    for c in range(n_chunks):
        s_f = _gla_chunk(qf_ref, kf_ref, vf_ref, of_ref, b_f, s_f, c, reverse=False)
        s_b = _gla_chunk(qb_ref, kb_ref, vb_ref, ob_ref, b_b, s_b, n_chunks - 1 - c, reverse=True)
    sf_ref[...] = s_f
    sb_ref[...] = s_b


def _gla(cq, ck, cv, la, bsz, seq):
    tb = min(GLA_BLOCK, seq)
    nb = seq // tb
    fwd = lambda col: pl.BlockSpec((tb, C_WIDTH), lambda b, i: (b * nb + i, col))
    bwd = lambda col: pl.BlockSpec((tb, C_WIDTH), lambda b, i: (b * nb + nb - 1 - i, col))
    out = jax.ShapeDtypeStruct((bsz * seq, C_WIDTH), F32)
    state = pltpu.VMEM((C_WIDTH, C_WIDTH), F32)
    return pl.pallas_call(
        _gla_kernel,
        grid=(bsz, nb),
        in_specs=[fwd(0), fwd(0), fwd(0), fwd(0), bwd(0), bwd(0), bwd(0), bwd(1)],
        out_specs=[fwd(0), bwd(0)],
        out_shape=[out, out],
        scratch_shapes=[state, state],
        compiler_params=_cparams(("parallel", "arbitrary")),
        name="gla",
    )(cq, ck, cv, la, cq, ck, cv, la)


def _out_proj_kernel(x_ref, oa_ref, ob_ref, of_ref, obw_ref, cg_ref, cng_ref, wo_ref, g2_ref, wr_ref,
                     x1_ref, h2_ref, aff_ref, mix_ref):
    gi = lax.broadcasted_iota(jnp.int32, (C_WIDTH, C_WIDTH), 0) // HEAD_DIM
    gj = lax.broadcasted_iota(jnp.int32, (C_WIDTH, C_WIDTH), 1) // HEAD_DIM
    ones_bd = (gi == gj).astype(BF16)
    half = x_ref.shape[0] // 2
    for rows in (slice(0, half), slice(half, 2 * half)):
        o = of_ref[rows, :] + obw_ref[rows, :]
        sq_hi, sq_lo = _split_bf16(o * o)
        ms = (_dot(sq_hi, ones_bd) + _dot(sq_lo, ones_bd)) * (1.0 / HEAD_DIM)
        cg = cg_ref[rows, :]
        oc = (o * lax.rsqrt(ms + EPS) * cng_ref[...]) * (cg / (1.0 + jnp.exp(-cg)))
        mix_ref[rows, 0:A_WIDTH] = oa_ref[rows, :]
        mix_ref[rows, A_WIDTH:A_WIDTH + B_WIDTH] = ob_ref[rows, :]
        mix_ref[rows, A_WIDTH + B_WIDTH:] = oc.astype(BF16)
        x1 = x_ref[rows, :] + _dot(mix_ref[rows, :], wo_ref[...])
        x1_ref[rows, :] = x1
        h2 = _rms(x1, g2_ref[...]).astype(BF16)
        h2_ref[rows, :] = h2
        logits = _dot(h2, wr_ref[...])
        lane = lax.broadcasted_iota(jnp.int32, logits.shape, 1)
        logits = jnp.where(lane < N_EXPERTS, logits, NEG_INF)
        p = jnp.exp(logits - jnp.max(logits, axis=1, keepdims=True))
        aff = p / jnp.sum(p, axis=1, keepdims=True)
        aff_ref[:, rows] = aff.T[0:N_EXPERTS, :]


def _out_proj(x, oa, ob, o_f, o_b, cg, cng, wo_bf, layer, g2, wr_cat):
    n = x.shape[0]
    tm = min(OUT_PROJ_ROWS, n)
    row = lambda width: pl.BlockSpec((tm, width), lambda i: (i, 0))
    full = lambda a: pl.BlockSpec(a.shape, lambda i: (0,) * a.ndim)
    return pl.pallas_call(
        _out_proj_kernel,
        grid=(n // tm,),
        in_specs=[row(D_MODEL), row(A_WIDTH), row(B_WIDTH), row(C_WIDTH), row(C_WIDTH), row(C_WIDTH),
                  full(cng), pl.BlockSpec((None, D_MODEL, D_MODEL), lambda i: (layer, 0, 0)), full(g2), full(wr_cat)],
        out_specs=[row(D_MODEL), row(D_MODEL), pl.BlockSpec((N_EXPERTS, tm), lambda i: (0, i))],
        out_shape=[jax.ShapeDtypeStruct((n, D_MODEL), F32), jax.ShapeDtypeStruct((n, D_MODEL), BF16),
                   jax.ShapeDtypeStruct((N_EXPERTS, n), F32)],
        scratch_shapes=[pltpu.VMEM((tm, D_MODEL), BF16)],
        compiler_params=_cparams(("parallel",)),
        name="out_proj",
    )(x, oa, ob, o_f, o_b, cg, cng, wo_bf, g2, wr_cat)


def _route_kernel(aff_ref, pos_ref, offs_ref, *, cap):
    n = aff_ref.shape[1]
    nt = n // ROUTE_TILE
    bits = pltpu.bitcast(aff_ref[...], jnp.int32)

    def search(it, thr):
        cand = thr | jnp.left_shift(jnp.int32(1), 30 - it)
        cnt = jnp.sum(jnp.where(bits >= cand, 1.0, 0.0), axis=1, keepdims=True)
        return jnp.where(cnt >= cap, cand, thr)

    thr = lax.fori_loop(0, 31, search, jnp.zeros((N_EXPERTS, 1), jnp.int32))
    n_gt = jnp.sum(jnp.where(bits > thr, 1.0, 0.0), axis=1, keepdims=True)
    n_tie = cap - n_gt

    ui = lax.broadcasted_iota(jnp.int32, (ROUTE_TILE, ROUTE_TILE), 0)
    uj = lax.broadcasted_iota(jnp.int32, (ROUTE_TILE, ROUTE_TILE), 1)
    upper = (ui <= uj).astype(BF16)
    lane = lax.broadcasted_iota(jnp.int32, offs_ref.shape, 1)

    def tile(i, carry):
        c_gt, c_eq, offs = carry
        col = pl.multiple_of(i * ROUTE_TILE, ROUTE_TILE)
        b = pltpu.bitcast(aff_ref[:, pl.ds(col, ROUTE_TILE)], jnp.int32)
        gt = jnp.where(b > thr, 1.0, 0.0)
        eq = jnp.where(b == thr, 1.0, 0.0)
        inc = _dot(jnp.concatenate([gt, eq], axis=0).astype(BF16), upper)
        gt_before = c_gt + inc[:N_EXPERTS] - gt
        eq_before = c_eq + inc[N_EXPERTS:] - eq
        sel = (gt > 0.0) | ((eq > 0.0) & (eq_before < n_tie))
        pos = gt_before + jnp.minimum(eq_before, n_tie)
        pos_ref[:, pl.ds(col, ROUTE_TILE)] = jnp.where(sel, pos, -1.0).astype(jnp.int32)
        start = c_gt + jnp.minimum(c_eq, n_tie)
        offs = jnp.where(lane == i, start.astype(jnp.int32), offs)
        return (c_gt + jnp.sum(gt, axis=1, keepdims=True), c_eq + jnp.sum(eq, axis=1, keepdims=True), offs)

    zero = jnp.zeros((N_EXPERTS, 1), F32)
    offs = jnp.where(lane == nt, cap, 0).astype(jnp.int32)
    _, _, offs = lax.fori_loop(0, nt, tile, (zero, zero, offs), unroll=4 if nt % 4 == 0 else 1)
    offs_ref[...] = offs


def _route(aff_t, cap):
    n = aff_t.shape[1]
    nt = n // ROUTE_TILE
    return pl.pallas_call(
        functools.partial(_route_kernel, cap=cap),
        out_shape=[jax.ShapeDtypeStruct((N_EXPERTS, n), jnp.int32),
                   jax.ShapeDtypeStruct((N_EXPERTS, nt + 1), jnp.int32)],
        compiler_params=pltpu.CompilerParams(vmem_limit_bytes=VMEM_LIMIT),
        name="route",
    )(aff_t)


def _align(v):
    return (v // MOE_ALIGN) * MOE_ALIGN


def _moe_passes(offs_ref, t, subs, x0, s, win):
    k = jnp.int32(1)
    for g in range(MOE_PAIR):
        need = offs_ref[x0 + g, t * subs + s + 1] - _align(offs_ref[x0 + g, t * subs + s])
        k = jnp.maximum(k, (need + win - 1) // win)
    return k


def _moe_windows(offs_ref, pos_ref, t, subs, x0, s, j, win):
    slot = lax.broadcasted_iota(jnp.int32, (win, ROUTE_TILE), 0)
    hits, rels = [], []
    for g in range(MOE_PAIR):
        x = x0 + g
        ws = _align(offs_ref[x, t * subs + s]) + j * win
        pos = pos_ref[pl.ds(x, 1), s * ROUTE_TILE:(s + 1) * ROUTE_TILE]
        hits.append((pos - ws) == slot)
        rels.append(pl.multiple_of(ws - _align(offs_ref[x, t * subs]), MOE_ALIGN))
    onehot = jnp.concatenate([jnp.where(h, 1.0, 0.0) for h in hits], axis=0).astype(BF16)
    return hits, rels, onehot


def _moe_chunks(offs_ref, t, subs, x):
    base = _align(offs_ref[x, t * subs])
    return base, (offs_ref[x, (t + 1) * subs] - base + MOE_CHUNK - 1) // MOE_CHUNK


def _moe_gather_kernel(offs_ref, h_ref, pos_ref, gate_ref, xe_ref, stage_ref, carry_ref, sem):
    t = pl.program_id(0)
    steps = pl.num_programs(1)
    e0 = pl.program_id(1) * MOE_STEP
    tt = h_ref.shape[0]
    subs = tt // ROUTE_TILE
    k = t * steps + pl.program_id(1)
    slot = k % 2
    x_ref = stage_ref.at[slot]

    def off(x, s):
        return offs_ref[x, t * subs + s]

    def copies(kk, half, act):
        step_t, step_q = kk // steps, kk % steps
        for g in range(MOE_STEP):
            x = step_q * MOE_STEP + g
            base, n_chunks = _moe_chunks(offs_ref, step_t, subs, x)

            def body(c, carry):
                r0 = pl.multiple_of(c * MOE_CHUNK, MOE_CHUNK)
                act(pltpu.make_async_copy(stage_ref.at[half, g, pl.ds(r0, MOE_CHUNK), :],
                                          xe_ref.at[x, pl.ds(pl.multiple_of(base + r0, MOE_ALIGN), MOE_CHUNK), :],
                                          sem.at[half]))
                return carry

            lax.fori_loop(0, n_chunks, body, 0)

    @pl.when(k >= 2)
    def _():
        copies(k - 2, slot, lambda cp: cp.wait())

    @pl.when(t == 0)
    def _():
        carry_ref[pl.ds(e0, MOE_STEP)] = jnp.zeros((MOE_STEP, MOE_ALIGN, XE_WIDTH), BF16)
        cap = xe_ref.shape[1] - MOE_CHUNK
        x_ref[0, 0:MOE_CHUNK, :] = jnp.zeros((MOE_CHUNK, XE_WIDTH), BF16)
        pads = [pltpu.make_async_copy(x_ref.at[0, 0:MOE_CHUNK, :], xe_ref.at[e0 + g, cap:cap + MOE_CHUNK, :],
                                      sem.at[slot]) for g in range(MOE_STEP)]
        for p in pads:
            p.start()
        for p in pads:
            p.wait()

    for g in range(MOE_STEP):
        x_ref[g, 0:MOE_ALIGN, :] = carry_ref[e0 + g]

    row = lax.broadcasted_iota(jnp.int32, (MOE_WIN, 1), 0)
    lane = lax.broadcasted_iota(jnp.int32, (MOE_WIN, XE_WIDTH - D_MODEL), 1)

    def gather_pass(pair, s, j):
        x0 = e0 + pair * MOE_PAIR
        cols = slice(s * ROUTE_TILE, (s + 1) * ROUTE_TILE)
        hits, rels, onehot = _moe_windows(offs_ref, pos_ref, t, subs, x0, s, j, MOE_WIN)
        r = _dot(onehot, h_ref[cols, :])
        for g in range(MOE_PAIR):
            gate = jnp.sum(jnp.where(hits[g], gate_ref[pl.ds(x0 + g, 1), cols], 0.0), axis=1, keepdims=True)
            g_hi = gate.astype(BF16).astype(F32)
            extra = jnp.where(lane == 0, g_hi, jnp.where(lane == 1, gate - g_hi, 0.0))
            new = jnp.concatenate([r[g * MOE_WIN:(g + 1) * MOE_WIN, :], extra], axis=1)
            ws = _align(off(x0 + g, s)) + j * MOE_WIN
            own = row >= off(x0 + g, s) - ws
            buf = pair * MOE_PAIR + g
            if isinstance(j, int):
                head = pl.ds(rels[g], MOE_ALIGN)
                x_ref[buf, head, :] = jnp.where(own[:MOE_ALIGN], new[:MOE_ALIGN],
                                                x_ref[buf, head, :].astype(F32)).astype(BF16)
                x_ref[buf, pl.ds(rels[g] + MOE_ALIGN, MOE_WIN - MOE_ALIGN), :] = new[MOE_ALIGN:].astype(BF16)
            else:
                own = own & (row < off(x0 + g, s + 1) - ws)
                win = pl.ds(rels[g], MOE_WIN)
                x_ref[buf, win, :] = jnp.where(own, new, x_ref[buf, win, :].astype(F32)).astype(BF16)

    for pair in range(MOE_STEP // MOE_PAIR):
        for s in range(subs):
            gather_pass(pair, s, 0)
    for pair in range(MOE_STEP // MOE_PAIR):
        for s in range(subs):
            lax.fori_loop(1, _moe_passes(offs_ref, t, subs, e0 + pair * MOE_PAIR, s, MOE_WIN),
                          lambda j, carry: (gather_pass(pair, s, j), carry)[1], 0)

    for g in range(MOE_STEP):
        x = e0 + g
        base, n_chunks = _moe_chunks(offs_ref, t, subs, x)
        last_group = pl.multiple_of(_align(off(x, subs)) - base, MOE_ALIGN)
        tail = x_ref[g, pl.ds(last_group, MOE_ALIGN), :].astype(F32)
        tail = jnp.where(row[:MOE_ALIGN] < off(x, subs) - base - last_group, tail, 0.0).astype(BF16)
        x_ref[g, pl.ds(last_group, MOE_ALIGN), :] = tail
        carry_ref[x] = tail

        def fill(i, carry):
            x_ref[g, pl.ds(pl.multiple_of(i * MOE_ALIGN, MOE_ALIGN), MOE_ALIGN), :] = jnp.zeros((MOE_ALIGN, XE_WIDTH), BF16)
            return carry

        lax.fori_loop(last_group // MOE_ALIGN + 1, n_chunks * MOE_CHUNK // MOE_ALIGN, fill, 0)

    copies(k, slot, lambda cp: cp.start())

    @pl.when(k == pl.num_programs(0) * steps - 1)
    def _():
        copies(k - 1, 1 - slot, lambda cp: cp.wait())
        copies(k, slot, lambda cp: cp.wait())


def _moe_ffn_kernel(x_ref, wg_ref, wu_ref, wd_ref, y_ref, wg_bf, wu_bf, wd_bf):
    j = pl.program_id(1)

    @pl.when(j == 0)
    def _():
        wg_bf[...] = wg_ref[...].astype(BF16)
        wu_bf[...] = wu_ref[...].astype(BF16)
        wd_bf[...] = wd_ref[...].astype(BF16)
        y_ref[0] = jnp.zeros(y_ref.shape[1:], BF16)

    @pl.when(j > 0)
    def _():
        x = x_ref[0, :, 0:D_MODEL]
        gate = jnp.sum(x_ref[0, :, D_MODEL:XE_WIDTH].astype(F32), axis=1, keepdims=True)
        hg = _dot(x, wg_bf[...])
        hu = _dot(x, wu_bf[...])
        act = (hg / (1.0 + jnp.exp(-hg))) * hu * gate
        y_ref[0] = _dot(act.astype(BF16), wd_bf[...]).astype(BF16)


def _moe_combine_kernel(offs_ref, x1_ref, pos_ref, y_ref, fg_ref, o_ref, y_buf, sem, *, final_norm):
    t = pl.program_id(0)
    q = pl.program_id(1)
    steps = pl.num_programs(1)
    e0 = q * MOE_STEP
    tt = x1_ref.shape[0]
    subs = tt // ROUTE_TILE
    k = t * steps + q
    slot = k % 2

    def fetch(kk, half, act):
        step_t, step_q = kk // steps, kk % steps
        for g in range(MOE_STEP):
            x = step_q * MOE_STEP + g
            base, n_chunks = _moe_chunks(offs_ref, step_t, subs, x)

            def body(c, carry):
                r0 = pl.multiple_of(c * MOE_CHUNK, MOE_CHUNK)
                act(pltpu.make_async_copy(y_ref.at[x, pl.ds(pl.multiple_of(base + r0, MOE_ALIGN), MOE_CHUNK), :],
                                          y_buf.at[half, g, pl.ds(r0, MOE_CHUNK), :], sem.at[half]))
                return carry

            lax.fori_loop(0, n_chunks, body, 0)

    @pl.when(k == 0)
    def _():
        y_buf[...] = jnp.zeros_like(y_buf)
        fetch(k, slot, lambda cp: cp.start())

    @pl.when(k + 1 < pl.num_programs(0) * steps)
    def _():
        fetch(k + 1, 1 - slot, lambda cp: cp.start())

    fetch(k, slot, lambda cp: cp.wait())

    @pl.when(q == 0)
    def _():
        o_ref[...] = x1_ref[...]

    def combine_pass(pair, s, j):
        cols = slice(s * ROUTE_TILE, (s + 1) * ROUTE_TILE)
        _, rels, onehot = _moe_windows(offs_ref, pos_ref, t, subs, e0 + pair * MOE_PAIR, s, j, MOE_COMBINE_WIN)
        ycat = jnp.concatenate([y_buf[slot, pair * MOE_PAIR + g, pl.ds(rels[g], MOE_COMBINE_WIN), :]
                                for g in range(MOE_PAIR)], axis=0)
        o_ref[cols, :] += _tn_dot(onehot, ycat)

    for pair in range(MOE_STEP // MOE_PAIR):
        for s in range(subs):
            combine_pass(pair, s, 0)
    for pair in range(MOE_STEP // MOE_PAIR):
        for s in range(subs):
            lax.fori_loop(1, _moe_passes(offs_ref, t, subs, e0 + pair * MOE_PAIR, s, MOE_COMBINE_WIN),
                          lambda j, carry: (combine_pass(pair, s, j), carry)[1], 0)

    if final_norm:
        @pl.when(q == pl.num_programs(1) - 1)
        def _():
            o_ref[...] = _rms(o_ref[...], fg_ref[...])


def _moe(offs, h2, x1, pos, gate, wg, wu, wd, layer, final_g, final_norm, cap):
    n = h2.shape[0]
    tt, tc = min(MOE_TILE, n), min(MOE_COMBINE_TILE, n)
    assert cap % MOE_CHUNK == 0 and n % tt == 0 and n % tc == 0
    steps = N_EXPERTS // MOE_STEP
    stage_rows = lambda tile, win: tile + MOE_ALIGN + win * pl.cdiv(ROUTE_TILE + MOE_ALIGN, win)
    tok = lambda tile: pl.BlockSpec((tile, D_MODEL), lambda t, q, offs: (t, 0))
    per_tok = lambda tile: pl.BlockSpec((N_EXPERTS, tile), lambda t, q, offs: (0, t))
    hbm = pl.BlockSpec(memory_space=pl.ANY)
    sems = ("arbitrary", "arbitrary")

    xe = pl.pallas_call(
        _moe_gather_kernel,
        grid_spec=pltpu.PrefetchScalarGridSpec(
            num_scalar_prefetch=1, grid=(n // tt, steps),
            in_specs=[tok(tt), per_tok(tt), per_tok(tt)], out_specs=hbm,
            scratch_shapes=[pltpu.VMEM((2, MOE_STEP, stage_rows(tt, MOE_WIN), XE_WIDTH), BF16),
                            pltpu.VMEM((N_EXPERTS, MOE_ALIGN, XE_WIDTH), BF16), pltpu.SemaphoreType.DMA((2,))],
        ),
        out_shape=jax.ShapeDtypeStruct((N_EXPERTS, cap + MOE_CHUNK, XE_WIDTH), BF16),
        compiler_params=_cparams(sems),
        name="moe_gather",
    )(offs, h2, pos, gate)

    fb = min(MOE_FFN_ROWS, cap)
    nb = cap // fb
    wspec = pl.BlockSpec((None, None, D_MODEL, D_MODEL), lambda e, j: (layer, e, 0, 0))
    w_bf = pltpu.VMEM((D_MODEL, D_MODEL), BF16)
    y = pl.pallas_call(
        _moe_ffn_kernel,
        grid=(N_EXPERTS, nb + 1),
        in_specs=[pl.BlockSpec((1, fb, XE_WIDTH), lambda e, j: (e, jnp.maximum(j - 1, 0), 0)), wspec, wspec, wspec],
        out_specs=pl.BlockSpec((1, fb, D_MODEL), lambda e, j: (e, jnp.where(j == 0, nb, j - 1), 0)),
        out_shape=jax.ShapeDtypeStruct((N_EXPERTS, cap + fb, D_MODEL), BF16),
        scratch_shapes=[w_bf, w_bf, w_bf],
        compiler_params=_cparams(("parallel", "arbitrary")),
        name="moe_ffn",
    )(xe, wg, wu, wd)

    return pl.pallas_call(
        functools.partial(_moe_combine_kernel, final_norm=final_norm),
        grid_spec=pltpu.PrefetchScalarGridSpec(
            num_scalar_prefetch=1, grid=(n // tc, steps),
            in_specs=[tok(tc), per_tok(tc), hbm, pl.BlockSpec((1, D_MODEL), lambda t, q, offs: (0, 0))],
            out_specs=tok(tc),
            scratch_shapes=[pltpu.VMEM((2, MOE_STEP, stage_rows(tc, MOE_COMBINE_WIN), D_MODEL), BF16),
                            pltpu.SemaphoreType.DMA((2,))],
        ),
        out_shape=jax.ShapeDtypeStruct((n, D_MODEL), F32),
        compiler_params=_cparams(sems),
        name="moe_combine",
    )(offs, x1, pos, y, final_g)


def _prep_layer(l, norm1_g, w_in, a_sink, a_norm_g, b_rel_bias, b_norm_g, c_alpha_w2_f, c_alpha_b_f,
                c_alpha_w2_b, c_alpha_b_b, c_norm_g, w_out, norm2_g, w_router, w_gate, w_up, w_down):
    z = jnp.zeros((C_LOWRANK, C_WIDTH), F32)
    w2 = jnp.concatenate([jnp.concatenate([c_alpha_w2_f[l], z], axis=1),
                          jnp.concatenate([z, c_alpha_w2_b[l]], axis=1)], axis=0).astype(BF16)
    b2 = jnp.concatenate([c_alpha_b_f[l], c_alpha_b_b[l]])[None, :]
    wr_cat = jnp.pad(w_router[l].astype(BF16), ((0, 0), (0, LANES - N_EXPERTS)))
    return dict(
        layer=l, g1=norm1_g[l][None, :], w_in=w_in.astype(BF16), w2=w2, b2=b2,
        sink=a_sink[l], a_g=a_norm_g[l][None, :], bias=_bias_table(b_rel_bias[l]), b_g=b_norm_g[l][None, :],
        c_g=c_norm_g[l][None, :], w_out=w_out.astype(BF16), g2=norm2_g[l][None, :],
        wr_cat=wr_cat, wg=w_gate, wu=w_up, wd=w_down)


def _layer(x, p, bsz, seq, rope, final_g, final_norm):
    n = bsz * seq
    l = p["layer"]
    aq, ak, av, bq, bk, bv, cq, ck, cv, cg, la = _in_proj(x, p["g1"], p["w_in"], l, p["w2"], p["b2"], seq, rope)
    oa = _attn_a(aq, ak, av, p["sink"], p["a_g"], bsz, seq)
    ob = _attn_b(bq, bk, bv, p["bias"], p["b_g"], bsz, seq)
    o_f, o_b = _gla(cq, ck, cv, la, bsz, seq)
    x1, h2, aff = _out_proj(x, oa, ob, o_f, o_b, cg, p["c_g"], p["w_out"], l, p["g2"], p["wr_cat"])
    cap = EC_CAPACITY * n // N_EXPERTS
    pos, offs = _route(aff, cap)
    return _moe(offs, h2, x1, pos, aff, p["wg"], p["wu"], p["wd"], l, final_g, final_norm, cap)


def _trunk(x, layers, rope, final_g):
    bsz, seq, _ = x.shape
    y = x.reshape(bsz * seq, D_MODEL)
    for l, p in enumerate(layers):
        y = _layer(y, p, bsz, seq, rope, final_g, final_norm=(l == len(layers) - 1))
    return y.reshape(bsz, seq, D_MODEL)


def kernel(x_prompt, x_sample, norm1_g, w_in, a_sink, a_norm_g, b_rel_bias, b_norm_g, c_alpha_w2_f, c_alpha_b_f,
           c_alpha_w2_b, c_alpha_b_b, c_norm_g, w_out, norm2_g, w_router, w_gate, w_up, w_down, final_g):
    depth = w_in.shape[0]
    layers = [_prep_layer(l, norm1_g, w_in, a_sink, a_norm_g, b_rel_bias, b_norm_g, c_alpha_w2_f, c_alpha_b_f,
                          c_alpha_w2_b, c_alpha_b_b, c_norm_g, w_out, norm2_g, w_router, w_gate, w_up, w_down)
              for l in range(depth)]
    fg = final_g[None, :]
    rope = _rope_tables(max(x_prompt.shape[1], x_sample.shape[1]))
    return _trunk(x_prompt, layers, rope, fg), _trunk(x_sample, layers, rope, fg)
```

```python
import functools

import jax
import jax.numpy as jnp
from jax import lax
from jax.experimental import pallas as pl
from jax.experimental.pallas import tpu as pltpu

F32 = jnp.float32
BF16 = jnp.bfloat16

D_MODEL = 1024
HEAD_DIM = 64
A_WIDTH = 512
A_HEADS = 8
A_KV_HEADS = 2
A_KV_WIDTH = 128
WINDOW = 128
ROPE_THETA = 500000.0
ROPE_DIM = 16
B_WIDTH = 256
B_HEADS = 4
GRID_W = 64
WIN_H = 8
WIN_W = 16
C_WIDTH = 256
C_HEADS = 4
C_LOWRANK = 16
C_TAU = 16.0
C_CHUNK = 64
N_EXPERTS = 16
EC_CAPACITY = 2
EPS = 1e-6
NEG_INF = -1e30
IN_WIDTH = 2592

_OFF_AQ, _OFF_AK, _OFF_AV = 0, 512, 640
_OFF_BQ, _OFF_BK, _OFF_BV = 768, 1024, 1280
_OFF_CQ, _OFF_CK, _OFF_CV, _OFF_CG = 1536, 1792, 2048, 2304
_OFF_LR = 2560

LANES = 128
VMEM_LIMIT = 56 * 1024 * 1024

IN_PROJ_ROWS = 1024
OUT_PROJ_ROWS = 1024
GLA_BLOCK = 1024
GLA_CUM_ROWS = 256
ROUTE_TILE = 256
MOE_TILE = 2048
MOE_COMBINE_TILE = 1024
MOE_STEP = 4
MOE_PAIR = 4
MOE_WIN = 64
MOE_COMBINE_WIN = 128
MOE_ALIGN = 16
MOE_CHUNK = 128
MOE_FFN_ROWS = 1024
XE_WIDTH = D_MODEL + LANES


def _cparams(sem):
    return pltpu.CompilerParams(dimension_semantics=sem, vmem_limit_bytes=VMEM_LIMIT)


def _nt_dot(a, b):
    return lax.dot_general(a, b, (((1,), (1,)), ((), ())), preferred_element_type=F32)


def _tn_dot(a, b):
    return lax.dot_general(a, b, (((0,), (0,)), ((), ())), preferred_element_type=F32)


def _dot(a, b):
    return jnp.dot(a, b, preferred_element_type=F32)


def _rms(x, g):
    return x * lax.rsqrt(jnp.mean(x * x, axis=-1, keepdims=True) + EPS) * g


def _same_head_mask(n_rows, rows_per_head):
    width = (n_rows // rows_per_head) * HEAD_DIM
    ri = lax.broadcasted_iota(jnp.int32, (n_rows, width), 0) // rows_per_head
    li = lax.broadcasted_iota(jnp.int32, (n_rows, width), 1) // HEAD_DIM
    return ri == li


def _split_bf16(x):
    hi = x.astype(BF16)
    return hi, (x - hi.astype(F32)).astype(BF16)


def _in_proj_kernel(x_ref, g_ref, w_ref, cos_ref, s1_ref, s2_ref, w2_ref, b2_ref,
                    aq_ref, ak_ref, av_ref, bq_ref, bk_ref, bv_ref,
                    cq_ref, ck_ref, cv_ref, cg_ref, la_ref):
    h = _rms(x_ref[...], g_ref[...]).astype(BF16)
    y = _dot(h, w_ref[...])

    def proj(lo, width):
        return y[:, lo:lo + width]

    cos, s1, s2 = cos_ref[...], s1_ref[...], s2_ref[...]

    def rope(t):
        return t * cos + pltpu.roll(t, LANES - ROPE_DIM // 2, 1) * s1 + pltpu.roll(t, ROPE_DIM // 2, 1) * s2

    scale = HEAD_DIM ** -0.5
    aq = proj(_OFF_AQ, A_WIDTH)
    for c in range(A_WIDTH // LANES):
        aq_ref[:, c * LANES:(c + 1) * LANES] = (rope(aq[:, c * LANES:(c + 1) * LANES]) * scale).astype(BF16)
    ak_ref[...] = rope(proj(_OFF_AK, A_KV_WIDTH)).astype(BF16)
    av_ref[...] = proj(_OFF_AV, A_KV_WIDTH).astype(BF16)
    bq_ref[...] = (proj(_OFF_BQ, B_WIDTH) * scale).astype(BF16)
    bk_ref[...] = proj(_OFF_BK, B_WIDTH).astype(BF16)
    bv_ref[...] = proj(_OFF_BV, B_WIDTH).astype(BF16)
    cq_ref[...] = proj(_OFF_CQ, C_WIDTH) * scale
    ck_ref[...] = proj(_OFF_CK, C_WIDTH)
    cv_ref[...] = proj(_OFF_CV, C_WIDTH).astype(BF16)
    cg_ref[...] = proj(_OFF_CG, C_WIDTH)
    z = _dot(proj(_OFF_LR, 2 * C_LOWRANK).astype(BF16), w2_ref[...]) + b2_ref[...]
    la_ref[...] = (jnp.minimum(z, 0.0) - jnp.log(1.0 + jnp.exp(-jnp.abs(z)))) * (1.0 / C_TAU)


def _rope_tables(seq):
    half = ROPE_DIM // 2
    inv = jnp.power(jnp.float32(ROPE_THETA), -jnp.arange(half, dtype=F32) * (2.0 / ROPE_DIM))
    d = jnp.arange(LANES) % HEAD_DIM
    ang = jnp.arange(seq, dtype=F32)[:, None] * inv[d % half][None, :]
    cos, sin = jnp.cos(ang), jnp.sin(ang)
    first, second = (d < half)[None, :], ((d >= half) & (d < ROPE_DIM))[None, :]
    return (jnp.where(first | second, cos, 1.0), jnp.where(first, -sin, 0.0), jnp.where(second, sin, 0.0))


def _in_proj(x, g1, w_in_bf, layer, w2, b2, seq, rope):
    n = x.shape[0]
    tm = min(IN_PROJ_ROWS, seq)
    per_seq = seq // tm
    cos, s1, s2 = rope
    row = lambda width: pl.BlockSpec((tm, width), lambda i: (i, 0))
    full = lambda a: pl.BlockSpec(a.shape, lambda i: (0,) * a.ndim)
    tab = pl.BlockSpec((tm, LANES), lambda i: (i % per_seq, 0))
    widths = [(A_WIDTH, BF16), (A_KV_WIDTH, BF16), (A_KV_WIDTH, BF16), (B_WIDTH, BF16), (B_WIDTH, BF16),
              (B_WIDTH, BF16), (C_WIDTH, F32), (C_WIDTH, F32), (C_WIDTH, BF16), (C_WIDTH, F32), (2 * C_WIDTH, F32)]
    return pl.pallas_call(
        _in_proj_kernel,
        grid=(n // tm,),
        in_specs=[row(D_MODEL), full(g1), pl.BlockSpec((None, D_MODEL, IN_WIDTH), lambda i: (layer, 0, 0)),
                  tab, tab, tab, full(w2), full(b2)],
        out_specs=[row(w) for w, _ in widths],
        out_shape=[jax.ShapeDtypeStruct((n, w), dt) for w, dt in widths],
        compiler_params=_cparams(("parallel",)),
        name="in_proj",
    )(x, g1, w_in_bf, cos, s1, s2, w2, b2)


A_QBLOCKS = 4


def _attn_a_kernel(sink_ref, q_ref, kp_ref, kc_ref, kn_ref, vp_ref, vc_ref, vn_ref, g_ref, o_ref, acc_ref, *, steps):
    i = pl.program_id(1)
    grp = A_HEADS // A_KV_HEADS
    k = jnp.concatenate([kp_ref[...], kc_ref[...], kn_ref[...]], axis=0)
    v = jnp.concatenate([vp_ref[...], vc_ref[...], vn_ref[...]], axis=0)
    qq = lax.broadcasted_iota(jnp.int32, (WINDOW, 3 * WINDOW), 0)
    kk = lax.broadcasted_iota(jnp.int32, (WINDOW, 3 * WINDOW), 1)
    band = jnp.abs(kk - WINDOW - qq) <= WINDOW
    for blk in range(A_QBLOCKS):
        visible = band
        if blk == 0:
            visible = visible & jnp.logical_not((kk < WINDOW) & (i == 0))
        if blk == A_QBLOCKS - 1:
            visible = visible & jnp.logical_not((kk >= 2 * WINDOW) & (i == steps - 1))
        bias = jnp.where(visible, 0.0, NEG_INF)
        rows = slice(blk * WINDOW, (blk + 1) * WINDOW)
        keys = slice(blk * WINDOW, (blk + 3) * WINDOW)
        for kv in range(A_KV_HEADS):
            heads = range(kv * grp, (kv + 1) * grp)
            q = jnp.concatenate([q_ref[rows, h * HEAD_DIM:(h + 1) * HEAD_DIM] for h in heads], axis=0)
            s = _nt_dot(q, k[keys, kv * HEAD_DIM:(kv + 1) * HEAD_DIM])
            ps, dens = [], []
            for g_i, h in enumerate(heads):
                sh = s[g_i * WINDOW:(g_i + 1) * WINDOW, :] + bias
                sink = sink_ref[h]
                m = jnp.maximum(jnp.max(sh, axis=-1, keepdims=True), sink)
                p = jnp.exp(sh - m)
                dens.append(jnp.sum(p, axis=-1, keepdims=True) + jnp.exp(sink - m))
                ps.append(p.astype(BF16))
            o = _dot(jnp.concatenate(ps, axis=0), v[keys, kv * HEAD_DIM:(kv + 1) * HEAD_DIM])
            for g_i, h in enumerate(heads):
                acc_ref[rows, h * HEAD_DIM:(h + 1) * HEAD_DIM] = o[g_i * WINDOW:(g_i + 1) * WINDOW, :] / dens[g_i]
    o_ref[...] = _rms(acc_ref[...], g_ref[...]).astype(BF16)


def _attn_a(aq, ak, av, sink, g, bsz, seq):
    nb = seq // WINDOW
    assert nb % A_QBLOCKS == 0
    steps = nb // A_QBLOCKS
    qrows = A_QBLOCKS * WINDOW
    qspec = pl.BlockSpec((qrows, A_WIDTH), lambda b, i: (b * steps + i, 0))
    kprev = pl.BlockSpec((WINDOW, A_KV_WIDTH), lambda b, i: (b * nb + jnp.maximum(i * A_QBLOCKS - 1, 0), 0))
    kcur = pl.BlockSpec((qrows, A_KV_WIDTH), lambda b, i: (b * steps + i, 0))
    knext = pl.BlockSpec((WINDOW, A_KV_WIDTH), lambda b, i: (b * nb + jnp.minimum((i + 1) * A_QBLOCKS, nb - 1), 0))
    return pl.pallas_call(
        functools.partial(_attn_a_kernel, steps=steps),
        grid=(bsz, steps),
        in_specs=[pl.BlockSpec(memory_space=pltpu.SMEM), qspec, kprev, kcur, knext, kprev, kcur, knext,
                  pl.BlockSpec((1, A_WIDTH), lambda b, i: (0, 0))],
        out_specs=qspec,
        out_shape=jax.ShapeDtypeStruct((bsz * seq, A_WIDTH), BF16),
        scratch_shapes=[pltpu.VMEM((qrows, A_WIDTH), F32)],
        compiler_params=_cparams(("parallel", "parallel")),
        name="attn_a",
    )(sink, aq, ak, ak, ak, av, av, av, g)


B_GROUP = 16
B_KEYS = WIN_H * GRID_W


def _bias_table_kernel(rel_ref, o_ref):
    c = lax.broadcasted_iota(jnp.int32, (GRID_W, GRID_W), 0)
    w = lax.broadcasted_iota(jnp.int32, (GRID_W, GRID_W), 1)
    cstart = jnp.clip(c - WIN_W // 2, 0, GRID_W - WIN_W)
    colmask = (w >= cstart) & (w < cstart + WIN_W)
    col_off = jnp.clip(w - c + (WIN_W - 1), 0, 2 * WIN_W - 2)
    n_ro, n_co = 2 * WIN_H - 1, 2 * WIN_W - 1
    for h in range(B_HEADS):
        for ro in range(n_ro):
            def body(j, acc):
                return jnp.where(col_off == j, rel_ref[(h * n_ro + ro) * n_co + j], acc)
            t = lax.fori_loop(0, n_co, body, jnp.zeros((GRID_W, GRID_W), F32), unroll=True)
            t = jnp.where(colmask, t, NEG_INF)
            for p in range(WIN_H):
                kidx = ro - (WIN_H - 1) + p
                if 0 <= kidx < WIN_H:
                    o_ref[p, h, :, kidx * GRID_W:(kidx + 1) * GRID_W] = t


def _bias_table(rel_bias):
    return pl.pallas_call(
        _bias_table_kernel,
        in_specs=[pl.BlockSpec(memory_space=pltpu.SMEM)],
        out_shape=jax.ShapeDtypeStruct((WIN_H, B_HEADS, GRID_W, B_KEYS), F32),
        name="bias_table",
    )(rel_bias.reshape(-1)).reshape(WIN_H, B_HEADS * GRID_W, B_KEYS)


def _attn_b_kernel(q_ref, kp_ref, kc_ref, kn_ref, vp_ref, vc_ref, vn_ref, bias_ref, g_ref, o_ref,
                   kbuf, vbuf, acc_ref, *, rows):
    gidx = pl.program_id(1)
    blk = B_GROUP * GRID_W
    kbuf[0:blk, :] = kp_ref[...]
    kbuf[blk:2 * blk, :] = kc_ref[...]
    kbuf[2 * blk:3 * blk, :] = kn_ref[...]
    vbuf[0:blk, :] = vp_ref[...]
    vbuf[blk:2 * blk, :] = vc_ref[...]
    vbuf[2 * blk:3 * blk, :] = vn_ref[...]
    same_head = _same_head_mask(B_HEADS * GRID_W, GRID_W)
    for j in range(B_GROUP):
        r = gidx * B_GROUP + j
        start = jnp.clip(r - WIN_H // 2, 0, rows - WIN_H)
        pat = r - start
        loc = pl.multiple_of((start - gidx * B_GROUP + B_GROUP) * GRID_W, GRID_W)
        kw = kbuf[pl.ds(loc, B_KEYS), :]
        vw = vbuf[pl.ds(loc, B_KEYS), :]
        qj = q_ref[j * GRID_W:(j + 1) * GRID_W, :]
        q = jnp.where(same_head, jnp.concatenate([qj] * B_HEADS, axis=0), jnp.zeros((), BF16))
        s = _nt_dot(q, kw) + bias_ref[pat]
        m = jnp.max(s, axis=-1, keepdims=True)
        p = jnp.exp(s - m)
        den = jnp.sum(p, axis=-1, keepdims=True)
        o = jnp.where(same_head, _dot(p.astype(BF16), vw) / den, 0.0)
        acc_ref[j * GRID_W:(j + 1) * GRID_W, :] = sum(o[h * GRID_W:(h + 1) * GRID_W, :] for h in range(B_HEADS))
    o_ref[...] = _rms(acc_ref[...], g_ref[...]).astype(BF16)


def _attn_b(bq, bk, bv, bias, g, bsz, seq):
    rows = seq // GRID_W
    assert rows % B_GROUP == 0 and rows >= WIN_H
    ng = rows // B_GROUP
    blk = B_GROUP * GRID_W
    cur = pl.BlockSpec((blk, B_WIDTH), lambda b, i: (b * ng + i, 0))
    prev = pl.BlockSpec((blk, B_WIDTH), lambda b, i: (b * ng + jnp.maximum(i - 1, 0), 0))
    nxt = pl.BlockSpec((blk, B_WIDTH), lambda b, i: (b * ng + jnp.minimum(i + 1, ng - 1), 0))
    return pl.pallas_call(
        functools.partial(_attn_b_kernel, rows=rows),
        grid=(bsz, ng),
        in_specs=[cur, prev, cur, nxt, prev, cur, nxt,
                  pl.BlockSpec(bias.shape, lambda b, i: (0, 0, 0)),
                  pl.BlockSpec((1, B_WIDTH), lambda b, i: (0, 0))],
        out_specs=cur,
        out_shape=jax.ShapeDtypeStruct((bsz * seq, B_WIDTH), BF16),
        scratch_shapes=[pltpu.VMEM((3 * blk, B_WIDTH), BF16), pltpu.VMEM((3 * blk, B_WIDTH), BF16),
                        pltpu.VMEM((blk, B_WIDTH), F32)],
        compiler_params=_cparams(("parallel", "parallel")),
        name="attn_b",
    )(bq, bk, bk, bk, bv, bv, bv, bias, g)


def _gla_decay(la_ref, reverse):
    seg = min(GLA_CUM_ROWS, la_ref.shape[0])
    bi = lax.broadcasted_iota(jnp.int32, (seg, seg), 0)
    bj = lax.broadcasted_iota(jnp.int32, (seg, seg), 1)
    in_chunk = (bi // C_CHUNK) == (bj // C_CHUNK)
    cum_w = (in_chunk & ((bj >= bi) if reverse else (bj <= bi))).astype(BF16)
    parts = []
    for r in range(0, la_ref.shape[0], seg):
        la_hi, la_lo = _split_bf16(la_ref[r:r + seg, :])
        parts.append(_dot(cum_w, la_hi) + _dot(cum_w, la_lo))
    return parts[0] if len(parts) == 1 else jnp.concatenate(parts, axis=0)


def _gla_chunk(q_ref, k_ref, v_ref, o_ref, b_all, st, c, reverse):
    same_head = _same_head_mask(C_HEADS * C_CHUNK, C_CHUNK)
    ti = lax.broadcasted_iota(jnp.int32, (C_CHUNK, C_HEADS * C_CHUNK), 0)
    si = lax.broadcasted_iota(jnp.int32, (C_CHUNK, C_HEADS * C_CHUNK), 1) % C_CHUNK
    tri = (si >= ti) if reverse else (si <= ti)
    zero = jnp.zeros((), BF16)
    rows = slice(c * C_CHUNK, (c + 1) * C_CHUNK)
    b = b_all[rows, :]
    b_tot = b[0:1, :] if reverse else b[C_CHUNK - 1:C_CHUNK, :]
    q_i = (q_ref[rows, :] * jnp.exp(b)).astype(BF16)
    kf = k_ref[rows, :]
    k_i = (kf * jnp.exp(-b)).astype(BF16)
    k_e = (kf * jnp.exp(b_tot - b)).astype(BF16)
    v = v_ref[rows, :]
    k_bd = jnp.where(same_head, jnp.concatenate([k_i] * C_HEADS, axis=0), zero)
    v_bd = jnp.where(same_head, jnp.concatenate([v] * C_HEADS, axis=0), zero)
    a = jnp.where(tri, _nt_dot(q_i, k_bd), 0.0)
    o_ref[rows, :] = _dot(a.astype(BF16), v_bd) + _nt_dot(q_i, st.astype(BF16))
    return st * jnp.exp(b_tot) + jnp.where(same_head, _tn_dot(v, k_e), 0.0)


def _gla_kernel(qf_ref, kf_ref, vf_ref, laf_ref, qb_ref, kb_ref, vb_ref, lab_ref, of_ref, ob_ref, sf_ref, sb_ref):
    @pl.when(pl.program_id(1) == 0)
    def _():
        sf_ref[...] = jnp.zeros_like(sf_ref)
        sb_ref[...] = jnp.zeros_like(sb_ref)

    n_chunks = qf_ref.shape[0] // C_CHUNK
    b_f = _gla_decay(laf_ref, reverse=False)
    b_b = _gla_decay(lab_ref, reverse=True)
    s_f, s_b = sf_ref[...], sb_ref[...]
    for c in range(n_chunks):
        s_f = _gla_chunk(qf_ref, kf_ref, vf_ref, of_ref, b_f, s_f, c, reverse=False)
        s_b = _gla_chunk(qb_ref, kb_ref, vb_ref, ob_ref, b_b, s_b, n_chunks - 1 - c, reverse=True)
    sf_ref[...] = s_f
    sb_ref[...] = s_b


def _gla(cq, ck, cv, la, bsz, seq):
    tb = min(GLA_BLOCK, seq)
    nb = seq // tb
    fwd = lambda col: pl.BlockSpec((tb, C_WIDTH), lambda b, i: (b * nb + i, col))
    bwd = lambda col: pl.BlockSpec((tb, C_WIDTH), lambda b, i: (b * nb + nb - 1 - i, col))
    out = jax.ShapeDtypeStruct((bsz * seq, C_WIDTH), F32)
    state = pltpu.VMEM((C_WIDTH, C_WIDTH), F32)
    return pl.pallas_call(
        _gla_kernel,
        grid=(bsz, nb),
        in_specs=[fwd(0), fwd(0), fwd(0), fwd(0), bwd(0), bwd(0), bwd(0), bwd(1)],
        out_specs=[fwd(0), bwd(0)],
        out_shape=[out, out],
        scratch_shapes=[state, state],
        compiler_params=_cparams(("parallel", "arbitrary")),
        name="gla",
    )(cq, ck, cv, la, cq, ck, cv, la)


def _out_proj_kernel(x_ref, oa_ref, ob_ref, of_ref, obw_ref, cg_ref, cng_ref, wo_ref, g2_ref, wr_ref,
                     x1_ref, h2_ref, aff_ref, mix_ref):
    gi = lax.broadcasted_iota(jnp.int32, (C_WIDTH, C_WIDTH), 0) // HEAD_DIM
    gj = lax.broadcasted_iota(jnp.int32, (C_WIDTH, C_WIDTH), 1) // HEAD_DIM
    ones_bd = (gi == gj).astype(BF16)
    half = x_ref.shape[0] // 2
    for rows in (slice(0, half), slice(half, 2 * half)):
        o = of_ref[rows, :] + obw_ref[rows, :]
        sq_hi, sq_lo = _split_bf16(o * o)
        ms = (_dot(sq_hi, ones_bd) + _dot(sq_lo, ones_bd)) * (1.0 / HEAD_DIM)
        cg = cg_ref[rows, :]
        oc = (o * lax.rsqrt(ms + EPS) * cng_ref[...]) * (cg / (1.0 + jnp.exp(-cg)))
        mix_ref[rows, 0:A_WIDTH] = oa_ref[rows, :]
        mix_ref[rows, A_WIDTH:A_WIDTH + B_WIDTH] = ob_ref[rows, :]
        mix_ref[rows, A_WIDTH + B_WIDTH:] = oc.astype(BF16)
        x1 = x_ref[rows, :] + _dot(mix_ref[rows, :], wo_ref[...])
        x1_ref[rows, :] = x1
        h2 = _rms(x1, g2_ref[...]).astype(BF16)
        h2_ref[rows, :] = h2
        logits = _dot(h2, wr_ref[...])
        lane = lax.broadcasted_iota(jnp.int32, logits.shape, 1)
        logits = jnp.where(lane < N_EXPERTS, logits, NEG_INF)
        p = jnp.exp(logits - jnp.max(logits, axis=1, keepdims=True))
        aff = p / jnp.sum(p, axis=1, keepdims=True)
        aff_ref[:, rows] = aff.T[0:N_EXPERTS, :]


def _out_proj(x, oa, ob, o_f, o_b, cg, cng, wo_bf, layer, g2, wr_cat):
    n = x.shape[0]
    tm = min(OUT_PROJ_ROWS, n)
    row = lambda width: pl.BlockSpec((tm, width), lambda i: (i, 0))
    full = lambda a: pl.BlockSpec(a.shape, lambda i: (0,) * a.ndim)
    return pl.pallas_call(
        _out_proj_kernel,
        grid=(n // tm,),
        in_specs=[row(D_MODEL), row(A_WIDTH), row(B_WIDTH), row(C_WIDTH), row(C_WIDTH), row(C_WIDTH),
                  full(cng), pl.BlockSpec((None, D_MODEL, D_MODEL), lambda i: (layer, 0, 0)), full(g2), full(wr_cat)],
        out_specs=[row(D_MODEL), row(D_MODEL), pl.BlockSpec((N_EXPERTS, tm), lambda i: (0, i))],
        out_shape=[jax.ShapeDtypeStruct((n, D_MODEL), F32), jax.ShapeDtypeStruct((n, D_MODEL), BF16),
                   jax.ShapeDtypeStruct((N_EXPERTS, n), F32)],
        scratch_shapes=[pltpu.VMEM((tm, D_MODEL), BF16)],
        compiler_params=_cparams(("parallel",)),
        name="out_proj",
    )(x, oa, ob, o_f, o_b, cg, cng, wo_bf, g2, wr_cat)


def _route_kernel(aff_ref, pos_ref, offs_ref, *, cap):
    n = aff_ref.shape[1]
    nt = n // ROUTE_TILE
    bits = pltpu.bitcast(aff_ref[...], jnp.int32)

    def search(it, thr):
        cand = thr | jnp.left_shift(jnp.int32(1), 30 - it)
        cnt = jnp.sum(jnp.where(bits >= cand, 1.0, 0.0), axis=1, keepdims=True)
        return jnp.where(cnt >= cap, cand, thr)

    thr = lax.fori_loop(0, 31, search, jnp.zeros((N_EXPERTS, 1), jnp.int32))
    n_gt = jnp.sum(jnp.where(bits > thr, 1.0, 0.0), axis=1, keepdims=True)
    n_tie = cap - n_gt

    ui = lax.broadcasted_iota(jnp.int32, (ROUTE_TILE, ROUTE_TILE), 0)
    uj = lax.broadcasted_iota(jnp.int32, (ROUTE_TILE, ROUTE_TILE), 1)
    upper = (ui <= uj).astype(BF16)
    lane = lax.broadcasted_iota(jnp.int32, offs_ref.shape, 1)

    def tile(i, carry):
        c_gt, c_eq, offs = carry
        col = pl.multiple_of(i * ROUTE_TILE, ROUTE_TILE)
        b = pltpu.bitcast(aff_ref[:, pl.ds(col, ROUTE_TILE)], jnp.int32)
        gt = jnp.where(b > thr, 1.0, 0.0)
        eq = jnp.where(b == thr, 1.0, 0.0)
        inc = _dot(jnp.concatenate([gt, eq], axis=0).astype(BF16), upper)
        gt_before = c_gt + inc[:N_EXPERTS] - gt
        eq_before = c_eq + inc[N_EXPERTS:] - eq
        sel = (gt > 0.0) | ((eq > 0.0) & (eq_before < n_tie))
        pos = gt_before + jnp.minimum(eq_before, n_tie)
        pos_ref[:, pl.ds(col, ROUTE_TILE)] = jnp.where(sel, pos, -1.0).astype(jnp.int32)
        start = c_gt + jnp.minimum(c_eq, n_tie)
        offs = jnp.where(lane == i, start.astype(jnp.int32), offs)
        return (c_gt + jnp.sum(gt, axis=1, keepdims=True), c_eq + jnp.sum(eq, axis=1, keepdims=True), offs)

    zero = jnp.zeros((N_EXPERTS, 1), F32)
    offs = jnp.where(lane == nt, cap, 0).astype(jnp.int32)
    _, _, offs = lax.fori_loop(0, nt, tile, (zero, zero, offs), unroll=4 if nt % 4 == 0 else 1)
    offs_ref[...] = offs


def _route(aff_t, cap):
    n = aff_t.shape[1]
    nt = n // ROUTE_TILE
    return pl.pallas_call(
        functools.partial(_route_kernel, cap=cap),
        out_shape=[jax.ShapeDtypeStruct((N_EXPERTS, n), jnp.int32),
                   jax.ShapeDtypeStruct((N_EXPERTS, nt + 1), jnp.int32)],
        compiler_params=pltpu.CompilerParams(vmem_limit_bytes=VMEM_LIMIT),
        name="route",
    )(aff_t)


def _align(v):
    return (v // MOE_ALIGN) * MOE_ALIGN


def _moe_passes(offs_ref, t, subs, x0, s, win):
    k = jnp.int32(1)
    for g in range(MOE_PAIR):
        need = offs_ref[x0 + g, t * subs + s + 1] - _align(offs_ref[x0 + g, t * subs + s])
        k = jnp.maximum(k, (need + win - 1) // win)
    return k


def _moe_windows(offs_ref, pos_ref, t, subs, x0, s, j, win):
    slot = lax.broadcasted_iota(jnp.int32, (win, ROUTE_TILE), 0)
    hits, rels = [], []
    for g in range(MOE_PAIR):
        x = x0 + g
        ws = _align(offs_ref[x, t * subs + s]) + j * win
        pos = pos_ref[pl.ds(x, 1), s * ROUTE_TILE:(s + 1) * ROUTE_TILE]
        hits.append((pos - ws) == slot)
        rels.append(pl.multiple_of(ws - _align(offs_ref[x, t * subs]), MOE_ALIGN))
    onehot = jnp.concatenate([jnp.where(h, 1.0, 0.0) for h in hits], axis=0).astype(BF16)
    return hits, rels, onehot


def _moe_chunks(offs_ref, t, subs, x):
    base = _align(offs_ref[x, t * subs])
    return base, (offs_ref[x, (t + 1) * subs] - base + MOE_CHUNK - 1) // MOE_CHUNK


def _moe_gather_kernel(offs_ref, h_ref, pos_ref, gate_ref, xe_ref, stage_ref, carry_ref, sem):
    t = pl.program_id(0)
    steps = pl.num_programs(1)
    e0 = pl.program_id(1) * MOE_STEP
    tt = h_ref.shape[0]
    subs = tt // ROUTE_TILE
    k = t * steps + pl.program_id(1)
    slot = k % 2
    x_ref = stage_ref.at[slot]

    def off(x, s):
        return offs_ref[x, t * subs + s]

    def copies(kk, half, act):
        step_t, step_q = kk // steps, kk % steps
        for g in range(MOE_STEP):
            x = step_q * MOE_STEP + g
            base, n_chunks = _moe_chunks(offs_ref, step_t, subs, x)

            def body(c, carry):
                r0 = pl.multiple_of(c * MOE_CHUNK, MOE_CHUNK)
                act(pltpu.make_async_copy(stage_ref.at[half, g, pl.ds(r0, MOE_CHUNK), :],
                                          xe_ref.at[x, pl.ds(pl.multiple_of(base + r0, MOE_ALIGN), MOE_CHUNK), :],
                                          sem.at[half]))
                return carry

            lax.fori_loop(0, n_chunks, body, 0)

    @pl.when(k >= 2)
    def _():
        copies(k - 2, slot, lambda cp: cp.wait())

    @pl.when(t == 0)
    def _():
        carry_ref[pl.ds(e0, MOE_STEP)] = jnp.zeros((MOE_STEP, MOE_ALIGN, XE_WIDTH), BF16)
        cap = xe_ref.shape[1] - MOE_CHUNK
        x_ref[0, 0:MOE_CHUNK, :] = jnp.zeros((MOE_CHUNK, XE_WIDTH), BF16)
        pads = [pltpu.make_async_copy(x_ref.at[0, 0:MOE_CHUNK, :], xe_ref.at[e0 + g, cap:cap + MOE_CHUNK, :],
                                      sem.at[slot]) for g in range(MOE_STEP)]
        for p in pads:
            p.start()
        for p in pads:
            p.wait()

    for g in range(MOE_STEP):
        x_ref[g, 0:MOE_ALIGN, :] = carry_ref[e0 + g]

    row = lax.broadcasted_iota(jnp.int32, (MOE_WIN, 1), 0)
    lane = lax.broadcasted_iota(jnp.int32, (MOE_WIN, XE_WIDTH - D_MODEL), 1)

    def gather_pass(pair, s, j):
        x0 = e0 + pair * MOE_PAIR
        cols = slice(s * ROUTE_TILE, (s + 1) * ROUTE_TILE)
        hits, rels, onehot = _moe_windows(offs_ref, pos_ref, t, subs, x0, s, j, MOE_WIN)
        r = _dot(onehot, h_ref[cols, :])
        for g in range(MOE_PAIR):
            gate = jnp.sum(jnp.where(hits[g], gate_ref[pl.ds(x0 + g, 1), cols], 0.0), axis=1, keepdims=True)
            g_hi = gate.astype(BF16).astype(F32)
            extra = jnp.where(lane == 0, g_hi, jnp.where(lane == 1, gate - g_hi, 0.0))
            new = jnp.concatenate([r[g * MOE_WIN:(g + 1) * MOE_WIN, :], extra], axis=1)
            ws = _align(off(x0 + g, s)) + j * MOE_WIN
            own = row >= off(x0 + g, s) - ws
            buf = pair * MOE_PAIR + g
            if isinstance(j, int):
                head = pl.ds(rels[g], MOE_ALIGN)
                x_ref[buf, head, :] = jnp.where(own[:MOE_ALIGN], new[:MOE_ALIGN],
                                                x_ref[buf, head, :].astype(F32)).astype(BF16)
                x_ref[buf, pl.ds(rels[g] + MOE_ALIGN, MOE_WIN - MOE_ALIGN), :] = new[MOE_ALIGN:].astype(BF16)
            else:
                own = own & (row < off(x0 + g, s + 1) - ws)
                win = pl.ds(rels[g], MOE_WIN)
                x_ref[buf, win, :] = jnp.where(own, new, x_ref[buf, win, :].astype(F32)).astype(BF16)

    for pair in range(MOE_STEP // MOE_PAIR):
        for s in range(subs):
            gather_pass(pair, s, 0)
    for pair in range(MOE_STEP // MOE_PAIR):
        for s in range(subs):
            lax.fori_loop(1, _moe_passes(offs_ref, t, subs, e0 + pair * MOE_PAIR, s, MOE_WIN),
                          lambda j, carry: (gather_pass(pair, s, j), carry)[1], 0)

    for g in range(MOE_STEP):
        x = e0 + g
        base, n_chunks = _moe_chunks(offs_ref, t, subs, x)
        last_group = pl.multiple_of(_align(off(x, subs)) - base, MOE_ALIGN)
        tail = x_ref[g, pl.ds(last_group, MOE_ALIGN), :].astype(F32)
        tail = jnp.where(row[:MOE_ALIGN] < off(x, subs) - base - last_group, tail, 0.0).astype(BF16)
        x_ref[g, pl.ds(last_group, MOE_ALIGN), :] = tail
        carry_ref[x] = tail

        def fill(i, carry):
            x_ref[g, pl.ds(pl.multiple_of(i * MOE_ALIGN, MOE_ALIGN), MOE_ALIGN), :] = jnp.zeros((MOE_ALIGN, XE_WIDTH), BF16)
            return carry

        lax.fori_loop(last_group // MOE_ALIGN + 1, n_chunks * MOE_CHUNK // MOE_ALIGN, fill, 0)

    copies(k, slot, lambda cp: cp.start())

    @pl.when(k == pl.num_programs(0) * steps - 1)
    def _():
        copies(k - 1, 1 - slot, lambda cp: cp.wait())
        copies(k, slot, lambda cp: cp.wait())


def _moe_ffn_kernel(x_ref, wg_ref, wu_ref, wd_ref, y_ref, wg_bf, wu_bf, wd_bf):
    j = pl.program_id(1)

    @pl.when(j == 0)
    def _():
        wg_bf[...] = wg_ref[...].astype(BF16)
        wu_bf[...] = wu_ref[...].astype(BF16)
        wd_bf[...] = wd_ref[...].astype(BF16)
        y_ref[0] = jnp.zeros(y_ref.shape[1:], BF16)

    @pl.when(j > 0)
    def _():
        x = x_ref[0, :, 0:D_MODEL]
        gate = jnp.sum(x_ref[0, :, D_MODEL:XE_WIDTH].astype(F32), axis=1, keepdims=True)
        hg = _dot(x, wg_bf[...])
        hu = _dot(x, wu_bf[...])
        act = (hg / (1.0 + jnp.exp(-hg))) * hu * gate
        y_ref[0] = _dot(act.astype(BF16), wd_bf[...]).astype(BF16)


def _moe_combine_kernel(offs_ref, x1_ref, pos_ref, y_ref, fg_ref, o_ref, y_buf, sem, *, final_norm):
    t = pl.program_id(0)
    q = pl.program_id(1)
    steps = pl.num_programs(1)
    e0 = q * MOE_STEP
    tt = x1_ref.shape[0]
    subs = tt // ROUTE_TILE
    k = t * steps + q
    slot = k % 2

    def fetch(kk, half, act):
        step_t, step_q = kk // steps, kk % steps
        for g in range(MOE_STEP):
            x = step_q * MOE_STEP + g
            base, n_chunks = _moe_chunks(offs_ref, step_t, subs, x)

            def body(c, carry):
                r0 = pl.multiple_of(c * MOE_CHUNK, MOE_CHUNK)
                act(pltpu.make_async_copy(y_ref.at[x, pl.ds(pl.multiple_of(base + r0, MOE_ALIGN), MOE_CHUNK), :],
                                          y_buf.at[half, g, pl.ds(r0, MOE_CHUNK), :], sem.at[half]))
                return carry

            lax.fori_loop(0, n_chunks, body, 0)

    @pl.when(k == 0)
    def _():
        y_buf[...] = jnp.zeros_like(y_buf)
        fetch(k, slot, lambda cp: cp.start())

    @pl.when(k + 1 < pl.num_programs(0) * steps)
    def _():
        fetch(k + 1, 1 - slot, lambda cp: cp.start())

    fetch(k, slot, lambda cp: cp.wait())

    def combine_pass(pair, s, j, init=False):
        cols = slice(s * ROUTE_TILE, (s + 1) * ROUTE_TILE)
        _, rels, onehot = _moe_windows(offs_ref, pos_ref, t, subs, e0 + pair * MOE_PAIR, s, j, MOE_COMBINE_WIN)
        ycat = jnp.concatenate([y_buf[slot, pair * MOE_PAIR + g, pl.ds(rels[g], MOE_COMBINE_WIN), :]
                                for g in range(MOE_PAIR)], axis=0)
        base = x1_ref[cols, :] if init else o_ref[cols, :]
        o_ref[cols, :] = base + _tn_dot(onehot, ycat)

    @pl.when(q == 0)
    def _():
        for s in range(subs):
            combine_pass(0, s, 0, init=True)

    @pl.when(q > 0)
    def _():
        for s in range(subs):
            combine_pass(0, s, 0)

    for pair in range(1, MOE_STEP // MOE_PAIR):
        for s in range(subs):
            combine_pass(pair, s, 0)
    for pair in range(MOE_STEP // MOE_PAIR):
        for s in range(subs):
            lax.fori_loop(1, _moe_passes(offs_ref, t, subs, e0 + pair * MOE_PAIR, s, MOE_COMBINE_WIN),
                          lambda j, carry: (combine_pass(pair, s, j), carry)[1], 0)

    if final_norm:
        @pl.when(q == pl.num_programs(1) - 1)
        def _():
            o_ref[...] = _rms(o_ref[...], fg_ref[...])


def _moe(offs, h2, x1, pos, gate, wg, wu, wd, layer, final_g, final_norm, cap):
    n = h2.shape[0]
    tt, tc = min(MOE_TILE, n), min(MOE_COMBINE_TILE, n)
    assert cap % MOE_CHUNK == 0 and n % tt == 0 and n % tc == 0
    steps = N_EXPERTS // MOE_STEP
    stage_rows = lambda tile, win: tile + MOE_ALIGN + win * pl.cdiv(ROUTE_TILE + MOE_ALIGN, win)
    tok = lambda tile: pl.BlockSpec((tile, D_MODEL), lambda t, q, offs: (t, 0))
    per_tok = lambda tile: pl.BlockSpec((N_EXPERTS, tile), lambda t, q, offs: (0, t))
    hbm = pl.BlockSpec(memory_space=pl.ANY)
    sems = ("arbitrary", "arbitrary")

    xe = pl.pallas_call(
        _moe_gather_kernel,
        grid_spec=pltpu.PrefetchScalarGridSpec(
            num_scalar_prefetch=1, grid=(n // tt, steps),
            in_specs=[tok(tt), per_tok(tt), per_tok(tt)], out_specs=hbm,
            scratch_shapes=[pltpu.VMEM((2, MOE_STEP, stage_rows(tt, MOE_WIN), XE_WIDTH), BF16),
                            pltpu.VMEM((N_EXPERTS, MOE_ALIGN, XE_WIDTH), BF16), pltpu.SemaphoreType.DMA((2,))],
        ),
        out_shape=jax.ShapeDtypeStruct((N_EXPERTS, cap + MOE_CHUNK, XE_WIDTH), BF16),
        compiler_params=_cparams(sems),
        name="moe_gather",
    )(offs, h2, pos, gate)

    fb = min(MOE_FFN_ROWS, cap)
    nb = cap // fb
    wspec = pl.BlockSpec((None, None, D_MODEL, D_MODEL), lambda e, j: (layer, e, 0, 0))
    w_bf = pltpu.VMEM((D_MODEL, D_MODEL), BF16)
    y = pl.pallas_call(
        _moe_ffn_kernel,
        grid=(N_EXPERTS, nb + 1),
        in_specs=[pl.BlockSpec((1, fb, XE_WIDTH), lambda e, j: (e, jnp.maximum(j - 1, 0), 0)), wspec, wspec, wspec],
        out_specs=pl.BlockSpec((1, fb, D_MODEL), lambda e, j: (e, jnp.where(j == 0, nb, j - 1), 0)),
        out_shape=jax.ShapeDtypeStruct((N_EXPERTS, cap + fb, D_MODEL), BF16),
        scratch_shapes=[w_bf, w_bf, w_bf],
        compiler_params=_cparams(("parallel", "arbitrary")),
        name="moe_ffn",
    )(xe, wg, wu, wd)

    return pl.pallas_call(
        functools.partial(_moe_combine_kernel, final_norm=final_norm),
        grid_spec=pltpu.PrefetchScalarGridSpec(
            num_scalar_prefetch=1, grid=(n // tc, steps),
            in_specs=[tok(tc), per_tok(tc), hbm, pl.BlockSpec((1, D_MODEL), lambda t, q, offs: (0, 0))],
            out_specs=tok(tc),
            scratch_shapes=[pltpu.VMEM((2, MOE_STEP, stage_rows(tc, MOE_COMBINE_WIN), D_MODEL), BF16),
                            pltpu.SemaphoreType.DMA((2,))],
        ),
        out_shape=jax.ShapeDtypeStruct((n, D_MODEL), F32),
        compiler_params=_cparams(sems),
        name="moe_combine",
    )(offs, x1, pos, y, final_g)


def _prep_layer(l, norm1_g, w_in, a_sink, a_norm_g, b_rel_bias, b_norm_g, c_alpha_w2_f, c_alpha_b_f,
                c_alpha_w2_b, c_alpha_b_b, c_norm_g, w_out, norm2_g, w_router, w_gate, w_up, w_down):
    z = jnp.zeros((C_LOWRANK, C_WIDTH), F32)
    w2 = jnp.concatenate([jnp.concatenate([c_alpha_w2_f[l], z], axis=1),
                          jnp.concatenate([z, c_alpha_w2_b[l]], axis=1)], axis=0).astype(BF16)
    b2 = jnp.concatenate([c_alpha_b_f[l], c_alpha_b_b[l]])[None, :]
    wr_cat = jnp.pad(w_router[l].astype(BF16), ((0, 0), (0, LANES - N_EXPERTS)))
    return dict(
        layer=l, g1=norm1_g[l][None, :], w_in=w_in.astype(BF16), w2=w2, b2=b2,
        sink=a_sink[l], a_g=a_norm_g[l][None, :], bias=_bias_table(b_rel_bias[l]), b_g=b_norm_g[l][None, :],
        c_g=c_norm_g[l][None, :], w_out=w_out.astype(BF16), g2=norm2_g[l][None, :],
        wr_cat=wr_cat, wg=w_gate, wu=w_up, wd=w_down)


def _layer(x, p, bsz, seq, rope, final_g, final_norm):
    n = bsz * seq
    l = p["layer"]
    aq, ak, av, bq, bk, bv, cq, ck, cv, cg, la = _in_proj(x, p["g1"], p["w_in"], l, p["w2"], p["b2"], seq, rope)
    oa = _attn_a(aq, ak, av, p["sink"], p["a_g"], bsz, seq)
    ob = _attn_b(bq, bk, bv, p["bias"], p["b_g"], bsz, seq)
    o_f, o_b = _gla(cq, ck, cv, la, bsz, seq)
    x1, h2, aff = _out_proj(x, oa, ob, o_f, o_b, cg, p["c_g"], p["w_out"], l, p["g2"], p["wr_cat"])
    cap = EC_CAPACITY * n // N_EXPERTS
    pos, offs = _route(aff, cap)
    return _moe(offs, h2, x1, pos, aff, p["wg"], p["wu"], p["wd"], l, final_g, final_norm, cap)


def _trunk(x, layers, rope, final_g):
    bsz, seq, _ = x.shape
    y = x.reshape(bsz * seq, D_MODEL)
    for l, p in enumerate(layers):
        y = _layer(y, p, bsz, seq, rope, final_g, final_norm=(l == len(layers) - 1))
    return y.reshape(bsz, seq, D_MODEL)


def kernel(x_prompt, x_sample, norm1_g, w_in, a_sink, a_norm_g, b_rel_bias, b_norm_g, c_alpha_w2_f, c_alpha_b_f,
           c_alpha_w2_b, c_alpha_b_b, c_norm_g, w_out, norm2_g, w_router, w_gate, w_up, w_down, final_g):
    depth = w_in.shape[0]
    layers = [_prep_layer(l, norm1_g, w_in, a_sink, a_norm_g, b_rel_bias, b_norm_g, c_alpha_w2_f, c_alpha_b_f,
                          c_alpha_w2_b, c_alpha_b_b, c_norm_g, w_out, norm2_g, w_router, w_gate, w_up, w_down)
              for l in range(depth)]
    fg = final_g[None, :]
    rope = _rope_tables(max(x_prompt.shape[1], x_sample.shape[1]))
    return _trunk(x_prompt, layers, rope, fg), _trunk(x_sample, layers, rope, fg)
```

```python
import functools

import jax
import jax.numpy as jnp
from jax import lax
from jax.experimental import pallas as pl
from jax.experimental.pallas import tpu as pltpu

F32 = jnp.float32
BF16 = jnp.bfloat16

D_MODEL = 1024
HEAD_DIM = 64
A_WIDTH = 512
A_HEADS = 8
A_KV_HEADS = 2
A_KV_WIDTH = 128
WINDOW = 128
ROPE_THETA = 500000.0
ROPE_DIM = 16
B_WIDTH = 256
B_HEADS = 4
GRID_W = 64
WIN_H = 8
WIN_W = 16
C_WIDTH = 256
C_HEADS = 4
C_LOWRANK = 16
C_TAU = 16.0
C_CHUNK = 64
N_EXPERTS = 16
EC_CAPACITY = 2
EPS = 1e-6
NEG_INF = -1e30
IN_WIDTH = 2592

_OFF_AQ, _OFF_AK, _OFF_AV = 0, 512, 640
_OFF_BQ, _OFF_BK, _OFF_BV = 768, 1024, 1280
_OFF_CQ, _OFF_CK, _OFF_CV, _OFF_CG = 1536, 1792, 2048, 2304
_OFF_LR = 2560

LANES = 128
VMEM_LIMIT = 56 * 1024 * 1024

IN_PROJ_ROWS = 1024
OUT_PROJ_ROWS = 1024
GLA_BLOCK = 1024
GLA_CUM_ROWS = 256
ROUTE_TILE = 256
MOE_TILE = 2048
MOE_COMBINE_TILE = 1024
MOE_STEP = 4
MOE_PAIR = 4
MOE_WIN = 64
MOE_COMBINE_WIN = 128
MOE_ALIGN = 16
MOE_CHUNK = 128
MOE_FFN_ROWS = 1024
XE_WIDTH = D_MODEL + LANES


def _cparams(sem):
    return pltpu.CompilerParams(dimension_semantics=sem, vmem_limit_bytes=VMEM_LIMIT)


def _nt_dot(a, b):
    return lax.dot_general(a, b, (((1,), (1,)), ((), ())), preferred_element_type=F32)


def _tn_dot(a, b):
    return lax.dot_general(a, b, (((0,), (0,)), ((), ())), preferred_element_type=F32)


def _dot(a, b):
    return jnp.dot(a, b, preferred_element_type=F32)


def _rms(x, g):
    return x * lax.rsqrt(jnp.mean(x * x, axis=-1, keepdims=True) + EPS) * g


def _same_head_mask(n_rows, rows_per_head):
    width = (n_rows // rows_per_head) * HEAD_DIM
    ri = lax.broadcasted_iota(jnp.int32, (n_rows, width), 0) // rows_per_head
    li = lax.broadcasted_iota(jnp.int32, (n_rows, width), 1) // HEAD_DIM
    return ri == li


def _split_bf16(x):
    hi = x.astype(BF16)
    return hi, (x - hi.astype(F32)).astype(BF16)


def _in_proj_kernel(x_ref, g_ref, w_ref, cos_ref, s1_ref, s2_ref, w2_ref, b2_ref,
                    aq_ref, ak_ref, av_ref, bq_ref, bk_ref, bv_ref,
                    cq_ref, ck_ref, cv_ref, cg_ref, la_ref):
    h = _rms(x_ref[...], g_ref[...]).astype(BF16)
    y = _dot(h, w_ref[...])

    def proj(lo, width):
        return y[:, lo:lo + width]

    cos, s1, s2 = cos_ref[...], s1_ref[...], s2_ref[...]

    def rope(t):
        return t * cos + pltpu.roll(t, LANES - ROPE_DIM // 2, 1) * s1 + pltpu.roll(t, ROPE_DIM // 2, 1) * s2

    scale = HEAD_DIM ** -0.5
    aq = proj(_OFF_AQ, A_WIDTH)
    for c in range(A_WIDTH // LANES):
        aq_ref[:, c * LANES:(c + 1) * LANES] = (rope(aq[:, c * LANES:(c + 1) * LANES]) * scale).astype(BF16)
    ak_ref[...] = rope(proj(_OFF_AK, A_KV_WIDTH)).astype(BF16)
    av_ref[...] = proj(_OFF_AV, A_KV_WIDTH).astype(BF16)
    bq_ref[...] = (proj(_OFF_BQ, B_WIDTH) * scale).astype(BF16)
    bk_ref[...] = proj(_OFF_BK, B_WIDTH).astype(BF16)
    bv_ref[...] = proj(_OFF_BV, B_WIDTH).astype(BF16)
    cq_ref[...] = proj(_OFF_CQ, C_WIDTH) * scale
    ck_ref[...] = proj(_OFF_CK, C_WIDTH)
    cv_ref[...] = proj(_OFF_CV, C_WIDTH).astype(BF16)
    cg_ref[...] = proj(_OFF_CG, C_WIDTH)
    z = _dot(proj(_OFF_LR, 2 * C_LOWRANK).astype(BF16), w2_ref[...]) + b2_ref[...]
    la_ref[...] = (jnp.minimum(z, 0.0) - jnp.log(1.0 + jnp.exp(-jnp.abs(z)))) * (1.0 / C_TAU)


def _rope_tables(seq):
    half = ROPE_DIM // 2
    inv = jnp.power(jnp.float32(ROPE_THETA), -jnp.arange(half, dtype=F32) * (2.0 / ROPE_DIM))
    d = jnp.arange(LANES) % HEAD_DIM
    ang = jnp.arange(seq, dtype=F32)[:, None] * inv[d % half][None, :]
    cos, sin = jnp.cos(ang), jnp.sin(ang)
    first, second = (d < half)[None, :], ((d >= half) & (d < ROPE_DIM))[None, :]
    return (jnp.where(first | second, cos, 1.0), jnp.where(first, -sin, 0.0), jnp.where(second, sin, 0.0))


def _in_proj(x, g1, w_in_bf, layer, w2, b2, seq, rope):
    n = x.shape[0]
    tm = min(IN_PROJ_ROWS, seq)
    per_seq = seq // tm
    cos, s1, s2 = rope
    row = lambda width: pl.BlockSpec((tm, width), lambda i: (i, 0))
    full = lambda a: pl.BlockSpec(a.shape, lambda i: (0,) * a.ndim)
    tab = pl.BlockSpec((tm, LANES), lambda i: (i % per_seq, 0))
    widths = [(A_WIDTH, BF16), (A_KV_WIDTH, BF16), (A_KV_WIDTH, BF16), (B_WIDTH, BF16), (B_WIDTH, BF16),
              (B_WIDTH, BF16), (C_WIDTH, F32), (C_WIDTH, F32), (C_WIDTH, BF16), (C_WIDTH, F32), (2 * C_WIDTH, F32)]
    return pl.pallas_call(
        _in_proj_kernel,
        grid=(n // tm,),
        in_specs=[row(D_MODEL), full(g1), pl.BlockSpec((None, D_MODEL, IN_WIDTH), lambda i: (layer, 0, 0)),
                  tab, tab, tab, full(w2), full(b2)],
        out_specs=[row(w) for w, _ in widths],
        out_shape=[jax.ShapeDtypeStruct((n, w), dt) for w, dt in widths],
        compiler_params=_cparams(("parallel",)),
        name="in_proj",
    )(x, g1, w_in_bf, cos, s1, s2, w2, b2)


A_QBLOCKS = 4


def _attn_a_kernel(sink_ref, q_ref, kp_ref, kc_ref, kn_ref, vp_ref, vc_ref, vn_ref, g_ref, o_ref, acc_ref, *, steps):
    i = pl.program_id(1)
    grp = A_HEADS // A_KV_HEADS
    k = jnp.concatenate([kp_ref[...], kc_ref[...], kn_ref[...]], axis=0)
    v = jnp.concatenate([vp_ref[...], vc_ref[...], vn_ref[...]], axis=0)
    qq = lax.broadcasted_iota(jnp.int32, (WINDOW, 3 * WINDOW), 0)
    kk = lax.broadcasted_iota(jnp.int32, (WINDOW, 3 * WINDOW), 1)
    band = jnp.abs(kk - WINDOW - qq) <= WINDOW
    for blk in range(A_QBLOCKS):
        visible = band
        if blk == 0:
            visible = visible & jnp.logical_not((kk < WINDOW) & (i == 0))
        if blk == A_QBLOCKS - 1:
            visible = visible & jnp.logical_not((kk >= 2 * WINDOW) & (i == steps - 1))
        bias = jnp.where(visible, 0.0, NEG_INF)
        rows = slice(blk * WINDOW, (blk + 1) * WINDOW)
        keys = slice(blk * WINDOW, (blk + 3) * WINDOW)
        for kv in range(A_KV_HEADS):
            heads = range(kv * grp, (kv + 1) * grp)
            q = jnp.concatenate([q_ref[rows, h * HEAD_DIM:(h + 1) * HEAD_DIM] for h in heads], axis=0)
            s = _nt_dot(q, k[keys, kv * HEAD_DIM:(kv + 1) * HEAD_DIM])
            ps, dens = [], []
            for g_i, h in enumerate(heads):
                sh = s[g_i * WINDOW:(g_i + 1) * WINDOW, :] + bias
                sink = sink_ref[h]
                m = jnp.maximum(jnp.max(sh, axis=-1, keepdims=True), sink)
                p = jnp.exp(sh - m)
                dens.append(jnp.sum(p, axis=-1, keepdims=True) + jnp.exp(sink - m))
                ps.append(p.astype(BF16))
            o = _dot(jnp.concatenate(ps, axis=0), v[keys, kv * HEAD_DIM:(kv + 1) * HEAD_DIM])
            for g_i, h in enumerate(heads):
                acc_ref[rows, h * HEAD_DIM:(h + 1) * HEAD_DIM] = o[g_i * WINDOW:(g_i + 1) * WINDOW, :] / dens[g_i]
    o_ref[...] = _rms(acc_ref[...], g_ref[...]).astype(BF16)


def _attn_a(aq, ak, av, sink, g, bsz, seq):
    nb = seq // WINDOW
    assert nb % A_QBLOCKS == 0
    steps = nb // A_QBLOCKS
    qrows = A_QBLOCKS * WINDOW
    qspec = pl.BlockSpec((qrows, A_WIDTH), lambda b, i: (b * steps + i, 0))
    kprev = pl.BlockSpec((WINDOW, A_KV_WIDTH), lambda b, i: (b * nb + jnp.maximum(i * A_QBLOCKS - 1, 0), 0))
    kcur = pl.BlockSpec((qrows, A_KV_WIDTH), lambda b, i: (b * steps + i, 0))
    knext = pl.BlockSpec((WINDOW, A_KV_WIDTH), lambda b, i: (b * nb + jnp.minimum((i + 1) * A_QBLOCKS, nb - 1), 0))
    return pl.pallas_call(
        functools.partial(_attn_a_kernel, steps=steps),
        grid=(bsz, steps),
        in_specs=[pl.BlockSpec(memory_space=pltpu.SMEM), qspec, kprev, kcur, knext, kprev, kcur, knext,
                  pl.BlockSpec((1, A_WIDTH), lambda b, i: (0, 0))],
        out_specs=qspec,
        out_shape=jax.ShapeDtypeStruct((bsz * seq, A_WIDTH), BF16),
        scratch_shapes=[pltpu.VMEM((qrows, A_WIDTH), F32)],
        compiler_params=_cparams(("parallel", "parallel")),
        name="attn_a",
    )(sink, aq, ak, ak, ak, av, av, av, g)


B_GROUP = 16
B_KEYS = WIN_H * GRID_W


def _bias_table_kernel(rel_ref, o_ref):
    c = lax.broadcasted_iota(jnp.int32, (GRID_W, GRID_W), 0)
    w = lax.broadcasted_iota(jnp.int32, (GRID_W, GRID_W), 1)
    cstart = jnp.clip(c - WIN_W // 2, 0, GRID_W - WIN_W)
    colmask = (w >= cstart) & (w < cstart + WIN_W)
    col_off = jnp.clip(w - c + (WIN_W - 1), 0, 2 * WIN_W - 2)
    n_ro, n_co = 2 * WIN_H - 1, 2 * WIN_W - 1
    for h in range(B_HEADS):
        for ro in range(n_ro):
            def body(j, acc):
                return jnp.where(col_off == j, rel_ref[(h * n_ro + ro) * n_co + j], acc)
            t = lax.fori_loop(0, n_co, body, jnp.zeros((GRID_W, GRID_W), F32), unroll=True)
            t = jnp.where(colmask, t, NEG_INF)
            for p in range(WIN_H):
                kidx = ro - (WIN_H - 1) + p
                if 0 <= kidx < WIN_H:
                    o_ref[p, h, :, kidx * GRID_W:(kidx + 1) * GRID_W] = t


def _bias_table(rel_bias):
    return pl.pallas_call(
        _bias_table_kernel,
        in_specs=[pl.BlockSpec(memory_space=pltpu.SMEM)],
        out_shape=jax.ShapeDtypeStruct((WIN_H, B_HEADS, GRID_W, B_KEYS), F32),
        name="bias_table",
    )(rel_bias.reshape(-1)).reshape(WIN_H, B_HEADS * GRID_W, B_KEYS)


def _attn_b_kernel(q_ref, kp_ref, kc_ref, kn_ref, vp_ref, vc_ref, vn_ref, bias_ref, g_ref, o_ref,
                   kbuf, vbuf, acc_ref, *, rows):
    gidx = pl.program_id(1)
    blk = B_GROUP * GRID_W
    kbuf[0:blk, :] = kp_ref[...]
    kbuf[blk:2 * blk, :] = kc_ref[...]
    kbuf[2 * blk:3 * blk, :] = kn_ref[...]
    vbuf[0:blk, :] = vp_ref[...]
    vbuf[blk:2 * blk, :] = vc_ref[...]
    vbuf[2 * blk:3 * blk, :] = vn_ref[...]
    same_head = _same_head_mask(B_HEADS * GRID_W, GRID_W)
    for j in range(B_GROUP):
        r = gidx * B_GROUP + j
        start = jnp.clip(r - WIN_H // 2, 0, rows - WIN_H)
        pat = r - start
        loc = pl.multiple_of((start - gidx * B_GROUP + B_GROUP) * GRID_W, GRID_W)
        kw = kbuf[pl.ds(loc, B_KEYS), :]
        vw = vbuf[pl.ds(loc, B_KEYS), :]
        qj = q_ref[j * GRID_W:(j + 1) * GRID_W, :]
        q = jnp.where(same_head, jnp.concatenate([qj] * B_HEADS, axis=0), jnp.zeros((), BF16))
        s = _nt_dot(q, kw) + bias_ref[pat]
        m = jnp.max(s, axis=-1, keepdims=True)
        p = jnp.exp(s - m)
        den = jnp.sum(p, axis=-1, keepdims=True)
        o = jnp.where(same_head, _dot(p.astype(BF16), vw) / den, 0.0)
        acc_ref[j * GRID_W:(j + 1) * GRID_W, :] = sum(o[h * GRID_W:(h + 1) * GRID_W, :] for h in range(B_HEADS))
    o_ref[...] = _rms(acc_ref[...], g_ref[...]).astype(BF16)


def _attn_b(bq, bk, bv, bias, g, bsz, seq):
    rows = seq // GRID_W
    assert rows % B_GROUP == 0 and rows >= WIN_H
    ng = rows // B_GROUP
    blk = B_GROUP * GRID_W
    cur = pl.BlockSpec((blk, B_WIDTH), lambda b, i: (b * ng + i, 0))
    prev = pl.BlockSpec((blk, B_WIDTH), lambda b, i: (b * ng + jnp.maximum(i - 1, 0), 0))
    nxt = pl.BlockSpec((blk, B_WIDTH), lambda b, i: (b * ng + jnp.minimum(i + 1, ng - 1), 0))
    return pl.pallas_call(
        functools.partial(_attn_b_kernel, rows=rows),
        grid=(bsz, ng),
        in_specs=[cur, prev, cur, nxt, prev, cur, nxt,
                  pl.BlockSpec(bias.shape, lambda b, i: (0, 0, 0)),
                  pl.BlockSpec((1, B_WIDTH), lambda b, i: (0, 0))],
        out_specs=cur,
        out_shape=jax.ShapeDtypeStruct((bsz * seq, B_WIDTH), BF16),
        scratch_shapes=[pltpu.VMEM((3 * blk, B_WIDTH), BF16), pltpu.VMEM((3 * blk, B_WIDTH), BF16),
                        pltpu.VMEM((blk, B_WIDTH), F32)],
        compiler_params=_cparams(("parallel", "parallel")),
        name="attn_b",
    )(bq, bk, bk, bk, bv, bv, bv, bias, g)


def _gla_decay(la_ref, reverse):
    seg = min(GLA_CUM_ROWS, la_ref.shape[0])
    bi = lax.broadcasted_iota(jnp.int32, (seg, seg), 0)
    bj = lax.broadcasted_iota(jnp.int32, (seg, seg), 1)
    in_chunk = (bi // C_CHUNK) == (bj // C_CHUNK)
    cum_w = (in_chunk & ((bj >= bi) if reverse else (bj <= bi))).astype(BF16)
    parts = []
    for r in range(0, la_ref.shape[0], seg):
        la_hi, la_lo = _split_bf16(la_ref[r:r + seg, :])
        parts.append(_dot(cum_w, la_hi) + _dot(cum_w, la_lo))
    return parts[0] if len(parts) == 1 else jnp.concatenate(parts, axis=0)


def _gla_chunk(q_ref, k_ref, v_ref, o_ref, b_all, st, c, reverse):
    same_head = _same_head_mask(C_HEADS * C_CHUNK, C_CHUNK)
    ti = lax.broadcasted_iota(jnp.int32, (C_CHUNK, C_HEADS * C_CHUNK), 0)
    si = lax.broadcasted_iota(jnp.int32, (C_CHUNK, C_HEADS * C_CHUNK), 1) % C_CHUNK
    tri = (si >= ti) if reverse else (si <= ti)
    zero = jnp.zeros((), BF16)
    rows = slice(c * C_CHUNK, (c + 1) * C_CHUNK)
    b = b_all[rows, :]
    b_tot = b[0:1, :] if reverse else b[C_CHUNK - 1:C_CHUNK, :]
    q_i = (q_ref[rows, :] * jnp.exp(b)).astype(BF16)
    kf = k_ref[rows, :]
    k_i = (kf * jnp.exp(-b)).astype(BF16)
    k_e = (kf * jnp.exp(b_tot - b)).astype(BF16)
    v = v_ref[rows, :]
    k_bd = jnp.where(same_head, jnp.concatenate([k_i] * C_HEADS, axis=0), zero)
    v_bd = jnp.where(same_head, jnp.concatenate([v] * C_HEADS, axis=0), zero)
    a = jnp.where(tri, _nt_dot(q_i, k_bd), 0.0)
    o_ref[rows, :] = _dot(a.astype(BF16), v_bd) + _nt_dot(q_i, st.astype(BF16))
    return st * jnp.exp(b_tot) + jnp.where(same_head, _tn_dot(v, k_e), 0.0)


def _gla_kernel(qf_ref, kf_ref, vf_ref, laf_ref, qb_ref, kb_ref, vb_ref, lab_ref, of_ref, ob_ref, sf_ref, sb_ref):
    @pl.when(pl.program_id(1) == 0)
    def _():
        sf_ref[...] = jnp.zeros_like(sf_ref)
        sb_ref[...] = jnp.zeros_like(sb_ref)

    n_chunks = qf_ref.shape[0] // C_CHUNK
    b_f = _gla_decay(laf_ref, reverse=False)
    b_b = _gla_decay(lab_ref, reverse=True)
    s_f, s_b = sf_ref[...], sb_ref[...]
    for c in range(n_chunks):
        s_f = _gla_chunk(qf_ref, kf_ref, vf_ref, of_ref, b_f, s_f, c, reverse=False)
        s_b = _gla_chunk(qb_ref, kb_ref, vb_ref, ob_ref, b_b, s_b, n_chunks - 1 - c, reverse=True)
    sf_ref[...] = s_f
    sb_ref[...] = s_b


def _gla(cq, ck, cv, la, bsz, seq):
    tb = min(GLA_BLOCK, seq)
    nb = seq // tb
    fwd = lambda col: pl.BlockSpec((tb, C_WIDTH), lambda b, i: (b * nb + i, col))
    bwd = lambda col: pl.BlockSpec((tb, C_WIDTH), lambda b, i: (b * nb + nb - 1 - i, col))
    out = jax.ShapeDtypeStruct((bsz * seq, C_WIDTH), F32)
    state = pltpu.VMEM((C_WIDTH, C_WIDTH), F32)
    return pl.pallas_call(
        _gla_kernel,
        grid=(bsz, nb),
        in_specs=[fwd(0), fwd(0), fwd(0), fwd(0), bwd(0), bwd(0), bwd(0), bwd(1)],
        out_specs=[fwd(0), bwd(0)],
        out_shape=[out, out],
        scratch_shapes=[state, state],
        compiler_params=_cparams(("parallel", "arbitrary")),
        name="gla",
    )(cq, ck, cv, la, cq, ck, cv, la)


def _out_proj_kernel(x_ref, oa_ref, ob_ref, of_ref, obw_ref, cg_ref, cng_ref, wo_ref, g2_ref, wr_ref,
                     x1_ref, h2_ref, aff_ref, mix_ref):
    gi = lax.broadcasted_iota(jnp.int32, (C_WIDTH, C_WIDTH), 0) // HEAD_DIM
    gj = lax.broadcasted_iota(jnp.int32, (C_WIDTH, C_WIDTH), 1) // HEAD_DIM
    ones_bd = (gi == gj).astype(BF16)
    half = x_ref.shape[0] // 2
    for rows in (slice(0, half), slice(half, 2 * half)):
        o = of_ref[rows, :] + obw_ref[rows, :]
        sq_hi, sq_lo = _split_bf16(o * o)
        ms = (_dot(sq_hi, ones_bd) + _dot(sq_lo, ones_bd)) * (1.0 / HEAD_DIM)
        cg = cg_ref[rows, :]
        oc = (o * lax.rsqrt(ms + EPS) * cng_ref[...]) * (cg / (1.0 + jnp.exp(-cg)))
        mix_ref[rows, 0:A_WIDTH] = oa_ref[rows, :]
        mix_ref[rows, A_WIDTH:A_WIDTH + B_WIDTH] = ob_ref[rows, :]
        mix_ref[rows, A_WIDTH + B_WIDTH:] = oc.astype(BF16)
        x1 = x_ref[rows, :] + _dot(mix_ref[rows, :], wo_ref[...])
        x1_ref[rows, :] = x1
        h2 = _rms(x1, g2_ref[...]).astype(BF16)
        h2_ref[rows, :] = h2
        logits = _dot(h2, wr_ref[...])
        lane = lax.broadcasted_iota(jnp.int32, logits.shape, 1)
        logits = jnp.where(lane < N_EXPERTS, logits, NEG_INF)
        p = jnp.exp(logits - jnp.max(logits, axis=1, keepdims=True))
        aff = p / jnp.sum(p, axis=1, keepdims=True)
        aff_ref[:, rows] = aff.T[0:N_EXPERTS, :]


def _out_proj(x, oa, ob, o_f, o_b, cg, cng, wo_bf, layer, g2, wr_cat):
    n = x.shape[0]
    tm = min(OUT_PROJ_ROWS, n)
    row = lambda width: pl.BlockSpec((tm, width), lambda i: (i, 0))
    full = lambda a: pl.BlockSpec(a.shape, lambda i: (0,) * a.ndim)
    return pl.pallas_call(
        _out_proj_kernel,
        grid=(n // tm,),
        in_specs=[row(D_MODEL), row(A_WIDTH), row(B_WIDTH), row(C_WIDTH), row(C_WIDTH), row(C_WIDTH),
                  full(cng), pl.BlockSpec((None, D_MODEL, D_MODEL), lambda i: (layer, 0, 0)), full(g2), full(wr_cat)],
        out_specs=[row(D_MODEL), row(D_MODEL), pl.BlockSpec((N_EXPERTS, tm), lambda i: (0, i))],
        out_shape=[jax.ShapeDtypeStruct((n, D_MODEL), F32), jax.ShapeDtypeStruct((n, D_MODEL), BF16),
                   jax.ShapeDtypeStruct((N_EXPERTS, n), F32)],
        scratch_shapes=[pltpu.VMEM((tm, D_MODEL), BF16)],
        compiler_params=_cparams(("parallel",)),
        name="out_proj",
    )(x, oa, ob, o_f, o_b, cg, cng, wo_bf, g2, wr_cat)


def _route_kernel(aff_ref, pos_ref, offs_ref, *, cap):
    n = aff_ref.shape[1]
    nt = n // ROUTE_TILE
    bits = pltpu.bitcast(aff_ref[...], jnp.int32)

    def search(it, thr):
        cand = thr | jnp.left_shift(jnp.int32(1), 30 - it)
        cnt = jnp.sum(jnp.where(bits >= cand, 1.0, 0.0), axis=1, keepdims=True)
        return jnp.where(cnt >= cap, cand, thr)

    thr = lax.fori_loop(0, 31, search, jnp.zeros((N_EXPERTS, 1), jnp.int32))
    n_gt = jnp.sum(jnp.where(bits > thr, 1.0, 0.0), axis=1, keepdims=True)
    n_tie = cap - n_gt

    ui = lax.broadcasted_iota(jnp.int32, (ROUTE_TILE, ROUTE_TILE), 0)
    uj = lax.broadcasted_iota(jnp.int32, (ROUTE_TILE, ROUTE_TILE), 1)
    upper = (ui <= uj).astype(BF16)
    lane = lax.broadcasted_iota(jnp.int32, offs_ref.shape, 1)

    def tile(i, carry):
        c_gt, c_eq, offs = carry
        col = pl.multiple_of(i * ROUTE_TILE, ROUTE_TILE)
        b = pltpu.bitcast(aff_ref[:, pl.ds(col, ROUTE_TILE)], jnp.int32)
        gt = jnp.where(b > thr, 1.0, 0.0)
        eq = jnp.where(b == thr, 1.0, 0.0)
        inc = _dot(jnp.concatenate([gt, eq], axis=0).astype(BF16), upper)
        gt_before = c_gt + inc[:N_EXPERTS] - gt
        eq_before = c_eq + inc[N_EXPERTS:] - eq
        sel = (gt > 0.0) | ((eq > 0.0) & (eq_before < n_tie))
        pos = gt_before + jnp.minimum(eq_before, n_tie)
        pos_ref[:, pl.ds(col, ROUTE_TILE)] = jnp.where(sel, pos, -1.0).astype(jnp.int32)
        start = c_gt + jnp.minimum(c_eq, n_tie)
        offs = jnp.where(lane == i, start.astype(jnp.int32), offs)
        return (c_gt + jnp.sum(gt, axis=1, keepdims=True), c_eq + jnp.sum(eq, axis=1, keepdims=True), offs)

    zero = jnp.zeros((N_EXPERTS, 1), F32)
    offs = jnp.where(lane == nt, cap, 0).astype(jnp.int32)
    _, _, offs = lax.fori_loop(0, nt, tile, (zero, zero, offs), unroll=4 if nt % 4 == 0 else 1)
    offs_ref[...] = offs


def _route(aff_t, cap):
    n = aff_t.shape[1]
    nt = n // ROUTE_TILE
    return pl.pallas_call(
        functools.partial(_route_kernel, cap=cap),
        out_shape=[jax.ShapeDtypeStruct((N_EXPERTS, n), jnp.int32),
                   jax.ShapeDtypeStruct((N_EXPERTS, nt + 1), jnp.int32)],
        compiler_params=pltpu.CompilerParams(vmem_limit_bytes=VMEM_LIMIT),
        name="route",
    )(aff_t)


def _align(v):
    return (v // MOE_ALIGN) * MOE_ALIGN


def _moe_passes(offs_ref, t, subs, x0, s, win):
    k = jnp.int32(1)
    for g in range(MOE_PAIR):
        need = offs_ref[x0 + g, t * subs + s + 1] - _align(offs_ref[x0 + g, t * subs + s])
        k = jnp.maximum(k, (need + win - 1) // win)
    return k


def _moe_windows(offs_ref, pos_ref, t, subs, x0, s, j, win):
    slot = lax.broadcasted_iota(jnp.int32, (win, ROUTE_TILE), 0)
    hits, rels = [], []
    for g in range(MOE_PAIR):
        x = x0 + g
        ws = _align(offs_ref[x, t * subs + s]) + j * win
        pos = pos_ref[pl.ds(x, 1), s * ROUTE_TILE:(s + 1) * ROUTE_TILE]
        hits.append((pos - ws) == slot)
        rels.append(pl.multiple_of(ws - _align(offs_ref[x, t * subs]), MOE_ALIGN))
    onehot = jnp.concatenate([jnp.where(h, 1.0, 0.0) for h in hits], axis=0).astype(BF16)
    return hits, rels, onehot


def _moe_chunks(offs_ref, t, subs, x):
    base = _align(offs_ref[x, t * subs])
    return base, (offs_ref[x, (t + 1) * subs] - base + MOE_CHUNK - 1) // MOE_CHUNK


def _moe_gather_kernel(offs_ref, h_ref, pos_ref, gate_ref, xe_ref, stage_ref, carry_ref, sem):
    t = pl.program_id(0)
    steps = pl.num_programs(1)
    e0 = pl.program_id(1) * MOE_STEP
    tt = h_ref.shape[0]
    subs = tt // ROUTE_TILE
    k = t * steps + pl.program_id(1)
    slot = k % 2
    x_ref = stage_ref.at[slot]

    def off(x, s):
        return offs_ref[x, t * subs + s]

    def copies(kk, half, act):
        step_t, step_q = kk // steps, kk % steps
        for g in range(MOE_STEP):
            x = step_q * MOE_STEP + g
            base, n_chunks = _moe_chunks(offs_ref, step_t, subs, x)

            def body(c, carry):
                r0 = pl.multiple_of(c * MOE_CHUNK, MOE_CHUNK)
                act(pltpu.make_async_copy(stage_ref.at[half, g, pl.ds(r0, MOE_CHUNK), :],
                                          xe_ref.at[x, pl.ds(pl.multiple_of(base + r0, MOE_ALIGN), MOE_CHUNK), :],
                                          sem.at[half]))
                return carry

            lax.fori_loop(0, n_chunks, body, 0)

    @pl.when(k >= 2)
    def _():
        copies(k - 2, slot, lambda cp: cp.wait())

    @pl.when(t == 0)
    def _():
        carry_ref[pl.ds(e0, MOE_STEP)] = jnp.zeros((MOE_STEP, MOE_ALIGN, XE_WIDTH), BF16)
        cap = xe_ref.shape[1] - MOE_CHUNK
        x_ref[0, 0:MOE_CHUNK, :] = jnp.zeros((MOE_CHUNK, XE_WIDTH), BF16)
        pads = [pltpu.make_async_copy(x_ref.at[0, 0:MOE_CHUNK, :], xe_ref.at[e0 + g, cap:cap + MOE_CHUNK, :],
                                      sem.at[slot]) for g in range(MOE_STEP)]
        for p in pads:
            p.start()
        for p in pads:
            p.wait()

    for g in range(MOE_STEP):
        x_ref[g, 0:MOE_ALIGN, :] = carry_ref[e0 + g]

    row = lax.broadcasted_iota(jnp.int32, (MOE_WIN, 1), 0)
    lane = lax.broadcasted_iota(jnp.int32, (MOE_WIN, XE_WIDTH - D_MODEL), 1)

    def gather_pass(pair, s, j):
        x0 = e0 + pair * MOE_PAIR
        cols = slice(s * ROUTE_TILE, (s + 1) * ROUTE_TILE)
        hits, rels, onehot = _moe_windows(offs_ref, pos_ref, t, subs, x0, s, j, MOE_WIN)
        r = _dot(onehot, h_ref[cols, :])
        for g in range(MOE_PAIR):
            gate = jnp.sum(jnp.where(hits[g], gate_ref[pl.ds(x0 + g, 1), cols], 0.0), axis=1, keepdims=True)
            g_hi = gate.astype(BF16).astype(F32)
            extra = jnp.where(lane == 0, g_hi, jnp.where(lane == 1, gate - g_hi, 0.0))
            new = jnp.concatenate([r[g * MOE_WIN:(g + 1) * MOE_WIN, :], extra], axis=1)
            ws = _align(off(x0 + g, s)) + j * MOE_WIN
            own = row >= off(x0 + g, s) - ws
            buf = pair * MOE_PAIR + g
            if isinstance(j, int):
                head = pl.ds(rels[g], MOE_ALIGN)
                x_ref[buf, head, :] = jnp.where(own[:MOE_ALIGN], new[:MOE_ALIGN],
                                                x_ref[buf, head, :].astype(F32)).astype(BF16)
                x_ref[buf, pl.ds(rels[g] + MOE_ALIGN, MOE_WIN - MOE_ALIGN), :] = new[MOE_ALIGN:].astype(BF16)
            else:
                own = own & (row < off(x0 + g, s + 1) - ws)
                win = pl.ds(rels[g], MOE_WIN)
                x_ref[buf, win, :] = jnp.where(own, new, x_ref[buf, win, :].astype(F32)).astype(BF16)

    for pair in range(MOE_STEP // MOE_PAIR):
        for s in range(subs):
            gather_pass(pair, s, 0)
    for pair in range(MOE_STEP // MOE_PAIR):
        for s in range(subs):
            lax.fori_loop(1, _moe_passes(offs_ref, t, subs, e0 + pair * MOE_PAIR, s, MOE_WIN),
                          lambda j, carry: (gather_pass(pair, s, j), carry)[1], 0)

    for g in range(MOE_STEP):
        x = e0 + g
        base, n_chunks = _moe_chunks(offs_ref, t, subs, x)
        last_group = pl.multiple_of(_align(off(x, subs)) - base, MOE_ALIGN)
        tail = x_ref[g, pl.ds(last_group, MOE_ALIGN), :].astype(F32)
        tail = jnp.where(row[:MOE_ALIGN] < off(x, subs) - base - last_group, tail, 0.0).astype(BF16)
        x_ref[g, pl.ds(last_group, MOE_ALIGN), :] = tail
        carry_ref[x] = tail

        def fill(i, carry):
            x_ref[g, pl.ds(pl.multiple_of(i * MOE_ALIGN, MOE_ALIGN), MOE_ALIGN), :] = jnp.zeros((MOE_ALIGN, XE_WIDTH), BF16)
            return carry

        lax.fori_loop(last_group // MOE_ALIGN + 1, n_chunks * MOE_CHUNK // MOE_ALIGN, fill, 0)

    copies(k, slot, lambda cp: cp.start())

    @pl.when(k == pl.num_programs(0) * steps - 1)
    def _():
        copies(k - 1, 1 - slot, lambda cp: cp.wait())
        copies(k, slot, lambda cp: cp.wait())


def _moe_ffn_kernel(x_ref, wg_ref, wu_ref, wd_ref, y_ref, wg_bf, wu_bf, wd_bf):
    j = pl.program_id(1)

    @pl.when(j == 0)
    def _():
        wg_bf[...] = wg_ref[...].astype(BF16)
        wu_bf[...] = wu_ref[...].astype(BF16)
        wd_bf[...] = wd_ref[...].astype(BF16)
        y_ref[0] = jnp.zeros(y_ref.shape[1:], BF16)

    @pl.when(j > 0)
    def _():
        x = x_ref[0, :, 0:D_MODEL]
        gate = jnp.sum(x_ref[0, :, D_MODEL:XE_WIDTH].astype(F32), axis=1, keepdims=True)
        hg = _dot(x, wg_bf[...])
        hu = _dot(x, wu_bf[...])
        act = (hg / (1.0 + jnp.exp(-hg))) * hu * gate
        y_ref[0] = _dot(act.astype(BF16), wd_bf[...]).astype(BF16)


def _moe_combine_kernel(offs_ref, x1_ref, pos_ref, y_ref, fg_ref, o_ref, y_buf, sem, *, final_norm):
    t = pl.program_id(0)
    q = pl.program_id(1)
    steps = pl.num_programs(1)
    e0 = q * MOE_STEP
    tt = x1_ref.shape[0]
    subs = tt // ROUTE_TILE
    k = t * steps + q
    slot = k % 2

    def fetch(kk, half, act):
        step_t, step_q = kk // steps, kk % steps
        for g in range(MOE_STEP):
            x = step_q * MOE_STEP + g
            base, n_chunks = _moe_chunks(offs_ref, step_t, subs, x)

            def body(c, carry):
                r0 = pl.multiple_of(c * MOE_CHUNK, MOE_CHUNK)
                act(pltpu.make_async_copy(y_ref.at[x, pl.ds(pl.multiple_of(base + r0, MOE_ALIGN), MOE_CHUNK), :],
                                          y_buf.at[half, g, pl.ds(r0, MOE_CHUNK), :], sem.at[half]))
                return carry

            lax.fori_loop(0, n_chunks, body, 0)

    @pl.when(k == 0)
    def _():
        y_buf[...] = jnp.zeros_like(y_buf)
        fetch(k, slot, lambda cp: cp.start())

    @pl.when(k + 1 < pl.num_programs(0) * steps)
    def _():
        fetch(k + 1, 1 - slot, lambda cp: cp.start())

    fetch(k, slot, lambda cp: cp.wait())

    @pl.when(q == 0)
    def _():
        o_ref[...] = x1_ref[...]

    def combine_pass(pair, s, j):
        cols = slice(s * ROUTE_TILE, (s + 1) * ROUTE_TILE)
        _, rels, onehot = _moe_windows(offs_ref, pos_ref, t, subs, e0 + pair * MOE_PAIR, s, j, MOE_COMBINE_WIN)
        ycat = jnp.concatenate([y_buf[slot, pair * MOE_PAIR + g, pl.ds(rels[g], MOE_COMBINE_WIN), :]
                                for g in range(MOE_PAIR)], axis=0)
        o_ref[cols, :] += _tn_dot(onehot, ycat)

    for pair in range(MOE_STEP // MOE_PAIR):
        for s in range(subs):
            combine_pass(pair, s, 0)
    for pair in range(MOE_STEP // MOE_PAIR):
        for s in range(subs):
            lax.fori_loop(1, _moe_passes(offs_ref, t, subs, e0 + pair * MOE_PAIR, s, MOE_COMBINE_WIN),
                          lambda j, carry: (combine_pass(pair, s, j), carry)[1], 0)

    if final_norm:
        @pl.when(q == pl.num_programs(1) - 1)
        def _():
            o_ref[...] = _rms(o_ref[...], fg_ref[...])


def _moe(offs, h2, x1, pos, gate, wg, wu, wd, layer, final_g, final_norm, cap):
    n = h2.shape[0]
    tt, tc = min(MOE_TILE, n), min(MOE_COMBINE_TILE, n)
    assert cap % MOE_CHUNK == 0 and n % tt == 0 and n % tc == 0
    steps = N_EXPERTS // MOE_STEP
    stage_rows = lambda tile, win: tile + MOE_ALIGN + win * pl.cdiv(ROUTE_TILE + MOE_ALIGN, win)
    tok = lambda tile: pl.BlockSpec((tile, D_MODEL), lambda t, q, offs: (t, 0))
    per_tok = lambda tile: pl.BlockSpec((N_EXPERTS, tile), lambda t, q, offs: (0, t))
    hbm = pl.BlockSpec(memory_space=pl.ANY)
    sems = ("arbitrary", "arbitrary")

    xe = pl.pallas_call(
        _moe_gather_kernel,
        grid_spec=pltpu.PrefetchScalarGridSpec(
            num_scalar_prefetch=1, grid=(n // tt, steps),
            in_specs=[tok(tt), per_tok(tt), per_tok(tt)], out_specs=hbm,
            scratch_shapes=[pltpu.VMEM((2, MOE_STEP, stage_rows(tt, MOE_WIN), XE_WIDTH), BF16),
                            pltpu.VMEM((N_EXPERTS, MOE_ALIGN, XE_WIDTH), BF16), pltpu.SemaphoreType.DMA((2,))],
        ),
        out_shape=jax.ShapeDtypeStruct((N_EXPERTS, cap + MOE_CHUNK, XE_WIDTH), BF16),
        compiler_params=_cparams(sems),
        name="moe_gather",
    )(offs, h2, pos, gate)

    fb = min(MOE_FFN_ROWS, cap)
    nb = cap // fb
    wspec = pl.BlockSpec((None, None, D_MODEL, D_MODEL), lambda e, j: (layer, e, 0, 0))
    w_bf = pltpu.VMEM((D_MODEL, D_MODEL), BF16)
    y = pl.pallas_call(
        _moe_ffn_kernel,
        grid=(N_EXPERTS, nb + 1),
        in_specs=[pl.BlockSpec((1, fb, XE_WIDTH), lambda e, j: (e, jnp.maximum(j - 1, 0), 0)), wspec, wspec, wspec],
        out_specs=pl.BlockSpec((1, fb, D_MODEL), lambda e, j: (e, jnp.where(j == 0, nb, j - 1), 0)),
        out_shape=jax.ShapeDtypeStruct((N_EXPERTS, cap + fb, D_MODEL), BF16),
        scratch_shapes=[w_bf, w_bf, w_bf],
        compiler_params=_cparams(("parallel", "arbitrary")),
        name="moe_ffn",
    )(xe, wg, wu, wd)

    return pl.pallas_call(
        functools.partial(_moe_combine_kernel, final_norm=final_norm),
        grid_spec=pltpu.PrefetchScalarGridSpec(
            num_scalar_prefetch=1, grid=(n // tc, steps),
            in_specs=[tok(tc), per_tok(tc), hbm, pl.BlockSpec((1, D_MODEL), lambda t, q, offs: (0, 0))],
            out_specs=tok(tc),
            scratch_shapes=[pltpu.VMEM((2, MOE_STEP, stage_rows(tc, MOE_COMBINE_WIN), D_MODEL), BF16),
                            pltpu.SemaphoreType.DMA((2,))],
        ),
        out_shape=jax.ShapeDtypeStruct((n, D_MODEL), F32),
        compiler_params=_cparams(sems),
        name="moe_combine",
    )(offs, x1, pos, y, final_g)


def _prep_layer(l, norm1_g, w_in, a_sink, a_norm_g, b_rel_bias, b_norm_g, c_alpha_w2_f, c_alpha_b_f,
                c_alpha_w2_b, c_alpha_b_b, c_norm_g, w_out, norm2_g, w_router, w_gate, w_up, w_down):
    z = jnp.zeros((C_LOWRANK, C_WIDTH), F32)
    w2 = jnp.concatenate([jnp.concatenate([c_alpha_w2_f[l], z], axis=1),
                          jnp.concatenate([z, c_alpha_w2_b[l]], axis=1)], axis=0).astype(BF16)
    b2 = jnp.concatenate([c_alpha_b_f[l], c_alpha_b_b[l]])[None, :]
    wr_cat = jnp.pad(w_router[l].astype(BF16), ((0, 0), (0, LANES - N_EXPERTS)))
    return dict(
        layer=l, g1=norm1_g[l][None, :], w_in=w_in.astype(BF16), w2=w2, b2=b2,
        sink=a_sink[l], a_g=a_norm_g[l][None, :], bias=_bias_table(b_rel_bias[l]), b_g=b_norm_g[l][None, :],
        c_g=c_norm_g[l][None, :], w_out=w_out.astype(BF16), g2=norm2_g[l][None, :],
        wr_cat=wr_cat, wg=w_gate, wu=w_up, wd=w_down)


def _layer(x, p, bsz, seq, rope, final_g, final_norm):
    n = bsz * seq
    l = p["layer"]
    aq, ak, av, bq, bk, bv, cq, ck, cv, cg, la = _in_proj(x, p["g1"], p["w_in"], l, p["w2"], p["b2"], seq, rope)
    oa = _attn_a(aq, ak, av, p["sink"], p["a_g"], bsz, seq)
    ob = _attn_b(bq, bk, bv, p["bias"], p["b_g"], bsz, seq)
    o_f, o_b = _gla(cq, ck, cv, la, bsz, seq)
    x1, h2, aff = _out_proj(x, oa, ob, o_f, o_b, cg, p["c_g"], p["w_out"], l, p["g2"], p["wr_cat"])
    cap = EC_CAPACITY * n // N_EXPERTS
    pos, offs = _route(aff, cap)
    return _moe(offs, h2, x1, pos, aff, p["wg"], p["wu"], p["wd"], l, final_g, final_norm, cap)


def _trunk(x, layers, rope, final_g):
    bsz, seq, _ = x.shape
    y = x.reshape(bsz * seq, D_MODEL)
    for l, p in enumerate(layers):
        y = _layer(y, p, bsz, seq, rope, final_g, final_norm=(l == len(layers) - 1))
    return y.reshape(bsz, seq, D_MODEL)


def kernel(x_prompt, x_sample, norm1_g, w_in, a_sink, a_norm_g, b_rel_bias, b_norm_g, c_alpha_w2_f, c_alpha_b_f,
           c_alpha_w2_b, c_alpha_b_b, c_norm_g, w_out, norm2_g, w_router, w_gate, w_up, w_down, final_g):
    depth = w_in.shape[0]
    layers = [_prep_layer(l, norm1_g, w_in, a_sink, a_norm_g, b_rel_bias, b_norm_g, c_alpha_w2_f, c_alpha_b_f,
                          c_alpha_w2_b, c_alpha_b_b, c_norm_g, w_out, norm2_g, w_router, w_gate, w_up, w_down)
              for l in range(depth)]
    fg = final_g[None, :]
    rope = _rope_tables(max(x_prompt.shape[1], x_sample.shape[1]))
    return _trunk(x_prompt, layers, rope, fg), _trunk(x_sample, layers, rope, fg)
```

```python
import functools

import jax
import jax.numpy as jnp
from jax import lax
from jax.experimental import pallas as pl
from jax.experimental.pallas import tpu as pltpu

F32 = jnp.float32
BF16 = jnp.bfloat16

D_MODEL = 1024
HEAD_DIM = 64
A_WIDTH = 512
A_HEADS = 8
A_KV_HEADS = 2
A_KV_WIDTH = 128
WINDOW = 128
ROPE_THETA = 500000.0
ROPE_DIM = 16
B_WIDTH = 256
B_HEADS = 4
GRID_W = 64
WIN_H = 8
WIN_W = 16
C_WIDTH = 256
C_HEADS = 4
C_LOWRANK = 16
C_TAU = 16.0
C_CHUNK = 64
N_EXPERTS = 16
EC_CAPACITY = 2
EPS = 1e-6
NEG_INF = -1e30
IN_WIDTH = 2592

_OFF_AQ, _OFF_AK, _OFF_AV = 0, 512, 640
_OFF_BQ, _OFF_BK, _OFF_BV = 768, 1024, 1280
_OFF_CQ, _OFF_CK, _OFF_CV, _OFF_CG = 1536, 1792, 2048, 2304
_OFF_LR = 2560

LANES = 128
VMEM_LIMIT = 56 * 1024 * 1024

IN_PROJ_ROWS = 1024
OUT_PROJ_ROWS = 1024
GLA_BLOCK = 1024
GLA_CUM_ROWS = 256
ROUTE_TILE = 256
MOE_TILE = 2048
MOE_COMBINE_TILE = 1024
MOE_STEP = 4
MOE_PAIR = 4
MOE_WIN = 64
MOE_COMBINE_WIN = 128
MOE_ALIGN = 16
MOE_CHUNK = 128
MOE_FFN_ROWS = 1024
XE_WIDTH = D_MODEL + LANES


def _cparams(sem):
    return pltpu.CompilerParams(dimension_semantics=sem, vmem_limit_bytes=VMEM_LIMIT)


def _nt_dot(a, b):
    return lax.dot_general(a, b, (((1,), (1,)), ((), ())), preferred_element_type=F32)


def _tn_dot(a, b):
    return lax.dot_general(a, b, (((0,), (0,)), ((), ())), preferred_element_type=F32)


def _dot(a, b):
    return jnp.dot(a, b, preferred_element_type=F32)


def _rms(x, g):
    return x * lax.rsqrt(jnp.mean(x * x, axis=-1, keepdims=True) + EPS) * g


def _same_head_mask(n_rows, rows_per_head):
    width = (n_rows // rows_per_head) * HEAD_DIM
    ri = lax.broadcasted_iota(jnp.int32, (n_rows, width), 0) // rows_per_head
    li = lax.broadcasted_iota(jnp.int32, (n_rows, width), 1) // HEAD_DIM
    return ri == li


def _split_bf16(x):
    hi = x.astype(BF16)
    return hi, (x - hi.astype(F32)).astype(BF16)


def _in_proj_kernel(x_ref, g_ref, w_ref, cos_ref, s1_ref, s2_ref, w2_ref, b2_ref,
                    aq_ref, ak_ref, av_ref, bq_ref, bk_ref, bv_ref,
                    cq_ref, ck_ref, cv_ref, cg_ref, la_ref):
    h = _rms(x_ref[...], g_ref[...]).astype(BF16)
    y = _dot(h, w_ref[...])

    def proj(lo, width):
        return y[:, lo:lo + width]

    cos, s1, s2 = cos_ref[...], s1_ref[...], s2_ref[...]

    def rope(t):
        return t * cos + pltpu.roll(t, LANES - ROPE_DIM // 2, 1) * s1 + pltpu.roll(t, ROPE_DIM // 2, 1) * s2

    scale = HEAD_DIM ** -0.5
    aq = proj(_OFF_AQ, A_WIDTH)
    for c in range(A_WIDTH // LANES):
        aq_ref[:, c * LANES:(c + 1) * LANES] = (rope(aq[:, c * LANES:(c + 1) * LANES]) * scale).astype(BF16)
    ak_ref[...] = rope(proj(_OFF_AK, A_KV_WIDTH)).astype(BF16)
    av_ref[...] = proj(_OFF_AV, A_KV_WIDTH).astype(BF16)
    bq_ref[...] = (proj(_OFF_BQ, B_WIDTH) * scale).astype(BF16)
    bk_ref[...] = proj(_OFF_BK, B_WIDTH).astype(BF16)
    bv_ref[...] = proj(_OFF_BV, B_WIDTH).astype(BF16)
    cq_ref[...] = proj(_OFF_CQ, C_WIDTH) * scale
    ck_ref[...] = proj(_OFF_CK, C_WIDTH)
    cv_ref[...] = proj(_OFF_CV, C_WIDTH).astype(BF16)
    cg_ref[...] = proj(_OFF_CG, C_WIDTH)
    z = _dot(proj(_OFF_LR, 2 * C_LOWRANK).astype(BF16), w2_ref[...]) + b2_ref[...]
    la_ref[...] = (jnp.minimum(z, 0.0) - jnp.log(1.0 + jnp.exp(-jnp.abs(z)))) * (1.0 / C_TAU)


def _rope_tables(seq):
    half = ROPE_DIM // 2
    inv = jnp.power(jnp.float32(ROPE_THETA), -jnp.arange(half, dtype=F32) * (2.0 / ROPE_DIM))
    d = jnp.arange(LANES) % HEAD_DIM
    ang = jnp.arange(seq, dtype=F32)[:, None] * inv[d % half][None, :]
    cos, sin = jnp.cos(ang), jnp.sin(ang)
    first, second = (d < half)[None, :], ((d >= half) & (d < ROPE_DIM))[None, :]
    return (jnp.where(first | second, cos, 1.0), jnp.where(first, -sin, 0.0), jnp.where(second, sin, 0.0))


def _in_proj(x, g1, w_in_bf, layer, w2, b2, seq, rope):
    n = x.shape[0]
    tm = min(IN_PROJ_ROWS, seq)
    per_seq = seq // tm
    cos, s1, s2 = rope
    row = lambda width: pl.BlockSpec((tm, width), lambda i: (i, 0))
    full = lambda a: pl.BlockSpec(a.shape, lambda i: (0,) * a.ndim)
    tab = pl.BlockSpec((tm, LANES), lambda i: (i % per_seq, 0))
    widths = [(A_WIDTH, BF16), (A_KV_WIDTH, BF16), (A_KV_WIDTH, BF16), (B_WIDTH, BF16), (B_WIDTH, BF16),
              (B_WIDTH, BF16), (C_WIDTH, F32), (C_WIDTH, F32), (C_WIDTH, BF16), (C_WIDTH, F32), (2 * C_WIDTH, F32)]
    return pl.pallas_call(
        _in_proj_kernel,
        grid=(n // tm,),
        in_specs=[row(D_MODEL), full(g1), pl.BlockSpec((None, D_MODEL, IN_WIDTH), lambda i: (layer, 0, 0)),
                  tab, tab, tab, full(w2), full(b2)],
        out_specs=[row(w) for w, _ in widths],
        out_shape=[jax.ShapeDtypeStruct((n, w), dt) for w, dt in widths],
        compiler_params=_cparams(("parallel",)),
        name="in_proj",
    )(x, g1, w_in_bf, cos, s1, s2, w2, b2)


A_QBLOCKS = 4


def _attn_a_kernel(sink_ref, q_ref, kp_ref, kc_ref, kn_ref, vp_ref, vc_ref, vn_ref, g_ref, o_ref, acc_ref, *, steps):
    i = pl.program_id(1)
    grp = A_HEADS // A_KV_HEADS
    k = jnp.concatenate([kp_ref[...], kc_ref[...], kn_ref[...]], axis=0)
    v = jnp.concatenate([vp_ref[...], vc_ref[...], vn_ref[...]], axis=0)
    qq = lax.broadcasted_iota(jnp.int32, (WINDOW, 3 * WINDOW), 0)
    kk = lax.broadcasted_iota(jnp.int32, (WINDOW, 3 * WINDOW), 1)
    band = jnp.abs(kk - WINDOW - qq) <= WINDOW
    for blk in range(A_QBLOCKS):
        visible = band
        if blk == 0:
            visible = visible & jnp.logical_not((kk < WINDOW) & (i == 0))
        if blk == A_QBLOCKS - 1:
            visible = visible & jnp.logical_not((kk >= 2 * WINDOW) & (i == steps - 1))
        bias = jnp.where(visible, 0.0, NEG_INF)
        rows = slice(blk * WINDOW, (blk + 1) * WINDOW)
        keys = slice(blk * WINDOW, (blk + 3) * WINDOW)
        for kv in range(A_KV_HEADS):
            heads = range(kv * grp, (kv + 1) * grp)
            q = jnp.concatenate([q_ref[rows, h * HEAD_DIM:(h + 1) * HEAD_DIM] for h in heads], axis=0)
            s = _nt_dot(q, k[keys, kv * HEAD_DIM:(kv + 1) * HEAD_DIM])
            ps, dens = [], []
            for g_i, h in enumerate(heads):
                sh = s[g_i * WINDOW:(g_i + 1) * WINDOW, :] + bias
                sink = sink_ref[h]
                m = jnp.maximum(jnp.max(sh, axis=-1, keepdims=True), sink)
                p = jnp.exp(sh - m)
                dens.append(jnp.sum(p, axis=-1, keepdims=True) + jnp.exp(sink - m))
                ps.append(p.astype(BF16))
            o = _dot(jnp.concatenate(ps, axis=0), v[keys, kv * HEAD_DIM:(kv + 1) * HEAD_DIM])
            for g_i, h in enumerate(heads):
                acc_ref[rows, h * HEAD_DIM:(h + 1) * HEAD_DIM] = o[g_i * WINDOW:(g_i + 1) * WINDOW, :] / dens[g_i]
    o_ref[...] = _rms(acc_ref[...], g_ref[...]).astype(BF16)


def _attn_a(aq, ak, av, sink, g, bsz, seq):
    nb = seq // WINDOW
    assert nb % A_QBLOCKS == 0
    steps = nb // A_QBLOCKS
    qrows = A_QBLOCKS * WINDOW
    qspec = pl.BlockSpec((qrows, A_WIDTH), lambda b, i: (b * steps + i, 0))
    kprev = pl.BlockSpec((WINDOW, A_KV_WIDTH), lambda b, i: (b * nb + jnp.maximum(i * A_QBLOCKS - 1, 0), 0))
    kcur = pl.BlockSpec((qrows, A_KV_WIDTH), lambda b, i: (b * steps + i, 0))
    knext = pl.BlockSpec((WINDOW, A_KV_WIDTH), lambda b, i: (b * nb + jnp.minimum((i + 1) * A_QBLOCKS, nb - 1), 0))
    return pl.pallas_call(
        functools.partial(_attn_a_kernel, steps=steps),
        grid=(bsz, steps),
        in_specs=[pl.BlockSpec(memory_space=pltpu.SMEM), qspec, kprev, kcur, knext, kprev, kcur, knext,
                  pl.BlockSpec((1, A_WIDTH), lambda b, i: (0, 0))],
        out_specs=qspec,
        out_shape=jax.ShapeDtypeStruct((bsz * seq, A_WIDTH), BF16),
        scratch_shapes=[pltpu.VMEM((qrows, A_WIDTH), F32)],
        compiler_params=_cparams(("parallel", "parallel")),
        name="attn_a",
    )(sink, aq, ak, ak, ak, av, av, av, g)


B_GROUP = 16
B_KEYS = WIN_H * GRID_W


def _bias_table_kernel(rel_ref, o_ref):
    c = lax.broadcasted_iota(jnp.int32, (GRID_W, GRID_W), 0)
    w = lax.broadcasted_iota(jnp.int32, (GRID_W, GRID_W), 1)
    cstart = jnp.clip(c - WIN_W // 2, 0, GRID_W - WIN_W)
    colmask = (w >= cstart) & (w < cstart + WIN_W)
    col_off = jnp.clip(w - c + (WIN_W - 1), 0, 2 * WIN_W - 2)
    n_ro, n_co = 2 * WIN_H - 1, 2 * WIN_W - 1
    for h in range(B_HEADS):
        for ro in range(n_ro):
            def body(j, acc):
                return jnp.where(col_off == j, rel_ref[(h * n_ro + ro) * n_co + j], acc)
            t = lax.fori_loop(0, n_co, body, jnp.zeros((GRID_W, GRID_W), F32), unroll=True)
            t = jnp.where(colmask, t, NEG_INF)
            for p in range(WIN_H):
                kidx = ro - (WIN_H - 1) + p
                if 0 <= kidx < WIN_H:
                    o_ref[p, h, :, kidx * GRID_W:(kidx + 1) * GRID_W] = t


def _bias_table(rel_bias):
    return pl.pallas_call(
        _bias_table_kernel,
        in_specs=[pl.BlockSpec(memory_space=pltpu.SMEM)],
        out_shape=jax.ShapeDtypeStruct((WIN_H, B_HEADS, GRID_W, B_KEYS), F32),
        name="bias_table",
    )(rel_bias.reshape(-1)).reshape(WIN_H, B_HEADS * GRID_W, B_KEYS)


def _attn_b_kernel(q_ref, kp_ref, kc_ref, kn_ref, vp_ref, vc_ref, vn_ref, bias_ref, g_ref, o_ref,
                   kbuf, vbuf, acc_ref, *, rows):
    gidx = pl.program_id(1)
    blk = B_GROUP * GRID_W
    kbuf[0:blk, :] = kp_ref[...]
    kbuf[blk:2 * blk, :] = kc_ref[...]
    kbuf[2 * blk:3 * blk, :] = kn_ref[...]
    vbuf[0:blk, :] = vp_ref[...]
    vbuf[blk:2 * blk, :] = vc_ref[...]
    vbuf[2 * blk:3 * blk, :] = vn_ref[...]
    same_head = _same_head_mask(B_HEADS * GRID_W, GRID_W)
    for j in range(B_GROUP):
        r = gidx * B_GROUP + j
        start = jnp.clip(r - WIN_H // 2, 0, rows - WIN_H)
        pat = r - start
        loc = pl.multiple_of((start - gidx * B_GROUP + B_GROUP) * GRID_W, GRID_W)
        kw = kbuf[pl.ds(loc, B_KEYS), :]
        vw = vbuf[pl.ds(loc, B_KEYS), :]
        qj = q_ref[j * GRID_W:(j + 1) * GRID_W, :]
        q = jnp.where(same_head, jnp.concatenate([qj] * B_HEADS, axis=0), jnp.zeros((), BF16))
        s = _nt_dot(q, kw) + bias_ref[pat]
        m = jnp.max(s, axis=-1, keepdims=True)
        p = jnp.exp(s - m)
        den = jnp.sum(p, axis=-1, keepdims=True)
        o = jnp.where(same_head, _dot(p.astype(BF16), vw) / den, 0.0)
        acc_ref[j * GRID_W:(j + 1) * GRID_W, :] = sum(o[h * GRID_W:(h + 1) * GRID_W, :] for h in range(B_HEADS))
    o_ref[...] = _rms(acc_ref[...], g_ref[...]).astype(BF16)


def _attn_b(bq, bk, bv, bias, g, bsz, seq):
    rows = seq // GRID_W
    assert rows % B_GROUP == 0 and rows >= WIN_H
    ng = rows // B_GROUP
    blk = B_GROUP * GRID_W
    cur = pl.BlockSpec((blk, B_WIDTH), lambda b, i: (b * ng + i, 0))
    prev = pl.BlockSpec((blk, B_WIDTH), lambda b, i: (b * ng + jnp.maximum(i - 1, 0), 0))
    nxt = pl.BlockSpec((blk, B_WIDTH), lambda b, i: (b * ng + jnp.minimum(i + 1, ng - 1), 0))
    return pl.pallas_call(
        functools.partial(_attn_b_kernel, rows=rows),
        grid=(bsz, ng),
        in_specs=[cur, prev, cur, nxt, prev, cur, nxt,
                  pl.BlockSpec(bias.shape, lambda b, i: (0, 0, 0)),
                  pl.BlockSpec((1, B_WIDTH), lambda b, i: (0, 0))],
        out_specs=cur,
        out_shape=jax.ShapeDtypeStruct((bsz * seq, B_WIDTH), BF16),
        scratch_shapes=[pltpu.VMEM((3 * blk, B_WIDTH), BF16), pltpu.VMEM((3 * blk, B_WIDTH), BF16),
                        pltpu.VMEM((blk, B_WIDTH), F32)],
        compiler_params=_cparams(("parallel", "parallel")),
        name="attn_b",
    )(bq, bk, bk, bk, bv, bv, bv, bias, g)


def _gla_decay(la_ref, reverse):
    seg = min(GLA_CUM_ROWS, la_ref.shape[0])
    bi = lax.broadcasted_iota(jnp.int32, (seg, seg), 0)
    bj = lax.broadcasted_iota(jnp.int32, (seg, seg), 1)
    in_chunk = (bi // C_CHUNK) == (bj // C_CHUNK)
    cum_w = (in_chunk & ((bj >= bi) if reverse else (bj <= bi))).astype(BF16)
    parts = []
    for r in range(0, la_ref.shape[0], seg):
        la_hi, la_lo = _split_bf16(la_ref[r:r + seg, :])
        parts.append(_dot(cum_w, la_hi) + _dot(cum_w, la_lo))
    return parts[0] if len(parts) == 1 else jnp.concatenate(parts, axis=0)


def _gla_chunk(q_ref, k_ref, v_ref, o_ref, b_all, st, c, reverse):
    same_head = _same_head_mask(C_HEADS * C_CHUNK, C_CHUNK)
    ti = lax.broadcasted_iota(jnp.int32, (C_CHUNK, C_HEADS * C_CHUNK), 0)
    si = lax.broadcasted_iota(jnp.int32, (C_CHUNK, C_HEADS * C_CHUNK), 1) % C_CHUNK
    tri = (si >= ti) if reverse else (si <= ti)
    zero = jnp.zeros((), BF16)
    rows = slice(c * C_CHUNK, (c + 1) * C_CHUNK)
    b = b_all[rows, :]
    b_tot = b[0:1, :] if reverse else b[C_CHUNK - 1:C_CHUNK, :]
    q_i = (q_ref[rows, :] * jnp.exp(b)).astype(BF16)
    kf = k_ref[rows, :]
    k_i = (kf * jnp.exp(-b)).astype(BF16)
    k_e = (kf * jnp.exp(b_tot - b)).astype(BF16)
    v = v_ref[rows, :]
    k_bd = jnp.where(same_head, jnp.concatenate([k_i] * C_HEADS, axis=0), zero)
    v_bd = jnp.where(same_head, jnp.concatenate([v] * C_HEADS, axis=0), zero)
    a = jnp.where(tri, _nt_dot(q_i, k_bd), 0.0)
    o_ref[rows, :] = _dot(a.astype(BF16), v_bd) + _nt_dot(q_i, st.astype(BF16))
    return st * jnp.exp(b_tot) + jnp.where(same_head, _tn_dot(v, k_e), 0.0)


def _gla_kernel(qf_ref, kf_ref, vf_ref, laf_ref, qb_ref, kb_ref, vb_ref, lab_ref, of_ref, ob_ref, sf_ref, sb_ref):
    @pl.when(pl.program_id(1) == 0)
    def _():
        sf_ref[...] = jnp.zeros_like(sf_ref)
        sb_ref[...] = jnp.zeros_like(sb_ref)

    n_chunks = qf_ref.shape[0] // C_CHUNK
    b_f = _gla_decay(laf_ref, reverse=False)
    b_b = _gla_decay(lab_ref, reverse=True)
    s_f, s_b = sf_ref[...], sb_ref[...]
    for c in range(n_chunks):
        s_f = _gla_chunk(qf_ref, kf_ref, vf_ref, of_ref, b_f, s_f, c, reverse=False)
        s_b = _gla_chunk(qb_ref, kb_ref, vb_ref, ob_ref, b_b, s_b, n_chunks - 1 - c, reverse=True)
    sf_ref[...] = s_f
    sb_ref[...] = s_b


def _gla(cq, ck, cv, la, bsz, seq):
    tb = min(GLA_BLOCK, seq)
    nb = seq // tb
    fwd = lambda col: pl.BlockSpec((tb, C_WIDTH), lambda b, i: (b * nb + i, col))
    bwd = lambda col: pl.BlockSpec((tb, C_WIDTH), lambda b, i: (b * nb + nb - 1 - i, col))
    out = jax.ShapeDtypeStruct((bsz * seq, C_WIDTH), F32)
    state = pltpu.VMEM((C_WIDTH, C_WIDTH), F32)
    return pl.pallas_call(
        _gla_kernel,
        grid=(bsz, nb),
        in_specs=[fwd(0), fwd(0), fwd(0), fwd(0), bwd(0), bwd(0), bwd(0), bwd(1)],
        out_specs=[fwd(0), bwd(0)],
        out_shape=[out, out],
        scratch_shapes=[state, state],
        compiler_params=_cparams(("parallel", "arbitrary")),
        name="gla",
    )(cq, ck, cv, la, cq, ck, cv, la)


def _out_proj_kernel(x_ref, oa_ref, ob_ref, of_ref, obw_ref, cg_ref, cng_ref, wo_ref, g2_ref, wr_ref,
                     x1_ref, h2_ref, aff_ref, mix_ref):
    gi = lax.broadcasted_iota(jnp.int32, (C_WIDTH, C_WIDTH), 0) // HEAD_DIM
    gj = lax.broadcasted_iota(jnp.int32, (C_WIDTH, C_WIDTH), 1) // HEAD_DIM
    ones_bd = (gi == gj).astype(BF16)
    half = x_ref.shape[0] // 2
    for rows in (slice(0, half), slice(half, 2 * half)):
        o = of_ref[rows, :] + obw_ref[rows, :]
        sq_hi, sq_lo = _split_bf16(o * o)
        ms = (_dot(sq_hi, ones_bd) + _dot(sq_lo, ones_bd)) * (1.0 / HEAD_DIM)
        cg = cg_ref[rows, :]
        oc = (o * lax.rsqrt(ms + EPS) * cng_ref[...]) * (cg / (1.0 + jnp.exp(-cg)))
        mix_ref[rows, 0:A_WIDTH] = oa_ref[rows, :]
        mix_ref[rows, A_WIDTH:A_WIDTH + B_WIDTH] = ob_ref[rows, :]
        mix_ref[rows, A_WIDTH + B_WIDTH:] = oc.astype(BF16)
        x1 = x_ref[rows, :] + _dot(mix_ref[rows, :], wo_ref[...])
        x1_ref[rows, :] = x1
        h2 = _rms(x1, g2_ref[...]).astype(BF16)
        h2_ref[rows, :] = h2
        logits = _dot(h2, wr_ref[...])
        lane = lax.broadcasted_iota(jnp.int32, logits.shape, 1)
        logits = jnp.where(lane < N_EXPERTS, logits, NEG_INF)
        p = jnp.exp(logits - jnp.max(logits, axis=1, keepdims=True))
        aff = p / jnp.sum(p, axis=1, keepdims=True)
        aff_ref[:, rows] = aff.T[0:N_EXPERTS, :]


def _out_proj(x, oa, ob, o_f, o_b, cg, cng, wo_bf, layer, g2, wr_cat):
    n = x.shape[0]
    tm = min(OUT_PROJ_ROWS, n)
    row = lambda width: pl.BlockSpec((tm, width), lambda i: (i, 0))
    full = lambda a: pl.BlockSpec(a.shape, lambda i: (0,) * a.ndim)
    return pl.pallas_call(
        _out_proj_kernel,
        grid=(n // tm,),
        in_specs=[row(D_MODEL), row(A_WIDTH), row(B_WIDTH), row(C_WIDTH), row(C_WIDTH), row(C_WIDTH),
                  full(cng), pl.BlockSpec((None, D_MODEL, D_MODEL), lambda i: (layer, 0, 0)), full(g2), full(wr_cat)],
        out_specs=[row(D_MODEL), row(D_MODEL), pl.BlockSpec((N_EXPERTS, tm), lambda i: (0, i))],
        out_shape=[jax.ShapeDtypeStruct((n, D_MODEL), F32), jax.ShapeDtypeStruct((n, D_MODEL), BF16),
                   jax.ShapeDtypeStruct((N_EXPERTS, n), F32)],
        scratch_shapes=[pltpu.VMEM((tm, D_MODEL), BF16)],
        compiler_params=_cparams(("parallel",)),
        name="out_proj",
    )(x, oa, ob, o_f, o_b, cg, cng, wo_bf, g2, wr_cat)


def _route_kernel(aff_ref, pos_ref, offs_ref, *, cap):
    n = aff_ref.shape[1]
    nt = n // ROUTE_TILE
    bits = pltpu.bitcast(aff_ref[...], jnp.int32)

    def search(it, thr):
        cand = thr | jnp.left_shift(jnp.int32(1), 30 - it)
        cnt = jnp.sum(jnp.where(bits >= cand, 1.0, 0.0), axis=1, keepdims=True)
        return jnp.where(cnt >= cap, cand, thr)

    thr = lax.fori_loop(0, 31, search, jnp.zeros((N_EXPERTS, 1), jnp.int32))
    n_gt = jnp.sum(jnp.where(bits > thr, 1.0, 0.0), axis=1, keepdims=True)
    n_tie = cap - n_gt

    ui = lax.broadcasted_iota(jnp.int32, (ROUTE_TILE, ROUTE_TILE), 0)
    uj = lax.broadcasted_iota(jnp.int32, (ROUTE_TILE, ROUTE_TILE), 1)
    upper = (ui <= uj).astype(BF16)
    lane = lax.broadcasted_iota(jnp.int32, offs_ref.shape, 1)

    def tile(i, carry):
        c_gt, c_eq, offs = carry
        col = pl.multiple_of(i * ROUTE_TILE, ROUTE_TILE)
        b = pltpu.bitcast(aff_ref[:, pl.ds(col, ROUTE_TILE)], jnp.int32)
        gt = jnp.where(b > thr, 1.0, 0.0)
        eq = jnp.where(b == thr, 1.0, 0.0)
        inc = _dot(jnp.concatenate([gt, eq], axis=0).astype(BF16), upper)
        gt_before = c_gt + inc[:N_EXPERTS] - gt
        eq_before = c_eq + inc[N_EXPERTS:] - eq
        sel = (gt > 0.0) | ((eq > 0.0) & (eq_before < n_tie))
        pos = gt_before + jnp.minimum(eq_before, n_tie)
        pos_ref[:, pl.ds(col, ROUTE_TILE)] = jnp.where(sel, pos, -1.0).astype(jnp.int32)
        start = c_gt + jnp.minimum(c_eq, n_tie)
        offs = jnp.where(lane == i, start.astype(jnp.int32), offs)
        return (c_gt + jnp.sum(gt, axis=1, keepdims=True), c_eq + jnp.sum(eq, axis=1, keepdims=True), offs)

    zero = jnp.zeros((N_EXPERTS, 1), F32)
    offs = jnp.where(lane == nt, cap, 0).astype(jnp.int32)
    _, _, offs = lax.fori_loop(0, nt, tile, (zero, zero, offs), unroll=4 if nt % 4 == 0 else 1)
    offs_ref[...] = offs


def _route(aff_t, cap):
    n = aff_t.shape[1]
    nt = n // ROUTE_TILE
    return pl.pallas_call(
        functools.partial(_route_kernel, cap=cap),
        out_shape=[jax.ShapeDtypeStruct((N_EXPERTS, n), jnp.int32),
                   jax.ShapeDtypeStruct((N_EXPERTS, nt + 1), jnp.int32)],
        compiler_params=pltpu.CompilerParams(vmem_limit_bytes=VMEM_LIMIT),
        name="route",
    )(aff_t)


def _align(v):
    return (v // MOE_ALIGN) * MOE_ALIGN


def _moe_passes(offs_ref, t, subs, x0, s, win):
    k = jnp.int32(1)
    for g in range(MOE_PAIR):
        need = offs_ref[x0 + g, t * subs + s + 1] - _align(offs_ref[x0 + g, t * subs + s])
        k = jnp.maximum(k, (need + win - 1) // win)
    return k


def _moe_windows(offs_ref, pos_ref, t, subs, x0, s, j, win):
    slot = lax.broadcasted_iota(jnp.int32, (win, ROUTE_TILE), 0)
    hits, rels = [], []
    for g in range(MOE_PAIR):
        x = x0 + g
        ws = _align(offs_ref[x, t * subs + s]) + j * win
        pos = pos_ref[pl.ds(x, 1), s * ROUTE_TILE:(s + 1) * ROUTE_TILE]
        hits.append((pos - ws) == slot)
        rels.append(pl.multiple_of(ws - _align(offs_ref[x, t * subs]), MOE_ALIGN))
    onehot = jnp.concatenate([jnp.where(h, 1.0, 0.0) for h in hits], axis=0).astype(BF16)
    return hits, rels, onehot


def _moe_chunks(offs_ref, t, subs, x):
    base = _align(offs_ref[x, t * subs])
    return base, (offs_ref[x, (t + 1) * subs] - base + MOE_CHUNK - 1) // MOE_CHUNK


def _moe_gather_kernel(offs_ref, h_ref, pos_ref, gate_ref, xe_ref, stage_ref, carry_ref, sem):
    t = pl.program_id(0)
    steps = pl.num_programs(1)
    e0 = pl.program_id(1) * MOE_STEP
    tt = h_ref.shape[0]
    subs = tt // ROUTE_TILE
    k = t * steps + pl.program_id(1)
    slot = k % 2
    x_ref = stage_ref.at[slot]

    def off(x, s):
        return offs_ref[x, t * subs + s]

    def copies(kk, half, act):
        step_t, step_q = kk // steps, kk % steps
        for g in range(MOE_STEP):
            x = step_q * MOE_STEP + g
            base, n_chunks = _moe_chunks(offs_ref, step_t, subs, x)

            def body(c, carry):
                r0 = pl.multiple_of(c * MOE_CHUNK, MOE_CHUNK)
                act(pltpu.make_async_copy(stage_ref.at[half, g, pl.ds(r0, MOE_CHUNK), :],
                                          xe_ref.at[x, pl.ds(pl.multiple_of(base + r0, MOE_ALIGN), MOE_CHUNK), :],
                                          sem.at[half]))
                return carry

            lax.fori_loop(0, n_chunks, body, 0)

    @pl.when(k >= 2)
    def _():
        copies(k - 2, slot, lambda cp: cp.wait())

    @pl.when(k < 2)
    def _():
        x_ref[...] = jnp.zeros(x_ref.shape, BF16)

    @pl.when(t == 0)
    def _():
        carry_ref[pl.ds(e0, MOE_STEP)] = jnp.zeros((MOE_STEP, MOE_ALIGN, XE_WIDTH), BF16)
        cap = xe_ref.shape[1] - MOE_CHUNK
        x_ref[0, 0:MOE_CHUNK, :] = jnp.zeros((MOE_CHUNK, XE_WIDTH), BF16)
        pads = [pltpu.make_async_copy(x_ref.at[0, 0:MOE_CHUNK, :], xe_ref.at[e0 + g, cap:cap + MOE_CHUNK, :],
                                      sem.at[slot]) for g in range(MOE_STEP)]
        for p in pads:
            p.start()
        for p in pads:
            p.wait()

    for g in range(MOE_STEP):
        x_ref[g, 0:MOE_ALIGN, :] = carry_ref[e0 + g]

    row = lax.broadcasted_iota(jnp.int32, (MOE_WIN, 1), 0)
    lane = lax.broadcasted_iota(jnp.int32, (MOE_WIN, XE_WIDTH - D_MODEL), 1)

    def gather_pass(pair, s, j):
        x0 = e0 + pair * MOE_PAIR
        cols = slice(s * ROUTE_TILE, (s + 1) * ROUTE_TILE)
        hits, rels, onehot = _moe_windows(offs_ref, pos_ref, t, subs, x0, s, j, MOE_WIN)
        r = _dot(onehot, h_ref[cols, :])
        for g in range(MOE_PAIR):
            gate = jnp.sum(jnp.where(hits[g], gate_ref[pl.ds(x0 + g, 1), cols], 0.0), axis=1, keepdims=True)
            g_hi = gate.astype(BF16).astype(F32)
            extra = jnp.where(lane == 0, g_hi, jnp.where(lane == 1, gate - g_hi, 0.0))
            new = jnp.concatenate([r[g * MOE_WIN:(g + 1) * MOE_WIN, :], extra], axis=1)
            ws = _align(off(x0 + g, s)) + j * MOE_WIN
            own = row >= off(x0 + g, s) - ws
            buf = pair * MOE_PAIR + g
            if isinstance(j, int):
                head = pl.ds(rels[g], MOE_ALIGN)
                x_ref[buf, head, :] = jnp.where(own[:MOE_ALIGN], new[:MOE_ALIGN],
                                                x_ref[buf, head, :].astype(F32)).astype(BF16)
                x_ref[buf, pl.ds(rels[g] + MOE_ALIGN, MOE_WIN - MOE_ALIGN), :] = new[MOE_ALIGN:].astype(BF16)
            else:
                own = own & (row < off(x0 + g, s + 1) - ws)
                win = pl.ds(rels[g], MOE_WIN)
                x_ref[buf, win, :] = jnp.where(own, new, x_ref[buf, win, :].astype(F32)).astype(BF16)

    for pair in range(MOE_STEP // MOE_PAIR):
        for s in range(subs):
            gather_pass(pair, s, 0)
    for pair in range(MOE_STEP // MOE_PAIR):
        for s in range(subs):
            lax.fori_loop(1, _moe_passes(offs_ref, t, subs, e0 + pair * MOE_PAIR, s, MOE_WIN),
                          lambda j, carry: (gather_pass(pair, s, j), carry)[1], 0)

    for g in range(MOE_STEP):
        x = e0 + g
        base, n_chunks = _moe_chunks(offs_ref, t, subs, x)
        last_group = pl.multiple_of(_align(off(x, subs)) - base, MOE_ALIGN)
        tail = x_ref[g, pl.ds(last_group, MOE_ALIGN), :].astype(F32)
        tail = jnp.where(row[:MOE_ALIGN] < off(x, subs) - base - last_group, tail, 0.0).astype(BF16)
        x_ref[g, pl.ds(last_group, MOE_ALIGN), :] = tail
        carry_ref[x] = tail

        def fill(i, carry):
            x_ref[g, pl.ds(pl.multiple_of(i * MOE_ALIGN, MOE_ALIGN), MOE_ALIGN), :] = jnp.zeros((MOE_ALIGN, XE_WIDTH), BF16)
            return carry

        lax.fori_loop(last_group // MOE_ALIGN + 1, n_chunks * MOE_CHUNK // MOE_ALIGN, fill, 0)

    copies(k, slot, lambda cp: cp.start())

    @pl.when(k == pl.num_programs(0) * steps - 1)
    def _():
        copies(k - 1, 1 - slot, lambda cp: cp.wait())
        copies(k, slot, lambda cp: cp.wait())


def _moe_ffn_kernel(x_ref, wg_ref, wu_ref, wd_ref, y_ref, wg_bf, wu_bf, wd_bf):
    j = pl.program_id(1)

    @pl.when(j == 0)
    def _():
        wg_bf[...] = wg_ref[...].astype(BF16)
        wu_bf[...] = wu_ref[...].astype(BF16)
        wd_bf[...] = wd_ref[...].astype(BF16)
        y_ref[0] = jnp.zeros(y_ref.shape[1:], BF16)

    @pl.when(j > 0)
    def _():
        x = x_ref[0, :, 0:D_MODEL]
        gate = jnp.sum(x_ref[0, :, D_MODEL:XE_WIDTH].astype(F32), axis=1, keepdims=True)
        hg = _dot(x, wg_bf[...])
        hu = _dot(x, wu_bf[...])
        act = (hg / (1.0 + jnp.exp(-hg))) * hu * gate
        y_ref[0] = _dot(act.astype(BF16), wd_bf[...]).astype(BF16)


def _moe_combine_kernel(offs_ref, x1_ref, pos_ref, y_ref, fg_ref, o_ref, y_buf, sem, *, final_norm):
    t = pl.program_id(0)
    q = pl.program_id(1)
    steps = pl.num_programs(1)
    e0 = q * MOE_STEP
    tt = x1_ref.shape[0]
    subs = tt // ROUTE_TILE
    k = t * steps + q
    slot = k % 2

    def fetch(kk, half, act):
        step_t, step_q = kk // steps, kk % steps
        for g in range(MOE_STEP):
            x = step_q * MOE_STEP + g
            base, n_chunks = _moe_chunks(offs_ref, step_t, subs, x)

            def body(c, carry):
                r0 = pl.multiple_of(c * MOE_CHUNK, MOE_CHUNK)
                act(pltpu.make_async_copy(y_ref.at[x, pl.ds(pl.multiple_of(base + r0, MOE_ALIGN), MOE_CHUNK), :],
                                          y_buf.at[half, g, pl.ds(r0, MOE_CHUNK), :], sem.at[half]))
                return carry

            lax.fori_loop(0, n_chunks, body, 0)

    @pl.when(k == 0)
    def _():
        y_buf[...] = jnp.zeros_like(y_buf)
        fetch(k, slot, lambda cp: cp.start())

    @pl.when(k + 1 < pl.num_programs(0) * steps)
    def _():
        fetch(k + 1, 1 - slot, lambda cp: cp.start())

    fetch(k, slot, lambda cp: cp.wait())

    @pl.when(q == 0)
    def _():
        o_ref[...] = x1_ref[...]

    def combine_pass(pair, s, j):
        cols = slice(s * ROUTE_TILE, (s + 1) * ROUTE_TILE)
        _, rels, onehot = _moe_windows(offs_ref, pos_ref, t, subs, e0 + pair * MOE_PAIR, s, j, MOE_COMBINE_WIN)
        ycat = jnp.concatenate([y_buf[slot, pair * MOE_PAIR + g, pl.ds(rels[g], MOE_COMBINE_WIN), :]
                                for g in range(MOE_PAIR)], axis=0)
        o_ref[cols, :] += _tn_dot(onehot, ycat)

    for pair in range(MOE_STEP // MOE_PAIR):
        for s in range(subs):
            combine_pass(pair, s, 0)
    for pair in range(MOE_STEP // MOE_PAIR):
        for s in range(subs):
            lax.fori_loop(1, _moe_passes(offs_ref, t, subs, e0 + pair * MOE_PAIR, s, MOE_COMBINE_WIN),
                          lambda j, carry: (combine_pass(pair, s, j), carry)[1], 0)

    if final_norm:
        @pl.when(q == pl.num_programs(1) - 1)
        def _():
            o_ref[...] = _rms(o_ref[...], fg_ref[...])


def _moe(offs, h2, x1, pos, gate, wg, wu, wd, layer, final_g, final_norm, cap):
    n = h2.shape[0]
    tt, tc = min(MOE_TILE, n), min(MOE_COMBINE_TILE, n)
    assert cap % MOE_CHUNK == 0 and n % tt == 0 and n % tc == 0
    steps = N_EXPERTS // MOE_STEP
    stage_rows = lambda tile, win: tile + MOE_ALIGN + win * pl.cdiv(ROUTE_TILE + MOE_ALIGN, win)
    tok = lambda tile: pl.BlockSpec((tile, D_MODEL), lambda t, q, offs: (t, 0))
    per_tok = lambda tile: pl.BlockSpec((N_EXPERTS, tile), lambda t, q, offs: (0, t))
    hbm = pl.BlockSpec(memory_space=pl.ANY)
    sems = ("arbitrary", "arbitrary")

    xe = pl.pallas_call(
        _moe_gather_kernel,
        grid_spec=pltpu.PrefetchScalarGridSpec(
            num_scalar_prefetch=1, grid=(n // tt, steps),
            in_specs=[tok(tt), per_tok(tt), per_tok(tt)], out_specs=hbm,
            scratch_shapes=[pltpu.VMEM((2, MOE_STEP, stage_rows(tt, MOE_WIN), XE_WIDTH), BF16),
                            pltpu.VMEM((N_EXPERTS, MOE_ALIGN, XE_WIDTH), BF16), pltpu.SemaphoreType.DMA((2,))],
        ),
        out_shape=jax.ShapeDtypeStruct((N_EXPERTS, cap + MOE_CHUNK, XE_WIDTH), BF16),
        compiler_params=_cparams(sems),
        name="moe_gather",
    )(offs, h2, pos, gate)

    fb = min(MOE_FFN_ROWS, cap)
    nb = cap // fb
    wspec = pl.BlockSpec((None, None, D_MODEL, D_MODEL), lambda e, j: (layer, e, 0, 0))
    w_bf = pltpu.VMEM((D_MODEL, D_MODEL), BF16)
    y = pl.pallas_call(
        _moe_ffn_kernel,
        grid=(N_EXPERTS, nb + 1),
        in_specs=[pl.BlockSpec((1, fb, XE_WIDTH), lambda e, j: (e, jnp.maximum(j - 1, 0), 0)), wspec, wspec, wspec],
        out_specs=pl.BlockSpec((1, fb, D_MODEL), lambda e, j: (e, jnp.where(j == 0, nb, j - 1), 0)),
        out_shape=jax.ShapeDtypeStruct((N_EXPERTS, cap + fb, D_MODEL), BF16),
        scratch_shapes=[w_bf, w_bf, w_bf],
        compiler_params=_cparams(("parallel", "arbitrary")),
        name="moe_ffn",
    )(xe, wg, wu, wd)

    return pl.pallas_call(
        functools.partial(_moe_combine_kernel, final_norm=final_norm),
        grid_spec=pltpu.PrefetchScalarGridSpec(
            num_scalar_prefetch=1, grid=(n // tc, steps),
            in_specs=[tok(tc), per_tok(tc), hbm, pl.BlockSpec((1, D_MODEL), lambda t, q, offs: (0, 0))],
            out_specs=tok(tc),
            scratch_shapes=[pltpu.VMEM((2, MOE_STEP, stage_rows(tc, MOE_COMBINE_WIN), D_MODEL), BF16),
                            pltpu.SemaphoreType.DMA((2,))],
        ),
        out_shape=jax.ShapeDtypeStruct((n, D_MODEL), F32),
        compiler_params=_cparams(sems),
        name="moe_combine",
    )(offs, x1, pos, y, final_g)


def _prep_layer(l, norm1_g, w_in, a_sink, a_norm_g, b_rel_bias, b_norm_g, c_alpha_w2_f, c_alpha_b_f,
                c_alpha_w2_b, c_alpha_b_b, c_norm_g, w_out, norm2_g, w_router, w_gate, w_up, w_down):
    z = jnp.zeros((C_LOWRANK, C_WIDTH), F32)
    w2 = jnp.concatenate([jnp.concatenate([c_alpha_w2_f[l], z], axis=1),
                          jnp.concatenate([z, c_alpha_w2_b[l]], axis=1)], axis=0).astype(BF16)
    b2 = jnp.concatenate([c_alpha_b_f[l], c_alpha_b_b[l]])[None, :]
    wr_cat = jnp.pad(w_router[l].astype(BF16), ((0, 0), (0, LANES - N_EXPERTS)))
    return dict(
        layer=l, g1=norm1_g[l][None, :], w_in=w_in.astype(BF16), w2=w2, b2=b2,
        sink=a_sink[l], a_g=a_norm_g[l][None, :], bias=_bias_table(b_rel_bias[l]), b_g=b_norm_g[l][None, :],
        c_g=c_norm_g[l][None, :], w_out=w_out.astype(BF16), g2=norm2_g[l][None, :],
        wr_cat=wr_cat, wg=w_gate, wu=w_up, wd=w_down)


def _layer(x, p, bsz, seq, rope, final_g, final_norm):
    n = bsz * seq
    l = p["layer"]
    aq, ak, av, bq, bk, bv, cq, ck, cv, cg, la = _in_proj(x, p["g1"], p["w_in"], l, p["w2"], p["b2"], seq, rope)
    oa = _attn_a(aq, ak, av, p["sink"], p["a_g"], bsz, seq)
    ob = _attn_b(bq, bk, bv, p["bias"], p["b_g"], bsz, seq)
    o_f, o_b = _gla(cq, ck, cv, la, bsz, seq)
    x1, h2, aff = _out_proj(x, oa, ob, o_f, o_b, cg, p["c_g"], p["w_out"], l, p["g2"], p["wr_cat"])
    cap = EC_CAPACITY * n // N_EXPERTS
    pos, offs = _route(aff, cap)
    return _moe(offs, h2, x1, pos, aff, p["wg"], p["wu"], p["wd"], l, final_g, final_norm, cap)


def _trunk(x, layers, rope, final_g):
    bsz, seq, _ = x.shape
    y = x.reshape(bsz * seq, D_MODEL)
    for l, p in enumerate(layers):
        y = _layer(y, p, bsz, seq, rope, final_g, final_norm=(l == len(layers) - 1))
    return y.reshape(bsz, seq, D_MODEL)


def kernel(x_prompt, x_sample, norm1_g, w_in, a_sink, a_norm_g, b_rel_bias, b_norm_g, c_alpha_w2_f, c_alpha_b_f,
           c_alpha_w2_b, c_alpha_b_b, c_norm_g, w_out, norm2_g, w_router, w_gate, w_up, w_down, final_g):
    depth = w_in.shape[0]
    layers = [_prep_layer(l, norm1_g, w_in, a_sink, a_norm_g, b_rel_bias, b_norm_g, c_alpha_w2_f, c_alpha_b_f,
                          c_alpha_w2_b, c_alpha_b_b, c_norm_g, w_out, norm2_g, w_router, w_gate, w_up, w_down)
              for l in range(depth)]
    fg = final_g[None, :]
    rope = _rope_tables(max(x_prompt.shape[1], x_sample.shape[1]))
    return _trunk(x_prompt, layers, rope, fg), _trunk(x_sample, layers, rope, fg)
```

```python
import functools

import jax
import jax.numpy as jnp
from jax import lax
from jax.experimental import pallas as pl
from jax.experimental.pallas import tpu as pltpu

F32 = jnp.float32
BF16 = jnp.bfloat16

D_MODEL = 1024
HEAD_DIM = 64
A_WIDTH = 512
A_HEADS = 8
A_KV_HEADS = 2
A_KV_WIDTH = 128
WINDOW = 128
ROPE_THETA = 500000.0
ROPE_DIM = 16
B_WIDTH = 256
B_HEADS = 4
GRID_W = 64
WIN_H = 8
WIN_W = 16
C_WIDTH = 256
C_HEADS = 4
C_LOWRANK = 16
C_TAU = 16.0
C_CHUNK = 64
N_EXPERTS = 16
EC_CAPACITY = 2
EPS = 1e-6
NEG_INF = -1e30
IN_WIDTH = 2592

_OFF_AQ, _OFF_AK, _OFF_AV = 0, 512, 640
_OFF_BQ, _OFF_BK, _OFF_BV = 768, 1024, 1280
_OFF_CQ, _OFF_CK, _OFF_CV, _OFF_CG = 1536, 1792, 2048, 2304
_OFF_LR = 2560

LANES = 128
VMEM_LIMIT = 56 * 1024 * 1024

IN_PROJ_ROWS = 1024
OUT_PROJ_ROWS = 1024
GLA_BLOCK = 1024
GLA_CUM_ROWS = 256
ROUTE_TILE = 256
MOE_TILE = 2048
MOE_COMBINE_TILE = 1024
MOE_STEP = 4
MOE_PAIR = 4
MOE_WIN = 64
MOE_COMBINE_WIN = 128
MOE_ALIGN = 16
MOE_CHUNK = 128
MOE_FFN_ROWS = 1024
XE_WIDTH = D_MODEL + LANES


def _cparams(sem, fuse_inputs=None):
    return pltpu.CompilerParams(dimension_semantics=sem, vmem_limit_bytes=VMEM_LIMIT, allow_input_fusion=fuse_inputs)


def _nt_dot(a, b):
    return lax.dot_general(a, b, (((1,), (1,)), ((), ())), preferred_element_type=F32)


def _tn_dot(a, b):
    return lax.dot_general(a, b, (((0,), (0,)), ((), ())), preferred_element_type=F32)


def _dot(a, b):
    return jnp.dot(a, b, preferred_element_type=F32)


def _rms(x, g):
    return x * lax.rsqrt(jnp.mean(x * x, axis=-1, keepdims=True) + EPS) * g


def _same_head_mask(n_rows, rows_per_head):
    width = (n_rows // rows_per_head) * HEAD_DIM
    ri = lax.broadcasted_iota(jnp.int32, (n_rows, width), 0) // rows_per_head
    li = lax.broadcasted_iota(jnp.int32, (n_rows, width), 1) // HEAD_DIM
    return ri == li


def _split_bf16(x):
    hi = x.astype(BF16)
    return hi, (x - hi.astype(F32)).astype(BF16)


def _in_proj_kernel(x_ref, g_ref, w_ref, cos_ref, s1_ref, s2_ref, w2_ref, b2_ref,
                    aq_ref, ak_ref, av_ref, bq_ref, bk_ref, bv_ref,
                    cq_ref, ck_ref, cv_ref, cg_ref, la_ref):
    h = _rms(x_ref[...], g_ref[...]).astype(BF16)
    y = _dot(h, w_ref[...])

    def proj(lo, width):
        return y[:, lo:lo + width]

    cos, s1, s2 = cos_ref[...], s1_ref[...], s2_ref[...]

    def rope(t):
        return t * cos + pltpu.roll(t, LANES - ROPE_DIM // 2, 1) * s1 + pltpu.roll(t, ROPE_DIM // 2, 1) * s2

    scale = HEAD_DIM ** -0.5
    aq = proj(_OFF_AQ, A_WIDTH)
    for c in range(A_WIDTH // LANES):
        aq_ref[:, c * LANES:(c + 1) * LANES] = (rope(aq[:, c * LANES:(c + 1) * LANES]) * scale).astype(BF16)
    ak_ref[...] = rope(proj(_OFF_AK, A_KV_WIDTH)).astype(BF16)
    av_ref[...] = proj(_OFF_AV, A_KV_WIDTH).astype(BF16)
    bq_ref[...] = (proj(_OFF_BQ, B_WIDTH) * scale).astype(BF16)
    bk_ref[...] = proj(_OFF_BK, B_WIDTH).astype(BF16)
    bv_ref[...] = proj(_OFF_BV, B_WIDTH).astype(BF16)
    cq_ref[...] = proj(_OFF_CQ, C_WIDTH) * scale
    ck_ref[...] = proj(_OFF_CK, C_WIDTH)
    cv_ref[...] = proj(_OFF_CV, C_WIDTH).astype(BF16)
    cg_ref[...] = proj(_OFF_CG, C_WIDTH)
    z = _dot(proj(_OFF_LR, 2 * C_LOWRANK).astype(BF16), w2_ref[...]) + b2_ref[...]
    la_ref[...] = (jnp.minimum(z, 0.0) - jnp.log(1.0 + jnp.exp(-jnp.abs(z)))) * (1.0 / C_TAU)


def _rope_tables(seq):
    half = ROPE_DIM // 2
    inv = jnp.power(jnp.float32(ROPE_THETA), -jnp.arange(half, dtype=F32) * (2.0 / ROPE_DIM))
    d = jnp.arange(LANES) % HEAD_DIM
    ang = jnp.arange(seq, dtype=F32)[:, None] * inv[d % half][None, :]
    cos, sin = jnp.cos(ang), jnp.sin(ang)
    first, second = (d < half)[None, :], ((d >= half) & (d < ROPE_DIM))[None, :]
    return (jnp.where(first | second, cos, 1.0), jnp.where(first, -sin, 0.0), jnp.where(second, sin, 0.0))


def _in_proj(x, g1, w_in_bf, layer, w2, b2, seq, rope):
    n = x.shape[0]
    tm = min(IN_PROJ_ROWS, seq)
    per_seq = seq // tm
    cos, s1, s2 = rope
    row = lambda width: pl.BlockSpec((tm, width), lambda i: (i, 0))
    full = lambda a: pl.BlockSpec(a.shape, lambda i: (0,) * a.ndim)
    tab = pl.BlockSpec((tm, LANES), lambda i: (i % per_seq, 0))
    widths = [(A_WIDTH, BF16), (A_KV_WIDTH, BF16), (A_KV_WIDTH, BF16), (B_WIDTH, BF16), (B_WIDTH, BF16),
              (B_WIDTH, BF16), (C_WIDTH, F32), (C_WIDTH, F32), (C_WIDTH, BF16), (C_WIDTH, F32), (2 * C_WIDTH, F32)]
    return pl.pallas_call(
        _in_proj_kernel,
        grid=(n // tm,),
        in_specs=[row(D_MODEL), full(g1), pl.BlockSpec((None, D_MODEL, IN_WIDTH), lambda i: (layer, 0, 0)),
                  tab, tab, tab, full(w2), full(b2)],
        out_specs=[row(w) for w, _ in widths],
        out_shape=[jax.ShapeDtypeStruct((n, w), dt) for w, dt in widths],
        compiler_params=_cparams(("parallel",), fuse_inputs=[False, False, True, False, False, False, False, False]),
        name="in_proj",
    )(x, g1, w_in_bf, cos, s1, s2, w2, b2)


A_QBLOCKS = 4


def _attn_a_kernel(sink_ref, q_ref, kp_ref, kc_ref, kn_ref, vp_ref, vc_ref, vn_ref, g_ref, o_ref, acc_ref, *, steps):
    i = pl.program_id(1)
    grp = A_HEADS // A_KV_HEADS
    k = jnp.concatenate([kp_ref[...], kc_ref[...], kn_ref[...]], axis=0)
    v = jnp.concatenate([vp_ref[...], vc_ref[...], vn_ref[...]], axis=0)
    qq = lax.broadcasted_iota(jnp.int32, (WINDOW, 3 * WINDOW), 0)
    kk = lax.broadcasted_iota(jnp.int32, (WINDOW, 3 * WINDOW), 1)
    band = jnp.abs(kk - WINDOW - qq) <= WINDOW
    for blk in range(A_QBLOCKS):
        visible = band
        if blk == 0:
            visible = visible & jnp.logical_not((kk < WINDOW) & (i == 0))
        if blk == A_QBLOCKS - 1:
            visible = visible & jnp.logical_not((kk >= 2 * WINDOW) & (i == steps - 1))
        bias = jnp.where(visible, 0.0, NEG_INF)
        rows = slice(blk * WINDOW, (blk + 1) * WINDOW)
        keys = slice(blk * WINDOW, (blk + 3) * WINDOW)
        for kv in range(A_KV_HEADS):
            heads = range(kv * grp, (kv + 1) * grp)
            q = jnp.concatenate([q_ref[rows, h * HEAD_DIM:(h + 1) * HEAD_DIM] for h in heads], axis=0)
            s = _nt_dot(q, k[keys, kv * HEAD_DIM:(kv + 1) * HEAD_DIM])
            ps, dens = [], []
            for g_i, h in enumerate(heads):
                sh = s[g_i * WINDOW:(g_i + 1) * WINDOW, :] + bias
                sink = sink_ref[h]
                m = jnp.maximum(jnp.max(sh, axis=-1, keepdims=True), sink)
                p = jnp.exp(sh - m)
                dens.append(jnp.sum(p, axis=-1, keepdims=True) + jnp.exp(sink - m))
                ps.append(p.astype(BF16))
            o = _dot(jnp.concatenate(ps, axis=0), v[keys, kv * HEAD_DIM:(kv + 1) * HEAD_DIM])
            for g_i, h in enumerate(heads):
                acc_ref[rows, h * HEAD_DIM:(h + 1) * HEAD_DIM] = o[g_i * WINDOW:(g_i + 1) * WINDOW, :] / dens[g_i]
    o_ref[...] = _rms(acc_ref[...], g_ref[...]).astype(BF16)


def _attn_a(aq, ak, av, sink, g, bsz, seq):
    nb = seq // WINDOW
    assert nb % A_QBLOCKS == 0
    steps = nb // A_QBLOCKS
    qrows = A_QBLOCKS * WINDOW
    qspec = pl.BlockSpec((qrows, A_WIDTH), lambda b, i: (b * steps + i, 0))
    kprev = pl.BlockSpec((WINDOW, A_KV_WIDTH), lambda b, i: (b * nb + jnp.maximum(i * A_QBLOCKS - 1, 0), 0))
    kcur = pl.BlockSpec((qrows, A_KV_WIDTH), lambda b, i: (b * steps + i, 0))
    knext = pl.BlockSpec((WINDOW, A_KV_WIDTH), lambda b, i: (b * nb + jnp.minimum((i + 1) * A_QBLOCKS, nb - 1), 0))
    return pl.pallas_call(
        functools.partial(_attn_a_kernel, steps=steps),
        grid=(bsz, steps),
        in_specs=[pl.BlockSpec(memory_space=pltpu.SMEM), qspec, kprev, kcur, knext, kprev, kcur, knext,
                  pl.BlockSpec((1, A_WIDTH), lambda b, i: (0, 0))],
        out_specs=qspec,
        out_shape=jax.ShapeDtypeStruct((bsz * seq, A_WIDTH), BF16),
        scratch_shapes=[pltpu.VMEM((qrows, A_WIDTH), F32)],
        compiler_params=_cparams(("parallel", "parallel")),
        name="attn_a",
    )(sink, aq, ak, ak, ak, av, av, av, g)


B_GROUP = 16
B_KEYS = WIN_H * GRID_W


def _bias_table_kernel(rel_ref, o_ref):
    c = lax.broadcasted_iota(jnp.int32, (GRID_W, GRID_W), 0)
    w = lax.broadcasted_iota(jnp.int32, (GRID_W, GRID_W), 1)
    cstart = jnp.clip(c - WIN_W // 2, 0, GRID_W - WIN_W)
    colmask = (w >= cstart) & (w < cstart + WIN_W)
    col_off = jnp.clip(w - c + (WIN_W - 1), 0, 2 * WIN_W - 2)
    n_ro, n_co = 2 * WIN_H - 1, 2 * WIN_W - 1
    for h in range(B_HEADS):
        for ro in range(n_ro):
            def body(j, acc):
                return jnp.where(col_off == j, rel_ref[(h * n_ro + ro) * n_co + j], acc)
            t = lax.fori_loop(0, n_co, body, jnp.zeros((GRID_W, GRID_W), F32), unroll=True)
            t = jnp.where(colmask, t, NEG_INF)
            for p in range(WIN_H):
                kidx = ro - (WIN_H - 1) + p
                if 0 <= kidx < WIN_H:
                    o_ref[p, h, :, kidx * GRID_W:(kidx + 1) * GRID_W] = t


def _bias_table(rel_bias):
    return pl.pallas_call(
        _bias_table_kernel,
        in_specs=[pl.BlockSpec(memory_space=pltpu.SMEM)],
        out_shape=jax.ShapeDtypeStruct((WIN_H, B_HEADS, GRID_W, B_KEYS), F32),
        name="bias_table",
    )(rel_bias.reshape(-1)).reshape(WIN_H, B_HEADS * GRID_W, B_KEYS)


def _attn_b_kernel(q_ref, kp_ref, kc_ref, kn_ref, vp_ref, vc_ref, vn_ref, bias_ref, g_ref, o_ref,
                   kbuf, vbuf, acc_ref, *, rows):
    gidx = pl.program_id(1)
    blk = B_GROUP * GRID_W
    kbuf[0:blk, :] = kp_ref[...]
    kbuf[blk:2 * blk, :] = kc_ref[...]
    kbuf[2 * blk:3 * blk, :] = kn_ref[...]
    vbuf[0:blk, :] = vp_ref[...]
    vbuf[blk:2 * blk, :] = vc_ref[...]
    vbuf[2 * blk:3 * blk, :] = vn_ref[...]
    same_head = _same_head_mask(B_HEADS * GRID_W, GRID_W)
    for j in range(B_GROUP):
        r = gidx * B_GROUP + j
        start = jnp.clip(r - WIN_H // 2, 0, rows - WIN_H)
        pat = r - start
        loc = pl.multiple_of((start - gidx * B_GROUP + B_GROUP) * GRID_W, GRID_W)
        kw = kbuf[pl.ds(loc, B_KEYS), :]
        vw = vbuf[pl.ds(loc, B_KEYS), :]
        qj = q_ref[j * GRID_W:(j + 1) * GRID_W, :]
        q = jnp.where(same_head, jnp.concatenate([qj] * B_HEADS, axis=0), jnp.zeros((), BF16))
        s = _nt_dot(q, kw) + bias_ref[pat]
        m = jnp.max(s, axis=-1, keepdims=True)
        p = jnp.exp(s - m)
        den = jnp.sum(p, axis=-1, keepdims=True)
        o = jnp.where(same_head, _dot(p.astype(BF16), vw) / den, 0.0)
        acc_ref[j * GRID_W:(j + 1) * GRID_W, :] = sum(o[h * GRID_W:(h + 1) * GRID_W, :] for h in range(B_HEADS))
    o_ref[...] = _rms(acc_ref[...], g_ref[...]).astype(BF16)


def _attn_b(bq, bk, bv, bias, g, bsz, seq):
    rows = seq // GRID_W
    assert rows % B_GROUP == 0 and rows >= WIN_H
    ng = rows // B_GROUP
    blk = B_GROUP * GRID_W
    cur = pl.BlockSpec((blk, B_WIDTH), lambda b, i: (b * ng + i, 0))
    prev = pl.BlockSpec((blk, B_WIDTH), lambda b, i: (b * ng + jnp.maximum(i - 1, 0), 0))
    nxt = pl.BlockSpec((blk, B_WIDTH), lambda b, i: (b * ng + jnp.minimum(i + 1, ng - 1), 0))
    return pl.pallas_call(
        functools.partial(_attn_b_kernel, rows=rows),
        grid=(bsz, ng),
        in_specs=[cur, prev, cur, nxt, prev, cur, nxt,
                  pl.BlockSpec(bias.shape, lambda b, i: (0, 0, 0)),
                  pl.BlockSpec((1, B_WIDTH), lambda b, i: (0, 0))],
        out_specs=cur,
        out_shape=jax.ShapeDtypeStruct((bsz * seq, B_WIDTH), BF16),
        scratch_shapes=[pltpu.VMEM((3 * blk, B_WIDTH), BF16), pltpu.VMEM((3 * blk, B_WIDTH), BF16),
                        pltpu.VMEM((blk, B_WIDTH), F32)],
        compiler_params=_cparams(("parallel", "parallel")),
        name="attn_b",
    )(bq, bk, bk, bk, bv, bv, bv, bias, g)


def _gla_decay(la_ref, reverse):
    seg = min(GLA_CUM_ROWS, la_ref.shape[0])
    bi = lax.broadcasted_iota(jnp.int32, (seg, seg), 0)
    bj = lax.broadcasted_iota(jnp.int32, (seg, seg), 1)
    in_chunk = (bi // C_CHUNK) == (bj // C_CHUNK)
    cum_w = (in_chunk & ((bj >= bi) if reverse else (bj <= bi))).astype(BF16)
    parts = []
    for r in range(0, la_ref.shape[0], seg):
        la_hi, la_lo = _split_bf16(la_ref[r:r + seg, :])
        parts.append(_dot(cum_w, la_hi) + _dot(cum_w, la_lo))
    return parts[0] if len(parts) == 1 else jnp.concatenate(parts, axis=0)


def _gla_chunk(q_ref, k_ref, v_ref, o_ref, b_all, st, c, reverse):
    same_head = _same_head_mask(C_HEADS * C_CHUNK, C_CHUNK)
    ti = lax.broadcasted_iota(jnp.int32, (C_CHUNK, C_HEADS * C_CHUNK), 0)
    si = lax.broadcasted_iota(jnp.int32, (C_CHUNK, C_HEADS * C_CHUNK), 1) % C_CHUNK
    tri = (si >= ti) if reverse else (si <= ti)
    zero = jnp.zeros((), BF16)
    rows = slice(c * C_CHUNK, (c + 1) * C_CHUNK)
    b = b_all[rows, :]
    b_tot = b[0:1, :] if reverse else b[C_CHUNK - 1:C_CHUNK, :]
    q_i = (q_ref[rows, :] * jnp.exp(b)).astype(BF16)
    kf = k_ref[rows, :]
    k_i = (kf * jnp.exp(-b)).astype(BF16)
    k_e = (kf * jnp.exp(b_tot - b)).astype(BF16)
    v = v_ref[rows, :]
    k_bd = jnp.where(same_head, jnp.concatenate([k_i] * C_HEADS, axis=0), zero)
    v_bd = jnp.where(same_head, jnp.concatenate([v] * C_HEADS, axis=0), zero)
    a = jnp.where(tri, _nt_dot(q_i, k_bd), 0.0)
    o_ref[rows, :] = _dot(a.astype(BF16), v_bd) + _nt_dot(q_i, st.astype(BF16))
    return st * jnp.exp(b_tot) + jnp.where(same_head, _tn_dot(v, k_e), 0.0)


def _gla_kernel(qf_ref, kf_ref, vf_ref, laf_ref, qb_ref, kb_ref, vb_ref, lab_ref, of_ref, ob_ref, sf_ref, sb_ref):
    @pl.when(pl.program_id(1) == 0)
    def _():
        sf_ref[...] = jnp.zeros_like(sf_ref)
        sb_ref[...] = jnp.zeros_like(sb_ref)

    n_chunks = qf_ref.shape[0] // C_CHUNK
    b_f = _gla_decay(laf_ref, reverse=False)
    b_b = _gla_decay(lab_ref, reverse=True)
    s_f, s_b = sf_ref[...], sb_ref[...]
    for c in range(n_chunks):
        s_f = _gla_chunk(qf_ref, kf_ref, vf_ref, of_ref, b_f, s_f, c, reverse=False)
        s_b = _gla_chunk(qb_ref, kb_ref, vb_ref, ob_ref, b_b, s_b, n_chunks - 1 - c, reverse=True)
    sf_ref[...] = s_f
    sb_ref[...] = s_b


def _gla(cq, ck, cv, la, bsz, seq):
    tb = min(GLA_BLOCK, seq)
    nb = seq // tb
    fwd = lambda col: pl.BlockSpec((tb, C_WIDTH), lambda b, i: (b * nb + i, col))
    bwd = lambda col: pl.BlockSpec((tb, C_WIDTH), lambda b, i: (b * nb + nb - 1 - i, col))
    out = jax.ShapeDtypeStruct((bsz * seq, C_WIDTH), F32)
    state = pltpu.VMEM((C_WIDTH, C_WIDTH), F32)
    return pl.pallas_call(
        _gla_kernel,
        grid=(bsz, nb),
        in_specs=[fwd(0), fwd(0), fwd(0), fwd(0), bwd(0), bwd(0), bwd(0), bwd(1)],
        out_specs=[fwd(0), bwd(0)],
        out_shape=[out, out],
        scratch_shapes=[state, state],
        compiler_params=_cparams(("parallel", "arbitrary")),
        name="gla",
    )(cq, ck, cv, la, cq, ck, cv, la)


def _out_proj_kernel(x_ref, oa_ref, ob_ref, of_ref, obw_ref, cg_ref, cng_ref, wo_ref, g2_ref, wr_ref,
                     x1_ref, h2_ref, aff_ref, mix_ref):
    gi = lax.broadcasted_iota(jnp.int32, (C_WIDTH, C_WIDTH), 0) // HEAD_DIM
    gj = lax.broadcasted_iota(jnp.int32, (C_WIDTH, C_WIDTH), 1) // HEAD_DIM
    ones_bd = (gi == gj).astype(BF16)
    half = x_ref.shape[0] // 2
    for rows in (slice(0, half), slice(half, 2 * half)):
        o = of_ref[rows, :] + obw_ref[rows, :]
        sq_hi, sq_lo = _split_bf16(o * o)
        ms = (_dot(sq_hi, ones_bd) + _dot(sq_lo, ones_bd)) * (1.0 / HEAD_DIM)
        cg = cg_ref[rows, :]
        oc = (o * lax.rsqrt(ms + EPS) * cng_ref[...]) * (cg / (1.0 + jnp.exp(-cg)))
        mix_ref[rows, 0:A_WIDTH] = oa_ref[rows, :]
        mix_ref[rows, A_WIDTH:A_WIDTH + B_WIDTH] = ob_ref[rows, :]
        mix_ref[rows, A_WIDTH + B_WIDTH:] = oc.astype(BF16)
        x1 = x_ref[rows, :] + _dot(mix_ref[rows, :], wo_ref[...])
        x1_ref[rows, :] = x1
        h2 = _rms(x1, g2_ref[...]).astype(BF16)
        h2_ref[rows, :] = h2
        logits = _dot(h2, wr_ref[...])
        lane = lax.broadcasted_iota(jnp.int32, logits.shape, 1)
        logits = jnp.where(lane < N_EXPERTS, logits, NEG_INF)
        p = jnp.exp(logits - jnp.max(logits, axis=1, keepdims=True))
        aff = p / jnp.sum(p, axis=1, keepdims=True)
        aff_ref[:, rows] = aff.T[0:N_EXPERTS, :]


def _out_proj(x, oa, ob, o_f, o_b, cg, cng, wo_bf, layer, g2, wr_cat):
    n = x.shape[0]
    tm = min(OUT_PROJ_ROWS, n)
    row = lambda width: pl.BlockSpec((tm, width), lambda i: (i, 0))
    full = lambda a: pl.BlockSpec(a.shape, lambda i: (0,) * a.ndim)
    return pl.pallas_call(
        _out_proj_kernel,
        grid=(n // tm,),
        in_specs=[row(D_MODEL), row(A_WIDTH), row(B_WIDTH), row(C_WIDTH), row(C_WIDTH), row(C_WIDTH),
                  full(cng), pl.BlockSpec((None, D_MODEL, D_MODEL), lambda i: (layer, 0, 0)), full(g2), full(wr_cat)],
        out_specs=[row(D_MODEL), row(D_MODEL), pl.BlockSpec((N_EXPERTS, tm), lambda i: (0, i))],
        out_shape=[jax.ShapeDtypeStruct((n, D_MODEL), F32), jax.ShapeDtypeStruct((n, D_MODEL), BF16),
                   jax.ShapeDtypeStruct((N_EXPERTS, n), F32)],
        scratch_shapes=[pltpu.VMEM((tm, D_MODEL), BF16)],
        compiler_params=_cparams(("parallel",), fuse_inputs=[False] * 7 + [True, False, False]),
        name="out_proj",
    )(x, oa, ob, o_f, o_b, cg, cng, wo_bf, g2, wr_cat)


def _route_kernel(aff_ref, pos_ref, offs_ref, *, cap):
    n = aff_ref.shape[1]
    nt = n // ROUTE_TILE
    bits = pltpu.bitcast(aff_ref[...], jnp.int32)

    def search(it, thr):
        cand = thr | jnp.left_shift(jnp.int32(1), 30 - it)
        cnt = jnp.sum(jnp.where(bits >= cand, 1.0, 0.0), axis=1, keepdims=True)
        return jnp.where(cnt >= cap, cand, thr)

    thr = lax.fori_loop(0, 31, search, jnp.zeros((N_EXPERTS, 1), jnp.int32))
    n_gt = jnp.sum(jnp.where(bits > thr, 1.0, 0.0), axis=1, keepdims=True)
    n_tie = cap - n_gt

    ui = lax.broadcasted_iota(jnp.int32, (ROUTE_TILE, ROUTE_TILE), 0)
    uj = lax.broadcasted_iota(jnp.int32, (ROUTE_TILE, ROUTE_TILE), 1)
    upper = (ui <= uj).astype(BF16)
    lane = lax.broadcasted_iota(jnp.int32, offs_ref.shape, 1)

    def tile(i, carry):
        c_gt, c_eq, offs = carry
        col = pl.multiple_of(i * ROUTE_TILE, ROUTE_TILE)
        b = pltpu.bitcast(aff_ref[:, pl.ds(col, ROUTE_TILE)], jnp.int32)
        gt = jnp.where(b > thr, 1.0, 0.0)
        eq = jnp.where(b == thr, 1.0, 0.0)
        inc = _dot(jnp.concatenate([gt, eq], axis=0).astype(BF16), upper)
        gt_before = c_gt + inc[:N_EXPERTS] - gt
        eq_before = c_eq + inc[N_EXPERTS:] - eq
        sel = (gt > 0.0) | ((eq > 0.0) & (eq_before < n_tie))
        pos = gt_before + jnp.minimum(eq_before, n_tie)
        pos_ref[:, pl.ds(col, ROUTE_TILE)] = jnp.where(sel, pos, -1.0).astype(jnp.int32)
        start = c_gt + jnp.minimum(c_eq, n_tie)
        offs = jnp.where(lane == i, start.astype(jnp.int32), offs)
        return (c_gt + jnp.sum(gt, axis=1, keepdims=True), c_eq + jnp.sum(eq, axis=1, keepdims=True), offs)

    zero = jnp.zeros((N_EXPERTS, 1), F32)
    offs = jnp.where(lane == nt, cap, 0).astype(jnp.int32)
    _, _, offs = lax.fori_loop(0, nt, tile, (zero, zero, offs), unroll=4 if nt % 4 == 0 else 1)
    offs_ref[...] = offs


def _route(aff_t, cap):
    n = aff_t.shape[1]
    nt = n // ROUTE_TILE
    return pl.pallas_call(
        functools.partial(_route_kernel, cap=cap),
        out_shape=[jax.ShapeDtypeStruct((N_EXPERTS, n), jnp.int32),
                   jax.ShapeDtypeStruct((N_EXPERTS, nt + 1), jnp.int32)],
        compiler_params=pltpu.CompilerParams(vmem_limit_bytes=VMEM_LIMIT),
        name="route",
    )(aff_t)


def _align(v):
    return (v // MOE_ALIGN) * MOE_ALIGN


def _moe_passes(offs_ref, t, subs, x0, s, win):
    k = jnp.int32(1)
    for g in range(MOE_PAIR):
        need = offs_ref[x0 + g, t * subs + s + 1] - _align(offs_ref[x0 + g, t * subs + s])
        k = jnp.maximum(k, (need + win - 1) // win)
    return k


def _moe_windows(offs_ref, pos_ref, t, subs, x0, s, j, win):
    slot = lax.broadcasted_iota(jnp.int32, (win, ROUTE_TILE), 0)
    hits, rels = [], []
    for g in range(MOE_PAIR):
        x = x0 + g
        ws = _align(offs_ref[x, t * subs + s]) + j * win
        pos = pos_ref[pl.ds(x, 1), s * ROUTE_TILE:(s + 1) * ROUTE_TILE]
        hits.append((pos - ws) == slot)
        rels.append(pl.multiple_of(ws - _align(offs_ref[x, t * subs]), MOE_ALIGN))
    onehot = jnp.concatenate([jnp.where(h, 1.0, 0.0) for h in hits], axis=0).astype(BF16)
    return hits, rels, onehot


def _moe_chunks(offs_ref, t, subs, x):
    base = _align(offs_ref[x, t * subs])
    return base, (offs_ref[x, (t + 1) * subs] - base + MOE_CHUNK - 1) // MOE_CHUNK


def _moe_gather_kernel(offs_ref, h_ref, pos_ref, gate_ref, xe_ref, stage_ref, carry_ref, sem):
    t = pl.program_id(0)
    steps = pl.num_programs(1)
    e0 = pl.program_id(1) * MOE_STEP
    tt = h_ref.shape[0]
    subs = tt // ROUTE_TILE
    k = t * steps + pl.program_id(1)
    slot = k % 2
    x_ref = stage_ref.at[slot]

    def off(x, s):
        return offs_ref[x, t * subs + s]

    def copies(kk, half, act):
        step_t, step_q = kk // steps, kk % steps
        for g in range(MOE_STEP):
            x = step_q * MOE_STEP + g
            base, n_chunks = _moe_chunks(offs_ref, step_t, subs, x)

            def body(c, carry):
                r0 = pl.multiple_of(c * MOE_CHUNK, MOE_CHUNK)
                act(pltpu.make_async_copy(stage_ref.at[half, g, pl.ds(r0, MOE_CHUNK), :],
                                          xe_ref.at[x, pl.ds(pl.multiple_of(base + r0, MOE_ALIGN), MOE_CHUNK), :],
                                          sem.at[half]))
                return carry

            lax.fori_loop(0, n_chunks, body, 0)

    @pl.when(k >= 2)
    def _():
        copies(k - 2, slot, lambda cp: cp.wait())

    @pl.when(k < 2)
    def _():
        x_ref[...] = jnp.zeros(x_ref.shape, BF16)

    @pl.when(t == 0)
    def _():
        carry_ref[pl.ds(e0, MOE_STEP)] = jnp.zeros((MOE_STEP, MOE_ALIGN, XE_WIDTH), BF16)
        cap = xe_ref.shape[1] - MOE_CHUNK
        x_ref[0, 0:MOE_CHUNK, :] = jnp.zeros((MOE_CHUNK, XE_WIDTH), BF16)
        pads = [pltpu.make_async_copy(x_ref.at[0, 0:MOE_CHUNK, :], xe_ref.at[e0 + g, cap:cap + MOE_CHUNK, :],
                                      sem.at[slot]) for g in range(MOE_STEP)]
        for p in pads:
            p.start()
        for p in pads:
            p.wait()

    for g in range(MOE_STEP):
        x_ref[g, 0:MOE_ALIGN, :] = carry_ref[e0 + g]

    row = lax.broadcasted_iota(jnp.int32, (MOE_WIN, 1), 0)
    lane = lax.broadcasted_iota(jnp.int32, (MOE_WIN, XE_WIDTH - D_MODEL), 1)

    def gather_pass(pair, s, j):
        x0 = e0 + pair * MOE_PAIR
        cols = slice(s * ROUTE_TILE, (s + 1) * ROUTE_TILE)
        hits, rels, onehot = _moe_windows(offs_ref, pos_ref, t, subs, x0, s, j, MOE_WIN)
        r = _dot(onehot, h_ref[cols, :])
        for g in range(MOE_PAIR):
            gate = jnp.sum(jnp.where(hits[g], gate_ref[pl.ds(x0 + g, 1), cols], 0.0), axis=1, keepdims=True)
            g_hi = gate.astype(BF16).astype(F32)
            extra = jnp.where(lane == 0, g_hi, jnp.where(lane == 1, gate - g_hi, 0.0))
            new = jnp.concatenate([r[g * MOE_WIN:(g + 1) * MOE_WIN, :], extra], axis=1)
            ws = _align(off(x0 + g, s)) + j * MOE_WIN
            own = row >= off(x0 + g, s) - ws
            buf = pair * MOE_PAIR + g
            if isinstance(j, int):
                head = pl.ds(rels[g], MOE_ALIGN)
                x_ref[buf, head, :] = jnp.where(own[:MOE_ALIGN], new[:MOE_ALIGN],
                                                x_ref[buf, head, :].astype(F32)).astype(BF16)
                x_ref[buf, pl.ds(rels[g] + MOE_ALIGN, MOE_WIN - MOE_ALIGN), :] = new[MOE_ALIGN:].astype(BF16)
            else:
                own = own & (row < off(x0 + g, s + 1) - ws)
                win = pl.ds(rels[g], MOE_WIN)
                x_ref[buf, win, :] = jnp.where(own, new, x_ref[buf, win, :].astype(F32)).astype(BF16)

    for pair in range(MOE_STEP // MOE_PAIR):
        for s in range(subs):
            gather_pass(pair, s, 0)
    for pair in range(MOE_STEP // MOE_PAIR):
        for s in range(subs):
            lax.fori_loop(1, _moe_passes(offs_ref, t, subs, e0 + pair * MOE_PAIR, s, MOE_WIN),
                          lambda j, carry: (gather_pass(pair, s, j), carry)[1], 0)

    for g in range(MOE_STEP):
        x = e0 + g
        base, n_chunks = _moe_chunks(offs_ref, t, subs, x)
        last_group = pl.multiple_of(_align(off(x, subs)) - base, MOE_ALIGN)
        tail = x_ref[g, pl.ds(last_group, MOE_ALIGN), :].astype(F32)
        tail = jnp.where(row[:MOE_ALIGN] < off(x, subs) - base - last_group, tail, 0.0).astype(BF16)
        x_ref[g, pl.ds(last_group, MOE_ALIGN), :] = tail
        carry_ref[x] = tail

        def fill(i, carry):
            x_ref[g, pl.ds(pl.multiple_of(i * MOE_ALIGN, MOE_ALIGN), MOE_ALIGN), :] = jnp.zeros((MOE_ALIGN, XE_WIDTH), BF16)
            return carry

        lax.fori_loop(last_group // MOE_ALIGN + 1, n_chunks * MOE_CHUNK // MOE_ALIGN, fill, 0)

    copies(k, slot, lambda cp: cp.start())

    @pl.when(k == pl.num_programs(0) * steps - 1)
    def _():
        copies(k - 1, 1 - slot, lambda cp: cp.wait())
        copies(k, slot, lambda cp: cp.wait())


def _moe_ffn_kernel(x_ref, wg_ref, wu_ref, wd_ref, y_ref, wg_bf, wu_bf, wd_bf):
    j = pl.program_id(1)

    @pl.when(j == 0)
    def _():
        wg_bf[...] = wg_ref[...].astype(BF16)
        wu_bf[...] = wu_ref[...].astype(BF16)
        wd_bf[...] = wd_ref[...].astype(BF16)
        y_ref[0] = jnp.zeros(y_ref.shape[1:], BF16)

    @pl.when(j > 0)
    def _():
        x = x_ref[0, :, 0:D_MODEL]
        gate = jnp.sum(x_ref[0, :, D_MODEL:XE_WIDTH].astype(F32), axis=1, keepdims=True)
        hg = _dot(x, wg_bf[...])
        hu = _dot(x, wu_bf[...])
        act = (hg / (1.0 + jnp.exp(-hg))) * hu * gate
        y_ref[0] = _dot(act.astype(BF16), wd_bf[...]).astype(BF16)


def _moe_combine_kernel(offs_ref, x1_ref, pos_ref, y_ref, fg_ref, o_ref, y_buf, sem, *, final_norm):
    t = pl.program_id(0)
    q = pl.program_id(1)
    steps = pl.num_programs(1)
    e0 = q * MOE_STEP
    tt = x1_ref.shape[0]
    subs = tt // ROUTE_TILE
    k = t * steps + q
    slot = k % 2

    def fetch(kk, half, act):
        step_t, step_q = kk // steps, kk % steps
        for g in range(MOE_STEP):
            x = step_q * MOE_STEP + g
            base, n_chunks = _moe_chunks(offs_ref, step_t, subs, x)

            def body(c, carry):
                r0 = pl.multiple_of(c * MOE_CHUNK, MOE_CHUNK)
                act(pltpu.make_async_copy(y_ref.at[x, pl.ds(pl.multiple_of(base + r0, MOE_ALIGN), MOE_CHUNK), :],
                                          y_buf.at[half, g, pl.ds(r0, MOE_CHUNK), :], sem.at[half]))
                return carry

            lax.fori_loop(0, n_chunks, body, 0)

    @pl.when(k == 0)
    def _():
        y_buf[...] = jnp.zeros_like(y_buf)
        fetch(k, slot, lambda cp: cp.start())

    @pl.when(k + 1 < pl.num_programs(0) * steps)
    def _():
        fetch(k + 1, 1 - slot, lambda cp: cp.start())

    fetch(k, slot, lambda cp: cp.wait())

    @pl.when(q == 0)
    def _():
        o_ref[...] = x1_ref[...]

    def combine_pass(pair, s, j):
        cols = slice(s * ROUTE_TILE, (s + 1) * ROUTE_TILE)
        _, rels, onehot = _moe_windows(offs_ref, pos_ref, t, subs, e0 + pair * MOE_PAIR, s, j, MOE_COMBINE_WIN)
        ycat = jnp.concatenate([y_buf[slot, pair * MOE_PAIR + g, pl.ds(rels[g], MOE_COMBINE_WIN), :]
                                for g in range(MOE_PAIR)], axis=0)
        o_ref[cols, :] += _tn_dot(onehot, ycat)

    for pair in range(MOE_STEP // MOE_PAIR):
        for s in range(subs):
            combine_pass(pair, s, 0)
    for pair in range(MOE_STEP // MOE_PAIR):
        for s in range(subs):
            lax.fori_loop(1, _moe_passes(offs_ref, t, subs, e0 + pair * MOE_PAIR, s, MOE_COMBINE_WIN),
                          lambda j, carry: (combine_pass(pair, s, j), carry)[1], 0)

    if final_norm:
        @pl.when(q == pl.num_programs(1) - 1)
        def _():
            o_ref[...] = _rms(o_ref[...], fg_ref[...])


def _moe(offs, h2, x1, pos, gate, wg, wu, wd, layer, final_g, final_norm, cap):
    n = h2.shape[0]
    tt, tc = min(MOE_TILE, n), min(MOE_COMBINE_TILE, n)
    assert cap % MOE_CHUNK == 0 and n % tt == 0 and n % tc == 0
    steps = N_EXPERTS // MOE_STEP
    stage_rows = lambda tile, win: tile + MOE_ALIGN + win * pl.cdiv(ROUTE_TILE + MOE_ALIGN, win)
    tok = lambda tile: pl.BlockSpec((tile, D_MODEL), lambda t, q, offs: (t, 0))
    per_tok = lambda tile: pl.BlockSpec((N_EXPERTS, tile), lambda t, q, offs: (0, t))
    hbm = pl.BlockSpec(memory_space=pl.ANY)
    sems = ("arbitrary", "arbitrary")

    xe = pl.pallas_call(
        _moe_gather_kernel,
        grid_spec=pltpu.PrefetchScalarGridSpec(
            num_scalar_prefetch=1, grid=(n // tt, steps),
            in_specs=[tok(tt), per_tok(tt), per_tok(tt)], out_specs=hbm,
            scratch_shapes=[pltpu.VMEM((2, MOE_STEP, stage_rows(tt, MOE_WIN), XE_WIDTH), BF16),
                            pltpu.VMEM((N_EXPERTS, MOE_ALIGN, XE_WIDTH), BF16), pltpu.SemaphoreType.DMA((2,))],
        ),
        out_shape=jax.ShapeDtypeStruct((N_EXPERTS, cap + MOE_CHUNK, XE_WIDTH), BF16),
        compiler_params=_cparams(sems),
        name="moe_gather",
    )(offs, h2, pos, gate)

    fb = min(MOE_FFN_ROWS, cap)
    nb = cap // fb
    wspec = pl.BlockSpec((None, None, D_MODEL, D_MODEL), lambda e, j: (layer, e, 0, 0))
    w_bf = pltpu.VMEM((D_MODEL, D_MODEL), BF16)
    y = pl.pallas_call(
        _moe_ffn_kernel,
        grid=(N_EXPERTS, nb + 1),
        in_specs=[pl.BlockSpec((1, fb, XE_WIDTH), lambda e, j: (e, jnp.maximum(j - 1, 0), 0)), wspec, wspec, wspec],
        out_specs=pl.BlockSpec((1, fb, D_MODEL), lambda e, j: (e, jnp.where(j == 0, nb, j - 1), 0)),
        out_shape=jax.ShapeDtypeStruct((N_EXPERTS, cap + fb, D_MODEL), BF16),
        scratch_shapes=[w_bf, w_bf, w_bf],
        compiler_params=_cparams(("parallel", "arbitrary")),
        name="moe_ffn",
    )(xe, wg, wu, wd)

    return pl.pallas_call(
        functools.partial(_moe_combine_kernel, final_norm=final_norm),
        grid_spec=pltpu.PrefetchScalarGridSpec(
            num_scalar_prefetch=1, grid=(n // tc, steps),
            in_specs=[tok(tc), per_tok(tc), hbm, pl.BlockSpec((1, D_MODEL), lambda t, q, offs: (0, 0))],
            out_specs=tok(tc),
            scratch_shapes=[pltpu.VMEM((2, MOE_STEP, stage_rows(tc, MOE_COMBINE_WIN), D_MODEL), BF16),
                            pltpu.SemaphoreType.DMA((2,))],
        ),
        out_shape=jax.ShapeDtypeStruct((n, D_MODEL), F32),
        compiler_params=_cparams(sems),
        name="moe_combine",
    )(offs, x1, pos, y, final_g)


def _prep_layer(l, norm1_g, w_in, a_sink, a_norm_g, b_rel_bias, b_norm_g, c_alpha_w2_f, c_alpha_b_f,
                c_alpha_w2_b, c_alpha_b_b, c_norm_g, w_out, norm2_g, w_router, w_gate, w_up, w_down):
    z = jnp.zeros((C_LOWRANK, C_WIDTH), F32)
    w2 = jnp.concatenate([jnp.concatenate([c_alpha_w2_f[l], z], axis=1),
                          jnp.concatenate([z, c_alpha_w2_b[l]], axis=1)], axis=0).astype(BF16)
    b2 = jnp.concatenate([c_alpha_b_f[l], c_alpha_b_b[l]])[None, :]
    wr_cat = jnp.pad(w_router[l].astype(BF16), ((0, 0), (0, LANES - N_EXPERTS)))
    return dict(
        layer=l, g1=norm1_g[l][None, :], w_in=w_in.astype(BF16), w2=w2, b2=b2,
        sink=a_sink[l], a_g=a_norm_g[l][None, :], bias=_bias_table(b_rel_bias[l]), b_g=b_norm_g[l][None, :],
        c_g=c_norm_g[l][None, :], w_out=w_out.astype(BF16), g2=norm2_g[l][None, :],
        wr_cat=wr_cat, wg=w_gate, wu=w_up, wd=w_down)


def _layer(x, p, bsz, seq, rope, final_g, final_norm):
    n = bsz * seq
    l = p["layer"]
    aq, ak, av, bq, bk, bv, cq, ck, cv, cg, la = _in_proj(x, p["g1"], p["w_in"], l, p["w2"], p["b2"], seq, rope)
    oa = _attn_a(aq, ak, av, p["sink"], p["a_g"], bsz, seq)
    ob = _attn_b(bq, bk, bv, p["bias"], p["b_g"], bsz, seq)
    o_f, o_b = _gla(cq, ck, cv, la, bsz, seq)
    x1, h2, aff = _out_proj(x, oa, ob, o_f, o_b, cg, p["c_g"], p["w_out"], l, p["g2"], p["wr_cat"])
    cap = EC_CAPACITY * n // N_EXPERTS
    pos, offs = _route(aff, cap)
    return _moe(offs, h2, x1, pos, aff, p["wg"], p["wu"], p["wd"], l, final_g, final_norm, cap)


def _trunk(x, layers, rope, final_g):
    bsz, seq, _ = x.shape
    y = x.reshape(bsz * seq, D_MODEL)
    for l, p in enumerate(layers):
        y = _layer(y, p, bsz, seq, rope, final_g, final_norm=(l == len(layers) - 1))
    return y.reshape(bsz, seq, D_MODEL)


def kernel(x_prompt, x_sample, norm1_g, w_in, a_sink, a_norm_g, b_rel_bias, b_norm_g, c_alpha_w2_f, c_alpha_b_f,
           c_alpha_w2_b, c_alpha_b_b, c_norm_g, w_out, norm2_g, w_router, w_gate, w_up, w_down, final_g):
    depth = w_in.shape[0]
    layers = [_prep_layer(l, norm1_g, w_in, a_sink, a_norm_g, b_rel_bias, b_norm_g, c_alpha_w2_f, c_alpha_b_f,
                          c_alpha_w2_b, c_alpha_b_b, c_norm_g, w_out, norm2_g, w_router, w_gate, w_up, w_down)
              for l in range(depth)]
    fg = final_g[None, :]
    rope = _rope_tables(max(x_prompt.shape[1], x_sample.shape[1]))
    return _trunk(x_prompt, layers, rope, fg), _trunk(x_sample, layers, rope, fg)
```
